```python
import jax, jax.numpy as jnp
from jax import lax

D_MODEL = 1024
BATCH = 32
SEQ = 256
DEPTH = 2
DEC_BATCH = 4
DEC_SEQ = 1024
PAST_LEN = 256

GRID_W = 64
MIX_W = D_MODEL
CONV_W = D_MODEL // 4
CONV_K = 3
LRU_W = D_MODEL // 4
LRU_BLOCKS = 4
LRU_BLK = LRU_W // LRU_BLOCKS
LRU_CONV_K = 4
LRU_C = 8.0
MLA_HEADS = 8
QK_NOPE = 64
QK_ROPE = 32
V_DIM = 64
QK_DIM = QK_NOPE + QK_ROPE
MLA_W = MLA_HEADS * V_DIM
Q_LORA = 256
KV_LORA = 128
ROPE_THETA = 10000.0
FF = ((8 * D_MODEL // 3 + 255) // 256) * 256
Q_BLOCK = 128
SPLIT_CONV = 3 * CONV_W
SPLIT_LRU = SPLIT_CONV + 2 * LRU_W
SPLIT_Q = SPLIT_LRU + Q_LORA
SPLIT_KV = SPLIT_Q + KV_LORA
IN_W = SPLIT_KV + QK_ROPE
EPS = 1e-6

kernel_name = 'hybrid_diffusion_prefix_step'


def rms_norm(x, g):
    xf = x.astype(jnp.float32)
    y = xf * lax.rsqrt(jnp.mean(xf * xf, axis=-1, keepdims=True) + EPS)
    return (y * g.astype(jnp.float32)).astype(x.dtype)


def dwconv(x, w, b, pad_left, pad_right):
    C = x.shape[-1]
    y = lax.conv_general_dilated(x, w.astype(x.dtype)[:, None, :], window_strides=(1,),
                                 padding=[(pad_left, pad_right)],
                                 dimension_numbers=('NWC', 'WIO', 'NWC'),
                                 feature_group_count=C)
    return y + b.astype(x.dtype)


def short_conv_mixer(u, w, b):
    bg, cg, h = jnp.split(u, 3, axis=-1)
    return bg * dwconv(cg * h, w, b, 1, 1)


def block_diag_linear(x, w, b):
    Bsz, S, _ = x.shape
    xb = x.reshape(Bsz, S, LRU_BLOCKS, LRU_BLK)
    return jnp.einsum('bsnk,nkj->bsnj', xb, w).reshape(Bsz, S, LRU_W) + b


def _lin_combine(left, right):
    a1, b1 = left
    a2, b2 = right
    return a1 * a2, a2 * b1 + b2


def rglru_scan(x, wa, ba, wi, bi, lam, h0, reverse):
    xf = x.astype(jnp.float32)
    r = jax.nn.sigmoid(block_diag_linear(xf, wa, ba).astype(jnp.float32))
    i = jax.nn.sigmoid(block_diag_linear(xf, wi, bi).astype(jnp.float32))
    log_a = -LRU_C * r * jax.nn.softplus(-lam.astype(jnp.float32))
    a = jnp.exp(log_a)
    bt = jnp.sqrt(-jnp.expm1(2.0 * log_a)) * (i * xf)
    edge = -1 if reverse else 0
    bt = bt.at[:, edge].add(a[:, edge] * h0.astype(jnp.float32))
    _, h = lax.associative_scan(_lin_combine, (a, bt), axis=1, reverse=reverse)
    return h


def rglru_mixer(u, lp, h0f, h0b):
    xb, yb = jnp.split(u, 2, axis=-1)
    xc = dwconv(xb, lp['lru_conv_w'], lp['lru_conv_b'], 2, 1)
    hf = rglru_scan(xc, lp['lru_wa'][0], lp['lru_ba'][0], lp['lru_wi'][0], lp['lru_bi'][0],
                    lp['lru_lambda'][0], h0f, False)
    hb = rglru_scan(xc, lp['lru_wa'][1], lp['lru_ba'][1], lp['lru_wi'][1], lp['lru_bi'][1],
                    lp['lru_lambda'][1], h0b, True)
    y = ((hf + hb) * jax.nn.gelu(yb.astype(jnp.float32))).astype(u.dtype)
    return y, hf[:, -1].astype(u.dtype), hb[:, 0].astype(u.dtype)


def axial_rope_tables(n_tokens):
    n_rows = n_tokens // GRID_W
    row = jnp.repeat(jnp.arange(n_rows, dtype=jnp.float32), GRID_W)
    col = jnp.tile(jnp.arange(GRID_W, dtype=jnp.float32), n_rows)
    n_freq = QK_ROPE // 4
    inv_freq = jnp.power(ROPE_THETA, -jnp.arange(n_freq, dtype=jnp.float32) / n_freq)
    ang = jnp.stack([row[:, None] * inv_freq, col[:, None] * inv_freq], axis=1)
    return jnp.cos(ang), jnp.sin(ang)


def apply_axial_rope(x, cos, sin):
    n_freq = QK_ROPE // 4
    xs = x.astype(jnp.float32).reshape(x.shape[:-1] + (2, 2, n_freq))
    x1 = xs[..., 0, :]
    x2 = xs[..., 1, :]
    c = cos[None, :, None]
    s = sin[None, :, None]
    y = jnp.stack([x1 * c - x2 * s, x2 * c + x1 * s], axis=-2)
    return y.reshape(x.shape).astype(x.dtype)


def mla_queries(u_q, lp, rope):
    Bsz, S, _ = u_q.shape
    q = (rms_norm(u_q, lp['mla_qnorm_g']) @ lp['mla_wq_up']).reshape(Bsz, S, MLA_HEADS, QK_DIM)
    q = rms_norm(q, lp['q_norm_g'])
    if rope is not None:
        q = jnp.concatenate([q[..., :QK_NOPE], apply_axial_rope(q[..., QK_NOPE:], *rope)], axis=-1)
    return q


def mla_keys_values(ckv, k_rope, lp, rope):
    Bsz, S, _ = ckv.shape
    kv = (ckv @ lp['mla_wkv_up']).reshape(Bsz, S, MLA_HEADS, QK_NOPE + V_DIM)
    k_nope, v = kv[..., :QK_NOPE], kv[..., QK_NOPE:]
    k_r = jnp.broadcast_to(k_rope[:, :, None, :], (Bsz, S, MLA_HEADS, QK_ROPE)).astype(k_nope.dtype)
    k = rms_norm(jnp.concatenate([k_nope, k_r], axis=-1), lp['k_norm_g'])
    if rope is not None:
        k = jnp.concatenate([k[..., :QK_NOPE], apply_axial_rope(k[..., QK_NOPE:], *rope)], axis=-1)
    return k, v


def attend(q, k, v):
    Bsz, Sq, H, Dk = q.shape
    nb = Sq // Q_BLOCK
    qb = q.reshape(Bsz, nb, Q_BLOCK, H, Dk).transpose(1, 0, 2, 3, 4)
    scale = Dk ** -0.5

    def one(qblk):
        s = jnp.einsum('bqhd,bkhd->bhqk', qblk, k).astype(jnp.float32) * scale
        p = jax.nn.softmax(s, axis=-1).astype(v.dtype)
        return jnp.einsum('bhqk,bkhd->bqhd', p, v)

    o = lax.map(one, qb)
    return o.transpose(1, 0, 2, 3, 4).reshape(Bsz, Sq, H * v.shape[-1])


def mixing(hn, lp, rope, ctx_ckv, ctx_kr, h0f, h0b):
    u = hn @ lp['w_in']
    u_conv, u_lru, u_q, u_ckv, u_kr = jnp.split(u, [SPLIT_CONV, SPLIT_LRU, SPLIT_Q, SPLIT_KV], axis=-1)
    y_conv = short_conv_mixer(u_conv, lp['conv_w'], lp['conv_b'])
    y_lru, hf_last, hb_first = rglru_mixer(u_lru, lp, h0f, h0b)
    ckv = rms_norm(u_ckv, lp['mla_kvnorm_g'])
    q = mla_queries(u_q, lp, rope)
    k, v = mla_keys_values(ckv, u_kr, lp, rope)
    if ctx_ckv is not None:
        k_c, v_c = mla_keys_values(ctx_ckv.astype(ckv.dtype), ctx_kr, lp, None)
        k = jnp.concatenate([k, k_c], axis=1)
        v = jnp.concatenate([v, v_c], axis=1)
    y_mla = attend(q, k, v)
    y = jnp.concatenate([rms_norm(y_conv, lp['gnorm_conv']),
                         rms_norm(y_lru, lp['gnorm_lru']),
                         rms_norm(y_mla, lp['gnorm_mla'])], axis=-1) @ lp['w_out']
    return y, ckv, u_kr, hf_last, hb_first


def trunk_layer(x, cond, lp, rope, ctx_ckv, ctx_kr, h0f, h0b):
    mod = (jax.nn.silu(cond) @ lp['ada_w'] + lp['ada_b'])[:, None, :]
    sh1, sc1, g1, sh2, sc2, g2 = jnp.split(mod, 6, axis=-1)
    hn = rms_norm(x, lp['norm1_g']) * (1 + sc1) + sh1
    y, ckv, kr, hf, hb = mixing(hn, lp, rope, ctx_ckv, ctx_kr, h0f, h0b)
    x = x + g1 * y
    hn = rms_norm(x, lp['norm2_g']) * (1 + sc2) + sh2
    ff = (jax.nn.silu(hn @ lp['w_gate']) * (hn @ lp['w_up'])) @ lp['w_down']
    x = x + g2 * ff
    return x, ckv, kr, hf, hb


def _w(k, shape, fan_in, scale=1.0):
    return jax.random.normal(k, shape, jnp.float32) * (scale * fan_in ** -0.5)


def _g(k, shape):
    return 1.0 + 0.02 * jax.random.normal(k, shape, jnp.float32)


def _b(k, shape):
    return 0.02 * jax.random.normal(k, shape, jnp.float32)


def setup_inputs(seed: int = 0) -> dict:
    key = jax.random.key(seed)
    ks = jax.random.split(key, 40)
    a_init = jax.random.uniform(ks[19], (DEPTH, 2, LRU_W), jnp.float32, 0.9, 0.999)
    return {
        'x_prompt': jax.random.normal(ks[0], (BATCH, SEQ, D_MODEL), jnp.float32),
        'x_sample': jax.random.normal(ks[1], (DEC_BATCH, DEC_SEQ, D_MODEL), jnp.float32),
        'cache_ckv': jax.random.normal(ks[2], (DEC_BATCH, DEPTH, PAST_LEN, KV_LORA), jnp.float32),
        'cache_krope': jax.random.normal(ks[3], (DEC_BATCH, DEPTH, PAST_LEN, QK_ROPE), jnp.float32),
        'state_lru': 0.5 * jax.random.normal(ks[4], (DEC_BATCH, DEPTH, 2, LRU_W), jnp.float32),
        'c': jax.random.normal(ks[5], (DEC_BATCH, D_MODEL), jnp.float32),
        'c_ctx': jax.random.normal(ks[6], (D_MODEL,), jnp.float32),
        'norm1_g': _g(ks[7], (DEPTH, D_MODEL)),
        'ada_w': _w(ks[8], (DEPTH, D_MODEL, 6 * D_MODEL), D_MODEL, 0.5),
        'ada_b': _b(ks[9], (DEPTH, 6 * D_MODEL)),
        'w_in': _w(ks[10], (DEPTH, D_MODEL, IN_W), D_MODEL),
        'conv_w': _w(ks[11], (DEPTH, CONV_K, CONV_W), CONV_K),
        'conv_b': _b(ks[12], (DEPTH, CONV_W)),
        'lru_conv_w': _w(ks[13], (DEPTH, LRU_CONV_K, LRU_W), LRU_CONV_K),
        'lru_conv_b': _b(ks[14], (DEPTH, LRU_W)),
        'lru_wa': _w(ks[15], (DEPTH, 2, LRU_BLOCKS, LRU_BLK, LRU_BLK), LRU_BLK),
        'lru_ba': _b(ks[16], (DEPTH, 2, LRU_W)),
        'lru_wi': _w(ks[17], (DEPTH, 2, LRU_BLOCKS, LRU_BLK, LRU_BLK), LRU_BLK),
        'lru_bi': _b(ks[18], (DEPTH, 2, LRU_W)),
        'lru_lambda': jnp.log(a_init) - jnp.log1p(-a_init),
        'mla_qnorm_g': _g(ks[20], (DEPTH, Q_LORA)),
        'mla_wq_up': _w(ks[21], (DEPTH, Q_LORA, MLA_HEADS * QK_DIM), Q_LORA),
        'mla_kvnorm_g': _g(ks[22], (DEPTH, KV_LORA)),
        'mla_wkv_up': _w(ks[23], (DEPTH, KV_LORA, MLA_HEADS * (QK_NOPE + V_DIM)), KV_LORA),
        'q_norm_g': _g(ks[24], (DEPTH, QK_DIM)),
        'k_norm_g': _g(ks[25], (DEPTH, QK_DIM)),
        'gnorm_conv': _g(ks[26], (DEPTH, CONV_W)),
        'gnorm_lru': _g(ks[27], (DEPTH, LRU_W)),
        'gnorm_mla': _g(ks[28], (DEPTH, MLA_W)),
        'w_out': _w(ks[29], (DEPTH, MIX_W, D_MODEL), MIX_W),
        'norm2_g': _g(ks[30], (DEPTH, D_MODEL)),
        'w_gate': _w(ks[31], (DEPTH, D_MODEL, FF), D_MODEL),
        'w_up': _w(ks[32], (DEPTH, D_MODEL, FF), D_MODEL),
        'w_down': _w(ks[33], (DEPTH, FF, D_MODEL), FF),
    }


def reference(x_prompt, x_sample, cache_ckv, cache_krope, state_lru, c, c_ctx,
              norm1_g, ada_w, ada_b, w_in, conv_w, conv_b, lru_conv_w, lru_conv_b,
              lru_wa, lru_ba, lru_wi, lru_bi, lru_lambda, mla_qnorm_g, mla_wq_up,
              mla_kvnorm_g, mla_wkv_up, q_norm_g, k_norm_g, gnorm_conv, gnorm_lru,
              gnorm_mla, w_out, norm2_g, w_gate, w_up, w_down):
    rope = axial_rope_tables(x_sample.shape[1])
    ctx_cond = c_ctx[None, :]
    h0_zero = jnp.zeros((x_prompt.shape[0], LRU_W), x_prompt.dtype)
    xp = x_prompt
    xs = x_sample
    ckv_list, kr_list, lru_list = [], [], []
    for l in range(DEPTH):
        lp = {
            'norm1_g': norm1_g[l], 'ada_w': ada_w[l], 'ada_b': ada_b[l], 'w_in': w_in[l],
            'conv_w': conv_w[l], 'conv_b': conv_b[l], 'lru_conv_w': lru_conv_w[l],
            'lru_conv_b': lru_conv_b[l], 'lru_wa': lru_wa[l], 'lru_ba': lru_ba[l],
            'lru_wi': lru_wi[l], 'lru_bi': lru_bi[l], 'lru_lambda': lru_lambda[l],
            'mla_qnorm_g': mla_qnorm_g[l], 'mla_wq_up': mla_wq_up[l],
            'mla_kvnorm_g': mla_kvnorm_g[l], 'mla_wkv_up': mla_wkv_up[l],
            'q_norm_g': q_norm_g[l], 'k_norm_g': k_norm_g[l], 'gnorm_conv': gnorm_conv[l],
            'gnorm_lru': gnorm_lru[l], 'gnorm_mla': gnorm_mla[l], 'w_out': w_out[l],
            'norm2_g': norm2_g[l], 'w_gate': w_gate[l], 'w_up': w_up[l], 'w_down': w_down[l],
        }
        xp, ckv, kr, hf, hb = trunk_layer(xp, ctx_cond, lp, None, None, None, h0_zero, h0_zero)
        ckv_list.append(ckv)
        kr_list.append(kr)
        lru_list.append(jnp.stack([hf, hb], axis=1))
        xs, _, _, _, _ = trunk_layer(xs, c, lp, rope, cache_ckv[:, l], cache_krope[:, l],
                                     state_lru[:, l, 0], state_lru[:, l, 1])
    new_cache_ckv = jnp.stack(ckv_list, axis=1)
    new_cache_krope = jnp.stack(kr_list, axis=1)
    new_state_lru = jnp.stack(lru_list, axis=1)
    return (xp, xs, new_cache_ckv, new_cache_krope, new_state_lru)
```

```python
import functools
import math

import jax
import jax.numpy as jnp
from jax import lax
from jax.experimental import pallas as pl
from jax.experimental.pallas import tpu as pltpu

F32 = jnp.float32
BF16 = jnp.bfloat16

D_MODEL = 1024
DEPTH = 2
GRID_W = 64
CONV_W = 256
LRU_W = 256
LRU_BLOCKS = 4
LRU_BLK = 64
LRU_C = 8.0
MLA_HEADS = 8
QK_NOPE = 64
QK_ROPE = 32
V_DIM = 64
QK_DIM = QK_NOPE + QK_ROPE
Q_LORA = 256
KV_LORA = 128
ROPE_THETA = 10000.0
FF = 2816
EPS = 1e-6

LANES = 128
SUBLANES = 8
HEAD_PAD = LANES
VMEM_LIMIT_BYTES = 60 * 1024 * 1024

C_BG, C_CG, C_H = 0, 256, 512
C_XB, C_YB = 768, 1024
C_Q = 1280
C_CKV = 1536
C_KR = 1664
U_W = 1792
ROPE_LO = QK_NOPE

ROW_CHUNK = 256
SCAN_CHUNK = 32
Q_CHUNK = 256
FF_CHUNK = 256
FFN_ROWS = 512
PAD_ROWS = SUBLANES


def _rms_scale(x, n):
    ms = jnp.sum(x * x, axis=-1, keepdims=True) * (1.0 / n)
    return lax.rsqrt(ms + EPS)


def _sigmoid(x):
    return 1.0 / (1.0 + jnp.exp(-x))


def _gelu_tanh(x):
    c = math.sqrt(2.0 / math.pi)
    return 0.5 * x * (1.0 + jnp.tanh(c * (x + 0.044715 * (x * x * x))))


def _dot(a, b):
    return jnp.dot(a, b, preferred_element_type=F32)


ADA_TN = 1536


def _ada_kernel(cond_ref, w_ref, b_ref, o_ref):
    cnd = cond_ref[...]
    s = (cnd * _sigmoid(cnd)).astype(BF16)
    o_ref[...] = _dot(s, w_ref[...].astype(BF16)) + b_ref[...]


def _ada_call(cond8, ada_w, ada_b):
    n_out = 6 * D_MODEL
    return pl.pallas_call(
        _ada_kernel,
        grid=(DEPTH, n_out // ADA_TN),
        in_specs=[
            pl.BlockSpec((8, D_MODEL), lambda l, j: (0, 0)),
            pl.BlockSpec((None, D_MODEL, ADA_TN), lambda l, j: (l, 0, j)),
            pl.BlockSpec((None, 1, ADA_TN), lambda l, j: (l, 0, j)),
        ],
        out_specs=pl.BlockSpec((None, 8, ADA_TN), lambda l, j: (l, 0, j)),
        out_shape=jax.ShapeDtypeStruct((DEPTH, 8, n_out), F32),
        compiler_params=pltpu.CompilerParams(
            dimension_semantics=("arbitrary", "arbitrary"),
            vmem_limit_bytes=VMEM_LIMIT_BYTES),
        name="ada_mod",
    )(cond8, ada_w, ada_b.reshape(DEPTH, 1, n_out))


def _mix_kernel(*refs, seq, n_seq, latent):
    T = seq * n_seq
    n_ctx = 256 if latent else 0
    (x_ref, mod_ref, n1g_ref, win_ref, convw_ref, convb_ref, lcw_ref, lcb_ref,
     wg_ref, bg_ref, lam_ref, qng_ref, wq_ref, kvng_ref, wk_ref, wv_ref,
     qg_ref, kg_ref, gnc_ref, gnl_ref, gnm_ref, wout_ref) = refs[:22]
    pos = 22
    if latent:
        (wqp_ref, qgp_ref, kgp_ref, ropec_ref, ropes_ref, cckv_ref, ckr_ref,
         h0_ref) = refs[pos:pos + 8]
        pos += 8
    h_out_ref = refs[pos]
    pos += 1
    if not latent:
        ckv_out_ref, kr_out_ref, st_out_ref = refs[pos:pos + 3]
        pos += 3
    u_scr, ab_scr, q_scr, k_scr, v_scr, ym_scr, ycat_scr = refs[pos:pos + 7]

    n_chunks = T // ROW_CHUNK
    lane = lax.broadcasted_iota(jnp.int32, (1, LANES), 1)
    rope_lanes = (lane >= ROPE_LO) & (lane < ROPE_LO + QK_ROPE)

    sh1 = mod_ref[:, 0:D_MODEL]
    sc1 = mod_ref[:, D_MODEL:2 * D_MODEL]
    g1 = mod_ref[:, 2 * D_MODEL:3 * D_MODEL]

    u_scr[0:PAD_ROWS, :] = jnp.zeros((PAD_ROWS, U_W), F32)
    u_scr[PAD_ROWS + T:2 * PAD_ROWS + T, :] = jnp.zeros((PAD_ROWS, U_W), F32)

    def phase_a(c, carry):
        r0 = pl.multiple_of(c * ROW_CHUNK, ROW_CHUNK)
        x = x_ref[pl.ds(r0, ROW_CHUNK), :]
        hn = x * _rms_scale(x, D_MODEL) * n1g_ref[...]
        hn = hn * (1.0 + sc1) + sh1
        u = _dot(hn.astype(BF16), win_ref[...])
        urows = pl.ds(pl.multiple_of(PAD_ROWS + r0, SUBLANES), ROW_CHUNK)
        u_scr[urows, :] = u
        u_scr[urows, C_CG:C_CG + CONV_W] = (
            u[:, C_CG:C_CG + CONV_W] * u[:, C_H:C_H + CONV_W])
        return carry

    lax.fori_loop(0, n_chunks, phase_a, 0)

    def phase_b(c, carry):
        r0 = pl.multiple_of(c * ROW_CHUNK, ROW_CHUNK)
        rows = pl.ds(pl.multiple_of(PAD_ROWS + r0, SUBLANES), ROW_CHUNK)
        uq = u_scr[rows, C_Q:C_Q + Q_LORA]
        qn = (uq * _rms_scale(uq, Q_LORA) * qng_ref[...]).astype(BF16)
        qa = _dot(qn, wq_ref[...])
        uc = u_scr[rows, C_CKV:C_CKV + KV_LORA]
        ckv = uc * _rms_scale(uc, KV_LORA) * kvng_ref[...]
        krb = u_scr[rows, C_KR:C_KR + LANES]
        kr_rolled = pltpu.roll(krb, 64, 1)
        if not latent:
            ckv_out_ref[pl.ds(r0, ROW_CHUNK), :] = ckv
            kr_out_ref[pl.ds(r0, ROW_CHUNK), :] = kr_rolled[:, 0:QK_ROPE]
        krm = jnp.where(rope_lanes, krb, 0.0)
        cb = ckv.astype(BF16)
        ka = _dot(cb, wk_ref[...])
        va = _dot(cb, wv_ref[...])
        if latent:
            qpa = _dot(qn, wqp_ref[...])
            rc = ropec_ref[pl.ds(r0, ROW_CHUNK), :]
            rs = ropes_ref[pl.ds(r0, ROW_CHUNK), :]
            q_tc = rc * qg_ref[...]
            q_ts = rs * qgp_ref[...]
            k_tc = rc * kg_ref[...]
            k_ts = rs * kgp_ref[...]
        kr0 = r0
        for h in range(MLA_HEADS):
            sl = slice(h * HEAD_PAD, (h + 1) * HEAD_PAD)
            qh = qa[:, sl]
            rq = _rms_scale(qh, QK_DIM)
            if latent:
                qh = (qh * q_tc + qpa[:, sl] * q_ts) * rq
            else:
                qh = qh * qg_ref[...] * rq
            q_scr[h, pl.ds(r0, ROW_CHUNK), :] = qh.astype(BF16)
            kpre = ka[:, sl] + krm
            rk = _rms_scale(kpre, QK_DIM)
            if latent:
                kh = (kpre * k_tc + kr_rolled * k_ts) * rk
            else:
                kh = kpre * kg_ref[...] * rk
            k_scr[h, pl.ds(kr0, ROW_CHUNK), :] = kh.astype(BF16)
        for j in range(MLA_HEADS // 2):
            v_scr[j, pl.ds(kr0, ROW_CHUNK), :] = (
                va[:, j * LANES:(j + 1) * LANES].astype(BF16))
        return carry

    lax.fori_loop(0, n_chunks, phase_b, 0)

    if latent:
        cc = cckv_ref[...].astype(BF16)
        ka = _dot(cc, wk_ref[...])
        va = _dot(cc, wv_ref[...])
        krc = ckr_ref[...]
        for h in range(MLA_HEADS):
            kpre = ka[:, h * HEAD_PAD:(h + 1) * HEAD_PAD] + krc
            kh = kpre * kg_ref[...] * _rms_scale(kpre, QK_DIM)
            k_scr[h, seq:seq + n_ctx, :] = kh.astype(BF16)
        for j in range(MLA_HEADS // 2):
            v_scr[j, seq:seq + n_ctx, :] = va[:, j * LANES:(j + 1) * LANES].astype(BF16)

    neg_lam = -lam_ref[...]
    sp = jnp.maximum(neg_lam, 0.0) + jnp.log(1.0 + jnp.exp(-jnp.abs(neg_lam)))
    row_rc = lax.broadcasted_iota(jnp.int32, (ROW_CHUNK, 1), 0)
    for ci in range(n_chunks):
        r0 = ci * ROW_CHUNK
        base = PAD_ROWS + r0
        first = (r0 % seq) == 0
        last = ((r0 + ROW_CHUNK) % seq) == 0

        def win(col, shift, width=CONV_W):
            w = u_scr[base + shift:base + shift + ROW_CHUNK, col:col + width]
            if shift < 0 and first:
                w = jnp.where(row_rc < -shift, 0.0, w)
            if shift > 0 and last:
                w = jnp.where(row_rc >= ROW_CHUNK - shift, 0.0, w)
            return w

        z_m, z_0, z_p = win(C_CG, -1), win(C_CG, 0), win(C_CG, 1)
        conv = (z_m * convw_ref[0:1, :] + z_0 * convw_ref[1:2, :]
                + z_p * convw_ref[2:3, :] + convb_ref[...])
        yc = u_scr[base:base + ROW_CHUNK, C_BG:C_BG + CONV_W] * conv
        ycn = yc * _rms_scale(yc, CONV_W) * gnc_ref[...]
        ycat_scr[r0:r0 + ROW_CHUNK, 0:CONV_W] = ycn.astype(BF16)

        xc = (win(C_XB, -2) * lcw_ref[0:1, :] + win(C_XB, -1) * lcw_ref[1:2, :]
              + win(C_XB, 0) * lcw_ref[2:3, :] + win(C_XB, 1) * lcw_ref[3:4, :]
              + lcb_ref[...])
        gates = _dot(xc.astype(BF16), wg_ref[...]) + bg_ref[...]
        for d in range(2):
            rgate = _sigmoid(gates[:, (2 * d) * LRU_W:(2 * d + 1) * LRU_W])
            igate = _sigmoid(gates[:, (2 * d + 1) * LRU_W:(2 * d + 2) * LRU_W])
            log_a = (-LRU_C) * rgate * sp[d:d + 1, :]
            a = jnp.exp(log_a)
            mult = jnp.sqrt(1.0 - a * a)
            ab_scr[r0:r0 + ROW_CHUNK, (2 * d) * LRU_W:(2 * d + 1) * LRU_W] = a
            ab_scr[r0:r0 + ROW_CHUNK, (2 * d + 1) * LRU_W:(2 * d + 2) * LRU_W] = (
                mult * (igate * xc))

    n_sc = seq // SCAN_CHUNK
    row_sc = lax.broadcasted_iota(jnp.int32, (SCAN_CHUNK, LRU_W), 0)
    steps = [1 << i for i in range(int(math.log2(SCAN_CHUNK)))]
    for s in range(n_seq):
        def scan_body(c, carry, s=s):
            cf, cb_ = carry
            rf = pl.multiple_of(s * seq + c * SCAN_CHUNK, SCAN_CHUNK)
            rb = pl.multiple_of(s * seq + (n_sc - 1 - c) * SCAN_CHUNK, SCAN_CHUNK)
            a = ab_scr[pl.ds(rf, SCAN_CHUNK), 0:LRU_W]
            b = ab_scr[pl.ds(rf, SCAN_CHUNK), LRU_W:2 * LRU_W]
            b = b + jnp.where(row_sc == 0, a * cf, 0.0)
            for d in steps:
                b = b + a * jnp.where(row_sc >= d, pltpu.roll(b, d, 0), 0.0)
                if d != steps[-1]:
                    a = a * jnp.where(row_sc >= d, pltpu.roll(a, d, 0), 1.0)
            ab_scr[pl.ds(rf, SCAN_CHUNK), LRU_W:2 * LRU_W] = b
            cf = b[SCAN_CHUNK - 1:SCAN_CHUNK, :]
            a = ab_scr[pl.ds(rb, SCAN_CHUNK), 2 * LRU_W:3 * LRU_W]
            b = ab_scr[pl.ds(rb, SCAN_CHUNK), 3 * LRU_W:4 * LRU_W]
            b = b + jnp.where(row_sc == SCAN_CHUNK - 1, a * cb_, 0.0)
            for d in steps:
                keep = row_sc < SCAN_CHUNK - d
                b = b + a * jnp.where(keep, pltpu.roll(b, SCAN_CHUNK - d, 0), 0.0)
                if d != steps[-1]:
                    a = a * jnp.where(keep, pltpu.roll(a, SCAN_CHUNK - d, 0), 1.0)
            ab_scr[pl.ds(rb, SCAN_CHUNK), 3 * LRU_W:4 * LRU_W] = b
            cb_ = b[0:1, :]
            return cf, cb_

        if latent:
            init = (h0_ref[0:1, :], h0_ref[1:2, :])
        else:
            init = (jnp.zeros((1, LRU_W), F32), jnp.zeros((1, LRU_W), F32))
        cf, cb_ = lax.fori_loop(0, n_sc, scan_body, init)
        if not latent:
            st_out_ref[s, 0:1, :] = cf
            st_out_ref[s, 1:2, :] = cb_

    n_q = seq // Q_CHUNK
    n_keys = seq + n_ctx
    low_half = lane < V_DIM
    for s in range(n_seq):
        k0 = s * seq

        def att_body(idx, carry, s=s, k0=k0):
            j = idx // n_q
            qc = idx % n_q
            q0 = pl.multiple_of(s * seq + qc * Q_CHUNK, Q_CHUNK)
            v = v_scr[j, k0:k0 + n_keys, :]
            outs = []
            for par in range(2):
                h = 2 * j + par
                q = q_scr[h, pl.ds(q0, Q_CHUNK), :]
                k = k_scr[h, k0:k0 + n_keys, :]
                sc = lax.dot_general(q, k, (((1,), (1,)), ((), ())),
                                     preferred_element_type=F32)
                m = jnp.max(sc, axis=-1, keepdims=True)
                p = jnp.exp(sc - m)
                l = jnp.sum(p, axis=-1, keepdims=True)
                o = _dot(p.astype(BF16), v)
                outs.append(o * (1.0 / l))
            ym_scr[j, pl.ds(q0, Q_CHUNK), :] = jnp.where(low_half, outs[0], outs[1])
            return carry

        lax.fori_loop(0, (MLA_HEADS // 2) * n_q, att_body, 0)

    def phase_f(c, carry):
        r0 = pl.multiple_of(c * ROW_CHUNK, ROW_CHUNK)
        rows = pl.ds(r0, ROW_CHUNK)
        hf = ab_scr[rows, LRU_W:2 * LRU_W]
        hb = ab_scr[rows, 3 * LRU_W:4 * LRU_W]
        yb = u_scr[pl.ds(pl.multiple_of(PAD_ROWS + r0, SUBLANES), ROW_CHUNK),
                   C_YB:C_YB + LRU_W]
        yl = (hf + hb) * _gelu_tanh(yb)
        yln = yl * _rms_scale(yl, LRU_W) * gnl_ref[...]
        ycat_scr[rows, CONV_W:CONV_W + LRU_W] = yln.astype(BF16)
        yms = [ym_scr[j, rows, :] for j in range(MLA_HEADS // 2)]
        ssq = sum(jnp.sum(y * y, axis=-1, keepdims=True) for y in yms)
        rm = lax.rsqrt(ssq * (1.0 / (MLA_HEADS * V_DIM)) + EPS)
        for j in range(MLA_HEADS // 2):
            c0 = CONV_W + LRU_W + j * LANES
            ycat_scr[rows, c0:c0 + LANES] = (
                yms[j] * rm * gnm_ref[:, j * LANES:(j + 1) * LANES]).astype(BF16)
        y = _dot(ycat_scr[rows, :], wout_ref[...])
        h_out_ref[rows, :] = x_ref[rows, :] + g1 * y
        return carry

    lax.fori_loop(0, n_chunks, phase_f, 0)


def _const_spec(shape):
    nd = len(shape)
    return pl.BlockSpec(shape, lambda i, _n=nd: (0,) * _n,
                        pipeline_mode=pl.Buffered(1))


def _mix_call(x2d, mods, layer, lw, *, seq, n_seq, latent, extra=None):
    n_tok = x2d.shape[0]
    T = seq * n_seq
    grid = (n_tok // T,)
    n_keys_buf = T if not latent else seq + 256

    if latent:
        mod_map = lambda i: (layer, i, 0, 0)
    else:
        mod_map = lambda i: (layer, 4, 0, 0)

    consts = [lw["norm1_g"], lw["w_in"], lw["conv_w"], lw["conv_b"], lw["lru_conv_w"],
              lw["lru_conv_b"], lw["w_gates"], lw["b_gates"], lw["lru_lambda"],
              lw["mla_qnorm_g"], lw["wq"], lw["mla_kvnorm_g"], lw["wk"], lw["wv"],
              lw["qg"], lw["kg"], lw["gnorm_conv"], lw["gnorm_lru"], lw["gnorm_mla"],
              lw["w_out"]]
    args = [x2d, mods] + consts
    in_specs = [pl.BlockSpec((T, D_MODEL), lambda i: (i, 0)),
                pl.BlockSpec((None, None, 1, 6 * D_MODEL), mod_map)]
    in_specs += [_const_spec(a.shape) for a in consts]
    if latent:
        lconsts = [lw["wq_partner"], lw["qg_partner"], lw["kg_partner"],
                   extra["rope_c"], extra["rope_s"]]
        args += lconsts
        in_specs += [_const_spec(a.shape) for a in lconsts]
        args += [extra["cache_ckv"], extra["cache_kr"], extra["state"]]
        in_specs += [
            pl.BlockSpec((None, None, 256, KV_LORA), lambda i: (i, layer, 0, 0)),
            pl.BlockSpec((None, None, 256, LANES), lambda i: (i, layer, 0, 0)),
            pl.BlockSpec((None, None, 2, LRU_W), lambda i: (i, layer, 0, 0)),
        ]

    out_shape = [jax.ShapeDtypeStruct((n_tok, D_MODEL), F32)]
    out_specs = [pl.BlockSpec((T, D_MODEL), lambda i: (i, 0))]
    if not latent:
        out_shape += [jax.ShapeDtypeStruct((n_tok, KV_LORA), F32),
                      jax.ShapeDtypeStruct((n_tok, QK_ROPE), F32),
                      jax.ShapeDtypeStruct((n_tok // seq, 2, LRU_W), F32)]
        out_specs += [pl.BlockSpec((T, KV_LORA), lambda i: (i, 0)),
                      pl.BlockSpec((T, QK_ROPE), lambda i: (i, 0)),
                      pl.BlockSpec((n_seq, 2, LRU_W), lambda i: (i, 0, 0))]

    scratch = [
        pltpu.VMEM((T + 2 * PAD_ROWS, U_W), F32),
        pltpu.VMEM((T, 4 * LRU_W), F32),
        pltpu.VMEM((MLA_HEADS, T, HEAD_PAD), BF16),
        pltpu.VMEM((MLA_HEADS, n_keys_buf, HEAD_PAD), BF16),
        pltpu.VMEM((MLA_HEADS // 2, n_keys_buf, LANES), BF16),
        pltpu.VMEM((MLA_HEADS // 2, T, LANES), F32),
        pltpu.VMEM((T, D_MODEL), BF16),
    ]
    return pl.pallas_call(
        functools.partial(_mix_kernel, seq=seq, n_seq=n_seq, latent=latent),
        grid=grid,
        in_specs=in_specs,
        out_specs=out_specs,
        out_shape=out_shape,
        scratch_shapes=scratch,
        compiler_params=pltpu.CompilerParams(
            dimension_semantics=("arbitrary",),
            vmem_limit_bytes=VMEM_LIMIT_BYTES),
        name="mix_latent" if latent else "mix_context",
    )(*args)


def _ffn_kernel(h_ref, mod_ref, n2g_ref, wg_ref, wu_ref, wd_ref, o_ref):
    sh2 = mod_ref[:, 3 * D_MODEL:4 * D_MODEL]
    sc2 = mod_ref[:, 4 * D_MODEL:5 * D_MODEL]
    g2 = mod_ref[:, 5 * D_MODEL:6 * D_MODEL]
    h = h_ref[...]
    hn = h * _rms_scale(h, D_MODEL) * n2g_ref[...]
    hn = (hn * (1.0 + sc2) + sh2).astype(BF16)
    acc = None
    for c0 in range(0, FF, FF_CHUNK):
        g = _dot(hn, wg_ref[:, c0:c0 + FF_CHUNK])
        u = _dot(hn, wu_ref[:, c0:c0 + FF_CHUNK])
        act = (g * _sigmoid(g) * u).astype(BF16)
        part = _dot(act, wd_ref[c0:c0 + FF_CHUNK, :])
        acc = part if acc is None else acc + part
    o_ref[...] = h + g2 * acc


def _ffn_call(h2d, mods, layer, lw, *, rows_per_mod, mod_base):
    n_tok = h2d.shape[0]
    if mod_base is None:
        tiles_per_mod = rows_per_mod // FFN_ROWS
        mod_map = lambda i: (layer, i // tiles_per_mod, 0, 0)
    else:
        mod_map = lambda i: (layer, mod_base, 0, 0)
    consts = [lw["norm2_g"], lw["w_gate"], lw["w_up"], lw["w_down"]]
    return pl.pallas_call(
        _ffn_kernel,
        grid=(n_tok // FFN_ROWS,),
        in_specs=[pl.BlockSpec((FFN_ROWS, D_MODEL), lambda i: (i, 0)),
                  pl.BlockSpec((None, None, 1, 6 * D_MODEL), mod_map)]
        + [_const_spec(a.shape) for a in consts],
        out_specs=pl.BlockSpec((FFN_ROWS, D_MODEL), lambda i: (i, 0)),
        out_shape=jax.ShapeDtypeStruct((n_tok, D_MODEL), F32),
        compiler_params=pltpu.CompilerParams(
            dimension_semantics=("arbitrary",),
            vmem_limit_bytes=VMEM_LIMIT_BYTES),
        name="swiglu",
    )(h2d, mods, *consts)


def _rope_partner_index():
    return [d + 8 if (d % 16) < 8 else d - 8 for d in range(QK_ROPE)]


def _pad_heads(w, width, head_w):
    k = w.shape[0]
    w = w.reshape(k, MLA_HEADS, head_w)[:, :, :width]
    w = jnp.pad(w, ((0, 0), (0, 0), (0, HEAD_PAD - width)))
    return w.reshape(k, MLA_HEADS * HEAD_PAD)


def _block_diag(w):
    out = jnp.zeros((LRU_W, LRU_W), w.dtype)
    for n in range(LRU_BLOCKS):
        out = out.at[n * LRU_BLK:(n + 1) * LRU_BLK, n * LRU_BLK:(n + 1) * LRU_BLK].set(w[n])
    return out


def _prep_layer(l, p):
    partner = jnp.array(_rope_partner_index(), jnp.int32)
    scale = QK_DIM ** -0.5
    w_in = p["w_in"][l]
    w_kr = w_in[:, C_KR:C_KR + QK_ROPE]
    zeros32 = jnp.zeros((D_MODEL, QK_ROPE), F32)
    kr_block = jnp.concatenate([w_kr[:, partner], zeros32, w_kr, zeros32], axis=1)
    w_in_aug = jnp.concatenate([w_in[:, :C_KR], kr_block], axis=1).astype(BF16)

    wq = p["mla_wq_up"][l]
    wq_pad = _pad_heads(wq, QK_DIM, QK_DIM)
    wq_h = wq.reshape(Q_LORA, MLA_HEADS, QK_DIM)
    wq_rope_partner = wq_h[:, :, QK_NOPE:][:, :, partner]
    wq_partner = jnp.pad(wq_rope_partner,
                         ((0, 0), (0, 0), (ROPE_LO, HEAD_PAD - ROPE_LO - QK_ROPE)))
    wq_partner = wq_partner.reshape(Q_LORA, MLA_HEADS * HEAD_PAD)

    wkv = p["mla_wkv_up"][l].reshape(KV_LORA, MLA_HEADS, QK_NOPE + V_DIM)
    wk_pad = jnp.pad(wkv[:, :, :QK_NOPE], ((0, 0), (0, 0), (0, HEAD_PAD - QK_NOPE)))
    wk_pad = wk_pad.reshape(KV_LORA, MLA_HEADS * HEAD_PAD)
    wv = wkv[:, :, QK_NOPE:].reshape(KV_LORA, MLA_HEADS * V_DIM)

    def pad_gain(g):
        return jnp.pad(g, (0, HEAD_PAD - QK_DIM)).reshape(1, HEAD_PAD)

    def partner_gain(g):
        gp = g[QK_NOPE:][partner]
        return jnp.pad(gp, (ROPE_LO, HEAD_PAD - ROPE_LO - QK_ROPE)).reshape(1, HEAD_PAD)

    qg = p["q_norm_g"][l]
    kg = p["k_norm_g"][l]
    w_gates = jnp.concatenate(
        [_block_diag(p["lru_wa"][l, 0]), _block_diag(p["lru_wi"][l, 0]),
         _block_diag(p["lru_wa"][l, 1]), _block_diag(p["lru_wi"][l, 1])], axis=1)
    b_gates = jnp.concatenate(
        [p["lru_ba"][l, 0], p["lru_bi"][l, 0], p["lru_ba"][l, 1], p["lru_bi"][l, 1]])
    return {
        "norm1_g": p["norm1_g"][l].reshape(1, D_MODEL),
        "w_in": w_in_aug,
        "conv_w": p["conv_w"][l],
        "conv_b": p["conv_b"][l].reshape(1, CONV_W),
        "lru_conv_w": p["lru_conv_w"][l],
        "lru_conv_b": p["lru_conv_b"][l].reshape(1, LRU_W),
        "w_gates": w_gates.astype(BF16),
        "b_gates": b_gates.reshape(1, 4 * LRU_W),
        "lru_lambda": p["lru_lambda"][l],
        "mla_qnorm_g": p["mla_qnorm_g"][l].reshape(1, Q_LORA),
        "wq": wq_pad.astype(BF16),
        "wq_partner": wq_partner.astype(BF16),
        "mla_kvnorm_g": p["mla_kvnorm_g"][l].reshape(1, KV_LORA),
        "wk": wk_pad.astype(BF16),
        "wv": wv.astype(BF16),
        "qg": pad_gain(qg) * scale,
        "qg_partner": partner_gain(qg) * scale,
        "kg": pad_gain(kg),
        "kg_partner": partner_gain(kg),
        "gnorm_conv": p["gnorm_conv"][l].reshape(1, CONV_W),
        "gnorm_lru": p["gnorm_lru"][l].reshape(1, LRU_W),
        "gnorm_mla": p["gnorm_mla"][l].reshape(1, MLA_HEADS * V_DIM),
        "w_out": p["w_out"][l].astype(BF16),
        "norm2_g": p["norm2_g"][l].reshape(1, D_MODEL),
        "w_gate": p["w_gate"][l].astype(BF16),
        "w_up": p["w_up"][l].astype(BF16),
        "w_down": p["w_down"][l].astype(BF16),
    }


def _rope_tables(n_tokens):
    n_rows = n_tokens // GRID_W
    row = jnp.repeat(jnp.arange(n_rows, dtype=F32), GRID_W)
    col = jnp.tile(jnp.arange(GRID_W, dtype=F32), n_rows)
    n_freq = QK_ROPE // 4
    inv_freq = jnp.power(ROPE_THETA, -jnp.arange(n_freq, dtype=F32) / n_freq)
    ang_r = row[:, None] * inv_freq
    ang_c = col[:, None] * inv_freq
    cos = jnp.concatenate([jnp.cos(ang_r), jnp.cos(ang_r), jnp.cos(ang_c), jnp.cos(ang_c)], 1)
    sin = jnp.concatenate([-jnp.sin(ang_r), jnp.sin(ang_r), -jnp.sin(ang_c), jnp.sin(ang_c)], 1)
    ones = jnp.ones((n_tokens, QK_NOPE), F32)
    tail = jnp.zeros((n_tokens, HEAD_PAD - QK_DIM), F32)
    rope_c = jnp.concatenate([ones, cos, tail], axis=1)
    rope_s = jnp.concatenate([jnp.zeros((n_tokens, QK_NOPE), F32), sin, tail], axis=1)
    return rope_c, rope_s


def kernel(x_prompt, x_sample, cache_ckv, cache_krope, state_lru, c, c_ctx, norm1_g, ada_w, ada_b, w_in, conv_w, conv_b, lru_conv_w, lru_conv_b, lru_wa, lru_ba, lru_wi, lru_bi, lru_lambda, mla_qnorm_g, mla_wq_up, mla_kvnorm_g, mla_wkv_up, q_norm_g, k_norm_g, gnorm_conv, gnorm_lru, gnorm_mla, w_out, norm2_g, w_gate, w_up, w_down):
    params = dict(norm1_g=norm1_g, w_in=w_in, conv_w=conv_w, conv_b=conv_b,
                  lru_conv_w=lru_conv_w, lru_conv_b=lru_conv_b, lru_wa=lru_wa,
                  lru_ba=lru_ba, lru_wi=lru_wi, lru_bi=lru_bi, lru_lambda=lru_lambda,
                  mla_qnorm_g=mla_qnorm_g, mla_wq_up=mla_wq_up,
                  mla_kvnorm_g=mla_kvnorm_g, mla_wkv_up=mla_wkv_up, q_norm_g=q_norm_g,
                  k_norm_g=k_norm_g, gnorm_conv=gnorm_conv, gnorm_lru=gnorm_lru,
                  gnorm_mla=gnorm_mla, w_out=w_out, norm2_g=norm2_g, w_gate=w_gate,
                  w_up=w_up, w_down=w_down)
    batch, seq_p, _ = x_prompt.shape
    dec_batch, seq_s, _ = x_sample.shape
    assert dec_batch == 4 and cache_ckv.shape[2] == 256

    cond8 = jnp.concatenate(
        [c, c_ctx[None, :], jnp.zeros((8 - dec_batch - 1, D_MODEL), F32)], axis=0)
    mods = _ada_call(cond8, ada_w, ada_b).reshape(DEPTH, 8, 1, 6 * D_MODEL)

    rope_c, rope_s = _rope_tables(seq_s)
    cache_kr_pad = jnp.pad(
        cache_krope, ((0, 0), (0, 0), (0, 0), (ROPE_LO, HEAD_PAD - ROPE_LO - QK_ROPE)))
    extra = dict(rope_c=rope_c, rope_s=rope_s, cache_ckv=cache_ckv,
                 cache_kr=cache_kr_pad, state=state_lru)

    xp = x_prompt.reshape(batch * seq_p, D_MODEL)
    xs = x_sample.reshape(dec_batch * seq_s, D_MODEL)
    ckv_l, kr_l, st_l = [], [], []
    for l in range(DEPTH):
        lw = _prep_layer(l, params)
        hp, ckv, kr, st = _mix_call(xp, mods, l, lw, seq=seq_p, n_seq=2, latent=False)
        xp = _ffn_call(hp, mods, l, lw, rows_per_mod=None, mod_base=4)
        ckv_l.append(ckv.reshape(batch, seq_p, KV_LORA))
        kr_l.append(kr.reshape(batch, seq_p, QK_ROPE))
        st_l.append(st)
        (hs,) = _mix_call(xs, mods, l, lw, seq=seq_s, n_seq=1, latent=True, extra=extra)
        xs = _ffn_call(hs, mods, l, lw, rows_per_mod=seq_s, mod_base=None)
    return (xp.reshape(batch, seq_p, D_MODEL),
            xs.reshape(dec_batch, seq_s, D_MODEL),
            jnp.stack(ckv_l, axis=1),
            jnp.stack(kr_l, axis=1),
            jnp.stack(st_l, axis=1))
```

```python
import functools
import math

import jax
import jax.numpy as jnp
from jax import lax
from jax.experimental import pallas as pl
from jax.experimental.pallas import tpu as pltpu

F32 = jnp.float32
BF16 = jnp.bfloat16

D_MODEL = 1024
DEPTH = 2
GRID_W = 64
CONV_W = 256
LRU_W = 256
LRU_BLOCKS = 4
LRU_BLK = 64
LRU_C = 8.0
MLA_HEADS = 8
QK_NOPE = 64
QK_ROPE = 32
V_DIM = 64
QK_DIM = QK_NOPE + QK_ROPE
Q_LORA = 256
KV_LORA = 128
ROPE_THETA = 10000.0
FF = 2816
EPS = 1e-6

LANES = 128
SUBLANES = 8
HEAD_PAD = LANES
VMEM_LIMIT_BYTES = 60 * 1024 * 1024

C_BG, C_CG, C_H = 0, 256, 512
C_XB, C_YB = 768, 1024
C_Q = 1280
C_CKV = 1536
C_KR = 1664
U_W = 1792
ROPE_LO = QK_NOPE

ROW_CHUNK = 256
SCAN_CHUNK = 32
Q_CHUNK = 256
FF_CHUNK = 256
FFN_ROWS = 512
PAD_ROWS = SUBLANES


def _rms_scale(x, n):
    ms = jnp.sum(x * x, axis=-1, keepdims=True) * (1.0 / n)
    return lax.rsqrt(ms + EPS)


def _sigmoid(x):
    return 1.0 / (1.0 + jnp.exp(-x))


def _gelu_tanh(x):
    c = math.sqrt(2.0 / math.pi)
    return 0.5 * x * (1.0 + jnp.tanh(c * (x + 0.044715 * (x * x * x))))


def _dot(a, b):
    return jnp.dot(a, b, preferred_element_type=F32)


def _dot_nt(a, b):
    return lax.dot_general(a, b, (((1,), (1,)), ((), ())), preferred_element_type=F32)


ADA_TN = 1536


def _ada_kernel(cond_ref, w_ref, b_ref, o_ref):
    cnd = cond_ref[...]
    s = (cnd * _sigmoid(cnd)).astype(BF16)
    o_ref[...] = _dot(s, w_ref[...].astype(BF16)) + b_ref[...]


def _ada_call(cond8, ada_w, ada_b):
    n_out = 6 * D_MODEL
    return pl.pallas_call(
        _ada_kernel,
        grid=(DEPTH, n_out // ADA_TN),
        in_specs=[
            pl.BlockSpec((8, D_MODEL), lambda l, j: (0, 0)),
            pl.BlockSpec((None, D_MODEL, ADA_TN), lambda l, j: (l, 0, j)),
            pl.BlockSpec((None, 1, ADA_TN), lambda l, j: (l, 0, j)),
        ],
        out_specs=pl.BlockSpec((None, 8, ADA_TN), lambda l, j: (l, 0, j)),
        out_shape=jax.ShapeDtypeStruct((DEPTH, 8, n_out), F32),
        compiler_params=pltpu.CompilerParams(
            dimension_semantics=("arbitrary", "arbitrary"),
            vmem_limit_bytes=VMEM_LIMIT_BYTES),
        name="ada_mod",
    )(cond8, ada_w, ada_b.reshape(DEPTH, 1, n_out))


def _mix_kernel(*refs, seq, n_seq, latent):
    T = seq * n_seq
    n_ctx = 256 if latent else 0
    (x_ref, mod_ref, n1g_ref, win_ref, convw_ref, convb_ref, lcw_ref, lcb_ref,
     wg_ref, bg_ref, lam_ref, qng_ref, wqt_ref, kvng_ref, wk_ref, wvt_ref,
     qgt_ref, kg_ref, gnc_ref, gnl_ref, gnm_ref, wout_ref) = refs[:22]
    pos = 22
    if latent:
        (wqpt_ref, qgpt_ref, kgp_ref, ropect_ref, ropest_ref, ropec_ref, ropes_ref,
         cckv_ref, ckr_ref, h0_ref) = refs[pos:pos + 10]
        pos += 10
    h_out_ref = refs[pos]
    pos += 1
    if not latent:
        ckv_out_ref, kr_out_ref, st_out_ref = refs[pos:pos + 3]
        pos += 3
    u_scr, ab_scr, qt_scr, k_scr, vt_scr, ymt_scr, ycat_scr = refs[pos:pos + 7]
    if latent:
        st_scr = refs[pos + 7]

    n_chunks = T // ROW_CHUNK
    lane = lax.broadcasted_iota(jnp.int32, (1, LANES), 1)
    rope_lanes = (lane >= ROPE_LO) & (lane < ROPE_LO + QK_ROPE)

    sh1 = mod_ref[:, 0:D_MODEL]
    sc1 = mod_ref[:, D_MODEL:2 * D_MODEL]
    g1 = mod_ref[:, 2 * D_MODEL:3 * D_MODEL]

    u_scr[0:PAD_ROWS, :] = jnp.zeros((PAD_ROWS, U_W), F32)
    u_scr[PAD_ROWS + T:2 * PAD_ROWS + T, :] = jnp.zeros((PAD_ROWS, U_W), F32)

    def phase_a(c, carry):
        r0 = pl.multiple_of(c * ROW_CHUNK, ROW_CHUNK)
        x = x_ref[pl.ds(r0, ROW_CHUNK), :]
        hn = x * _rms_scale(x, D_MODEL) * n1g_ref[...]
        hn = hn * (1.0 + sc1) + sh1
        u = _dot(hn.astype(BF16), win_ref[...])
        urows = pl.ds(pl.multiple_of(PAD_ROWS + r0, SUBLANES), ROW_CHUNK)
        u_scr[urows, :] = u
        u_scr[urows, C_CG:C_CG + CONV_W] = (
            u[:, C_CG:C_CG + CONV_W] * u[:, C_H:C_H + CONV_W])
        return carry

    lax.fori_loop(0, n_chunks, phase_a, 0)

    def phase_b(c, carry):
        r0 = pl.multiple_of(c * ROW_CHUNK, ROW_CHUNK)
        rows = pl.ds(pl.multiple_of(PAD_ROWS + r0, SUBLANES), ROW_CHUNK)
        uq = u_scr[rows, C_Q:C_Q + Q_LORA]
        qn = (uq * _rms_scale(uq, Q_LORA) * qng_ref[...]).astype(BF16)
        qa_t = _dot_nt(wqt_ref[...], qn)
        uc = u_scr[rows, C_CKV:C_CKV + KV_LORA]
        ckv = uc * _rms_scale(uc, KV_LORA) * kvng_ref[...]
        krb = u_scr[rows, C_KR:C_KR + LANES]
        kr_rolled = pltpu.roll(krb, 64, 1)
        if not latent:
            ckv_out_ref[pl.ds(r0, ROW_CHUNK), :] = ckv
            kr_out_ref[pl.ds(r0, ROW_CHUNK), :] = kr_rolled[:, 0:QK_ROPE]
        krm = jnp.where(rope_lanes, krb, 0.0)
        cb = ckv.astype(BF16)
        ka = _dot(cb, wk_ref[...])
        vt_scr[c] = _dot_nt(wvt_ref[...], cb).astype(BF16)
        if latent:
            qpa_t = _dot_nt(wqpt_ref[...], qn)
            q_tc = ropect_ref[c] * qgt_ref[...]
            q_ts = ropest_ref[c] * qgpt_ref[...]
            rc = ropec_ref[pl.ds(r0, ROW_CHUNK), :]
            rs = ropes_ref[pl.ds(r0, ROW_CHUNK), :]
            k_tc = rc * kg_ref[...]
            k_ts = rs * kgp_ref[...]
        kr0 = r0
        for h in range(MLA_HEADS):
            sl = slice(h * HEAD_PAD, (h + 1) * HEAD_PAD)
            qh = qa_t[sl, :]
            rq = lax.rsqrt(jnp.sum(qh * qh, axis=0, keepdims=True) * (1.0 / QK_DIM) + EPS)
            if latent:
                qh = (qh * q_tc + qpa_t[sl, :] * q_ts) * rq
            else:
                qh = qh * qgt_ref[...] * rq
            qt_scr[h, c] = qh.astype(BF16)
            kpre = ka[:, sl] + krm
            rk = _rms_scale(kpre, QK_DIM)
            if latent:
                kh = (kpre * k_tc + kr_rolled * k_ts) * rk
            else:
                kh = kpre * kg_ref[...] * rk
            k_scr[h, pl.ds(kr0, ROW_CHUNK), :] = kh.astype(BF16)
        return carry

    lax.fori_loop(0, n_chunks, phase_b, 0)

    if latent:
        cc = cckv_ref[...].astype(BF16)
        ka = _dot(cc, wk_ref[...])
        vt_scr[n_chunks] = _dot_nt(wvt_ref[...], cc).astype(BF16)
        krc = ckr_ref[...]
        for h in range(MLA_HEADS):
            kpre = ka[:, h * HEAD_PAD:(h + 1) * HEAD_PAD] + krc
            kh = kpre * kg_ref[...] * _rms_scale(kpre, QK_DIM)
            k_scr[h, seq:seq + n_ctx, :] = kh.astype(BF16)

    neg_lam = -lam_ref[...]
    sp = jnp.maximum(neg_lam, 0.0) + jnp.log(1.0 + jnp.exp(-jnp.abs(neg_lam)))
    row_rc = lax.broadcasted_iota(jnp.int32, (ROW_CHUNK, 1), 0)
    for ci in range(n_chunks):
        r0 = ci * ROW_CHUNK
        base = PAD_ROWS + r0
        first = (r0 % seq) == 0
        last = ((r0 + ROW_CHUNK) % seq) == 0

        def win(col, shift, width=CONV_W):
            w = u_scr[base + shift:base + shift + ROW_CHUNK, col:col + width]
            if shift < 0 and first:
                w = jnp.where(row_rc < -shift, 0.0, w)
            if shift > 0 and last:
                w = jnp.where(row_rc >= ROW_CHUNK - shift, 0.0, w)
            return w

        z_m, z_0, z_p = win(C_CG, -1), win(C_CG, 0), win(C_CG, 1)
        conv = (z_m * convw_ref[0:1, :] + z_0 * convw_ref[1:2, :]
                + z_p * convw_ref[2:3, :] + convb_ref[...])
        yc = u_scr[base:base + ROW_CHUNK, C_BG:C_BG + CONV_W] * conv
        ycn = yc * _rms_scale(yc, CONV_W) * gnc_ref[...]
        ycat_scr[r0:r0 + ROW_CHUNK, 0:CONV_W] = ycn.astype(BF16)

        xc = (win(C_XB, -2) * lcw_ref[0:1, :] + win(C_XB, -1) * lcw_ref[1:2, :]
              + win(C_XB, 0) * lcw_ref[2:3, :] + win(C_XB, 1) * lcw_ref[3:4, :]
              + lcb_ref[...])
        gates = _dot(xc.astype(BF16), wg_ref[...]) + bg_ref[...]
        for d in range(2):
            rgate = _sigmoid(gates[:, (2 * d) * LRU_W:(2 * d + 1) * LRU_W])
            igate = _sigmoid(gates[:, (2 * d + 1) * LRU_W:(2 * d + 2) * LRU_W])
            log_a = (-LRU_C) * rgate * sp[d:d + 1, :]
            a = jnp.exp(log_a)
            mult = jnp.sqrt(1.0 - a * a)
            ab_scr[r0:r0 + ROW_CHUNK, (2 * d) * LRU_W:(2 * d + 1) * LRU_W] = a
            ab_scr[r0:r0 + ROW_CHUNK, (2 * d + 1) * LRU_W:(2 * d + 2) * LRU_W] = (
                mult * (igate * xc))

    n_sc = seq // SCAN_CHUNK
    row_sc = lax.broadcasted_iota(jnp.int32, (SCAN_CHUNK, LRU_W), 0)
    steps = [1 << i for i in range(int(math.log2(SCAN_CHUNK)))]
    for s in range(n_seq):
        def scan_body(c, carry, s=s):
            cf, cb_ = carry
            rf = pl.multiple_of(s * seq + c * SCAN_CHUNK, SCAN_CHUNK)
            rb = pl.multiple_of(s * seq + (n_sc - 1 - c) * SCAN_CHUNK, SCAN_CHUNK)
            a = ab_scr[pl.ds(rf, SCAN_CHUNK), 0:LRU_W]
            b = ab_scr[pl.ds(rf, SCAN_CHUNK), LRU_W:2 * LRU_W]
            b = b + jnp.where(row_sc == 0, a * cf, 0.0)
            for d in steps:
                b = b + a * jnp.where(row_sc >= d, pltpu.roll(b, d, 0), 0.0)
                if d != steps[-1]:
                    a = a * jnp.where(row_sc >= d, pltpu.roll(a, d, 0), 1.0)
            ab_scr[pl.ds(rf, SCAN_CHUNK), LRU_W:2 * LRU_W] = b
            cf = b[SCAN_CHUNK - 1:SCAN_CHUNK, :]
            a = ab_scr[pl.ds(rb, SCAN_CHUNK), 2 * LRU_W:3 * LRU_W]
            b = ab_scr[pl.ds(rb, SCAN_CHUNK), 3 * LRU_W:4 * LRU_W]
            b = b + jnp.where(row_sc == SCAN_CHUNK - 1, a * cb_, 0.0)
            for d in steps:
                keep = row_sc < SCAN_CHUNK - d
                b = b + a * jnp.where(keep, pltpu.roll(b, SCAN_CHUNK - d, 0), 0.0)
                if d != steps[-1]:
                    a = a * jnp.where(keep, pltpu.roll(a, SCAN_CHUNK - d, 0), 1.0)
            ab_scr[pl.ds(rb, SCAN_CHUNK), 3 * LRU_W:4 * LRU_W] = b
            cb_ = b[0:1, :]
            return cf, cb_

        if latent:
            init = (h0_ref[0:1, :], h0_ref[1:2, :])
        else:
            init = (jnp.zeros((1, LRU_W), F32), jnp.zeros((1, LRU_W), F32))
        cf, cb_ = lax.fori_loop(0, n_sc, scan_body, init)
        if not latent:
            st_out_ref[s, 0:1, :] = cf
            st_out_ref[s, 1:2, :] = cb_

    cps = seq // ROW_CHUNK

    def softmax_pv(st, vts):
        m = jnp.max(st, axis=0, keepdims=True)
        pt = jnp.exp(st - m)
        l = jnp.sum(pt, axis=0, keepdims=True)
        pt = pt.astype(BF16)
        ot = None
        for i, vt in enumerate(vts):
            part = _dot(vt, pt[i * ROW_CHUNK:(i + 1) * ROW_CHUNK, :])
            ot = part if ot is None else ot + part
        return ot * (1.0 / l)

    if latent:
        key_chunks = list(range(cps)) + [n_chunks]
        n_blk = cps * (MLA_HEADS // 2)

        def heads_of(idx):
            qc = idx // (MLA_HEADS // 2)
            j = idx % (MLA_HEADS // 2)
            return qc, [2 * j, 2 * j + 1]

        def scores(idx, slot):
            qc, hs = heads_of(idx)
            for par, h in enumerate(hs):
                st_scr[slot, par] = _dot(k_scr[h], qt_scr[h, qc])

        def finish(idx, slot):
            qc, hs = heads_of(idx)
            for par, h in enumerate(hs):
                hr = pl.ds(pl.multiple_of(h * V_DIM, V_DIM), V_DIM)
                vts = [vt_scr[kc, hr, :] for kc in key_chunks]
                ymt_scr[qc, hr, :] = softmax_pv(st_scr[slot, par], vts)

        scores(0, 0)

        def att_body(i, carry):
            b0 = 2 * i
            scores(b0 + 1, 1)
            finish(b0, 0)
            scores(jnp.minimum(b0 + 2, n_blk - 1), 0)
            finish(b0 + 1, 1)
            return carry

        lax.fori_loop(0, n_blk // 2, att_body, 0)
    else:
        blocks = [(s, qi, h) for s in range(n_seq) for qi in range(cps)
                  for h in range(MLA_HEADS)]
        group = MLA_HEADS // 2

        def scores(blk):
            s, qi, h = blk
            return _dot(k_scr[h, s * seq:(s + 1) * seq, :], qt_scr[h, s * cps + qi])

        def finish(blk, st):
            s, qi, h = blk
            hr = slice(h * V_DIM, (h + 1) * V_DIM)
            vts = [vt_scr[s * cps + i, hr, :] for i in range(cps)]
            ymt_scr[s * cps + qi, hr, :] = softmax_pv(st, vts)

        sts = [scores(b) for b in blocks[:group]]
        for g0 in range(0, len(blocks), group):
            nxt = [scores(b) for b in blocks[g0 + group:g0 + 2 * group]]
            for b, st in zip(blocks[g0:g0 + group], sts):
                finish(b, st)
            sts = nxt

    def phase_f(c, carry):
        r0 = pl.multiple_of(c * ROW_CHUNK, ROW_CHUNK)
        rows = pl.ds(r0, ROW_CHUNK)
        hf = ab_scr[rows, LRU_W:2 * LRU_W]
        hb = ab_scr[rows, 3 * LRU_W:4 * LRU_W]
        yb = u_scr[pl.ds(pl.multiple_of(PAD_ROWS + r0, SUBLANES), ROW_CHUNK),
                   C_YB:C_YB + LRU_W]
        yl = (hf + hb) * _gelu_tanh(yb)
        yln = yl * _rms_scale(yl, LRU_W) * gnl_ref[...]
        ycat_scr[rows, CONV_W:CONV_W + LRU_W] = yln.astype(BF16)
        ymt = ymt_scr[c]
        ssq = jnp.sum(ymt * ymt, axis=0, keepdims=True)
        ynt = ymt * lax.rsqrt(ssq * (1.0 / (MLA_HEADS * V_DIM)) + EPS)
        ycat_scr[rows, CONV_W + LRU_W:D_MODEL] = (ynt.T * gnm_ref[...]).astype(BF16)
        y = _dot(ycat_scr[rows, :], wout_ref[...])
        h_out_ref[rows, :] = x_ref[rows, :] + g1 * y
        return carry

    lax.fori_loop(0, n_chunks, phase_f, 0)


def _const_spec(shape):
    nd = len(shape)
    return pl.BlockSpec(shape, lambda i, _n=nd: (0,) * _n,
                        pipeline_mode=pl.Buffered(1))


def _mix_call(x2d, mods, layer, lw, *, seq, n_seq, latent, extra=None):
    n_tok = x2d.shape[0]
    T = seq * n_seq
    grid = (n_tok // T,)
    n_keys_buf = T if not latent else seq + 256

    if latent:
        mod_map = lambda i: (layer, i, 0, 0)
    else:
        mod_map = lambda i: (layer, 4, 0, 0)

    consts = [lw["norm1_g"], lw["w_in"], lw["conv_w"], lw["conv_b"], lw["lru_conv_w"],
              lw["lru_conv_b"], lw["w_gates"], lw["b_gates"], lw["lru_lambda"],
              lw["mla_qnorm_g"], lw["wq_t"], lw["mla_kvnorm_g"], lw["wk"], lw["wv_t"],
              lw["qg_t"], lw["kg"], lw["gnorm_conv"], lw["gnorm_lru"], lw["gnorm_mla"],
              lw["w_out"]]
    args = [x2d, mods] + consts
    in_specs = [pl.BlockSpec((T, D_MODEL), lambda i: (i, 0)),
                pl.BlockSpec((None, None, 1, 6 * D_MODEL), mod_map)]
    in_specs += [_const_spec(a.shape) for a in consts]
    if latent:
        lconsts = [lw["wq_partner_t"], lw["qg_partner_t"], lw["kg_partner"],
                   extra["rope_ct"], extra["rope_st"], extra["rope_c"], extra["rope_s"]]
        args += lconsts
        in_specs += [_const_spec(a.shape) for a in lconsts]
        args += [extra["cache_ckv"], extra["cache_kr"], extra["state"]]
        in_specs += [
            pl.BlockSpec((None, None, 256, KV_LORA), lambda i: (i, layer, 0, 0)),
            pl.BlockSpec((None, None, 256, LANES), lambda i: (i, layer, 0, 0)),
            pl.BlockSpec((None, None, 2, LRU_W), lambda i: (i, layer, 0, 0)),
        ]

    out_shape = [jax.ShapeDtypeStruct((n_tok, D_MODEL), F32)]
    out_specs = [pl.BlockSpec((T, D_MODEL), lambda i: (i, 0))]
    if not latent:
        out_shape += [jax.ShapeDtypeStruct((n_tok, KV_LORA), F32),
                      jax.ShapeDtypeStruct((n_tok, QK_ROPE), F32),
                      jax.ShapeDtypeStruct((n_tok // seq, 2, LRU_W), F32)]
        out_specs += [pl.BlockSpec((T, KV_LORA), lambda i: (i, 0)),
                      pl.BlockSpec((T, QK_ROPE), lambda i: (i, 0)),
                      pl.BlockSpec((n_seq, 2, LRU_W), lambda i: (i, 0, 0))]

    scratch = [
        pltpu.VMEM((T + 2 * PAD_ROWS, U_W), F32),
        pltpu.VMEM((T, 4 * LRU_W), F32),
        pltpu.VMEM((MLA_HEADS, T // ROW_CHUNK, HEAD_PAD, ROW_CHUNK), BF16),
        pltpu.VMEM((MLA_HEADS, n_keys_buf, HEAD_PAD), BF16),
        pltpu.VMEM((n_keys_buf // ROW_CHUNK, MLA_HEADS * V_DIM, ROW_CHUNK), BF16),
        pltpu.VMEM((T // ROW_CHUNK, MLA_HEADS * V_DIM, ROW_CHUNK), F32),
        pltpu.VMEM((T, D_MODEL), BF16),
    ]
    if latent:
        scratch.append(pltpu.VMEM((2, 2, n_keys_buf, ROW_CHUNK), F32))
    return pl.pallas_call(
        functools.partial(_mix_kernel, seq=seq, n_seq=n_seq, latent=latent),
        grid=grid,
        in_specs=in_specs,
        out_specs=out_specs,
        out_shape=out_shape,
        scratch_shapes=scratch,
        compiler_params=pltpu.CompilerParams(
            dimension_semantics=("arbitrary",),
            vmem_limit_bytes=VMEM_LIMIT_BYTES),
        name="mix_latent" if latent else "mix_context",
    )(*args)


def _ffn_kernel(h_ref, mod_ref, n2g_ref, wg_ref, wu_ref, wd_ref, o_ref):
    sh2 = mod_ref[:, 3 * D_MODEL:4 * D_MODEL]
    sc2 = mod_ref[:, 4 * D_MODEL:5 * D_MODEL]
    g2 = mod_ref[:, 5 * D_MODEL:6 * D_MODEL]
    h = h_ref[...]
    hn = h * _rms_scale(h, D_MODEL) * n2g_ref[...]
    hn = (hn * (1.0 + sc2) + sh2).astype(BF16)
    acc = None
    for c0 in range(0, FF, FF_CHUNK):
        g = _dot(hn, wg_ref[:, c0:c0 + FF_CHUNK])
        u = _dot(hn, wu_ref[:, c0:c0 + FF_CHUNK])
        act = (g * _sigmoid(g) * u).astype(BF16)
        part = _dot(act, wd_ref[c0:c0 + FF_CHUNK, :])
        acc = part if acc is None else acc + part
    o_ref[...] = h + g2 * acc


def _ffn_call(h2d, mods, layer, lw, *, rows_per_mod, mod_base):
    n_tok = h2d.shape[0]
    if mod_base is None:
        tiles_per_mod = rows_per_mod // FFN_ROWS
        mod_map = lambda i: (layer, i // tiles_per_mod, 0, 0)
    else:
        mod_map = lambda i: (layer, mod_base, 0, 0)
    consts = [lw["norm2_g"], lw["w_gate"], lw["w_up"], lw["w_down"]]
    return pl.pallas_call(
        _ffn_kernel,
        grid=(n_tok // FFN_ROWS,),
        in_specs=[pl.BlockSpec((FFN_ROWS, D_MODEL), lambda i: (i, 0)),
                  pl.BlockSpec((None, None, 1, 6 * D_MODEL), mod_map)]
        + [_const_spec(a.shape) for a in consts],
        out_specs=pl.BlockSpec((FFN_ROWS, D_MODEL), lambda i: (i, 0)),
        out_shape=jax.ShapeDtypeStruct((n_tok, D_MODEL), F32),
        compiler_params=pltpu.CompilerParams(
            dimension_semantics=("arbitrary",),
            vmem_limit_bytes=VMEM_LIMIT_BYTES),
        name="swiglu",
    )(h2d, mods, *consts)


def _rope_partner_index():
    return [d + 8 if (d % 16) < 8 else d - 8 for d in range(QK_ROPE)]


def _pad_heads(w, width, head_w):
    k = w.shape[0]
    w = w.reshape(k, MLA_HEADS, head_w)[:, :, :width]
    w = jnp.pad(w, ((0, 0), (0, 0), (0, HEAD_PAD - width)))
    return w.reshape(k, MLA_HEADS * HEAD_PAD)


def _block_diag(w):
    out = jnp.zeros((LRU_W, LRU_W), w.dtype)
    for n in range(LRU_BLOCKS):
        out = out.at[n * LRU_BLK:(n + 1) * LRU_BLK, n * LRU_BLK:(n + 1) * LRU_BLK].set(w[n])
    return out


def _prep_layer(l, p):
    partner = jnp.array(_rope_partner_index(), jnp.int32)
    scale = QK_DIM ** -0.5
    w_in = p["w_in"][l]
    w_kr = w_in[:, C_KR:C_KR + QK_ROPE]
    zeros32 = jnp.zeros((D_MODEL, QK_ROPE), F32)
    kr_block = jnp.concatenate([w_kr[:, partner], zeros32, w_kr, zeros32], axis=1)
    w_in_aug = jnp.concatenate([w_in[:, :C_KR], kr_block], axis=1).astype(BF16)

    wq = p["mla_wq_up"][l]
    wq_pad = _pad_heads(wq, QK_DIM, QK_DIM)
    wq_h = wq.reshape(Q_LORA, MLA_HEADS, QK_DIM)
    wq_rope_partner = wq_h[:, :, QK_NOPE:][:, :, partner]
    wq_partner = jnp.pad(wq_rope_partner,
                         ((0, 0), (0, 0), (ROPE_LO, HEAD_PAD - ROPE_LO - QK_ROPE)))
    wq_partner = wq_partner.reshape(Q_LORA, MLA_HEADS * HEAD_PAD)

    wkv = p["mla_wkv_up"][l].reshape(KV_LORA, MLA_HEADS, QK_NOPE + V_DIM)
    wk_pad = jnp.pad(wkv[:, :, :QK_NOPE], ((0, 0), (0, 0), (0, HEAD_PAD - QK_NOPE)))
    wk_pad = wk_pad.reshape(KV_LORA, MLA_HEADS * HEAD_PAD)
    wv = wkv[:, :, QK_NOPE:].reshape(KV_LORA, MLA_HEADS * V_DIM)

    def pad_gain(g):
        return jnp.pad(g, (0, HEAD_PAD - QK_DIM)).reshape(1, HEAD_PAD)

    def partner_gain(g):
        gp = g[QK_NOPE:][partner]
        return jnp.pad(gp, (ROPE_LO, HEAD_PAD - ROPE_LO - QK_ROPE)).reshape(1, HEAD_PAD)

    qg = p["q_norm_g"][l]
    kg = p["k_norm_g"][l]
    w_gates = jnp.concatenate(
        [_block_diag(p["lru_wa"][l, 0]), _block_diag(p["lru_wi"][l, 0]),
         _block_diag(p["lru_wa"][l, 1]), _block_diag(p["lru_wi"][l, 1])], axis=1)
    b_gates = jnp.concatenate(
        [p["lru_ba"][l, 0], p["lru_bi"][l, 0], p["lru_ba"][l, 1], p["lru_bi"][l, 1]])
    return {
        "norm1_g": p["norm1_g"][l].reshape(1, D_MODEL),
        "w_in": w_in_aug,
        "conv_w": p["conv_w"][l],
        "conv_b": p["conv_b"][l].reshape(1, CONV_W),
        "lru_conv_w": p["lru_conv_w"][l],
        "lru_conv_b": p["lru_conv_b"][l].reshape(1, LRU_W),
        "w_gates": w_gates.astype(BF16),
        "b_gates": b_gates.reshape(1, 4 * LRU_W),
        "lru_lambda": p["lru_lambda"][l],
        "mla_qnorm_g": p["mla_qnorm_g"][l].reshape(1, Q_LORA),
        "wq_t": wq_pad.T.astype(BF16),
        "wq_partner_t": wq_partner.T.astype(BF16),
        "mla_kvnorm_g": p["mla_kvnorm_g"][l].reshape(1, KV_LORA),
        "wk": wk_pad.astype(BF16),
        "wv_t": wv.T.astype(BF16),
        "qg_t": jnp.broadcast_to((pad_gain(qg) * scale).T, (HEAD_PAD, ROW_CHUNK)),
        "qg_partner_t": jnp.broadcast_to((partner_gain(qg) * scale).T,
                                         (HEAD_PAD, ROW_CHUNK)),
        "kg": pad_gain(kg),
        "kg_partner": partner_gain(kg),
        "gnorm_conv": p["gnorm_conv"][l].reshape(1, CONV_W),
        "gnorm_lru": p["gnorm_lru"][l].reshape(1, LRU_W),
        "gnorm_mla": p["gnorm_mla"][l].reshape(1, MLA_HEADS * V_DIM),
        "w_out": p["w_out"][l].astype(BF16),
        "norm2_g": p["norm2_g"][l].reshape(1, D_MODEL),
        "w_gate": p["w_gate"][l].astype(BF16),
        "w_up": p["w_up"][l].astype(BF16),
        "w_down": p["w_down"][l].astype(BF16),
    }


def _rope_tables(n_tokens):
    n_rows = n_tokens // GRID_W
    row = jnp.repeat(jnp.arange(n_rows, dtype=F32), GRID_W)
    col = jnp.tile(jnp.arange(GRID_W, dtype=F32), n_rows)
    n_freq = QK_ROPE // 4
    inv_freq = jnp.power(ROPE_THETA, -jnp.arange(n_freq, dtype=F32) / n_freq)
    ang_r = row[:, None] * inv_freq
    ang_c = col[:, None] * inv_freq
    cos = jnp.concatenate([jnp.cos(ang_r), jnp.cos(ang_r), jnp.cos(ang_c), jnp.cos(ang_c)], 1)
    sin = jnp.concatenate([-jnp.sin(ang_r), jnp.sin(ang_r), -jnp.sin(ang_c), jnp.sin(ang_c)], 1)
    ones = jnp.ones((n_tokens, QK_NOPE), F32)
    tail = jnp.zeros((n_tokens, HEAD_PAD - QK_DIM), F32)
    rope_c = jnp.concatenate([ones, cos, tail], axis=1)
    rope_s = jnp.concatenate([jnp.zeros((n_tokens, QK_NOPE), F32), sin, tail], axis=1)
    return rope_c, rope_s


def kernel(x_prompt, x_sample, cache_ckv, cache_krope, state_lru, c, c_ctx, norm1_g, ada_w, ada_b, w_in, conv_w, conv_b, lru_conv_w, lru_conv_b, lru_wa, lru_ba, lru_wi, lru_bi, lru_lambda, mla_qnorm_g, mla_wq_up, mla_kvnorm_g, mla_wkv_up, q_norm_g, k_norm_g, gnorm_conv, gnorm_lru, gnorm_mla, w_out, norm2_g, w_gate, w_up, w_down):
    params = dict(norm1_g=norm1_g, w_in=w_in, conv_w=conv_w, conv_b=conv_b,
                  lru_conv_w=lru_conv_w, lru_conv_b=lru_conv_b, lru_wa=lru_wa,
                  lru_ba=lru_ba, lru_wi=lru_wi, lru_bi=lru_bi, lru_lambda=lru_lambda,
                  mla_qnorm_g=mla_qnorm_g, mla_wq_up=mla_wq_up,
                  mla_kvnorm_g=mla_kvnorm_g, mla_wkv_up=mla_wkv_up, q_norm_g=q_norm_g,
                  k_norm_g=k_norm_g, gnorm_conv=gnorm_conv, gnorm_lru=gnorm_lru,
                  gnorm_mla=gnorm_mla, w_out=w_out, norm2_g=norm2_g, w_gate=w_gate,
                  w_up=w_up, w_down=w_down)
    batch, seq_p, _ = x_prompt.shape
    dec_batch, seq_s, _ = x_sample.shape
    assert dec_batch == 4 and cache_ckv.shape[2] == 256

    cond8 = jnp.concatenate(
        [c, c_ctx[None, :], jnp.zeros((8 - dec_batch - 1, D_MODEL), F32)], axis=0)
    mods = _ada_call(cond8, ada_w, ada_b).reshape(DEPTH, 8, 1, 6 * D_MODEL)

    rope_c, rope_s = _rope_tables(seq_s)
    cache_kr_pad = jnp.pad(
        cache_krope, ((0, 0), (0, 0), (0, 0), (ROPE_LO, HEAD_PAD - ROPE_LO - QK_ROPE)))

    def chunked_t(tab):
        return tab.reshape(seq_s // ROW_CHUNK, ROW_CHUNK, HEAD_PAD).transpose(0, 2, 1)

    extra = dict(rope_c=rope_c, rope_s=rope_s, rope_ct=chunked_t(rope_c),
                 rope_st=chunked_t(rope_s), cache_ckv=cache_ckv,
                 cache_kr=cache_kr_pad, state=state_lru)

    xp = x_prompt.reshape(batch * seq_p, D_MODEL)
    xs = x_sample.reshape(dec_batch * seq_s, D_MODEL)
    ckv_l, kr_l, st_l = [], [], []
    for l in range(DEPTH):
        lw = _prep_layer(l, params)
        hp, ckv, kr, st = _mix_call(xp, mods, l, lw, seq=seq_p, n_seq=2, latent=False)
        xp = _ffn_call(hp, mods, l, lw, rows_per_mod=None, mod_base=4)
        ckv_l.append(ckv.reshape(batch, seq_p, KV_LORA))
        kr_l.append(kr.reshape(batch, seq_p, QK_ROPE))
        st_l.append(st)
        (hs,) = _mix_call(xs, mods, l, lw, seq=seq_s, n_seq=1, latent=True, extra=extra)
        xs = _ffn_call(hs, mods, l, lw, rows_per_mod=seq_s, mod_base=None)
    return (xp.reshape(batch, seq_p, D_MODEL),
            xs.reshape(dec_batch, seq_s, D_MODEL),
            jnp.stack(ckv_l, axis=1),
            jnp.stack(kr_l, axis=1),
            jnp.stack(st_l, axis=1))
```

```python
import functools
import math

import jax
import jax.numpy as jnp
from jax import lax
from jax.experimental import pallas as pl
from jax.experimental.pallas import tpu as pltpu

F32 = jnp.float32
BF16 = jnp.bfloat16

D_MODEL = 1024
DEPTH = 2
GRID_W = 64
CONV_W = 256
LRU_W = 256
LRU_BLOCKS = 4
LRU_BLK = 64
LRU_C = 8.0
MLA_HEADS = 8
QK_NOPE = 64
QK_ROPE = 32
V_DIM = 64
V_EXT = V_DIM + 16
QK_DIM = QK_NOPE + QK_ROPE
Q_LORA = 256
KV_LORA = 128
ROPE_THETA = 10000.0
FF = 2816
EPS = 1e-6

LANES = 128
SUBLANES = 8
HEAD_PAD = LANES
VMEM_LIMIT_BYTES = 60 * 1024 * 1024

C_BG, C_CG, C_H = 0, 256, 512
C_XB, C_YB = 768, 1024
C_Q = 1280
C_CKV = 1536
C_KR = 1664
U_W = 1792
ROPE_LO = QK_NOPE

ROW_CHUNK = 256
SCAN_CHUNK = 32
Q_CHUNK = 256
FF_CHUNK = 256
FFN_ROWS = 1024
PAD_ROWS = SUBLANES


def _rms_scale(x, n):
    ms = jnp.sum(x * x, axis=-1, keepdims=True) * (1.0 / n)
    return lax.rsqrt(ms + EPS)


def _sigmoid(x):
    return 0.5 * jnp.tanh(0.5 * x) + 0.5


def _gelu_tanh(x):
    c = math.sqrt(2.0 / math.pi)
    return 0.5 * x * (1.0 + jnp.tanh(c * (x + 0.044715 * (x * x * x))))


def _dot(a, b):
    return jnp.dot(a, b, preferred_element_type=F32)


def _dot_nt(a, b):
    return lax.dot_general(a, b, (((1,), (1,)), ((), ())), preferred_element_type=F32)


ADA_TN = 1536


def _ada_kernel(cond_ref, w_ref, b_ref, o_ref):
    cnd = cond_ref[...]
    s = (cnd * _sigmoid(cnd)).astype(BF16)
    o_ref[...] = _dot(s, w_ref[...].astype(BF16)) + b_ref[...]


def _ada_call(cond8, ada_w, ada_b):
    n_out = 6 * D_MODEL
    return pl.pallas_call(
        _ada_kernel,
        grid=(DEPTH, n_out // ADA_TN),
        in_specs=[
            pl.BlockSpec((8, D_MODEL), lambda l, j: (0, 0)),
            pl.BlockSpec((None, D_MODEL, ADA_TN), lambda l, j: (l, 0, j)),
            pl.BlockSpec((None, 1, ADA_TN), lambda l, j: (l, 0, j)),
        ],
        out_specs=pl.BlockSpec((None, 8, ADA_TN), lambda l, j: (l, 0, j)),
        out_shape=jax.ShapeDtypeStruct((DEPTH, 8, n_out), F32),
        compiler_params=pltpu.CompilerParams(
            dimension_semantics=("arbitrary", "arbitrary"),
            vmem_limit_bytes=VMEM_LIMIT_BYTES),
        name="ada_mod",
    )(cond8, ada_w, ada_b.reshape(DEPTH, 1, n_out))


def _mix_kernel(*refs, seq, n_seq, latent):
    T = seq * n_seq
    n_ctx = 256 if latent else 0
    (x_ref, mod_ref, n1g_ref, win_ref, convw_ref, convb_ref, lcw_ref, lcb_ref,
     wg_ref, bg_ref, lam_ref, qng_ref, wqt_ref, kvng_ref, wk_ref, wvt_ref,
     qgt_ref, kg_ref, gnc_ref, gnl_ref, gnm_ref, wout_ref) = refs[:22]
    pos = 22
    if latent:
        (wqpt_ref, qgpt_ref, kgp_ref, ropect_ref, ropest_ref, ropec_ref, ropes_ref,
         cckv_ref, ckr_ref, h0_ref) = refs[pos:pos + 10]
        pos += 10
    h_out_ref = refs[pos]
    pos += 1
    if not latent:
        ckv_out_ref, kr_out_ref, st_out_ref = refs[pos:pos + 3]
        pos += 3
    u_scr, ab_scr, qt_scr, k_scr, vt_scr, ymt_scr, ycat_scr = refs[pos:pos + 7]
    if latent:
        st_scr = refs[pos + 7]

    n_chunks = T // ROW_CHUNK
    lane = lax.broadcasted_iota(jnp.int32, (1, LANES), 1)
    rope_lanes = (lane >= ROPE_LO) & (lane < ROPE_LO + QK_ROPE)

    sh1 = mod_ref[:, 0:D_MODEL]
    sc1 = mod_ref[:, D_MODEL:2 * D_MODEL]
    g1 = mod_ref[:, 2 * D_MODEL:3 * D_MODEL]

    u_scr[0:PAD_ROWS, :] = jnp.zeros((PAD_ROWS, U_W), F32)
    u_scr[PAD_ROWS + T:2 * PAD_ROWS + T, :] = jnp.zeros((PAD_ROWS, U_W), F32)

    def phase_a(c):
        r0 = c * ROW_CHUNK
        x = x_ref[r0:r0 + ROW_CHUNK, :]
        hn = x * _rms_scale(x, D_MODEL) * n1g_ref[...]
        hn = hn * (1.0 + sc1) + sh1
        u = _dot(hn.astype(BF16), win_ref[...])
        urows = slice(PAD_ROWS + r0, PAD_ROWS + r0 + ROW_CHUNK)
        u_scr[urows, :] = u
        u_scr[urows, C_CG:C_CG + CONV_W] = (
            u[:, C_CG:C_CG + CONV_W] * u[:, C_H:C_H + CONV_W])

    vrow = lax.broadcasted_iota(jnp.int32, (MLA_HEADS * V_EXT, 1), 0)
    ones_rows = (vrow % V_EXT) >= V_DIM

    def values_t(cb):
        return jnp.where(ones_rows, 1.0, _dot_nt(wvt_ref[...], cb)).astype(BF16)

    def phase_b(c):
        r0 = c * ROW_CHUNK
        rows = slice(PAD_ROWS + r0, PAD_ROWS + r0 + ROW_CHUNK)
        uq = u_scr[rows, C_Q:C_Q + Q_LORA]
        qn = (uq * _rms_scale(uq, Q_LORA) * qng_ref[...]).astype(BF16)
        qa_t = _dot_nt(wqt_ref[...], qn)
        uc = u_scr[rows, C_CKV:C_CKV + KV_LORA]
        ckv = uc * _rms_scale(uc, KV_LORA) * kvng_ref[...]
        krb = u_scr[rows, C_KR:C_KR + LANES]
        kr_rolled = pltpu.roll(krb, 64, 1)
        if not latent:
            ckv_out_ref[r0:r0 + ROW_CHUNK, :] = ckv
            kr_out_ref[r0:r0 + ROW_CHUNK, :] = kr_rolled[:, 0:QK_ROPE]
        krm = jnp.where(rope_lanes, krb, 0.0)
        cb = ckv.astype(BF16)
        ka = _dot(cb, wk_ref[...])
        vt_scr[c] = values_t(cb)
        if latent:
            qpa_t = _dot_nt(wqpt_ref[...], qn)
            q_tc = ropect_ref[c] * qgt_ref[...]
            q_ts = ropest_ref[c] * qgpt_ref[...]
            rc = ropec_ref[r0:r0 + ROW_CHUNK, :]
            rs = ropes_ref[r0:r0 + ROW_CHUNK, :]
            k_tc = rc * kg_ref[...]
            k_ts = rs * kgp_ref[...]
        for h in range(MLA_HEADS):
            sl = slice(h * HEAD_PAD, (h + 1) * HEAD_PAD)
            qh = qa_t[sl, :]
            rq = lax.rsqrt(jnp.sum(qh * qh, axis=0, keepdims=True) * (1.0 / QK_DIM) + EPS)
            if latent:
                qh = (qh * q_tc + qpa_t[sl, :] * q_ts) * rq
            else:
                qh = qh * qgt_ref[...] * rq
            qt_scr[h, c] = qh.astype(BF16)
            kpre = ka[:, sl] + krm
            rk = _rms_scale(kpre, QK_DIM)
            if latent:
                kh = (kpre * k_tc + kr_rolled * k_ts) * rk
            else:
                kh = kpre * kg_ref[...] * rk
            k_scr[h, r0:r0 + ROW_CHUNK, :] = kh.astype(BF16)

    if latent:
        cc = cckv_ref[...].astype(BF16)
        ka = _dot(cc, wk_ref[...])
        vt_scr[n_chunks] = values_t(cc)
        krc = ckr_ref[...]
        for h in range(MLA_HEADS):
            kpre = ka[:, h * HEAD_PAD:(h + 1) * HEAD_PAD] + krc
            kh = kpre * kg_ref[...] * _rms_scale(kpre, QK_DIM)
            k_scr[h, seq:seq + n_ctx, :] = kh.astype(BF16)

    neg_lam = -lam_ref[...]
    sp = jnp.maximum(neg_lam, 0.0) + jnp.log(1.0 + jnp.exp(-jnp.abs(neg_lam)))
    row_rc = lax.broadcasted_iota(jnp.int32, (ROW_CHUNK, 1), 0)

    def phase_c(ci):
        r0 = ci * ROW_CHUNK
        base = PAD_ROWS + r0
        first = (r0 % seq) == 0
        last = ((r0 + ROW_CHUNK) % seq) == 0

        def win(col, shift, width=CONV_W):
            w = u_scr[base + shift:base + shift + ROW_CHUNK, col:col + width]
            if shift < 0 and first:
                w = jnp.where(row_rc < -shift, 0.0, w)
            if shift > 0 and last:
                w = jnp.where(row_rc >= ROW_CHUNK - shift, 0.0, w)
            return w

        z_m, z_0, z_p = win(C_CG, -1), win(C_CG, 0), win(C_CG, 1)
        conv = (z_m * convw_ref[0:1, :] + z_0 * convw_ref[1:2, :]
                + z_p * convw_ref[2:3, :] + convb_ref[...])
        yc = u_scr[base:base + ROW_CHUNK, C_BG:C_BG + CONV_W] * conv
        ycn = yc * _rms_scale(yc, CONV_W) * gnc_ref[...]
        ycat_scr[r0:r0 + ROW_CHUNK, 0:CONV_W] = ycn.astype(BF16)

        xc = (win(C_XB, -2) * lcw_ref[0:1, :] + win(C_XB, -1) * lcw_ref[1:2, :]
              + win(C_XB, 0) * lcw_ref[2:3, :] + win(C_XB, 1) * lcw_ref[3:4, :]
              + lcb_ref[...])
        gates = _dot(xc.astype(BF16), wg_ref[...]) + bg_ref[...]
        for d in range(2):
            rgate = _sigmoid(gates[:, (2 * d) * LRU_W:(2 * d + 1) * LRU_W])
            igate = _sigmoid(gates[:, (2 * d + 1) * LRU_W:(2 * d + 2) * LRU_W])
            log_a = (-LRU_C) * rgate * sp[d:d + 1, :]
            a = jnp.exp(log_a)
            mult = jnp.sqrt(1.0 - a * a)
            ab_scr[r0:r0 + ROW_CHUNK, (2 * d) * LRU_W:(2 * d + 1) * LRU_W] = a
            ab_scr[r0:r0 + ROW_CHUNK, (2 * d + 1) * LRU_W:(2 * d + 2) * LRU_W] = (
                mult * (igate * xc))

    phase_a(0)
    for c in range(n_chunks):
        if c + 1 < n_chunks:
            phase_a(c + 1)
        phase_b(c)
        if c:
            phase_c(c - 1)
    phase_c(n_chunks - 1)

    n_sc = seq // SCAN_CHUNK
    row_sc = lax.broadcasted_iota(jnp.int32, (SCAN_CHUNK, LRU_W), 0)
    steps = [1 << i for i in range(int(math.log2(SCAN_CHUNK)))]
    for s in range(n_seq):
        def scan_body(c, carry, s=s):
            cf, cb_ = carry
            rf = pl.multiple_of(s * seq + c * SCAN_CHUNK, SCAN_CHUNK)
            rb = pl.multiple_of(s * seq + (n_sc - 1 - c) * SCAN_CHUNK, SCAN_CHUNK)
            a = ab_scr[pl.ds(rf, SCAN_CHUNK), 0:LRU_W]
            b = ab_scr[pl.ds(rf, SCAN_CHUNK), LRU_W:2 * LRU_W]
            b = b + jnp.where(row_sc == 0, a * cf, 0.0)
            for d in steps:
                b = b + a * jnp.where(row_sc >= d, pltpu.roll(b, d, 0), 0.0)
                if d != steps[-1]:
                    a = a * jnp.where(row_sc >= d, pltpu.roll(a, d, 0), 1.0)
            ab_scr[pl.ds(rf, SCAN_CHUNK), LRU_W:2 * LRU_W] = b
            cf = b[SCAN_CHUNK - 1:SCAN_CHUNK, :]
            a = ab_scr[pl.ds(rb, SCAN_CHUNK), 2 * LRU_W:3 * LRU_W]
            b = ab_scr[pl.ds(rb, SCAN_CHUNK), 3 * LRU_W:4 * LRU_W]
            b = b + jnp.where(row_sc == SCAN_CHUNK - 1, a * cb_, 0.0)
            for d in steps:
                keep = row_sc < SCAN_CHUNK - d
                b = b + a * jnp.where(keep, pltpu.roll(b, SCAN_CHUNK - d, 0), 0.0)
                if d != steps[-1]:
                    a = a * jnp.where(keep, pltpu.roll(a, SCAN_CHUNK - d, 0), 1.0)
            ab_scr[pl.ds(rb, SCAN_CHUNK), 3 * LRU_W:4 * LRU_W] = b
            cb_ = b[0:1, :]
            return cf, cb_

        if latent:
            init = (h0_ref[0:1, :], h0_ref[1:2, :])
        else:
            init = (jnp.zeros((1, LRU_W), F32), jnp.zeros((1, LRU_W), F32))
        cf, cb_ = lax.fori_loop(0, n_sc, scan_body, init)
        if not latent:
            st_out_ref[s, 0:1, :] = cf
            st_out_ref[s, 1:2, :] = cb_

    cps = seq // ROW_CHUNK

    def softmax_pv(st, vts):
        m = jnp.max(st, axis=0, keepdims=True)
        pt = jnp.exp2(st - m).astype(BF16)
        ot = None
        for i, vt in enumerate(vts):
            part = _dot(vt, pt[i * ROW_CHUNK:(i + 1) * ROW_CHUNK, :])
            ot = part if ot is None else ot + part
        return ot[0:V_DIM, :] * (1.0 / ot[V_DIM:V_DIM + 1, :])

    if latent:
        key_chunks = list(range(cps)) + [n_chunks]
        n_blk = cps * (MLA_HEADS // 2)

        def heads_of(idx):
            qc = idx // (MLA_HEADS // 2)
            j = idx % (MLA_HEADS // 2)
            return qc, [2 * j, 2 * j + 1]

        def scores(idx, slot):
            qc, hs = heads_of(idx)
            for par, h in enumerate(hs):
                st_scr[slot, par] = _dot(k_scr[h], qt_scr[h, qc])

        def finish(idx, slot):
            qc, hs = heads_of(idx)
            for par, h in enumerate(hs):
                hr = pl.ds(pl.multiple_of(h * V_DIM, V_DIM), V_DIM)
                vr = pl.ds(pl.multiple_of(h * V_EXT, V_EXT - V_DIM), V_EXT)
                vts = [vt_scr[kc, vr, :] for kc in key_chunks]
                ymt_scr[qc, hr, :] = softmax_pv(st_scr[slot, par], vts)

        scores(0, 0)

        def att_body(i, carry):
            b0 = 2 * i
            scores(b0 + 1, 1)
            finish(b0, 0)
            scores(jnp.minimum(b0 + 2, n_blk - 1), 0)
            finish(b0 + 1, 1)
            return carry

        lax.fori_loop(0, n_blk // 2, att_body, 0)
    else:
        blocks = [(s, qi, h) for s in range(n_seq) for qi in range(cps)
                  for h in range(MLA_HEADS)]
        group = MLA_HEADS // 2

        def scores(blk):
            s, qi, h = blk
            return _dot(k_scr[h, s * seq:(s + 1) * seq, :], qt_scr[h, s * cps + qi])

        def finish(blk, st):
            s, qi, h = blk
            hr = slice(h * V_DIM, (h + 1) * V_DIM)
            vr = slice(h * V_EXT, (h + 1) * V_EXT)
            vts = [vt_scr[s * cps + i, vr, :] for i in range(cps)]
            ymt_scr[s * cps + qi, hr, :] = softmax_pv(st, vts)

        sts = [scores(b) for b in blocks[:group]]
        for g0 in range(0, len(blocks), group):
            nxt = [scores(b) for b in blocks[g0 + group:g0 + 2 * group]]
            for b, st in zip(blocks[g0:g0 + group], sts):
                finish(b, st)
            sts = nxt

    def f_build(c):
        r0 = c * ROW_CHUNK
        rows = slice(r0, r0 + ROW_CHUNK)
        hf = ab_scr[rows, LRU_W:2 * LRU_W]
        hb = ab_scr[rows, 3 * LRU_W:4 * LRU_W]
        yb = u_scr[PAD_ROWS + r0:PAD_ROWS + r0 + ROW_CHUNK, C_YB:C_YB + LRU_W]
        yl = (hf + hb) * _gelu_tanh(yb)
        yln = yl * _rms_scale(yl, LRU_W) * gnl_ref[...]
        ycat_scr[rows, CONV_W:CONV_W + LRU_W] = yln.astype(BF16)
        ymt = ymt_scr[c]
        ssq = jnp.sum(ymt * ymt, axis=0, keepdims=True)
        ynt = ymt * lax.rsqrt(ssq * (1.0 / (MLA_HEADS * V_DIM)) + EPS)
        ycat_scr[rows, CONV_W + LRU_W:D_MODEL] = (ynt.T * gnm_ref[...]).astype(BF16)

    def f_project(c):
        rows = slice(c * ROW_CHUNK, (c + 1) * ROW_CHUNK)
        y = _dot(ycat_scr[rows, :], wout_ref[...])
        h_out_ref[rows, :] = x_ref[rows, :] + g1 * y

    f_build(0)
    for c in range(n_chunks):
        if c + 1 < n_chunks:
            f_build(c + 1)
        f_project(c)


def _const_spec(shape):
    nd = len(shape)
    return pl.BlockSpec(shape, lambda i, _n=nd: (0,) * _n,
                        pipeline_mode=pl.Buffered(1))


def _layer_spec(arr, layer):
    nd = arr.ndim - 1
    return pl.BlockSpec((None,) + arr.shape[1:], lambda i, _n=nd: (layer,) + (0,) * _n,
                        pipeline_mode=pl.Buffered(1))


def _mix_call(x2d, mods, layer, lw, *, seq, n_seq, latent, extra=None):
    n_tok = x2d.shape[0]
    T = seq * n_seq
    grid = (n_tok // T,)
    n_keys_buf = T if not latent else seq + 256

    if latent:
        mod_map = lambda i: (layer, i, 0, 0)
    else:
        mod_map = lambda i: (layer, 4, 0, 0)

    consts = [lw["norm1_g"], lw["w_in"], lw["conv_w"], lw["conv_b"], lw["lru_conv_w"],
              lw["lru_conv_b"], lw["w_gates"], lw["b_gates"], lw["lru_lambda"],
              lw["mla_qnorm_g"], lw["wq_t"], lw["mla_kvnorm_g"], lw["wk"], lw["wv_t"],
              lw["qg_t"], lw["kg"], lw["gnorm_conv"], lw["gnorm_lru"], lw["gnorm_mla"],
              lw["w_out"]]
    args = [x2d, mods] + consts
    in_specs = [pl.BlockSpec((T, D_MODEL), lambda i: (i, 0)),
                pl.BlockSpec((None, None, 1, 6 * D_MODEL), mod_map)]
    in_specs += [_layer_spec(a, layer) for a in consts]
    if latent:
        lconsts = [lw["wq_partner_t"], lw["qg_partner_t"], lw["kg_partner"]]
        tables = [extra["rope_ct"], extra["rope_st"], extra["rope_c"], extra["rope_s"]]
        args += lconsts + tables
        in_specs += [_layer_spec(a, layer) for a in lconsts]
        in_specs += [_const_spec(a.shape) for a in tables]
        args += [extra["cache_ckv"], extra["cache_kr"], extra["state"]]
        in_specs += [
            pl.BlockSpec((None, None, 256, KV_LORA), lambda i: (i, layer, 0, 0)),
            pl.BlockSpec((None, None, 256, LANES), lambda i: (i, layer, 0, 0)),
            pl.BlockSpec((None, None, 2, LRU_W), lambda i: (i, layer, 0, 0)),
        ]

    out_shape = [jax.ShapeDtypeStruct((n_tok, D_MODEL), F32)]
    out_specs = [pl.BlockSpec((T, D_MODEL), lambda i: (i, 0))]
    if not latent:
        out_shape += [jax.ShapeDtypeStruct((n_tok, KV_LORA), F32),
                      jax.ShapeDtypeStruct((n_tok, QK_ROPE), F32),
                      jax.ShapeDtypeStruct((n_tok // seq, 2, LRU_W), F32)]
        out_specs += [pl.BlockSpec((T, KV_LORA), lambda i: (i, 0)),
                      pl.BlockSpec((T, QK_ROPE), lambda i: (i, 0)),
                      pl.BlockSpec((n_seq, 2, LRU_W), lambda i: (i, 0, 0))]

    scratch = [
        pltpu.VMEM((T + 2 * PAD_ROWS, U_W), F32),
        pltpu.VMEM((T, 4 * LRU_W), F32),
        pltpu.VMEM((MLA_HEADS, T // ROW_CHUNK, HEAD_PAD, ROW_CHUNK), BF16),
        pltpu.VMEM((MLA_HEADS, n_keys_buf, HEAD_PAD), BF16),
        pltpu.VMEM((n_keys_buf // ROW_CHUNK, MLA_HEADS * V_EXT, ROW_CHUNK), BF16),
        pltpu.VMEM((T // ROW_CHUNK, MLA_HEADS * V_DIM, ROW_CHUNK), F32),
        pltpu.VMEM((T, D_MODEL), BF16),
    ]
    if latent:
        scratch.append(pltpu.VMEM((2, 2, n_keys_buf, ROW_CHUNK), F32))
    return pl.pallas_call(
        functools.partial(_mix_kernel, seq=seq, n_seq=n_seq, latent=latent),
        grid=grid,
        in_specs=in_specs,
        out_specs=out_specs,
        out_shape=out_shape,
        scratch_shapes=scratch,
        compiler_params=pltpu.CompilerParams(
            dimension_semantics=("arbitrary",),
            vmem_limit_bytes=VMEM_LIMIT_BYTES),
        name="mix_latent" if latent else "mix_context",
    )(*args)


def _ffn_kernel(h_ref, mod_ref, n2g_ref, wg_ref, wu_ref, wd_ref, o_ref):
    sh2 = mod_ref[:, 3 * D_MODEL:4 * D_MODEL]
    sc2 = mod_ref[:, 4 * D_MODEL:5 * D_MODEL]
    g2 = mod_ref[:, 5 * D_MODEL:6 * D_MODEL]
    h = h_ref[...]
    hn = h * _rms_scale(h, D_MODEL) * n2g_ref[...]
    hn = (hn * (1.0 + sc2) + sh2).astype(BF16)
    acc = None
    for c0 in range(0, FF, FF_CHUNK):
        g = _dot(hn, wg_ref[:, c0:c0 + FF_CHUNK])
        u = _dot(hn, wu_ref[:, c0:c0 + FF_CHUNK])
        act = (g * _sigmoid(g) * u).astype(BF16)
        part = _dot(act, wd_ref[c0:c0 + FF_CHUNK, :])
        acc = part if acc is None else acc + part
    o_ref[...] = h + g2 * acc


def _ffn_call(h2d, mods, layer, lw, *, rows_per_mod, mod_base):
    n_tok = h2d.shape[0]
    if mod_base is None:
        tiles_per_mod = rows_per_mod // FFN_ROWS
        mod_map = lambda i: (layer, i // tiles_per_mod, 0, 0)
    else:
        mod_map = lambda i: (layer, mod_base, 0, 0)
    consts = [lw["norm2_g"], lw["w_gate"], lw["w_up"], lw["w_down"]]
    return pl.pallas_call(
        _ffn_kernel,
        grid=(n_tok // FFN_ROWS,),
        in_specs=[pl.BlockSpec((FFN_ROWS, D_MODEL), lambda i: (i, 0)),
                  pl.BlockSpec((None, None, 1, 6 * D_MODEL), mod_map)]
        + [_layer_spec(a, layer) for a in consts],
        out_specs=pl.BlockSpec((FFN_ROWS, D_MODEL), lambda i: (i, 0)),
        out_shape=jax.ShapeDtypeStruct((n_tok, D_MODEL), F32),
        compiler_params=pltpu.CompilerParams(
            dimension_semantics=("arbitrary",),
            vmem_limit_bytes=VMEM_LIMIT_BYTES),
        name="swiglu",
    )(h2d, mods, *consts)


def _rope_partner_index():
    return [d + 8 if (d % 16) < 8 else d - 8 for d in range(QK_ROPE)]


def _prep_weights(p):
    partner = jnp.array(_rope_partner_index(), jnp.int32)
    q_scale = QK_DIM ** -0.5 * math.log2(math.e)
    w_in = p["w_in"]
    w_kr = w_in[:, :, C_KR:C_KR + QK_ROPE]
    zeros32 = jnp.zeros((DEPTH, D_MODEL, QK_ROPE), F32)
    w_in_aug = jnp.concatenate(
        [w_in[:, :, :C_KR], w_kr[:, :, partner], zeros32, w_kr, zeros32], axis=2)

    wq_h = p["mla_wq_up"].reshape(DEPTH, Q_LORA, MLA_HEADS, QK_DIM)
    wq_pad = jnp.pad(wq_h, ((0, 0), (0, 0), (0, 0), (0, HEAD_PAD - QK_DIM)))
    wq_t = wq_pad.reshape(DEPTH, Q_LORA, MLA_HEADS * HEAD_PAD).transpose(0, 2, 1)
    rope_pad = ((0, 0), (0, 0), (0, 0), (ROPE_LO, HEAD_PAD - ROPE_LO - QK_ROPE))
    wq_partner = jnp.pad(wq_h[:, :, :, QK_NOPE:][:, :, :, partner], rope_pad)
    wq_partner_t = wq_partner.reshape(DEPTH, Q_LORA, MLA_HEADS * HEAD_PAD).transpose(0, 2, 1)

    wkv = p["mla_wkv_up"].reshape(DEPTH, KV_LORA, MLA_HEADS, QK_NOPE + V_DIM)
    wk_pad = jnp.pad(wkv[:, :, :, :QK_NOPE], ((0, 0), (0, 0), (0, 0), (0, HEAD_PAD - QK_NOPE)))
    wk_pad = wk_pad.reshape(DEPTH, KV_LORA, MLA_HEADS * HEAD_PAD)
    wv_ext = jnp.pad(wkv[:, :, :, QK_NOPE:], ((0, 0), (0, 0), (0, 0), (0, V_EXT - V_DIM)))
    wv_t = wv_ext.reshape(DEPTH, KV_LORA, MLA_HEADS * V_EXT).transpose(0, 2, 1)

    def pad_gain(g):
        return jnp.pad(g, ((0, 0), (0, HEAD_PAD - QK_DIM)))[:, None, :]

    def partner_gain(g):
        gp = g[:, QK_NOPE:][:, partner]
        return jnp.pad(gp, ((0, 0), (ROPE_LO, HEAD_PAD - ROPE_LO - QK_ROPE)))[:, None, :]

    def on_sublanes(g):
        return jnp.broadcast_to(g.transpose(0, 2, 1), (DEPTH, HEAD_PAD, ROW_CHUNK))

    qg, kg = p["q_norm_g"], p["k_norm_g"]
    eye = jnp.eye(LRU_BLOCKS, dtype=F32)
    gate_w = jnp.stack([p["lru_wa"][:, 0], p["lru_wi"][:, 0],
                        p["lru_wa"][:, 1], p["lru_wi"][:, 1]], axis=1)
    w_gates = (gate_w[:, :, :, :, None, :] * eye[None, None, :, None, :, None])
    w_gates = w_gates.transpose(0, 2, 3, 1, 4, 5).reshape(DEPTH, LRU_W, 4 * LRU_W)
    b_gates = jnp.stack([p["lru_ba"][:, 0], p["lru_bi"][:, 0],
                         p["lru_ba"][:, 1], p["lru_bi"][:, 1]], axis=1)

    def row(v):
        return v[:, None, :]

    return {
        "norm1_g": row(p["norm1_g"]),
        "w_in": w_in_aug.astype(BF16),
        "conv_w": p["conv_w"],
        "conv_b": row(p["conv_b"]),
        "lru_conv_w": p["lru_conv_w"],
        "lru_conv_b": row(p["lru_conv_b"]),
        "w_gates": w_gates.astype(BF16),
        "b_gates": b_gates.reshape(DEPTH, 1, 4 * LRU_W),
        "lru_lambda": p["lru_lambda"],
        "mla_qnorm_g": row(p["mla_qnorm_g"]),
        "wq_t": wq_t.astype(BF16),
        "wq_partner_t": wq_partner_t.astype(BF16),
        "mla_kvnorm_g": row(p["mla_kvnorm_g"]),
        "wk": wk_pad.astype(BF16),
        "wv_t": wv_t.astype(BF16),
        "qg_t": on_sublanes(pad_gain(qg) * q_scale),
        "qg_partner_t": on_sublanes(partner_gain(qg) * q_scale),
        "kg": pad_gain(kg),
        "kg_partner": partner_gain(kg),
        "gnorm_conv": row(p["gnorm_conv"]),
        "gnorm_lru": row(p["gnorm_lru"]),
        "gnorm_mla": row(p["gnorm_mla"]),
        "w_out": p["w_out"].astype(BF16),
        "norm2_g": row(p["norm2_g"]),
        "w_gate": p["w_gate"].astype(BF16),
        "w_up": p["w_up"].astype(BF16),
        "w_down": p["w_down"].astype(BF16),
    }


def _rope_tables(n_tokens):
    n_rows = n_tokens // GRID_W
    row = jnp.repeat(jnp.arange(n_rows, dtype=F32), GRID_W)
    col = jnp.tile(jnp.arange(GRID_W, dtype=F32), n_rows)
    n_freq = QK_ROPE // 4
    inv_freq = jnp.power(ROPE_THETA, -jnp.arange(n_freq, dtype=F32) / n_freq)
    ang_r = row[:, None] * inv_freq
    ang_c = col[:, None] * inv_freq
    cos = jnp.concatenate([jnp.cos(ang_r), jnp.cos(ang_r), jnp.cos(ang_c), jnp.cos(ang_c)], 1)
    sin = jnp.concatenate([-jnp.sin(ang_r), jnp.sin(ang_r), -jnp.sin(ang_c), jnp.sin(ang_c)], 1)
    ones = jnp.ones((n_tokens, QK_NOPE), F32)
    tail = jnp.zeros((n_tokens, HEAD_PAD - QK_DIM), F32)
    rope_c = jnp.concatenate([ones, cos, tail], axis=1)
    rope_s = jnp.concatenate([jnp.zeros((n_tokens, QK_NOPE), F32), sin, tail], axis=1)
    return rope_c, rope_s


def kernel(x_prompt, x_sample, cache_ckv, cache_krope, state_lru, c, c_ctx, norm1_g, ada_w, ada_b, w_in, conv_w, conv_b, lru_conv_w, lru_conv_b, lru_wa, lru_ba, lru_wi, lru_bi, lru_lambda, mla_qnorm_g, mla_wq_up, mla_kvnorm_g, mla_wkv_up, q_norm_g, k_norm_g, gnorm_conv, gnorm_lru, gnorm_mla, w_out, norm2_g, w_gate, w_up, w_down):
    params = dict(norm1_g=norm1_g, w_in=w_in, conv_w=conv_w, conv_b=conv_b,
                  lru_conv_w=lru_conv_w, lru_conv_b=lru_conv_b, lru_wa=lru_wa,
                  lru_ba=lru_ba, lru_wi=lru_wi, lru_bi=lru_bi, lru_lambda=lru_lambda,
                  mla_qnorm_g=mla_qnorm_g, mla_wq_up=mla_wq_up,
                  mla_kvnorm_g=mla_kvnorm_g, mla_wkv_up=mla_wkv_up, q_norm_g=q_norm_g,
                  k_norm_g=k_norm_g, gnorm_conv=gnorm_conv, gnorm_lru=gnorm_lru,
                  gnorm_mla=gnorm_mla, w_out=w_out, norm2_g=norm2_g, w_gate=w_gate,
                  w_up=w_up, w_down=w_down)
    batch, seq_p, _ = x_prompt.shape
    dec_batch, seq_s, _ = x_sample.shape
    assert dec_batch == 4 and cache_ckv.shape[2] == 256

    cond8 = jnp.concatenate(
        [c, c_ctx[None, :], jnp.zeros((8 - dec_batch - 1, D_MODEL), F32)], axis=0)
    mods = _ada_call(cond8, ada_w, ada_b).reshape(DEPTH, 8, 1, 6 * D_MODEL)

    rope_c, rope_s = _rope_tables(seq_s)
    cache_kr_pad = jnp.pad(
        cache_krope, ((0, 0), (0, 0), (0, 0), (ROPE_LO, HEAD_PAD - ROPE_LO - QK_ROPE)))

    def chunked_t(tab):
        return tab.reshape(seq_s // ROW_CHUNK, ROW_CHUNK, HEAD_PAD).transpose(0, 2, 1)

    extra = dict(rope_c=rope_c, rope_s=rope_s, rope_ct=chunked_t(rope_c),
                 rope_st=chunked_t(rope_s), cache_ckv=cache_ckv,
                 cache_kr=cache_kr_pad, state=state_lru)

    xp = x_prompt.reshape(batch * seq_p, D_MODEL)
    xs = x_sample.reshape(dec_batch * seq_s, D_MODEL)
    ckv_l, kr_l, st_l = [], [], []
    lw = _prep_weights(params)
    for l in range(DEPTH):
        hp, ckv, kr, st = _mix_call(xp, mods, l, lw, seq=seq_p, n_seq=2, latent=False)
        xp = _ffn_call(hp, mods, l, lw, rows_per_mod=None, mod_base=4)
        ckv_l.append(ckv.reshape(batch, seq_p, KV_LORA))
        kr_l.append(kr.reshape(batch, seq_p, QK_ROPE))
        st_l.append(st)
        (hs,) = _mix_call(xs, mods, l, lw, seq=seq_s, n_seq=1, latent=True, extra=extra)
        xs = _ffn_call(hs, mods, l, lw, rows_per_mod=seq_s, mod_base=None)
    return (xp.reshape(batch, seq_p, D_MODEL),
            xs.reshape(dec_batch, seq_s, D_MODEL),
            jnp.stack(ckv_l, axis=1),
            jnp.stack(kr_l, axis=1),
            jnp.stack(st_l, axis=1))
```

```python
import functools
import itertools
import math

import jax
import jax.numpy as jnp
from jax import lax
from jax.experimental import pallas as pl
from jax.experimental.pallas import tpu as pltpu

F32 = jnp.float32
BF16 = jnp.bfloat16

D_MODEL = 1024
DEPTH = 2
GRID_W = 64
CONV_W = 256
LRU_W = 256
LRU_BLOCKS = 4
LRU_BLK = 64
LRU_C = 8.0
MLA_HEADS = 8
QK_NOPE = 64
QK_ROPE = 32
V_DIM = 64
V_EXT = V_DIM + 16
QK_DIM = QK_NOPE + QK_ROPE
Q_LORA = 256
KV_LORA = 128
ROPE_THETA = 10000.0
FF = 2816
EPS = 1e-6

LANES = 128
SUBLANES = 8
HEAD_PAD = LANES
VMEM_LIMIT_BYTES = 60 * 1024 * 1024

C_BG, C_CG, C_H = 0, 256, 512
C_XB, C_YB = 768, 1024
C_Q = 1280
C_CKV = 1536
C_KR = 1664
U_W = 1792
ROPE_LO = QK_NOPE

ROW_CHUNK = 256
SCAN_CHUNK = 32
Q_CHUNK = 256
FF_CHUNK = 256
PROJ_SLAB = 256
CTX_SEQS_PER_STEP = 4
SEQ_STAGGER = 10
FFN_ROWS = 1024
PAD_ROWS = SUBLANES


def _rms_scale(x, n):
    ms = jnp.sum(x * x, axis=-1, keepdims=True) * (1.0 / n)
    return lax.rsqrt(ms + EPS)


def _sigmoid(x):
    return 0.5 * jnp.tanh(0.5 * x) + 0.5


def _gelu_tanh(x):
    c = math.sqrt(2.0 / math.pi)
    return 0.5 * x * (1.0 + jnp.tanh(c * (x + 0.044715 * (x * x * x))))


def _dot(a, b):
    return jnp.dot(a, b, preferred_element_type=F32)


def _interleave(*gens):
    live = list(gens)
    while live:
        for g in list(live):
            try:
                next(g)
            except StopIteration:
                live.remove(g)


def _interleave_staggered(gens, lag):
    pending = list(gens)
    live = []
    tick = 0
    while live or pending:
        if pending and tick % lag == 0:
            live.append(pending.pop(0))
        for g in list(live):
            try:
                next(g)
            except StopIteration:
                live.remove(g)
        tick += 1


def _dot_nt(a, b):
    return lax.dot_general(a, b, (((1,), (1,)), ((), ())), preferred_element_type=F32)


ADA_TN = 1536


def _ada_kernel(cond_ref, w_ref, b_ref, o_ref):
    cnd = cond_ref[...]
    s = (cnd * _sigmoid(cnd)).astype(BF16)
    o_ref[...] = _dot(s, w_ref[...].astype(BF16)) + b_ref[...]


def _ada_call(cond8, ada_w, ada_b):
    n_out = 6 * D_MODEL
    return pl.pallas_call(
        _ada_kernel,
        grid=(DEPTH, n_out // ADA_TN),
        in_specs=[
            pl.BlockSpec((8, D_MODEL), lambda l, j: (0, 0)),
            pl.BlockSpec((None, D_MODEL, ADA_TN), lambda l, j: (l, 0, j)),
            pl.BlockSpec((None, 1, ADA_TN), lambda l, j: (l, 0, j)),
        ],
        out_specs=pl.BlockSpec((None, 8, ADA_TN), lambda l, j: (l, 0, j)),
        out_shape=jax.ShapeDtypeStruct((DEPTH, 8, n_out), F32),
        compiler_params=pltpu.CompilerParams(
            dimension_semantics=("arbitrary", "arbitrary"),
            vmem_limit_bytes=VMEM_LIMIT_BYTES),
        name="ada_mod",
    )(cond8, ada_w, ada_b.reshape(DEPTH, 1, n_out))


def _mix_kernel(*refs, seq, n_seq, latent):
    T = seq * n_seq
    n_ctx = 256 if latent else 0
    (x_ref, mod_ref, n1g_ref, win_ref, convw_ref, convb_ref, lcw_ref, lcb_ref,
     wg_ref, bg_ref, lam_ref, qng_ref, wqt_ref, kvng_ref, wk_ref, wvt_ref,
     qgt_ref, kg_ref, gnc_ref, gnl_ref, gnm_ref, wout_ref) = refs[:22]
    pos = 22
    if latent:
        (wqpt_ref, qgpt_ref, kgp_ref, ropect_ref, ropest_ref, ropec_ref, ropes_ref,
         cckv_ref, ckr_ref, h0_ref) = refs[pos:pos + 10]
        pos += 10
    h_out_ref = refs[pos]
    pos += 1
    if not latent:
        ckv_out_ref, kr_out_ref, st_out_ref = refs[pos:pos + 3]
        pos += 3
    u_scr, ab_scr, qt_scr, k_scr, vt_scr, ymt_scr, ycat_scr = refs[pos:pos + 7]
    if latent:
        st_scr = refs[pos + 7]

    n_chunks = T // ROW_CHUNK
    lane = lax.broadcasted_iota(jnp.int32, (1, LANES), 1)
    rope_lanes = (lane >= ROPE_LO) & (lane < ROPE_LO + QK_ROPE)

    sh1 = mod_ref[:, 0:D_MODEL]
    sc1 = mod_ref[:, D_MODEL:2 * D_MODEL]
    g1 = mod_ref[:, 2 * D_MODEL:3 * D_MODEL]
    gain1 = n1g_ref[...] * (1.0 + sc1)

    u_scr[0:PAD_ROWS, :] = jnp.zeros((PAD_ROWS, U_W), F32)
    u_scr[PAD_ROWS + T:2 * PAD_ROWS + T, :] = jnp.zeros((PAD_ROWS, U_W), F32)

    def phase_a(c):
        r0 = c * ROW_CHUNK
        x = x_ref[r0:r0 + ROW_CHUNK, :]
        hn = (x * _rms_scale(x, D_MODEL) * gain1 + sh1).astype(BF16)
        yield
        urows = slice(PAD_ROWS + r0, PAD_ROWS + r0 + ROW_CHUNK)
        for s0 in range(0, U_W, PROJ_SLAB):
            u_scr[urows, s0:s0 + PROJ_SLAB] = _dot(hn, win_ref[:, s0:s0 + PROJ_SLAB])
            yield
        u_scr[urows, C_CG:C_CG + CONV_W] = (
            u_scr[urows, C_CG:C_CG + CONV_W] * u_scr[urows, C_H:C_H + CONV_W])
        yield

    vrow = lax.broadcasted_iota(jnp.int32, (MLA_HEADS * V_EXT, 1), 0)
    ones_rows = (vrow % V_EXT) >= V_DIM

    def values_t(cb):
        return jnp.where(ones_rows, 1.0, _dot_nt(wvt_ref[...], cb)).astype(BF16)

    def phase_b(c):
        r0 = c * ROW_CHUNK
        rows = slice(PAD_ROWS + r0, PAD_ROWS + r0 + ROW_CHUNK)
        uq = u_scr[rows, C_Q:C_Q + Q_LORA]
        qn = (uq * _rms_scale(uq, Q_LORA) * qng_ref[...]).astype(BF16)
        qa_t = _dot_nt(wqt_ref[...], qn)
        if latent:
            qpa_t = _dot_nt(wqpt_ref[...], qn)
            q_tc = ropect_ref[c] * qgt_ref[...]
            q_ts = ropest_ref[c] * qgpt_ref[...]
        yield
        for h in range(MLA_HEADS):
            sl = slice(h * HEAD_PAD, (h + 1) * HEAD_PAD)
            qh = qa_t[sl, :]
            rq = lax.rsqrt(jnp.sum(qh * qh, axis=0, keepdims=True) * (1.0 / QK_DIM) + EPS)
            if latent:
                qh = (qh * q_tc + qpa_t[sl, :] * q_ts) * rq
            else:
                qh = qh * qgt_ref[...] * rq
            qt_scr[h, c] = qh.astype(BF16)
            if h % 2:
                yield
        uc = u_scr[rows, C_CKV:C_CKV + KV_LORA]
        ckv = uc * _rms_scale(uc, KV_LORA) * kvng_ref[...]
        krb = u_scr[rows, C_KR:C_KR + LANES]
        kr_rolled = pltpu.roll(krb, 64, 1)
        if not latent:
            ckv_out_ref[r0:r0 + ROW_CHUNK, :] = ckv
            kr_out_ref[r0:r0 + ROW_CHUNK, :] = kr_rolled[:, 0:QK_ROPE]
        krm = jnp.where(rope_lanes, krb, 0.0)
        cb = ckv.astype(BF16)
        ka = _dot(cb, wk_ref[...])
        vt_scr[c] = values_t(cb)
        if latent:
            k_tc = ropec_ref[r0:r0 + ROW_CHUNK, :] * kg_ref[...]
            k_ts = ropes_ref[r0:r0 + ROW_CHUNK, :] * kgp_ref[...]
        yield
        for h in range(MLA_HEADS):
            sl = slice(h * HEAD_PAD, (h + 1) * HEAD_PAD)
            kpre = ka[:, sl] + krm
            rk = _rms_scale(kpre, QK_DIM)
            if latent:
                kh = (kpre * k_tc + kr_rolled * k_ts) * rk
            else:
                kh = kpre * kg_ref[...] * rk
            k_scr[h, r0:r0 + ROW_CHUNK, :] = kh.astype(BF16)
            if h % 2:
                yield

    if latent:
        cc = cckv_ref[...].astype(BF16)
        ka = _dot(cc, wk_ref[...])
        vt_scr[n_chunks] = values_t(cc)
        krc = ckr_ref[...]
        for h in range(MLA_HEADS):
            kpre = ka[:, h * HEAD_PAD:(h + 1) * HEAD_PAD] + krc
            kh = kpre * kg_ref[...] * _rms_scale(kpre, QK_DIM)
            k_scr[h, seq:seq + n_ctx, :] = kh.astype(BF16)

    neg_lam = -lam_ref[...]
    sp = jnp.maximum(neg_lam, 0.0) + jnp.log(1.0 + jnp.exp(-jnp.abs(neg_lam)))
    log2a_coef = (-0.5 * LRU_C * math.log2(math.e)) * sp
    row_rc = lax.broadcasted_iota(jnp.int32, (ROW_CHUNK, 1), 0)

    def phase_c(ci):
        r0 = ci * ROW_CHUNK
        base = PAD_ROWS + r0
        first = (r0 % seq) == 0
        last = ((r0 + ROW_CHUNK) % seq) == 0

        def win(col, shift, width=CONV_W):
            w = u_scr[base + shift:base + shift + ROW_CHUNK, col:col + width]
            if shift < 0 and first:
                w = jnp.where(row_rc < -shift, 0.0, w)
            if shift > 0 and last:
                w = jnp.where(row_rc >= ROW_CHUNK - shift, 0.0, w)
            return w

        z_m, z_0, z_p = win(C_CG, -1), win(C_CG, 0), win(C_CG, 1)
        conv = (z_m * convw_ref[0:1, :] + z_0 * convw_ref[1:2, :]
                + z_p * convw_ref[2:3, :] + convb_ref[...])
        yc = u_scr[base:base + ROW_CHUNK, C_BG:C_BG + CONV_W] * conv
        ycn = yc * _rms_scale(yc, CONV_W) * gnc_ref[...]
        ycat_scr[r0:r0 + ROW_CHUNK, 0:CONV_W] = ycn.astype(BF16)
        yield

        xc = (win(C_XB, -2) * lcw_ref[0:1, :] + win(C_XB, -1) * lcw_ref[1:2, :]
              + win(C_XB, 0) * lcw_ref[2:3, :] + win(C_XB, 1) * lcw_ref[3:4, :]
              + lcb_ref[...])
        half_gates = _dot(xc.astype(BF16), wg_ref[...]) + bg_ref[...]
        half_xc = 0.5 * xc
        yield
        for d in range(2):
            t_r = jnp.tanh(half_gates[:, (2 * d) * LRU_W:(2 * d + 1) * LRU_W])
            t_i = jnp.tanh(half_gates[:, (2 * d + 1) * LRU_W:(2 * d + 2) * LRU_W])
            a = jnp.exp2(log2a_coef[d:d + 1, :] * (t_r + 1.0))
            v = 1.0 - a * a
            mult = v * lax.rsqrt(jnp.maximum(v, 1e-30))
            ab_scr[r0:r0 + ROW_CHUNK, (2 * d) * LRU_W:(2 * d + 1) * LRU_W] = a
            ab_scr[r0:r0 + ROW_CHUNK, (2 * d + 1) * LRU_W:(2 * d + 2) * LRU_W] = (
                mult * (half_xc * (t_i + 1.0)))
            yield

    n_sc = seq // SCAN_CHUNK
    row_sc = lax.broadcasted_iota(jnp.int32, (SCAN_CHUNK, LRU_W), 0)
    steps = [1 << i for i in range(int(math.log2(SCAN_CHUNK)))]

    def scan_step(rf, rb, cf, cb_):
        a = ab_scr[pl.ds(rf, SCAN_CHUNK), 0:LRU_W]
        b = ab_scr[pl.ds(rf, SCAN_CHUNK), LRU_W:2 * LRU_W]
        b = b + jnp.where(row_sc == 0, a * cf, 0.0)
        for d in steps:
            b = b + a * jnp.where(row_sc >= d, pltpu.roll(b, d, 0), 0.0)
            if d != steps[-1]:
                a = a * jnp.where(row_sc >= d, pltpu.roll(a, d, 0), 1.0)
        ab_scr[pl.ds(rf, SCAN_CHUNK), LRU_W:2 * LRU_W] = b
        cf = b[SCAN_CHUNK - 1:SCAN_CHUNK, :]
        a = ab_scr[pl.ds(rb, SCAN_CHUNK), 2 * LRU_W:3 * LRU_W]
        b = ab_scr[pl.ds(rb, SCAN_CHUNK), 3 * LRU_W:4 * LRU_W]
        b = b + jnp.where(row_sc == SCAN_CHUNK - 1, a * cb_, 0.0)
        for d in steps:
            keep = row_sc < SCAN_CHUNK - d
            b = b + a * jnp.where(keep, pltpu.roll(b, SCAN_CHUNK - d, 0), 0.0)
            if d != steps[-1]:
                a = a * jnp.where(keep, pltpu.roll(a, SCAN_CHUNK - d, 0), 1.0)
        ab_scr[pl.ds(rb, SCAN_CHUNK), 3 * LRU_W:4 * LRU_W] = b
        cb_ = b[0:1, :]
        return cf, cb_

    cps = seq // ROW_CHUNK

    def softmax_pv(st, vts):
        m = jnp.max(st, axis=0, keepdims=True)
        pt = jnp.exp2(st - m).astype(BF16)
        ot = None
        for i, vt in enumerate(vts):
            part = _dot(vt, pt[i * ROW_CHUNK:(i + 1) * ROW_CHUNK, :])
            ot = part if ot is None else ot + part
        return ot[0:V_DIM, :] * (1.0 / ot[V_DIM:V_DIM + 1, :])

    def f_build(c):
        r0 = c * ROW_CHUNK
        rows = slice(r0, r0 + ROW_CHUNK)
        hf = ab_scr[rows, LRU_W:2 * LRU_W]
        hb = ab_scr[rows, 3 * LRU_W:4 * LRU_W]
        yb = u_scr[PAD_ROWS + r0:PAD_ROWS + r0 + ROW_CHUNK, C_YB:C_YB + LRU_W]
        yl = (hf + hb) * _gelu_tanh(yb)
        yln = yl * _rms_scale(yl, LRU_W) * gnl_ref[...]
        ycat_scr[rows, CONV_W:CONV_W + LRU_W] = yln.astype(BF16)
        yield
        ymt = ymt_scr[c]
        ssq = jnp.sum(ymt * ymt, axis=0, keepdims=True)
        ynt = ymt * lax.rsqrt(ssq * (1.0 / (MLA_HEADS * V_DIM)) + EPS)
        ycat_scr[rows, CONV_W + LRU_W:D_MODEL] = (ynt.T * gnm_ref[...]).astype(BF16)
        yield

    def f_project(c):
        rows = slice(c * ROW_CHUNK, (c + 1) * ROW_CHUNK)
        for n0 in range(0, D_MODEL, PROJ_SLAB):
            cols = slice(n0, n0 + PROJ_SLAB)
            y = _dot(ycat_scr[rows, :], wout_ref[:, cols])
            h_out_ref[rows, cols] = x_ref[rows, cols] + g1[:, cols] * y
            yield

    if latent:
        _interleave(phase_a(0))
        for c in range(n_chunks):
            dependent = [phase_b(c)] + ([phase_c(c - 1)] if c else [])
            ahead = [phase_a(c + 1)] if c + 1 < n_chunks else []
            _interleave(*ahead, itertools.chain(*dependent))
        _interleave(phase_c(n_chunks - 1))

        key_chunks = list(range(cps)) + [n_chunks]
        n_blk = cps * (MLA_HEADS // 2)
        trips = n_blk // 2
        scans_per_trip = n_sc // trips

        def heads_of(idx):
            qc = idx // (MLA_HEADS // 2)
            j = idx % (MLA_HEADS // 2)
            return qc, [2 * j, 2 * j + 1]

        def scores(idx, slot):
            qc, hs = heads_of(idx)
            for par, h in enumerate(hs):
                st_scr[slot, par] = _dot(k_scr[h], qt_scr[h, qc])

        def finish(idx, slot):
            qc, hs = heads_of(idx)
            for par, h in enumerate(hs):
                hr = pl.ds(pl.multiple_of(h * V_DIM, V_DIM), V_DIM)
                vr = pl.ds(pl.multiple_of(h * V_EXT, V_EXT - V_DIM), V_EXT)
                vts = [vt_scr[kc, vr, :] for kc in key_chunks]
                ymt_scr[qc, hr, :] = softmax_pv(st_scr[slot, par], vts)

        scores(0, 0)

        def att_body(i, carry):
            b0 = 2 * i
            attention = [
                lambda: scores(b0 + 1, 1),
                lambda: finish(b0, 0),
                lambda: scores(jnp.minimum(b0 + 2, n_blk - 1), 0),
                lambda: finish(b0 + 1, 1),
            ]
            for k in range(max(len(attention), scans_per_trip)):
                if k < len(attention):
                    attention[k]()
                if k < scans_per_trip:
                    c = i * scans_per_trip + k
                    rf = pl.multiple_of(c * SCAN_CHUNK, SCAN_CHUNK)
                    rb = pl.multiple_of((n_sc - 1 - c) * SCAN_CHUNK, SCAN_CHUNK)
                    carry = scan_step(rf, rb, *carry)
            return carry

        lax.fori_loop(0, trips, att_body, (h0_ref[0:1, :], h0_ref[1:2, :]))

        _interleave(f_build(0))
        for c in range(n_chunks):
            ahead = [f_build(c + 1)] if c + 1 < n_chunks else []
            _interleave(f_project(c), *ahead)
    else:
        def phase_d(s):
            carry = (jnp.zeros((1, LRU_W), F32), jnp.zeros((1, LRU_W), F32))
            for c in range(n_sc):
                carry = scan_step(s * seq + c * SCAN_CHUNK,
                                  s * seq + (n_sc - 1 - c) * SCAN_CHUNK, *carry)
                yield
            st_out_ref[s, 0:1, :] = carry[0]
            st_out_ref[s, 1:2, :] = carry[1]

        def phase_e(s):
            blocks = [(qi, h) for qi in range(cps) for h in range(MLA_HEADS)]

            def scores(blk):
                qi, h = blk
                return _dot(k_scr[h, s * seq:(s + 1) * seq, :], qt_scr[h, s * cps + qi])

            st = scores(blocks[0])
            for i, (qi, h) in enumerate(blocks):
                nxt = scores(blocks[i + 1]) if i + 1 < len(blocks) else None
                hr = slice(h * V_DIM, (h + 1) * V_DIM)
                vr = slice(h * V_EXT, (h + 1) * V_EXT)
                vts = [vt_scr[s * cps + j, vr, :] for j in range(cps)]
                ymt_scr[s * cps + qi, hr, :] = softmax_pv(st, vts)
                st = nxt
                yield

        def sequence(s):
            chunks = range(s * cps, (s + 1) * cps)
            for c in chunks:
                yield from phase_a(c)
            for c in chunks:
                yield from phase_b(c)
            for c in chunks:
                yield from phase_c(c)
            yield from phase_d(s)
            yield from phase_e(s)
            for c in chunks:
                yield from f_build(c)
                yield from f_project(c)

        _interleave_staggered([sequence(s) for s in range(n_seq)], SEQ_STAGGER)


def _const_spec(shape):
    nd = len(shape)
    return pl.BlockSpec(shape, lambda i, _n=nd: (0,) * _n,
                        pipeline_mode=pl.Buffered(1))


def _layer_spec(arr, layer):
    nd = arr.ndim - 1
    return pl.BlockSpec((None,) + arr.shape[1:], lambda i, _n=nd: (layer,) + (0,) * _n,
                        pipeline_mode=pl.Buffered(1))


def _mix_call(x2d, mods, layer, lw, *, seq, n_seq, latent, extra=None):
    n_tok = x2d.shape[0]
    T = seq * n_seq
    grid = (n_tok // T,)
    n_keys_buf = T if not latent else seq + 256

    if latent:
        mod_map = lambda i: (layer, i, 0, 0)
    else:
        mod_map = lambda i: (layer, 4, 0, 0)

    consts = [lw["norm1_g"], lw["w_in"], lw["conv_w"], lw["conv_b"], lw["lru_conv_w"],
              lw["lru_conv_b"], lw["w_gates"], lw["b_gates"], lw["lru_lambda"],
              lw["mla_qnorm_g"], lw["wq_t"], lw["mla_kvnorm_g"], lw["wk"], lw["wv_t"],
              lw["qg_t"], lw["kg"], lw["gnorm_conv"], lw["gnorm_lru"], lw["gnorm_mla"],
              lw["w_out"]]
    args = [x2d, mods] + consts
    x_mode = dict(pipeline_mode=pl.Buffered(1)) if latent else {}
    in_specs = [pl.BlockSpec((T, D_MODEL), lambda i: (i, 0), **x_mode),
                pl.BlockSpec((None, None, 1, 6 * D_MODEL), mod_map)]
    in_specs += [_layer_spec(a, layer) for a in consts]
    if latent:
        lconsts = [lw["wq_partner_t"], lw["qg_partner_t"], lw["kg_partner"]]
        tables = [extra["rope_ct"], extra["rope_st"], extra["rope_c"], extra["rope_s"]]
        args += lconsts + tables
        in_specs += [_layer_spec(a, layer) for a in lconsts]
        in_specs += [_const_spec(a.shape) for a in tables]
        args += [extra["cache_ckv"], extra["cache_kr"], extra["state"]]
        in_specs += [
            pl.BlockSpec((None, None, 256, KV_LORA), lambda i: (i, layer, 0, 0)),
            pl.BlockSpec((None, None, 256, LANES), lambda i: (i, layer, 0, 0)),
            pl.BlockSpec((None, None, 2, LRU_W), lambda i: (i, layer, 0, 0)),
        ]

    out_shape = [jax.ShapeDtypeStruct((n_tok, D_MODEL), F32)]
    out_specs = [pl.BlockSpec((T, D_MODEL), lambda i: (i, 0))]
    if not latent:
        out_shape += [jax.ShapeDtypeStruct((n_tok, KV_LORA), F32),
                      jax.ShapeDtypeStruct((n_tok, QK_ROPE), F32),
                      jax.ShapeDtypeStruct((n_tok // seq, 2, LRU_W), F32)]
        out_specs += [pl.BlockSpec((T, KV_LORA), lambda i: (i, 0)),
                      pl.BlockSpec((T, QK_ROPE), lambda i: (i, 0)),
                      pl.BlockSpec((n_seq, 2, LRU_W), lambda i: (i, 0, 0))]

    scratch = [
        pltpu.VMEM((T + 2 * PAD_ROWS, U_W), F32),
        pltpu.VMEM((T, 4 * LRU_W), F32),
        pltpu.VMEM((MLA_HEADS, T // ROW_CHUNK, HEAD_PAD, ROW_CHUNK), BF16),
        pltpu.VMEM((MLA_HEADS, n_keys_buf, HEAD_PAD), BF16),
        pltpu.VMEM((n_keys_buf // ROW_CHUNK, MLA_HEADS * V_EXT, ROW_CHUNK), BF16),
        pltpu.VMEM((T // ROW_CHUNK, MLA_HEADS * V_DIM, ROW_CHUNK), F32),
        pltpu.VMEM((T, D_MODEL), BF16),
    ]
    if latent:
        scratch.append(pltpu.VMEM((2, 2, n_keys_buf, ROW_CHUNK), F32))
    return pl.pallas_call(
        functools.partial(_mix_kernel, seq=seq, n_seq=n_seq, latent=latent),
        grid=grid,
        in_specs=in_specs,
        out_specs=out_specs,
        out_shape=out_shape,
        scratch_shapes=scratch,
        compiler_params=pltpu.CompilerParams(
            dimension_semantics=("arbitrary",),
            vmem_limit_bytes=VMEM_LIMIT_BYTES),
        name="mix_latent" if latent else "mix_context",
    )(*args)


def _ffn_kernel(h_ref, mod_ref, n2g_ref, wg_ref, wu_ref, wd_ref, o_ref):
    sh2 = mod_ref[:, 3 * D_MODEL:4 * D_MODEL]
    sc2 = mod_ref[:, 4 * D_MODEL:5 * D_MODEL]
    g2 = mod_ref[:, 5 * D_MODEL:6 * D_MODEL]
    h = h_ref[...]
    hn = h * _rms_scale(h, D_MODEL) * n2g_ref[...]
    hn = (hn * (1.0 + sc2) + sh2).astype(BF16)
    acc = None
    for c0 in range(0, FF, FF_CHUNK):
        g = _dot(hn, wg_ref[:, c0:c0 + FF_CHUNK])
        u = _dot(hn, wu_ref[:, c0:c0 + FF_CHUNK])
        act = (g * _sigmoid(g) * u).astype(BF16)
        part = _dot(act, wd_ref[c0:c0 + FF_CHUNK, :])
        acc = part if acc is None else acc + part
    o_ref[...] = h + g2 * acc


def _ffn_call(h2d, mods, layer, lw, *, rows_per_mod, mod_base):
    n_tok = h2d.shape[0]
    if mod_base is None:
        tiles_per_mod = rows_per_mod // FFN_ROWS
        mod_map = lambda i: (layer, i // tiles_per_mod, 0, 0)
    else:
        mod_map = lambda i: (layer, mod_base, 0, 0)
    consts = [lw["norm2_g"], lw["w_gate"], lw["w_up"], lw["w_down"]]
    return pl.pallas_call(
        _ffn_kernel,
        grid=(n_tok // FFN_ROWS,),
        in_specs=[pl.BlockSpec((FFN_ROWS, D_MODEL), lambda i: (i, 0)),
                  pl.BlockSpec((None, None, 1, 6 * D_MODEL), mod_map)]
        + [_layer_spec(a, layer) for a in consts],
        out_specs=pl.BlockSpec((FFN_ROWS, D_MODEL), lambda i: (i, 0)),
        out_shape=jax.ShapeDtypeStruct((n_tok, D_MODEL), F32),
        compiler_params=pltpu.CompilerParams(
            dimension_semantics=("arbitrary",),
            vmem_limit_bytes=VMEM_LIMIT_BYTES),
        name="swiglu",
    )(h2d, mods, *consts)


def _rope_partner_index():
    return [d + 8 if (d % 16) < 8 else d - 8 for d in range(QK_ROPE)]


def _prep_weights(p):
    partner = jnp.array(_rope_partner_index(), jnp.int32)
    q_scale = QK_DIM ** -0.5 * math.log2(math.e)
    w_in = p["w_in"]
    w_kr = w_in[:, :, C_KR:C_KR + QK_ROPE]
    zeros32 = jnp.zeros((DEPTH, D_MODEL, QK_ROPE), F32)
    w_in_aug = jnp.concatenate(
        [w_in[:, :, :C_KR], w_kr[:, :, partner], zeros32, w_kr, zeros32], axis=2)

    wq_h = p["mla_wq_up"].reshape(DEPTH, Q_LORA, MLA_HEADS, QK_DIM)
    wq_pad = jnp.pad(wq_h, ((0, 0), (0, 0), (0, 0), (0, HEAD_PAD - QK_DIM)))
    wq_t = wq_pad.reshape(DEPTH, Q_LORA, MLA_HEADS * HEAD_PAD).transpose(0, 2, 1)
    rope_pad = ((0, 0), (0, 0), (0, 0), (ROPE_LO, HEAD_PAD - ROPE_LO - QK_ROPE))
    wq_partner = jnp.pad(wq_h[:, :, :, QK_NOPE:][:, :, :, partner], rope_pad)
    wq_partner_t = wq_partner.reshape(DEPTH, Q_LORA, MLA_HEADS * HEAD_PAD).transpose(0, 2, 1)

    wkv = p["mla_wkv_up"].reshape(DEPTH, KV_LORA, MLA_HEADS, QK_NOPE + V_DIM)
    wk_pad = jnp.pad(wkv[:, :, :, :QK_NOPE], ((0, 0), (0, 0), (0, 0), (0, HEAD_PAD - QK_NOPE)))
    wk_pad = wk_pad.reshape(DEPTH, KV_LORA, MLA_HEADS * HEAD_PAD)
    wv_ext = jnp.pad(wkv[:, :, :, QK_NOPE:], ((0, 0), (0, 0), (0, 0), (0, V_EXT - V_DIM)))
    wv_t = wv_ext.reshape(DEPTH, KV_LORA, MLA_HEADS * V_EXT).transpose(0, 2, 1)

    def pad_gain(g):
        return jnp.pad(g, ((0, 0), (0, HEAD_PAD - QK_DIM)))[:, None, :]

    def partner_gain(g):
        gp = g[:, QK_NOPE:][:, partner]
        return jnp.pad(gp, ((0, 0), (ROPE_LO, HEAD_PAD - ROPE_LO - QK_ROPE)))[:, None, :]

    def on_sublanes(g):
        return jnp.broadcast_to(g.transpose(0, 2, 1), (DEPTH, HEAD_PAD, ROW_CHUNK))

    qg, kg = p["q_norm_g"], p["k_norm_g"]
    eye = jnp.eye(LRU_BLOCKS, dtype=F32)
    gate_w = jnp.stack([p["lru_wa"][:, 0], p["lru_wi"][:, 0],
                        p["lru_wa"][:, 1], p["lru_wi"][:, 1]], axis=1)
    w_gates = (gate_w[:, :, :, :, None, :] * eye[None, None, :, None, :, None])
    w_gates = w_gates.transpose(0, 2, 3, 1, 4, 5).reshape(DEPTH, LRU_W, 4 * LRU_W)
    b_gates = jnp.stack([p["lru_ba"][:, 0], p["lru_bi"][:, 0],
                         p["lru_ba"][:, 1], p["lru_bi"][:, 1]], axis=1)

    def row(v):
        return v[:, None, :]

    return {
        "norm1_g": row(p["norm1_g"]),
        "w_in": w_in_aug.astype(BF16),
        "conv_w": p["conv_w"],
        "conv_b": row(p["conv_b"]),
        "lru_conv_w": p["lru_conv_w"],
        "lru_conv_b": row(p["lru_conv_b"]),
        "w_gates": (0.5 * w_gates).astype(BF16),
        "b_gates": 0.5 * b_gates.reshape(DEPTH, 1, 4 * LRU_W),
        "lru_lambda": p["lru_lambda"],
        "mla_qnorm_g": row(p["mla_qnorm_g"]),
        "wq_t": wq_t.astype(BF16),
        "wq_partner_t": wq_partner_t.astype(BF16),
        "mla_kvnorm_g": row(p["mla_kvnorm_g"]),
        "wk": wk_pad.astype(BF16),
        "wv_t": wv_t.astype(BF16),
        "qg_t": on_sublanes(pad_gain(qg) * q_scale),
        "qg_partner_t": on_sublanes(partner_gain(qg) * q_scale),
        "kg": pad_gain(kg),
        "kg_partner": partner_gain(kg),
        "gnorm_conv": row(p["gnorm_conv"]),
        "gnorm_lru": row(p["gnorm_lru"]),
        "gnorm_mla": row(p["gnorm_mla"]),
        "w_out": p["w_out"].astype(BF16),
        "norm2_g": row(p["norm2_g"]),
        "w_gate": p["w_gate"].astype(BF16),
        "w_up": p["w_up"].astype(BF16),
        "w_down": p["w_down"].astype(BF16),
    }


def _rope_tables(n_tokens):
    n_rows = n_tokens // GRID_W
    row = jnp.repeat(jnp.arange(n_rows, dtype=F32), GRID_W)
    col = jnp.tile(jnp.arange(GRID_W, dtype=F32), n_rows)
    n_freq = QK_ROPE // 4
    inv_freq = jnp.power(ROPE_THETA, -jnp.arange(n_freq, dtype=F32) / n_freq)
    ang_r = row[:, None] * inv_freq
    ang_c = col[:, None] * inv_freq
    cos = jnp.concatenate([jnp.cos(ang_r), jnp.cos(ang_r), jnp.cos(ang_c), jnp.cos(ang_c)], 1)
    sin = jnp.concatenate([-jnp.sin(ang_r), jnp.sin(ang_r), -jnp.sin(ang_c), jnp.sin(ang_c)], 1)
    ones = jnp.ones((n_tokens, QK_NOPE), F32)
    tail = jnp.zeros((n_tokens, HEAD_PAD - QK_DIM), F32)
    rope_c = jnp.concatenate([ones, cos, tail], axis=1)
    rope_s = jnp.concatenate([jnp.zeros((n_tokens, QK_NOPE), F32), sin, tail], axis=1)
    return rope_c, rope_s


def kernel(x_prompt, x_sample, cache_ckv, cache_krope, state_lru, c, c_ctx, norm1_g, ada_w, ada_b, w_in, conv_w, conv_b, lru_conv_w, lru_conv_b, lru_wa, lru_ba, lru_wi, lru_bi, lru_lambda, mla_qnorm_g, mla_wq_up, mla_kvnorm_g, mla_wkv_up, q_norm_g, k_norm_g, gnorm_conv, gnorm_lru, gnorm_mla, w_out, norm2_g, w_gate, w_up, w_down):
    params = dict(norm1_g=norm1_g, w_in=w_in, conv_w=conv_w, conv_b=conv_b,
                  lru_conv_w=lru_conv_w, lru_conv_b=lru_conv_b, lru_wa=lru_wa,
                  lru_ba=lru_ba, lru_wi=lru_wi, lru_bi=lru_bi, lru_lambda=lru_lambda,
                  mla_qnorm_g=mla_qnorm_g, mla_wq_up=mla_wq_up,
                  mla_kvnorm_g=mla_kvnorm_g, mla_wkv_up=mla_wkv_up, q_norm_g=q_norm_g,
                  k_norm_g=k_norm_g, gnorm_conv=gnorm_conv, gnorm_lru=gnorm_lru,
                  gnorm_mla=gnorm_mla, w_out=w_out, norm2_g=norm2_g, w_gate=w_gate,
                  w_up=w_up, w_down=w_down)
    batch, seq_p, _ = x_prompt.shape
    dec_batch, seq_s, _ = x_sample.shape
    assert dec_batch == 4 and cache_ckv.shape[2] == 256

    cond8 = jnp.concatenate(
        [c, c_ctx[None, :], jnp.zeros((8 - dec_batch - 1, D_MODEL), F32)], axis=0)
    mods = _ada_call(cond8, ada_w, ada_b).reshape(DEPTH, 8, 1, 6 * D_MODEL)

    rope_c, rope_s = _rope_tables(seq_s)
    cache_kr_pad = jnp.pad(
        cache_krope, ((0, 0), (0, 0), (0, 0), (ROPE_LO, HEAD_PAD - ROPE_LO - QK_ROPE)))

    def chunked_t(tab):
        return tab.reshape(seq_s // ROW_CHUNK, ROW_CHUNK, HEAD_PAD).transpose(0, 2, 1)

    extra = dict(rope_c=rope_c, rope_s=rope_s, rope_ct=chunked_t(rope_c),
                 rope_st=chunked_t(rope_s), cache_ckv=cache_ckv,
                 cache_kr=cache_kr_pad, state=state_lru)

    xp = x_prompt.reshape(batch * seq_p, D_MODEL)
    xs = x_sample.reshape(dec_batch * seq_s, D_MODEL)
    ckv_l, kr_l, st_l = [], [], []
    lw = _prep_weights(params)
    for l in range(DEPTH):
        hp, ckv, kr, st = _mix_call(xp, mods, l, lw, seq=seq_p, n_seq=CTX_SEQS_PER_STEP,
                                    latent=False)
        xp = _ffn_call(hp, mods, l, lw, rows_per_mod=None, mod_base=4)
        ckv_l.append(ckv.reshape(batch, seq_p, KV_LORA))
        kr_l.append(kr.reshape(batch, seq_p, QK_ROPE))
        st_l.append(st)
        (hs,) = _mix_call(xs, mods, l, lw, seq=seq_s, n_seq=1, latent=True, extra=extra)
        xs = _ffn_call(hs, mods, l, lw, rows_per_mod=seq_s, mod_base=None)
    return (xp.reshape(batch, seq_p, D_MODEL),
            xs.reshape(dec_batch, seq_s, D_MODEL),
            jnp.stack(ckv_l, axis=1),
            jnp.stack(kr_l, axis=1),
            jnp.stack(st_l, axis=1))
```

```python
import functools
import itertools
import math

import jax
import jax.numpy as jnp
import numpy as np
from jax import lax
from jax.experimental import pallas as pl
from jax.experimental.pallas import tpu as pltpu

F32 = jnp.float32
BF16 = jnp.bfloat16

D_MODEL = 1024
DEPTH = 2
GRID_W = 64
CONV_W = 256
LRU_W = 256
LRU_BLOCKS = 4
LRU_BLK = 64
LRU_C = 8.0
MLA_HEADS = 8
QK_NOPE = 64
QK_ROPE = 32
V_DIM = 64
V_EXT = V_DIM + 16
QK_DIM = QK_NOPE + QK_ROPE
Q_LORA = 256
KV_LORA = 128
ROPE_THETA = 10000.0
FF = 2816
EPS = 1e-6

LANES = 128
SUBLANES = 8
HEAD_PAD = LANES
VMEM_LIMIT_BYTES = 60 * 1024 * 1024

C_BG, C_CG, C_H = 0, 256, 512
C_XB, C_YB = 768, 1024
C_Q = 1280
C_CKV = 1536
C_KR = 1664
U_W = 1792
SCAN_A = (C_H, C_CKV)
SCAN_B = (C_Q, C_BG)
ROPE_LO = QK_NOPE

ROW_CHUNK = 256
SCAN_CHUNK = 32
Q_CHUNK = 256
FF_CHUNK = 256
PROJ_SLAB = 256
CTX_SEQS_PER_STEP = 4
SEQ_STAGGER = 10
FFN_ROWS = 1024
PAD_ROWS = SUBLANES


def _rms_scale(x, n):
    ms = jnp.sum(x * x, axis=-1, keepdims=True) * (1.0 / n)
    return lax.rsqrt(ms + EPS)


def _sigmoid(x):
    return 0.5 * jnp.tanh(0.5 * x) + 0.5


def _gelu_tanh(x):
    c = math.sqrt(2.0 / math.pi)
    return 0.5 * x * (1.0 + jnp.tanh(c * (x + 0.044715 * (x * x * x))))


def _dot(a, b):
    return jnp.dot(a, b, preferred_element_type=F32)


def _interleave(*gens):
    live = list(gens)
    while live:
        for g in list(live):
            try:
                next(g)
            except StopIteration:
                live.remove(g)


def _interleave_staggered(gens, lag):
    pending = list(gens)
    live = []
    tick = 0
    while live or pending:
        if pending and tick % lag == 0:
            live.append(pending.pop(0))
        for g in list(live):
            try:
                next(g)
            except StopIteration:
                live.remove(g)
        tick += 1


def _dot_nt(a, b):
    return lax.dot_general(a, b, (((1,), (1,)), ((), ())), preferred_element_type=F32)


ADA_TN = 1536


def _ada_kernel(cond_ref, w_ref, b_ref, o_ref):
    cnd = cond_ref[...]
    s = (cnd * _sigmoid(cnd)).astype(BF16)
    o_ref[...] = _dot(s, w_ref[...].astype(BF16)) + b_ref[...]


def _ada_call(cond8, ada_w, ada_b):
    n_out = 6 * D_MODEL
    return pl.pallas_call(
        _ada_kernel,
        grid=(DEPTH, n_out // ADA_TN),
        in_specs=[
            pl.BlockSpec((8, D_MODEL), lambda l, j: (0, 0)),
            pl.BlockSpec((None, D_MODEL, ADA_TN), lambda l, j: (l, 0, j)),
            pl.BlockSpec((None, 1, ADA_TN), lambda l, j: (l, 0, j)),
        ],
        out_specs=pl.BlockSpec((None, 8, ADA_TN), lambda l, j: (l, 0, j)),
        out_shape=jax.ShapeDtypeStruct((DEPTH, 8, n_out), F32),
        compiler_params=pltpu.CompilerParams(
            dimension_semantics=("arbitrary", "arbitrary"),
            vmem_limit_bytes=VMEM_LIMIT_BYTES),
        name="ada_mod",
    )(cond8, ada_w, ada_b.reshape(DEPTH, 1, n_out))


def _mix_kernel(*refs, seq, n_seq, latent):
    T = seq * n_seq
    n_ctx = 256 if latent else 0
    (x_ref, mod_ref, n1g_ref, win_ref, convw_ref, convb_ref, lcw_ref, lcb_ref,
     wg_ref, bg_ref, lam_ref, qng_ref, wqt_ref, kvng_ref, wk_ref, wvt_ref,
     qgt_ref, kg_ref, gnc_ref, gnl_ref, gnm_ref, wout_ref) = refs[:22]
    pos = 22
    if latent:
        (wqpt_ref, qgpt_ref, kgp_ref, ropect_ref, ropest_ref, ropec_ref, ropes_ref,
         cckv_ref, ckr_ref, h0_ref) = refs[pos:pos + 10]
        pos += 10
    h_out_ref = refs[pos]
    pos += 1
    if not latent:
        ckv_out_ref, kr_out_ref, st_out_ref = refs[pos:pos + 3]
        pos += 3
    u_scr, qt_scr, k_scr, vt_scr, ymt_scr, ycat_scr = refs[pos:pos + 6]
    if latent:
        st_scr = refs[pos + 6]

    n_chunks = T // ROW_CHUNK
    lane = lax.broadcasted_iota(jnp.int32, (1, LANES), 1)
    rope_lanes = (lane >= ROPE_LO) & (lane < ROPE_LO + QK_ROPE)

    sh1 = mod_ref[:, 0:D_MODEL]
    sc1 = mod_ref[:, D_MODEL:2 * D_MODEL]
    g1 = mod_ref[:, 2 * D_MODEL:3 * D_MODEL]
    gain1 = n1g_ref[...] * (1.0 + sc1)

    u_scr[0:PAD_ROWS, :] = jnp.zeros((PAD_ROWS, U_W), F32)
    u_scr[PAD_ROWS + T:2 * PAD_ROWS + T, :] = jnp.zeros((PAD_ROWS, U_W), F32)

    def phase_a(c):
        r0 = c * ROW_CHUNK
        x = x_ref[r0:r0 + ROW_CHUNK, :]
        hn = (x * _rms_scale(x, D_MODEL) * gain1 + sh1).astype(BF16)
        yield
        urows = slice(PAD_ROWS + r0, PAD_ROWS + r0 + ROW_CHUNK)
        for s0 in range(0, U_W, PROJ_SLAB):
            u_scr[urows, s0:s0 + PROJ_SLAB] = _dot(hn, win_ref[:, s0:s0 + PROJ_SLAB])
            yield
        u_scr[urows, C_CG:C_CG + CONV_W] = (
            u_scr[urows, C_CG:C_CG + CONV_W] * u_scr[urows, C_H:C_H + CONV_W])
        yield

    vrow = lax.broadcasted_iota(jnp.int32, (MLA_HEADS * V_EXT, 1), 0)
    ones_rows = (vrow % V_EXT) >= V_DIM

    def values_t(cb):
        return jnp.where(ones_rows, 1.0, _dot_nt(wvt_ref[...], cb)).astype(BF16)

    def phase_b(c):
        r0 = c * ROW_CHUNK
        rows = slice(PAD_ROWS + r0, PAD_ROWS + r0 + ROW_CHUNK)
        uq = u_scr[rows, C_Q:C_Q + Q_LORA]
        qn = (uq * _rms_scale(uq, Q_LORA) * qng_ref[...]).astype(BF16)
        qa_t = _dot_nt(wqt_ref[...], qn)
        if latent:
            qpa_t = _dot_nt(wqpt_ref[...], qn)
            q_tc = ropect_ref[c] * qgt_ref[...]
            q_ts = ropest_ref[c] * qgpt_ref[...]
        yield
        for h in range(MLA_HEADS):
            sl = slice(h * HEAD_PAD, (h + 1) * HEAD_PAD)
            qh = qa_t[sl, :]
            rq = lax.rsqrt(jnp.sum(qh * qh, axis=0, keepdims=True) * (1.0 / QK_DIM) + EPS)
            if latent:
                qh = (qh * q_tc + qpa_t[sl, :] * q_ts) * rq
            else:
                qh = qh * qgt_ref[...] * rq
            qt_scr[h, c] = qh.astype(BF16)
            if h % 2:
                yield
        uc = u_scr[rows, C_CKV:C_CKV + KV_LORA]
        ckv = uc * _rms_scale(uc, KV_LORA) * kvng_ref[...]
        krb = u_scr[rows, C_KR:C_KR + LANES]
        kr_rolled = pltpu.roll(krb, 64, 1)
        if not latent:
            ckv_out_ref[r0:r0 + ROW_CHUNK, :] = ckv
            kr_out_ref[r0:r0 + ROW_CHUNK, :] = kr_rolled[:, 0:QK_ROPE]
        krm = jnp.where(rope_lanes, krb, 0.0)
        cb = ckv.astype(BF16)
        ka = _dot(cb, wk_ref[...])
        vt_scr[c] = values_t(cb)
        if latent:
            k_tc = ropec_ref[r0:r0 + ROW_CHUNK, :] * kg_ref[...]
            k_ts = ropes_ref[r0:r0 + ROW_CHUNK, :] * kgp_ref[...]
        yield
        for h in range(MLA_HEADS):
            sl = slice(h * HEAD_PAD, (h + 1) * HEAD_PAD)
            kpre = ka[:, sl] + krm
            rk = _rms_scale(kpre, QK_DIM)
            if latent:
                kh = (kpre * k_tc + kr_rolled * k_ts) * rk
            else:
                kh = kpre * kg_ref[...] * rk
            k_scr[h, r0:r0 + ROW_CHUNK, :] = kh.astype(BF16)
            if h % 2:
                yield

    if latent:
        cc = cckv_ref[...].astype(BF16)
        ka = _dot(cc, wk_ref[...])
        vt_scr[n_chunks] = values_t(cc)
        krc = ckr_ref[...]
        for h in range(MLA_HEADS):
            kpre = ka[:, h * HEAD_PAD:(h + 1) * HEAD_PAD] + krc
            kh = kpre * kg_ref[...] * _rms_scale(kpre, QK_DIM)
            k_scr[h, seq:seq + n_ctx, :] = kh.astype(BF16)

    neg_lam = -lam_ref[...]
    sp = jnp.maximum(neg_lam, 0.0) + jnp.log(1.0 + jnp.exp(-jnp.abs(neg_lam)))
    log2a_coef = (-0.5 * LRU_C * math.log2(math.e)) * sp
    row_rc = lax.broadcasted_iota(jnp.int32, (ROW_CHUNK, 1), 0)

    def phase_c(ci):
        r0 = ci * ROW_CHUNK
        base = PAD_ROWS + r0
        first = (r0 % seq) == 0
        last = ((r0 + ROW_CHUNK) % seq) == 0

        def win(col, shift, width=CONV_W):
            w = u_scr[base + shift:base + shift + ROW_CHUNK, col:col + width]
            if shift < 0 and first:
                w = jnp.where(row_rc < -shift, 0.0, w)
            if shift > 0 and last:
                w = jnp.where(row_rc >= ROW_CHUNK - shift, 0.0, w)
            return w

        z_m, z_0, z_p = win(C_CG, -1), win(C_CG, 0), win(C_CG, 1)
        conv = (z_m * convw_ref[0:1, :] + z_0 * convw_ref[1:2, :]
                + z_p * convw_ref[2:3, :] + convb_ref[...])
        yc = u_scr[base:base + ROW_CHUNK, C_BG:C_BG + CONV_W] * conv
        ycn = yc * _rms_scale(yc, CONV_W) * gnc_ref[...]
        ycat_scr[r0:r0 + ROW_CHUNK, 0:CONV_W] = ycn.astype(BF16)
        yield

        xc = (win(C_XB, -2) * lcw_ref[0:1, :] + win(C_XB, -1) * lcw_ref[1:2, :]
              + win(C_XB, 0) * lcw_ref[2:3, :] + win(C_XB, 1) * lcw_ref[3:4, :]
              + lcb_ref[...])
        half_gates = _dot(xc.astype(BF16), wg_ref[...]) + bg_ref[...]
        half_xc = 0.5 * xc
        yield
        for d in range(2):
            t_r = jnp.tanh(half_gates[:, (2 * d) * LRU_W:(2 * d + 1) * LRU_W])
            t_i = jnp.tanh(half_gates[:, (2 * d + 1) * LRU_W:(2 * d + 2) * LRU_W])
            a = jnp.exp2(log2a_coef[d:d + 1, :] * (t_r + 1.0))
            v = 1.0 - a * a
            mult = v * lax.rsqrt(jnp.maximum(v, 1e-30))
            u_scr[base:base + ROW_CHUNK, SCAN_A[d]:SCAN_A[d] + LRU_W] = a
            u_scr[base:base + ROW_CHUNK, SCAN_B[d]:SCAN_B[d] + LRU_W] = (
                mult * (half_xc * (t_i + 1.0)))
            yield

    n_sc = seq // SCAN_CHUNK
    row_sc = lax.broadcasted_iota(jnp.int32, (SCAN_CHUNK, LRU_W), 0)
    steps = [1 << i for i in range(int(math.log2(SCAN_CHUNK)))]

    def scan_step(rf, rb, cf, cb_):
        rf = pl.multiple_of(rf + PAD_ROWS, SUBLANES)
        rb = pl.multiple_of(rb + PAD_ROWS, SUBLANES)
        a = u_scr[pl.ds(rf, SCAN_CHUNK), SCAN_A[0]:SCAN_A[0] + LRU_W]
        b = u_scr[pl.ds(rf, SCAN_CHUNK), SCAN_B[0]:SCAN_B[0] + LRU_W]
        b = b + jnp.where(row_sc == 0, a * cf, 0.0)
        for d in steps:
            b = b + a * jnp.where(row_sc >= d, pltpu.roll(b, d, 0), 0.0)
            if d != steps[-1]:
                a = a * jnp.where(row_sc >= d, pltpu.roll(a, d, 0), 1.0)
        u_scr[pl.ds(rf, SCAN_CHUNK), SCAN_B[0]:SCAN_B[0] + LRU_W] = b
        cf = b[SCAN_CHUNK - 1:SCAN_CHUNK, :]
        a = u_scr[pl.ds(rb, SCAN_CHUNK), SCAN_A[1]:SCAN_A[1] + LRU_W]
        b = u_scr[pl.ds(rb, SCAN_CHUNK), SCAN_B[1]:SCAN_B[1] + LRU_W]
        b = b + jnp.where(row_sc == SCAN_CHUNK - 1, a * cb_, 0.0)
        for d in steps:
            keep = row_sc < SCAN_CHUNK - d
            b = b + a * jnp.where(keep, pltpu.roll(b, SCAN_CHUNK - d, 0), 0.0)
            if d != steps[-1]:
                a = a * jnp.where(keep, pltpu.roll(a, SCAN_CHUNK - d, 0), 1.0)
        u_scr[pl.ds(rb, SCAN_CHUNK), SCAN_B[1]:SCAN_B[1] + LRU_W] = b
        cb_ = b[0:1, :]
        return cf, cb_

    cps = seq // ROW_CHUNK

    def softmax_pv(st, vts):
        m = jnp.max(st, axis=0, keepdims=True)
        pt = jnp.exp2(st - m).astype(BF16)
        ot = None
        for i, vt in enumerate(vts):
            part = _dot(vt, pt[i * ROW_CHUNK:(i + 1) * ROW_CHUNK, :])
            ot = part if ot is None else ot + part
        return ot[0:V_DIM, :] * (1.0 / ot[V_DIM:V_DIM + 1, :])

    def f_build(c):
        r0 = c * ROW_CHUNK
        rows = slice(r0, r0 + ROW_CHUNK)
        urows = slice(PAD_ROWS + r0, PAD_ROWS + r0 + ROW_CHUNK)
        hf = u_scr[urows, SCAN_B[0]:SCAN_B[0] + LRU_W]
        hb = u_scr[urows, SCAN_B[1]:SCAN_B[1] + LRU_W]
        yb = u_scr[urows, C_YB:C_YB + LRU_W]
        yl = (hf + hb) * _gelu_tanh(yb)
        yln = yl * _rms_scale(yl, LRU_W) * gnl_ref[...]
        ycat_scr[rows, CONV_W:CONV_W + LRU_W] = yln.astype(BF16)
        yield
        ymt = ymt_scr[c]
        ssq = jnp.sum(ymt * ymt, axis=0, keepdims=True)
        ynt = ymt * lax.rsqrt(ssq * (1.0 / (MLA_HEADS * V_DIM)) + EPS)
        ycat_scr[rows, CONV_W + LRU_W:D_MODEL] = (ynt.T * gnm_ref[...]).astype(BF16)
        yield

    def f_project(c):
        rows = slice(c * ROW_CHUNK, (c + 1) * ROW_CHUNK)
        for n0 in range(0, D_MODEL, PROJ_SLAB):
            cols = slice(n0, n0 + PROJ_SLAB)
            y = _dot(ycat_scr[rows, :], wout_ref[:, cols])
            h_out_ref[rows, cols] = x_ref[rows, cols] + g1[:, cols] * y
            yield

    if latent:
        _interleave(phase_a(0))
        for c in range(n_chunks):
            dependent = [phase_b(c)] + ([phase_c(c - 1)] if c else [])
            ahead = [phase_a(c + 1)] if c + 1 < n_chunks else []
            _interleave(*ahead, itertools.chain(*dependent))
        _interleave(phase_c(n_chunks - 1))

        key_chunks = list(range(cps)) + [n_chunks]
        n_blk = cps * (MLA_HEADS // 2)
        trips = n_blk // 2
        scans_per_trip = n_sc // trips

        def heads_of(idx):
            qc = idx // (MLA_HEADS // 2)
            j = idx % (MLA_HEADS // 2)
            return qc, [2 * j, 2 * j + 1]

        def scores(idx, slot):
            qc, hs = heads_of(idx)
            for par, h in enumerate(hs):
                st_scr[slot, par] = _dot(k_scr[h], qt_scr[h, qc])

        def finish(idx, slot):
            qc, hs = heads_of(idx)
            for par, h in enumerate(hs):
                hr = pl.ds(pl.multiple_of(h * V_DIM, V_DIM), V_DIM)
                vr = pl.ds(pl.multiple_of(h * V_EXT, V_EXT - V_DIM), V_EXT)
                vts = [vt_scr[kc, vr, :] for kc in key_chunks]
                ymt_scr[qc, hr, :] = softmax_pv(st_scr[slot, par], vts)

        scores(0, 0)

        def att_body(i, carry):
            b0 = 2 * i
            attention = [
                lambda: scores(b0 + 1, 1),
                lambda: finish(b0, 0),
                lambda: scores(jnp.minimum(b0 + 2, n_blk - 1), 0),
                lambda: finish(b0 + 1, 1),
            ]
            for k in range(max(len(attention), scans_per_trip)):
                if k < len(attention):
                    attention[k]()
                if k < scans_per_trip:
                    c = i * scans_per_trip + k
                    rf = pl.multiple_of(c * SCAN_CHUNK, SCAN_CHUNK)
                    rb = pl.multiple_of((n_sc - 1 - c) * SCAN_CHUNK, SCAN_CHUNK)
                    carry = scan_step(rf, rb, *carry)
            return carry

        lax.fori_loop(0, trips, att_body, (h0_ref[0:1, :], h0_ref[1:2, :]))

        _interleave(f_build(0))
        for c in range(n_chunks):
            ahead = [f_build(c + 1)] if c + 1 < n_chunks else []
            _interleave(f_project(c), *ahead)
    else:
        def phase_d(s):
            carry = (jnp.zeros((1, LRU_W), F32), jnp.zeros((1, LRU_W), F32))
            for c in range(n_sc):
                carry = scan_step(s * seq + c * SCAN_CHUNK,
                                  s * seq + (n_sc - 1 - c) * SCAN_CHUNK, *carry)
                yield
            st_out_ref[s, 0:1, :] = carry[0]
            st_out_ref[s, 1:2, :] = carry[1]

        def phase_e(s):
            blocks = [(qi, h) for qi in range(cps) for h in range(MLA_HEADS)]

            def scores(blk):
                qi, h = blk
                return _dot(k_scr[h, s * seq:(s + 1) * seq, :], qt_scr[h, s * cps + qi])

            st = scores(blocks[0])
            for i, (qi, h) in enumerate(blocks):
                nxt = scores(blocks[i + 1]) if i + 1 < len(blocks) else None
                hr = slice(h * V_DIM, (h + 1) * V_DIM)
                vr = slice(h * V_EXT, (h + 1) * V_EXT)
                vts = [vt_scr[s * cps + j, vr, :] for j in range(cps)]
                ymt_scr[s * cps + qi, hr, :] = softmax_pv(st, vts)
                st = nxt
                yield

        def sequence(s):
            chunks = range(s * cps, (s + 1) * cps)
            for c in chunks:
                yield from phase_a(c)
            for c in chunks:
                yield from phase_b(c)
            for c in chunks:
                yield from phase_c(c)
            yield from phase_d(s)
            yield from phase_e(s)
            for c in chunks:
                yield from f_build(c)
                yield from f_project(c)

        _interleave_staggered([sequence(s) for s in range(n_seq)], SEQ_STAGGER)


def _const_spec(shape):
    nd = len(shape)
    return pl.BlockSpec(shape, lambda i, _n=nd: (0,) * _n,
                        pipeline_mode=pl.Buffered(1))


def _layer_spec(arr, layer):
    nd = arr.ndim - 1
    return pl.BlockSpec((None,) + arr.shape[1:], lambda i, _n=nd: (layer,) + (0,) * _n,
                        pipeline_mode=pl.Buffered(1))


def _mix_call(x2d, mods, layer, lw, *, seq, n_seq, latent, extra=None):
    n_tok = x2d.shape[0]
    T = seq * n_seq
    grid = (n_tok // T,)
    n_keys_buf = T if not latent else seq + 256

    if latent:
        mod_map = lambda i: (layer, i, 0, 0)
    else:
        mod_map = lambda i: (layer, 4, 0, 0)

    consts = [lw["norm1_g"], lw["w_in"], lw["conv_w"], lw["conv_b"], lw["lru_conv_w"],
              lw["lru_conv_b"], lw["w_gates"], lw["b_gates"], lw["lru_lambda"],
              lw["mla_qnorm_g"], lw["wq_t"], lw["mla_kvnorm_g"], lw["wk"], lw["wv_t"],
              lw["qg_t"], lw["kg"], lw["gnorm_conv"], lw["gnorm_lru"], lw["gnorm_mla"],
              lw["w_out"]]
    args = [x2d, mods] + consts
    in_specs = [pl.BlockSpec((T, D_MODEL), lambda i: (i, 0)),
                pl.BlockSpec((None, None, 1, 6 * D_MODEL), mod_map)]
    in_specs += [_layer_spec(a, layer) for a in consts]
    if latent:
        lconsts = [lw["wq_partner_t"], lw["qg_partner_t"], lw["kg_partner"]]
        tables = [extra["rope_ct"], extra["rope_st"], extra["rope_c"], extra["rope_s"]]
        args += lconsts + tables
        in_specs += [_layer_spec(a, layer) for a in lconsts]
        in_specs += [_const_spec(a.shape) for a in tables]
        args += [extra["cache_ckv"], extra["cache_kr"], extra["state"]]
        in_specs += [
            pl.BlockSpec((None, None, 256, KV_LORA), lambda i: (i, layer, 0, 0)),
            pl.BlockSpec((None, None, 256, LANES), lambda i: (i, layer, 0, 0)),
            pl.BlockSpec((None, None, 2, LRU_W), lambda i: (i, layer, 0, 0)),
        ]

    out_shape = [jax.ShapeDtypeStruct((n_tok, D_MODEL), F32)]
    out_specs = [pl.BlockSpec((T, D_MODEL), lambda i: (i, 0))]
    if not latent:
        out_shape += [jax.ShapeDtypeStruct((n_tok, KV_LORA), F32),
                      jax.ShapeDtypeStruct((n_tok, QK_ROPE), F32),
                      jax.ShapeDtypeStruct((n_tok // seq, 2, LRU_W), F32)]
        out_specs += [pl.BlockSpec((T, KV_LORA), lambda i: (i, 0)),
                      pl.BlockSpec((T, QK_ROPE), lambda i: (i, 0)),
                      pl.BlockSpec((n_seq, 2, LRU_W), lambda i: (i, 0, 0))]

    scratch = [
        pltpu.VMEM((T + 2 * PAD_ROWS, U_W), F32),
        pltpu.VMEM((MLA_HEADS, T // ROW_CHUNK, HEAD_PAD, ROW_CHUNK), BF16),
        pltpu.VMEM((MLA_HEADS, n_keys_buf, HEAD_PAD), BF16),
        pltpu.VMEM((n_keys_buf // ROW_CHUNK, MLA_HEADS * V_EXT, ROW_CHUNK), BF16),
        pltpu.VMEM((T // ROW_CHUNK, MLA_HEADS * V_DIM, ROW_CHUNK), F32),
        pltpu.VMEM((T, D_MODEL), BF16),
    ]
    if latent:
        scratch.append(pltpu.VMEM((2, 2, n_keys_buf, ROW_CHUNK), F32))
    return pl.pallas_call(
        functools.partial(_mix_kernel, seq=seq, n_seq=n_seq, latent=latent),
        grid=grid,
        in_specs=in_specs,
        out_specs=out_specs,
        out_shape=out_shape,
        scratch_shapes=scratch,
        compiler_params=pltpu.CompilerParams(
            dimension_semantics=("arbitrary",),
            vmem_limit_bytes=VMEM_LIMIT_BYTES),
        name="mix_latent" if latent else "mix_context",
    )(*args)


def _ffn_kernel(h_ref, mod_ref, n2g_ref, wg_ref, wu_ref, wd_ref, o_ref):
    sh2 = mod_ref[:, 3 * D_MODEL:4 * D_MODEL]
    sc2 = mod_ref[:, 4 * D_MODEL:5 * D_MODEL]
    g2 = mod_ref[:, 5 * D_MODEL:6 * D_MODEL]
    h = h_ref[...]
    hn = h * _rms_scale(h, D_MODEL) * n2g_ref[...]
    hn = (hn * (1.0 + sc2) + sh2).astype(BF16)
    acc = None
    for c0 in range(0, FF, FF_CHUNK):
        g = _dot(hn, wg_ref[:, c0:c0 + FF_CHUNK])
        u = _dot(hn, wu_ref[:, c0:c0 + FF_CHUNK])
        act = (g * _sigmoid(g) * u).astype(BF16)
        part = _dot(act, wd_ref[c0:c0 + FF_CHUNK, :])
        acc = part if acc is None else acc + part
    o_ref[...] = h + g2 * acc


def _ffn_call(h2d, mods, layer, lw, *, rows_per_mod, mod_base):
    n_tok = h2d.shape[0]
    if mod_base is None:
        tiles_per_mod = rows_per_mod // FFN_ROWS
        mod_map = lambda i: (layer, i // tiles_per_mod, 0, 0)
    else:
        mod_map = lambda i: (layer, mod_base, 0, 0)
    consts = [lw["norm2_g"], lw["w_gate"], lw["w_up"], lw["w_down"]]
    return pl.pallas_call(
        _ffn_kernel,
        grid=(n_tok // FFN_ROWS,),
        in_specs=[pl.BlockSpec((FFN_ROWS, D_MODEL), lambda i: (i, 0)),
                  pl.BlockSpec((None, None, 1, 6 * D_MODEL), mod_map)]
        + [_layer_spec(a, layer) for a in consts],
        out_specs=pl.BlockSpec((FFN_ROWS, D_MODEL), lambda i: (i, 0)),
        out_shape=jax.ShapeDtypeStruct((n_tok, D_MODEL), F32),
        compiler_params=pltpu.CompilerParams(
            dimension_semantics=("arbitrary",),
            vmem_limit_bytes=VMEM_LIMIT_BYTES),
        name="swiglu",
    )(h2d, mods, *consts)


def _rope_partner_index():
    return [d + 8 if (d % 16) < 8 else d - 8 for d in range(QK_ROPE)]


def _prep_weights(p):
    partner = jnp.array(_rope_partner_index(), jnp.int32)
    q_scale = QK_DIM ** -0.5 * math.log2(math.e)
    w_in = p["w_in"].astype(BF16)
    w_kr = w_in[:, :, C_KR:C_KR + QK_ROPE]
    zeros32 = jnp.zeros((DEPTH, D_MODEL, QK_ROPE), BF16)
    w_in_aug = jnp.concatenate(
        [w_in[:, :, :C_KR], w_kr[:, :, partner], zeros32, w_kr, zeros32], axis=2)

    wq_h = p["mla_wq_up"].reshape(DEPTH, Q_LORA, MLA_HEADS, QK_DIM)
    wq_pad = jnp.pad(wq_h, ((0, 0), (0, 0), (0, 0), (0, HEAD_PAD - QK_DIM)))
    wq_t = wq_pad.reshape(DEPTH, Q_LORA, MLA_HEADS * HEAD_PAD).transpose(0, 2, 1)
    rope_pad = ((0, 0), (0, 0), (0, 0), (ROPE_LO, HEAD_PAD - ROPE_LO - QK_ROPE))
    wq_partner = jnp.pad(wq_h[:, :, :, QK_NOPE:][:, :, :, partner], rope_pad)
    wq_partner_t = wq_partner.reshape(DEPTH, Q_LORA, MLA_HEADS * HEAD_PAD).transpose(0, 2, 1)

    wkv = p["mla_wkv_up"].reshape(DEPTH, KV_LORA, MLA_HEADS, QK_NOPE + V_DIM)
    wk_pad = jnp.pad(wkv[:, :, :, :QK_NOPE], ((0, 0), (0, 0), (0, 0), (0, HEAD_PAD - QK_NOPE)))
    wk_pad = wk_pad.reshape(DEPTH, KV_LORA, MLA_HEADS * HEAD_PAD)
    wv_ext = jnp.pad(wkv[:, :, :, QK_NOPE:], ((0, 0), (0, 0), (0, 0), (0, V_EXT - V_DIM)))
    wv_t = wv_ext.reshape(DEPTH, KV_LORA, MLA_HEADS * V_EXT).transpose(0, 2, 1)

    def pad_gain(g):
        return jnp.pad(g, ((0, 0), (0, HEAD_PAD - QK_DIM)))[:, None, :]

    def partner_gain(g):
        gp = g[:, QK_NOPE:][:, partner]
        return jnp.pad(gp, ((0, 0), (ROPE_LO, HEAD_PAD - ROPE_LO - QK_ROPE)))[:, None, :]

    def on_sublanes(g):
        return jnp.broadcast_to(g.transpose(0, 2, 1), (DEPTH, HEAD_PAD, ROW_CHUNK))

    qg, kg = p["q_norm_g"], p["k_norm_g"]
    eye = jnp.eye(LRU_BLOCKS, dtype=F32)
    gate_w = jnp.stack([p["lru_wa"][:, 0], p["lru_wi"][:, 0],
                        p["lru_wa"][:, 1], p["lru_wi"][:, 1]], axis=1)
    w_gates = (gate_w[:, :, :, :, None, :] * eye[None, None, :, None, :, None])
    w_gates = w_gates.transpose(0, 2, 3, 1, 4, 5).reshape(DEPTH, LRU_W, 4 * LRU_W)
    b_gates = jnp.stack([p["lru_ba"][:, 0], p["lru_bi"][:, 0],
                         p["lru_ba"][:, 1], p["lru_bi"][:, 1]], axis=1)

    def row(v):
        return v[:, None, :]

    return {
        "norm1_g": row(p["norm1_g"]),
        "w_in": w_in_aug,
        "conv_w": p["conv_w"],
        "conv_b": row(p["conv_b"]),
        "lru_conv_w": p["lru_conv_w"],
        "lru_conv_b": row(p["lru_conv_b"]),
        "w_gates": (0.5 * w_gates).astype(BF16),
        "b_gates": 0.5 * b_gates.reshape(DEPTH, 1, 4 * LRU_W),
        "lru_lambda": p["lru_lambda"],
        "mla_qnorm_g": row(p["mla_qnorm_g"]),
        "wq_t": wq_t.astype(BF16),
        "wq_partner_t": wq_partner_t.astype(BF16),
        "mla_kvnorm_g": row(p["mla_kvnorm_g"]),
        "wk": wk_pad.astype(BF16),
        "wv_t": wv_t.astype(BF16),
        "qg_t": on_sublanes(pad_gain(qg) * q_scale),
        "qg_partner_t": on_sublanes(partner_gain(qg) * q_scale),
        "kg": pad_gain(kg),
        "kg_partner": partner_gain(kg),
        "gnorm_conv": row(p["gnorm_conv"]),
        "gnorm_lru": row(p["gnorm_lru"]),
        "gnorm_mla": row(p["gnorm_mla"]),
        "w_out": p["w_out"].astype(BF16),
        "norm2_g": row(p["norm2_g"]),
        "w_gate": p["w_gate"].astype(BF16),
        "w_up": p["w_up"].astype(BF16),
        "w_down": p["w_down"].astype(BF16),
    }


def _rope_tables(n_tokens):
    n_rows = n_tokens // GRID_W
    row = np.repeat(np.arange(n_rows, dtype=np.float32), GRID_W)
    col = np.tile(np.arange(GRID_W, dtype=np.float32), n_rows)
    n_freq = QK_ROPE // 4
    inv_freq = np.power(np.float32(ROPE_THETA),
                        -np.arange(n_freq, dtype=np.float32) / np.float32(n_freq))
    ang_r = (row[:, None] * inv_freq).astype(np.float32)
    ang_c = (col[:, None] * inv_freq).astype(np.float32)
    cos = np.concatenate([np.cos(ang_r), np.cos(ang_r), np.cos(ang_c), np.cos(ang_c)], 1)
    sin = np.concatenate([-np.sin(ang_r), np.sin(ang_r), -np.sin(ang_c), np.sin(ang_c)], 1)
    ones = np.ones((n_tokens, QK_NOPE), np.float32)
    tail = np.zeros((n_tokens, HEAD_PAD - QK_DIM), np.float32)
    rope_c = np.concatenate([ones, cos, tail], axis=1).astype(np.float32)
    rope_s = np.concatenate([np.zeros((n_tokens, QK_NOPE), np.float32), sin, tail],
                            axis=1).astype(np.float32)
    return rope_c, rope_s


def kernel(x_prompt, x_sample, cache_ckv, cache_krope, state_lru, c, c_ctx, norm1_g, ada_w, ada_b, w_in, conv_w, conv_b, lru_conv_w, lru_conv_b, lru_wa, lru_ba, lru_wi, lru_bi, lru_lambda, mla_qnorm_g, mla_wq_up, mla_kvnorm_g, mla_wkv_up, q_norm_g, k_norm_g, gnorm_conv, gnorm_lru, gnorm_mla, w_out, norm2_g, w_gate, w_up, w_down):
    params = dict(norm1_g=norm1_g, w_in=w_in, conv_w=conv_w, conv_b=conv_b,
                  lru_conv_w=lru_conv_w, lru_conv_b=lru_conv_b, lru_wa=lru_wa,
                  lru_ba=lru_ba, lru_wi=lru_wi, lru_bi=lru_bi, lru_lambda=lru_lambda,
                  mla_qnorm_g=mla_qnorm_g, mla_wq_up=mla_wq_up,
                  mla_kvnorm_g=mla_kvnorm_g, mla_wkv_up=mla_wkv_up, q_norm_g=q_norm_g,
                  k_norm_g=k_norm_g, gnorm_conv=gnorm_conv, gnorm_lru=gnorm_lru,
                  gnorm_mla=gnorm_mla, w_out=w_out, norm2_g=norm2_g, w_gate=w_gate,
                  w_up=w_up, w_down=w_down)
    batch, seq_p, _ = x_prompt.shape
    dec_batch, seq_s, _ = x_sample.shape
    assert dec_batch == 4 and cache_ckv.shape[2] == 256

    cond8 = jnp.concatenate(
        [c, c_ctx[None, :], jnp.zeros((8 - dec_batch - 1, D_MODEL), F32)], axis=0)
    mods = _ada_call(cond8, ada_w, ada_b).reshape(DEPTH, 8, 1, 6 * D_MODEL)

    rope_c, rope_s = _rope_tables(seq_s)
    cache_kr_pad = jnp.pad(
        cache_krope, ((0, 0), (0, 0), (0, 0), (ROPE_LO, HEAD_PAD - ROPE_LO - QK_ROPE)))

    def chunked_t(tab):
        return jnp.asarray(np.ascontiguousarray(
            tab.reshape(seq_s // ROW_CHUNK, ROW_CHUNK, HEAD_PAD).transpose(0, 2, 1)))

    extra = dict(rope_c=jnp.asarray(rope_c), rope_s=jnp.asarray(rope_s),
                 rope_ct=chunked_t(rope_c), rope_st=chunked_t(rope_s), cache_ckv=cache_ckv,
                 cache_kr=cache_kr_pad, state=state_lru)

    xp = x_prompt.reshape(batch * seq_p, D_MODEL)
    xs = x_sample.reshape(dec_batch * seq_s, D_MODEL)
    ckv_l, kr_l, st_l = [], [], []
    lw = _prep_weights(params)
    for l in range(DEPTH):
        hp, ckv, kr, st = _mix_call(xp, mods, l, lw, seq=seq_p, n_seq=CTX_SEQS_PER_STEP,
                                    latent=False)
        xp = _ffn_call(hp, mods, l, lw, rows_per_mod=None, mod_base=4)
        ckv_l.append(ckv.reshape(batch, seq_p, KV_LORA))
        kr_l.append(kr.reshape(batch, seq_p, QK_ROPE))
        st_l.append(st)
        (hs,) = _mix_call(xs, mods, l, lw, seq=seq_s, n_seq=1, latent=True, extra=extra)
        xs = _ffn_call(hs, mods, l, lw, rows_per_mod=seq_s, mod_base=None)
    return (xp.reshape(batch, seq_p, D_MODEL),
            xs.reshape(dec_batch, seq_s, D_MODEL),
            jnp.stack(ckv_l, axis=1),
            jnp.stack(kr_l, axis=1),
            jnp.stack(st_l, axis=1))
```

```python
import functools
import itertools
import math

import jax
import jax.numpy as jnp
import numpy as np
from jax import lax
from jax.experimental import pallas as pl
from jax.experimental.pallas import tpu as pltpu

F32 = jnp.float32
BF16 = jnp.bfloat16

D_MODEL = 1024
DEPTH = 2
GRID_W = 64
CONV_W = 256
LRU_W = 256
LRU_BLOCKS = 4
LRU_BLK = 64
LRU_C = 8.0
MLA_HEADS = 8
QK_NOPE = 64
QK_ROPE = 32
V_DIM = 64
V_EXT = V_DIM + 16
QK_DIM = QK_NOPE + QK_ROPE
Q_LORA = 256
KV_LORA = 128
ROPE_THETA = 10000.0
FF = 2816
EPS = 1e-6

LANES = 128
SUBLANES = 8
HEAD_PAD = LANES
VMEM_LIMIT_BYTES = 60 * 1024 * 1024

C_BG, C_CG, C_H = 0, 256, 512
C_XB, C_YB = 768, 1024
C_Q = 1280
C_CKV = 1536
C_KR = 1664
U_W = 1792
SCAN_A = (C_H, C_CKV)
SCAN_B = (C_Q, C_BG)
ROPE_LO = QK_NOPE

ROW_CHUNK = 256
SCAN_CHUNK = 32
Q_CHUNK = 256
FF_CHUNK = 256
PROJ_SLAB = 256
CTX_SEQS_PER_STEP = 2
FUSED_FFN_PERIOD = 4
SEQ_STAGGER = 10
FFN_ROWS = 1024
PAD_ROWS = SUBLANES


def _rms_scale(x, n):
    ms = jnp.sum(x * x, axis=-1, keepdims=True) * (1.0 / n)
    return lax.rsqrt(ms + EPS)


def _sigmoid(x):
    return 0.5 * jnp.tanh(0.5 * x) + 0.5


def _gelu_tanh(x):
    c = math.sqrt(2.0 / math.pi)
    return 0.5 * x * (1.0 + jnp.tanh(c * (x + 0.044715 * (x * x * x))))


def _dot(a, b):
    return jnp.dot(a, b, preferred_element_type=F32)


def _interleave(*gens):
    live = list(gens)
    while live:
        for g in list(live):
            try:
                next(g)
            except StopIteration:
                live.remove(g)


def _interleave_staggered(gens, lag, side=None, side_period=1):
    pending = list(gens)
    live = []
    tick = 0
    while live or pending or side is not None:
        if pending and tick % lag == 0:
            live.append(pending.pop(0))
        if side is not None and (tick % side_period == 0 or not (live or pending)):
            try:
                next(side)
            except StopIteration:
                side = None
        for g in list(live):
            try:
                next(g)
            except StopIteration:
                live.remove(g)
        tick += 1


def _dot_nt(a, b):
    return lax.dot_general(a, b, (((1,), (1,)), ((), ())), preferred_element_type=F32)


ADA_TN = 1536


def _ada_kernel(cond_ref, w_ref, b_ref, o_ref):
    cnd = cond_ref[...]
    s = (cnd * _sigmoid(cnd)).astype(BF16)
    o_ref[...] = _dot(s, w_ref[...].astype(BF16)) + b_ref[...]


def _ada_call(cond8, ada_w, ada_b):
    n_out = 6 * D_MODEL
    return pl.pallas_call(
        _ada_kernel,
        grid=(DEPTH, n_out // ADA_TN),
        in_specs=[
            pl.BlockSpec((8, D_MODEL), lambda l, j: (0, 0)),
            pl.BlockSpec((None, D_MODEL, ADA_TN), lambda l, j: (l, 0, j)),
            pl.BlockSpec((None, 1, ADA_TN), lambda l, j: (l, 0, j)),
        ],
        out_specs=pl.BlockSpec((None, 8, ADA_TN), lambda l, j: (l, 0, j)),
        out_shape=jax.ShapeDtypeStruct((DEPTH, 8, n_out), F32),
        compiler_params=pltpu.CompilerParams(
            dimension_semantics=("arbitrary", "arbitrary"),
            vmem_limit_bytes=VMEM_LIMIT_BYTES),
        name="ada_mod",
    )(cond8, ada_w, ada_b.reshape(DEPTH, 1, n_out))


def _mix_kernel(*refs, seq, n_seq, latent, fused_ffn):
    T = seq * n_seq
    n_ctx = 256 if latent else 0
    (x_ref, mod_ref, n1g_ref, win_ref, convw_ref, convb_ref, lcw_ref, lcb_ref,
     wg_ref, bg_ref, lam_ref, qng_ref, wqt_ref, kvng_ref, wk_ref, wvt_ref,
     qgt_ref, kg_ref, gnc_ref, gnl_ref, gnm_ref, wout_ref) = refs[:22]
    pos = 22
    if latent:
        (wqpt_ref, qgpt_ref, kgp_ref, ropect_ref, ropest_ref, ropec_ref, ropes_ref,
         cckv_ref, ckr_ref, h0_ref) = refs[pos:pos + 10]
        pos += 10
    if fused_ffn:
        ffn_in_refs = refs[pos:pos + 6]
        pos += 6
    h_out_ref = refs[pos]
    pos += 1
    if not latent:
        ckv_out_ref, kr_out_ref, st_out_ref = refs[pos:pos + 3]
        pos += 3
    if fused_ffn:
        ffn_out_ref = refs[pos]
        pos += 1
    u_scr, qt_scr, k_scr, vt_scr, ymt_scr, ycat_scr = refs[pos:pos + 6]
    if latent:
        st_scr = refs[pos + 6]

    n_chunks = T // ROW_CHUNK
    lane = lax.broadcasted_iota(jnp.int32, (1, LANES), 1)
    rope_lanes = (lane >= ROPE_LO) & (lane < ROPE_LO + QK_ROPE)

    sh1 = mod_ref[:, 0:D_MODEL]
    sc1 = mod_ref[:, D_MODEL:2 * D_MODEL]
    g1 = mod_ref[:, 2 * D_MODEL:3 * D_MODEL]
    gain1 = n1g_ref[...] * (1.0 + sc1)

    u_scr[0:PAD_ROWS, :] = jnp.zeros((PAD_ROWS, U_W), F32)
    u_scr[PAD_ROWS + T:2 * PAD_ROWS + T, :] = jnp.zeros((PAD_ROWS, U_W), F32)

    def phase_a(c):
        r0 = c * ROW_CHUNK
        x = x_ref[r0:r0 + ROW_CHUNK, :]
        hn = (x * _rms_scale(x, D_MODEL) * gain1 + sh1).astype(BF16)
        yield
        urows = slice(PAD_ROWS + r0, PAD_ROWS + r0 + ROW_CHUNK)
        for s0 in range(0, U_W, PROJ_SLAB):
            u_scr[urows, s0:s0 + PROJ_SLAB] = _dot(hn, win_ref[:, s0:s0 + PROJ_SLAB])
            yield
        u_scr[urows, C_CG:C_CG + CONV_W] = (
            u_scr[urows, C_CG:C_CG + CONV_W] * u_scr[urows, C_H:C_H + CONV_W])
        yield

    vrow = lax.broadcasted_iota(jnp.int32, (MLA_HEADS * V_EXT, 1), 0)
    ones_rows = (vrow % V_EXT) >= V_DIM

    def values_t(cb):
        return jnp.where(ones_rows, 1.0, _dot_nt(wvt_ref[...], cb)).astype(BF16)

    def phase_b(c):
        r0 = c * ROW_CHUNK
        rows = slice(PAD_ROWS + r0, PAD_ROWS + r0 + ROW_CHUNK)
        uq = u_scr[rows, C_Q:C_Q + Q_LORA]
        qn = (uq * _rms_scale(uq, Q_LORA) * qng_ref[...]).astype(BF16)
        qa_t = _dot_nt(wqt_ref[...], qn)
        if latent:
            qpa_t = _dot_nt(wqpt_ref[...], qn)
            q_tc = ropect_ref[c] * qgt_ref[...]
            q_ts = ropest_ref[c] * qgpt_ref[...]
        yield
        for h in range(MLA_HEADS):
            sl = slice(h * HEAD_PAD, (h + 1) * HEAD_PAD)
            qh = qa_t[sl, :]
            rq = lax.rsqrt(jnp.sum(qh * qh, axis=0, keepdims=True) * (1.0 / QK_DIM) + EPS)
            if latent:
                qh = (qh * q_tc + qpa_t[sl, :] * q_ts) * rq
            else:
                qh = qh * qgt_ref[...] * rq
            qt_scr[h, c] = qh.astype(BF16)
            if h % 2:
                yield
        uc = u_scr[rows, C_CKV:C_CKV + KV_LORA]
        ckv = uc * _rms_scale(uc, KV_LORA) * kvng_ref[...]
        krb = u_scr[rows, C_KR:C_KR + LANES]
        kr_rolled = pltpu.roll(krb, 64, 1)
        if not latent:
            ckv_out_ref[r0:r0 + ROW_CHUNK, :] = ckv
            kr_out_ref[r0:r0 + ROW_CHUNK, :] = kr_rolled[:, 0:QK_ROPE]
        krm = jnp.where(rope_lanes, krb, 0.0)
        cb = ckv.astype(BF16)
        ka = _dot(cb, wk_ref[...])
        vt_scr[c] = values_t(cb)
        if latent:
            k_tc = ropec_ref[r0:r0 + ROW_CHUNK, :] * kg_ref[...]
            k_ts = ropes_ref[r0:r0 + ROW_CHUNK, :] * kgp_ref[...]
        yield
        for h in range(MLA_HEADS):
            sl = slice(h * HEAD_PAD, (h + 1) * HEAD_PAD)
            kpre = ka[:, sl] + krm
            rk = _rms_scale(kpre, QK_DIM)
            if latent:
                kh = (kpre * k_tc + kr_rolled * k_ts) * rk
            else:
                kh = kpre * kg_ref[...] * rk
            k_scr[h, r0:r0 + ROW_CHUNK, :] = kh.astype(BF16)
            if h % 2:
                yield

    if latent:
        cc = cckv_ref[...].astype(BF16)
        ka = _dot(cc, wk_ref[...])
        vt_scr[n_chunks] = values_t(cc)
        krc = ckr_ref[...]
        for h in range(MLA_HEADS):
            kpre = ka[:, h * HEAD_PAD:(h + 1) * HEAD_PAD] + krc
            kh = kpre * kg_ref[...] * _rms_scale(kpre, QK_DIM)
            k_scr[h, seq:seq + n_ctx, :] = kh.astype(BF16)

    neg_lam = -lam_ref[...]
    sp = jnp.maximum(neg_lam, 0.0) + jnp.log(1.0 + jnp.exp(-jnp.abs(neg_lam)))
    log2a_coef = (-0.5 * LRU_C * math.log2(math.e)) * sp
    row_rc = lax.broadcasted_iota(jnp.int32, (ROW_CHUNK, 1), 0)

    def phase_c(ci):
        r0 = ci * ROW_CHUNK
        base = PAD_ROWS + r0
        first = (r0 % seq) == 0
        last = ((r0 + ROW_CHUNK) % seq) == 0

        def win(col, shift, width=CONV_W):
            w = u_scr[base + shift:base + shift + ROW_CHUNK, col:col + width]
            if shift < 0 and first:
                w = jnp.where(row_rc < -shift, 0.0, w)
            if shift > 0 and last:
                w = jnp.where(row_rc >= ROW_CHUNK - shift, 0.0, w)
            return w

        z_m, z_0, z_p = win(C_CG, -1), win(C_CG, 0), win(C_CG, 1)
        conv = (z_m * convw_ref[0:1, :] + z_0 * convw_ref[1:2, :]
                + z_p * convw_ref[2:3, :] + convb_ref[...])
        yc = u_scr[base:base + ROW_CHUNK, C_BG:C_BG + CONV_W] * conv
        ycn = yc * _rms_scale(yc, CONV_W) * gnc_ref[...]
        ycat_scr[r0:r0 + ROW_CHUNK, 0:CONV_W] = ycn.astype(BF16)
        yield

        xc = (win(C_XB, -2) * lcw_ref[0:1, :] + win(C_XB, -1) * lcw_ref[1:2, :]
              + win(C_XB, 0) * lcw_ref[2:3, :] + win(C_XB, 1) * lcw_ref[3:4, :]
              + lcb_ref[...])
        half_gates = _dot(xc.astype(BF16), wg_ref[...]) + bg_ref[...]
        half_xc = 0.5 * xc
        yield
        for d in range(2):
            t_r = jnp.tanh(half_gates[:, (2 * d) * LRU_W:(2 * d + 1) * LRU_W])
            t_i = jnp.tanh(half_gates[:, (2 * d + 1) * LRU_W:(2 * d + 2) * LRU_W])
            a = jnp.exp2(log2a_coef[d:d + 1, :] * (t_r + 1.0))
            v = 1.0 - a * a
            mult = v * lax.rsqrt(jnp.maximum(v, 1e-30))
            u_scr[base:base + ROW_CHUNK, SCAN_A[d]:SCAN_A[d] + LRU_W] = a
            u_scr[base:base + ROW_CHUNK, SCAN_B[d]:SCAN_B[d] + LRU_W] = (
                mult * (half_xc * (t_i + 1.0)))
            yield

    n_sc = seq // SCAN_CHUNK
    row_sc = lax.broadcasted_iota(jnp.int32, (SCAN_CHUNK, LRU_W), 0)
    steps = [1 << i for i in range(int(math.log2(SCAN_CHUNK)))]

    def scan_step(rf, rb, cf, cb_):
        rf = pl.multiple_of(rf + PAD_ROWS, SUBLANES)
        rb = pl.multiple_of(rb + PAD_ROWS, SUBLANES)
        a = u_scr[pl.ds(rf, SCAN_CHUNK), SCAN_A[0]:SCAN_A[0] + LRU_W]
        b = u_scr[pl.ds(rf, SCAN_CHUNK), SCAN_B[0]:SCAN_B[0] + LRU_W]
        b = b + jnp.where(row_sc == 0, a * cf, 0.0)
        for d in steps:
            b = b + a * jnp.where(row_sc >= d, pltpu.roll(b, d, 0), 0.0)
            if d != steps[-1]:
                a = a * jnp.where(row_sc >= d, pltpu.roll(a, d, 0), 1.0)
        u_scr[pl.ds(rf, SCAN_CHUNK), SCAN_B[0]:SCAN_B[0] + LRU_W] = b
        cf = b[SCAN_CHUNK - 1:SCAN_CHUNK, :]
        a = u_scr[pl.ds(rb, SCAN_CHUNK), SCAN_A[1]:SCAN_A[1] + LRU_W]
        b = u_scr[pl.ds(rb, SCAN_CHUNK), SCAN_B[1]:SCAN_B[1] + LRU_W]
        b = b + jnp.where(row_sc == SCAN_CHUNK - 1, a * cb_, 0.0)
        for d in steps:
            keep = row_sc < SCAN_CHUNK - d
            b = b + a * jnp.where(keep, pltpu.roll(b, SCAN_CHUNK - d, 0), 0.0)
            if d != steps[-1]:
                a = a * jnp.where(keep, pltpu.roll(a, SCAN_CHUNK - d, 0), 1.0)
        u_scr[pl.ds(rb, SCAN_CHUNK), SCAN_B[1]:SCAN_B[1] + LRU_W] = b
        cb_ = b[0:1, :]
        return cf, cb_

    cps = seq // ROW_CHUNK

    def softmax_pv(st, vts):
        m = jnp.max(st, axis=0, keepdims=True)
        pt = jnp.exp2(st - m).astype(BF16)
        ot = None
        for i, vt in enumerate(vts):
            part = _dot(vt, pt[i * ROW_CHUNK:(i + 1) * ROW_CHUNK, :])
            ot = part if ot is None else ot + part
        return ot[0:V_DIM, :] * (1.0 / ot[V_DIM:V_DIM + 1, :])

    def f_build(c):
        r0 = c * ROW_CHUNK
        rows = slice(r0, r0 + ROW_CHUNK)
        urows = slice(PAD_ROWS + r0, PAD_ROWS + r0 + ROW_CHUNK)
        hf = u_scr[urows, SCAN_B[0]:SCAN_B[0] + LRU_W]
        hb = u_scr[urows, SCAN_B[1]:SCAN_B[1] + LRU_W]
        yb = u_scr[urows, C_YB:C_YB + LRU_W]
        yl = (hf + hb) * _gelu_tanh(yb)
        yln = yl * _rms_scale(yl, LRU_W) * gnl_ref[...]
        ycat_scr[rows, CONV_W:CONV_W + LRU_W] = yln.astype(BF16)
        yield
        ymt = ymt_scr[c]
        ssq = jnp.sum(ymt * ymt, axis=0, keepdims=True)
        ynt = ymt * lax.rsqrt(ssq * (1.0 / (MLA_HEADS * V_DIM)) + EPS)
        ycat_scr[rows, CONV_W + LRU_W:D_MODEL] = (ynt.T * gnm_ref[...]).astype(BF16)
        yield

    def f_project(c):
        rows = slice(c * ROW_CHUNK, (c + 1) * ROW_CHUNK)
        for n0 in range(0, D_MODEL, PROJ_SLAB):
            cols = slice(n0, n0 + PROJ_SLAB)
            y = _dot(ycat_scr[rows, :], wout_ref[:, cols])
            h_out_ref[rows, cols] = x_ref[rows, cols] + g1[:, cols] * y
            yield

    if latent:
        _interleave(phase_a(0))
        for c in range(n_chunks):
            dependent = [phase_b(c)] + ([phase_c(c - 1)] if c else [])
            ahead = [phase_a(c + 1)] if c + 1 < n_chunks else []
            _interleave(*ahead, itertools.chain(*dependent))
        _interleave(phase_c(n_chunks - 1))

        key_chunks = list(range(cps)) + [n_chunks]
        n_blk = cps * (MLA_HEADS // 2)
        trips = n_blk // 2
        scans_per_trip = n_sc // trips

        def heads_of(idx):
            qc = idx // (MLA_HEADS // 2)
            j = idx % (MLA_HEADS // 2)
            return qc, [2 * j, 2 * j + 1]

        def scores(idx, slot):
            qc, hs = heads_of(idx)
            for par, h in enumerate(hs):
                st_scr[slot, par] = _dot(k_scr[h], qt_scr[h, qc])

        def finish(idx, slot):
            qc, hs = heads_of(idx)
            for par, h in enumerate(hs):
                hr = pl.ds(pl.multiple_of(h * V_DIM, V_DIM), V_DIM)
                vr = pl.ds(pl.multiple_of(h * V_EXT, V_EXT - V_DIM), V_EXT)
                vts = [vt_scr[kc, vr, :] for kc in key_chunks]
                ymt_scr[qc, hr, :] = softmax_pv(st_scr[slot, par], vts)

        scores(0, 0)

        def att_body(i, carry):
            b0 = 2 * i
            attention = [
                lambda: scores(b0 + 1, 1),
                lambda: finish(b0, 0),
                lambda: scores(jnp.minimum(b0 + 2, n_blk - 1), 0),
                lambda: finish(b0 + 1, 1),
            ]
            for k in range(max(len(attention), scans_per_trip)):
                if k < len(attention):
                    attention[k]()
                if k < scans_per_trip:
                    c = i * scans_per_trip + k
                    rf = pl.multiple_of(c * SCAN_CHUNK, SCAN_CHUNK)
                    rb = pl.multiple_of((n_sc - 1 - c) * SCAN_CHUNK, SCAN_CHUNK)
                    carry = scan_step(rf, rb, *carry)
            return carry

        lax.fori_loop(0, trips, att_body, (h0_ref[0:1, :], h0_ref[1:2, :]))

        _interleave(f_build(0))
        for c in range(n_chunks):
            ahead = [f_build(c + 1)] if c + 1 < n_chunks else []
            _interleave(f_project(c), *ahead)
    else:
        def phase_d(s):
            carry = (jnp.zeros((1, LRU_W), F32), jnp.zeros((1, LRU_W), F32))
            for c in range(n_sc):
                carry = scan_step(s * seq + c * SCAN_CHUNK,
                                  s * seq + (n_sc - 1 - c) * SCAN_CHUNK, *carry)
                yield
            st_out_ref[s, 0:1, :] = carry[0]
            st_out_ref[s, 1:2, :] = carry[1]

        def phase_e(s):
            blocks = [(qi, h) for qi in range(cps) for h in range(MLA_HEADS)]

            def scores(blk):
                qi, h = blk
                return _dot(k_scr[h, s * seq:(s + 1) * seq, :], qt_scr[h, s * cps + qi])

            st = scores(blocks[0])
            for i, (qi, h) in enumerate(blocks):
                nxt = scores(blocks[i + 1]) if i + 1 < len(blocks) else None
                hr = slice(h * V_DIM, (h + 1) * V_DIM)
                vr = slice(h * V_EXT, (h + 1) * V_EXT)
                vts = [vt_scr[s * cps + j, vr, :] for j in range(cps)]
                ymt_scr[s * cps + qi, hr, :] = softmax_pv(st, vts)
                st = nxt
                yield

        def sequence(s):
            chunks = range(s * cps, (s + 1) * cps)
            for c in chunks:
                yield from phase_a(c)
            for c in chunks:
                yield from phase_b(c)
            for c in chunks:
                yield from phase_c(c)
            yield from phase_d(s)
            yield from phase_e(s)
            for c in chunks:
                yield from f_build(c)
                yield from f_project(c)

        side = _ffn_steps(*ffn_in_refs, ffn_out_ref) if fused_ffn else None
        _interleave_staggered([sequence(s) for s in range(n_seq)], SEQ_STAGGER,
                              side=side, side_period=FUSED_FFN_PERIOD)


def _const_spec(shape):
    nd = len(shape)
    return pl.BlockSpec(shape, lambda i, _n=nd: (0,) * _n,
                        pipeline_mode=pl.Buffered(1))


def _layer_spec(arr, layer):
    nd = arr.ndim - 1
    return pl.BlockSpec((None,) + arr.shape[1:], lambda i, _n=nd: (layer,) + (0,) * _n,
                        pipeline_mode=pl.Buffered(1))


def _mix_call(x2d, mods, layer, lw, *, seq, n_seq, latent, extra=None, ffn_h2d=None,
              ffn_rows_per_mod=None):
    n_tok = x2d.shape[0]
    T = seq * n_seq
    grid = (n_tok // T,)
    fused_ffn = ffn_h2d is not None
    n_keys_buf = T if not latent else seq + 256

    if latent:
        mod_map = lambda i: (layer, i, 0, 0)
    else:
        mod_map = lambda i: (layer, 4, 0, 0)

    consts = [lw["norm1_g"], lw["w_in"], lw["conv_w"], lw["conv_b"], lw["lru_conv_w"],
              lw["lru_conv_b"], lw["w_gates"], lw["b_gates"], lw["lru_lambda"],
              lw["mla_qnorm_g"], lw["wq_t"], lw["mla_kvnorm_g"], lw["wk"], lw["wv_t"],
              lw["qg_t"], lw["kg"], lw["gnorm_conv"], lw["gnorm_lru"], lw["gnorm_mla"],
              lw["w_out"]]
    args = [x2d, mods] + consts
    in_specs = [pl.BlockSpec((T, D_MODEL), lambda i: (i, 0)),
                pl.BlockSpec((None, None, 1, 6 * D_MODEL), mod_map)]
    in_specs += [_layer_spec(a, layer) for a in consts]
    if latent:
        lconsts = [lw["wq_partner_t"], lw["qg_partner_t"], lw["kg_partner"]]
        tables = [extra["rope_ct"], extra["rope_st"], extra["rope_c"], extra["rope_s"]]
        args += lconsts + tables
        in_specs += [_layer_spec(a, layer) for a in lconsts]
        in_specs += [_const_spec(a.shape) for a in tables]
        args += [extra["cache_ckv"], extra["cache_kr"], extra["state"]]
        in_specs += [
            pl.BlockSpec((None, None, 256, KV_LORA), lambda i: (i, layer, 0, 0)),
            pl.BlockSpec((None, None, 256, LANES), lambda i: (i, layer, 0, 0)),
            pl.BlockSpec((None, None, 2, LRU_W), lambda i: (i, layer, 0, 0)),
        ]

    if fused_ffn:
        ffn_rows = ffn_h2d.shape[0] // grid[0]
        tiles_per_mod = ffn_rows_per_mod // ffn_rows
        ffn_consts = [lw["norm2_g"], lw["w_gate"], lw["w_up"], lw["w_down"]]
        args += [ffn_h2d, mods] + ffn_consts
        in_specs += [pl.BlockSpec((ffn_rows, D_MODEL), lambda i: (i, 0)),
                     pl.BlockSpec((None, None, 1, 6 * D_MODEL),
                                  lambda i: (layer, i // tiles_per_mod, 0, 0))]
        in_specs += [_layer_spec(a, layer) for a in ffn_consts]

    out_shape = [jax.ShapeDtypeStruct((n_tok, D_MODEL), F32)]
    out_specs = [pl.BlockSpec((T, D_MODEL), lambda i: (i, 0))]
    if not latent:
        out_shape += [jax.ShapeDtypeStruct((n_tok, KV_LORA), F32),
                      jax.ShapeDtypeStruct((n_tok, QK_ROPE), F32),
                      jax.ShapeDtypeStruct((n_tok // seq, 2, LRU_W), F32)]
        out_specs += [pl.BlockSpec((T, KV_LORA), lambda i: (i, 0)),
                      pl.BlockSpec((T, QK_ROPE), lambda i: (i, 0)),
                      pl.BlockSpec((n_seq, 2, LRU_W), lambda i: (i, 0, 0))]
    if fused_ffn:
        out_shape.append(jax.ShapeDtypeStruct(ffn_h2d.shape, F32))
        out_specs.append(pl.BlockSpec((ffn_rows, D_MODEL), lambda i: (i, 0)))

    scratch = [
        pltpu.VMEM((T + 2 * PAD_ROWS, U_W), F32),
        pltpu.VMEM((MLA_HEADS, T // ROW_CHUNK, HEAD_PAD, ROW_CHUNK), BF16),
        pltpu.VMEM((MLA_HEADS, n_keys_buf, HEAD_PAD), BF16),
        pltpu.VMEM((n_keys_buf // ROW_CHUNK, MLA_HEADS * V_EXT, ROW_CHUNK), BF16),
        pltpu.VMEM((T // ROW_CHUNK, MLA_HEADS * V_DIM, ROW_CHUNK), F32),
        pltpu.VMEM((T, D_MODEL), BF16),
    ]
    if latent:
        scratch.append(pltpu.VMEM((2, 2, n_keys_buf, ROW_CHUNK), F32))
    return pl.pallas_call(
        functools.partial(_mix_kernel, seq=seq, n_seq=n_seq, latent=latent,
                          fused_ffn=fused_ffn),
        grid=grid,
        in_specs=in_specs,
        out_specs=out_specs,
        out_shape=out_shape,
        scratch_shapes=scratch,
        compiler_params=pltpu.CompilerParams(
            dimension_semantics=("arbitrary",),
            vmem_limit_bytes=VMEM_LIMIT_BYTES),
        name="mix_latent" if latent else ("mix_context_swiglu" if fused_ffn else "mix_context"),
    )(*args)


def _ffn_steps(h_ref, mod_ref, n2g_ref, wg_ref, wu_ref, wd_ref, o_ref):
    sh2 = mod_ref[:, 3 * D_MODEL:4 * D_MODEL]
    sc2 = mod_ref[:, 4 * D_MODEL:5 * D_MODEL]
    g2 = mod_ref[:, 5 * D_MODEL:6 * D_MODEL]
    h = h_ref[...]
    hn = (h * _rms_scale(h, D_MODEL) * (n2g_ref[...] * (1.0 + sc2)) + sh2).astype(BF16)
    yield
    acc = None
    for c0 in range(0, FF, FF_CHUNK):
        g = _dot(hn, wg_ref[:, c0:c0 + FF_CHUNK])
        u = _dot(hn, wu_ref[:, c0:c0 + FF_CHUNK])
        act = (g * _sigmoid(g) * u).astype(BF16)
        part = _dot(act, wd_ref[c0:c0 + FF_CHUNK, :])
        acc = part if acc is None else acc + part
        yield
    o_ref[...] = h_ref[...] + g2 * acc
    yield


def _ffn_kernel(*refs):
    _interleave(_ffn_steps(*refs))


def _ffn_call(h2d, mods, layer, lw, *, rows_per_mod, mod_base):
    n_tok = h2d.shape[0]
    if mod_base is None:
        tiles_per_mod = rows_per_mod // FFN_ROWS
        mod_map = lambda i: (layer, i // tiles_per_mod, 0, 0)
    else:
        mod_map = lambda i: (layer, mod_base, 0, 0)
    consts = [lw["norm2_g"], lw["w_gate"], lw["w_up"], lw["w_down"]]
    return pl.pallas_call(
        _ffn_kernel,
        grid=(n_tok // FFN_ROWS,),
        in_specs=[pl.BlockSpec((FFN_ROWS, D_MODEL), lambda i: (i, 0)),
                  pl.BlockSpec((None, None, 1, 6 * D_MODEL), mod_map)]
        + [_layer_spec(a, layer) for a in consts],
        out_specs=pl.BlockSpec((FFN_ROWS, D_MODEL), lambda i: (i, 0)),
        out_shape=jax.ShapeDtypeStruct((n_tok, D_MODEL), F32),
        compiler_params=pltpu.CompilerParams(
            dimension_semantics=("arbitrary",),
            vmem_limit_bytes=VMEM_LIMIT_BYTES),
        name="swiglu",
    )(h2d, mods, *consts)


def _rope_partner_index():
    return [d + 8 if (d % 16) < 8 else d - 8 for d in range(QK_ROPE)]


def _prep_weights(p):
    partner = jnp.array(_rope_partner_index(), jnp.int32)
    q_scale = QK_DIM ** -0.5 * math.log2(math.e)
    w_in = p["w_in"].astype(BF16)
    w_kr = w_in[:, :, C_KR:C_KR + QK_ROPE]
    zeros32 = jnp.zeros((DEPTH, D_MODEL, QK_ROPE), BF16)
    w_in_aug = jnp.concatenate(
        [w_in[:, :, :C_KR], w_kr[:, :, partner], zeros32, w_kr, zeros32], axis=2)

    wq_h = p["mla_wq_up"].reshape(DEPTH, Q_LORA, MLA_HEADS, QK_DIM)
    wq_pad = jnp.pad(wq_h, ((0, 0), (0, 0), (0, 0), (0, HEAD_PAD - QK_DIM)))
    wq_t = wq_pad.reshape(DEPTH, Q_LORA, MLA_HEADS * HEAD_PAD).transpose(0, 2, 1)
    rope_pad = ((0, 0), (0, 0), (0, 0), (ROPE_LO, HEAD_PAD - ROPE_LO - QK_ROPE))
    wq_partner = jnp.pad(wq_h[:, :, :, QK_NOPE:][:, :, :, partner], rope_pad)
    wq_partner_t = wq_partner.reshape(DEPTH, Q_LORA, MLA_HEADS * HEAD_PAD).transpose(0, 2, 1)

    wkv = p["mla_wkv_up"].reshape(DEPTH, KV_LORA, MLA_HEADS, QK_NOPE + V_DIM)
    wk_pad = jnp.pad(wkv[:, :, :, :QK_NOPE], ((0, 0), (0, 0), (0, 0), (0, HEAD_PAD - QK_NOPE)))
    wk_pad = wk_pad.reshape(DEPTH, KV_LORA, MLA_HEADS * HEAD_PAD)
    wv_ext = jnp.pad(wkv[:, :, :, QK_NOPE:], ((0, 0), (0, 0), (0, 0), (0, V_EXT - V_DIM)))
    wv_t = wv_ext.reshape(DEPTH, KV_LORA, MLA_HEADS * V_EXT).transpose(0, 2, 1)

    def pad_gain(g):
        return jnp.pad(g, ((0, 0), (0, HEAD_PAD - QK_DIM)))[:, None, :]

    def partner_gain(g):
        gp = g[:, QK_NOPE:][:, partner]
        return jnp.pad(gp, ((0, 0), (ROPE_LO, HEAD_PAD - ROPE_LO - QK_ROPE)))[:, None, :]

    def on_sublanes(g):
        return jnp.broadcast_to(g.transpose(0, 2, 1), (DEPTH, HEAD_PAD, ROW_CHUNK))

    qg, kg = p["q_norm_g"], p["k_norm_g"]
    eye = jnp.eye(LRU_BLOCKS, dtype=F32)
    gate_w = jnp.stack([p["lru_wa"][:, 0], p["lru_wi"][:, 0],
                        p["lru_wa"][:, 1], p["lru_wi"][:, 1]], axis=1)
    w_gates = (gate_w[:, :, :, :, None, :] * eye[None, None, :, None, :, None])
    w_gates = w_gates.transpose(0, 2, 3, 1, 4, 5).reshape(DEPTH, LRU_W, 4 * LRU_W)
    b_gates = jnp.stack([p["lru_ba"][:, 0], p["lru_bi"][:, 0],
                         p["lru_ba"][:, 1], p["lru_bi"][:, 1]], axis=1)

    def row(v):
        return v[:, None, :]

    return {
        "norm1_g": row(p["norm1_g"]),
        "w_in": w_in_aug,
        "conv_w": p["conv_w"],
        "conv_b": row(p["conv_b"]),
        "lru_conv_w": p["lru_conv_w"],
        "lru_conv_b": row(p["lru_conv_b"]),
        "w_gates": (0.5 * w_gates).astype(BF16),
        "b_gates": 0.5 * b_gates.reshape(DEPTH, 1, 4 * LRU_W),
        "lru_lambda": p["lru_lambda"],
        "mla_qnorm_g": row(p["mla_qnorm_g"]),
        "wq_t": wq_t.astype(BF16),
        "wq_partner_t": wq_partner_t.astype(BF16),
        "mla_kvnorm_g": row(p["mla_kvnorm_g"]),
        "wk": wk_pad.astype(BF16),
        "wv_t": wv_t.astype(BF16),
        "qg_t": on_sublanes(pad_gain(qg) * q_scale),
        "qg_partner_t": on_sublanes(partner_gain(qg) * q_scale),
        "kg": pad_gain(kg),
        "kg_partner": partner_gain(kg),
        "gnorm_conv": row(p["gnorm_conv"]),
        "gnorm_lru": row(p["gnorm_lru"]),
        "gnorm_mla": row(p["gnorm_mla"]),
        "w_out": p["w_out"].astype(BF16),
        "norm2_g": row(p["norm2_g"]),
        "w_gate": p["w_gate"].astype(BF16),
        "w_up": p["w_up"].astype(BF16),
        "w_down": p["w_down"].astype(BF16),
    }


def _rope_tables(n_tokens):
    n_rows = n_tokens // GRID_W
    row = np.repeat(np.arange(n_rows, dtype=np.float32), GRID_W)
    col = np.tile(np.arange(GRID_W, dtype=np.float32), n_rows)
    n_freq = QK_ROPE // 4
    inv_freq = np.power(np.float32(ROPE_THETA),
                        -np.arange(n_freq, dtype=np.float32) / np.float32(n_freq))
    ang_r = (row[:, None] * inv_freq).astype(np.float32)
    ang_c = (col[:, None] * inv_freq).astype(np.float32)
    cos = np.concatenate([np.cos(ang_r), np.cos(ang_r), np.cos(ang_c), np.cos(ang_c)], 1)
    sin = np.concatenate([-np.sin(ang_r), np.sin(ang_r), -np.sin(ang_c), np.sin(ang_c)], 1)
    ones = np.ones((n_tokens, QK_NOPE), np.float32)
    tail = np.zeros((n_tokens, HEAD_PAD - QK_DIM), np.float32)
    rope_c = np.concatenate([ones, cos, tail], axis=1).astype(np.float32)
    rope_s = np.concatenate([np.zeros((n_tokens, QK_NOPE), np.float32), sin, tail],
                            axis=1).astype(np.float32)
    return rope_c, rope_s


def kernel(x_prompt, x_sample, cache_ckv, cache_krope, state_lru, c, c_ctx, norm1_g, ada_w, ada_b, w_in, conv_w, conv_b, lru_conv_w, lru_conv_b, lru_wa, lru_ba, lru_wi, lru_bi, lru_lambda, mla_qnorm_g, mla_wq_up, mla_kvnorm_g, mla_wkv_up, q_norm_g, k_norm_g, gnorm_conv, gnorm_lru, gnorm_mla, w_out, norm2_g, w_gate, w_up, w_down):
    params = dict(norm1_g=norm1_g, w_in=w_in, conv_w=conv_w, conv_b=conv_b,
                  lru_conv_w=lru_conv_w, lru_conv_b=lru_conv_b, lru_wa=lru_wa,
                  lru_ba=lru_ba, lru_wi=lru_wi, lru_bi=lru_bi, lru_lambda=lru_lambda,
                  mla_qnorm_g=mla_qnorm_g, mla_wq_up=mla_wq_up,
                  mla_kvnorm_g=mla_kvnorm_g, mla_wkv_up=mla_wkv_up, q_norm_g=q_norm_g,
                  k_norm_g=k_norm_g, gnorm_conv=gnorm_conv, gnorm_lru=gnorm_lru,
                  gnorm_mla=gnorm_mla, w_out=w_out, norm2_g=norm2_g, w_gate=w_gate,
                  w_up=w_up, w_down=w_down)
    batch, seq_p, _ = x_prompt.shape
    dec_batch, seq_s, _ = x_sample.shape
    assert dec_batch == 4 and cache_ckv.shape[2] == 256

    cond8 = jnp.concatenate(
        [c, c_ctx[None, :], jnp.zeros((8 - dec_batch - 1, D_MODEL), F32)], axis=0)
    mods = _ada_call(cond8, ada_w, ada_b).reshape(DEPTH, 8, 1, 6 * D_MODEL)

    rope_c, rope_s = _rope_tables(seq_s)
    cache_kr_pad = jnp.pad(
        cache_krope, ((0, 0), (0, 0), (0, 0), (ROPE_LO, HEAD_PAD - ROPE_LO - QK_ROPE)))

    def chunked_t(tab):
        return jnp.asarray(np.ascontiguousarray(
            tab.reshape(seq_s // ROW_CHUNK, ROW_CHUNK, HEAD_PAD).transpose(0, 2, 1)))

    extra = dict(rope_c=jnp.asarray(rope_c), rope_s=jnp.asarray(rope_s),
                 rope_ct=chunked_t(rope_c), rope_st=chunked_t(rope_s), cache_ckv=cache_ckv,
                 cache_kr=cache_kr_pad, state=state_lru)

    xp = x_prompt.reshape(batch * seq_p, D_MODEL)
    xs = x_sample.reshape(dec_batch * seq_s, D_MODEL)
    ckv_l, kr_l, st_l = [], [], []
    lw = _prep_weights(params)
    for l in range(DEPTH):
        (hs,) = _mix_call(xs, mods, l, lw, seq=seq_s, n_seq=1, latent=True, extra=extra)
        hp, ckv, kr, st, xs = _mix_call(xp, mods, l, lw, seq=seq_p, n_seq=CTX_SEQS_PER_STEP,
                                        latent=False, ffn_h2d=hs, ffn_rows_per_mod=seq_s)
        xp = _ffn_call(hp, mods, l, lw, rows_per_mod=None, mod_base=4)
        ckv_l.append(ckv.reshape(batch, seq_p, KV_LORA))
        kr_l.append(kr.reshape(batch, seq_p, QK_ROPE))
        st_l.append(st)
    return (xp.reshape(batch, seq_p, D_MODEL),
            xs.reshape(dec_batch, seq_s, D_MODEL),
            jnp.stack(ckv_l, axis=1),
            jnp.stack(kr_l, axis=1),
            jnp.stack(st_l, axis=1))
```

```python
import functools
import itertools
import math

import jax
import jax.numpy as jnp
import numpy as np
from jax import lax
from jax.experimental import pallas as pl
from jax.experimental.pallas import tpu as pltpu

F32 = jnp.float32
BF16 = jnp.bfloat16

D_MODEL = 1024
DEPTH = 2
GRID_W = 64
CONV_W = 256
LRU_W = 256
LRU_BLOCKS = 4
LRU_BLK = 64
LRU_C = 8.0
MLA_HEADS = 8
QK_NOPE = 64
QK_ROPE = 32
V_DIM = 64
V_EXT = V_DIM + 16
QK_DIM = QK_NOPE + QK_ROPE
Q_LORA = 256
KV_LORA = 128
ROPE_THETA = 10000.0
FF = 2816
EPS = 1e-6

LANES = 128
SUBLANES = 8
HEAD_PAD = LANES
VMEM_LIMIT_BYTES = 60 * 1024 * 1024

C_BG, C_CG, C_H = 0, 256, 512
C_XB, C_YB = 768, 1024
C_Q = 1280
C_CKV = 1536
C_KR = 1664
U_W = 1792
SCAN_A = (C_H, C_CKV)
SCAN_B = (C_Q, C_BG)
ROPE_LO = QK_NOPE

ROW_CHUNK = 256
SCAN_CHUNK = 32
Q_CHUNK = 256
FF_CHUNK = 256
PROJ_SLAB = 256
PROJ_CHUNKS = 2
CTX_SEQS_PER_STEP = 2
FUSED_FFN_PERIOD = 4
SEQ_STAGGER = 10
FFN_ROWS = 1024
PAD_ROWS = SUBLANES


def _rms_scale(x, n):
    ms = jnp.sum(x * x, axis=-1, keepdims=True) * (1.0 / n)
    return lax.rsqrt(ms + EPS)


def _sigmoid(x):
    return 0.5 * jnp.tanh(0.5 * x) + 0.5


def _gelu_tanh(x):
    c = math.sqrt(2.0 / math.pi)
    return 0.5 * x * (1.0 + jnp.tanh(c * (x + 0.044715 * (x * x * x))))


def _dot(a, b):
    return jnp.dot(a, b, preferred_element_type=F32)


def _interleave(*gens):
    live = list(gens)
    while live:
        for g in list(live):
            try:
                next(g)
            except StopIteration:
                live.remove(g)


def _interleave_staggered(gens, lag, side=None, side_period=1):
    pending = list(gens)
    live = []
    tick = 0
    while live or pending or side is not None:
        if pending and tick % lag == 0:
            live.append(pending.pop(0))
        if side is not None and (tick % side_period == 0 or not (live or pending)):
            try:
                next(side)
            except StopIteration:
                side = None
        for g in list(live):
            try:
                next(g)
            except StopIteration:
                live.remove(g)
        tick += 1


def _dot_nt(a, b):
    return lax.dot_general(a, b, (((1,), (1,)), ((), ())), preferred_element_type=F32)


ADA_TN = 1536


def _ada_kernel(cond_ref, w_ref, b_ref, o_ref):
    cnd = cond_ref[...]
    s = (cnd * _sigmoid(cnd)).astype(BF16)
    o_ref[...] = _dot(s, w_ref[...].astype(BF16)) + b_ref[...]


def _ada_call(cond8, ada_w, ada_b):
    n_out = 6 * D_MODEL
    return pl.pallas_call(
        _ada_kernel,
        grid=(DEPTH, n_out // ADA_TN),
        in_specs=[
            pl.BlockSpec((8, D_MODEL), lambda l, j: (0, 0)),
            pl.BlockSpec((None, D_MODEL, ADA_TN), lambda l, j: (l, 0, j)),
            pl.BlockSpec((None, 1, ADA_TN), lambda l, j: (l, 0, j)),
        ],
        out_specs=pl.BlockSpec((None, 8, ADA_TN), lambda l, j: (l, 0, j)),
        out_shape=jax.ShapeDtypeStruct((DEPTH, 8, n_out), F32),
        compiler_params=pltpu.CompilerParams(
            dimension_semantics=("arbitrary", "arbitrary"),
            vmem_limit_bytes=VMEM_LIMIT_BYTES),
        name="ada_mod",
    )(cond8, ada_w, ada_b.reshape(DEPTH, 1, n_out))


def _mix_kernel(*refs, seq, n_seq, latent, fused_ffn):
    T = seq * n_seq
    n_ctx = 256 if latent else 0
    (x_ref, mod_ref, n1g_ref, win_ref, wlast_ref, convw_ref, convb_ref, lcw_ref, lcb_ref,
     wg_ref, bg_ref, lam_ref, qng_ref, wqt_ref, kvng_ref, wk_ref, wvt_ref,
     qgt_ref, kg_ref, gnc_ref, gnl_ref, gnm_ref, wout_ref) = refs[:23]
    pos = 23
    if latent:
        (wqpt_ref, qgpt_ref, kgp_ref, ropect_ref, ropest_ref, ropec_ref, ropes_ref,
         cckv_ref, ckr_ref, h0_ref) = refs[pos:pos + 10]
        pos += 10
    if fused_ffn:
        ffn_in_refs = refs[pos:pos + 6]
        pos += 6
    h_out_ref = refs[pos]
    pos += 1
    if not latent:
        ckv_out_ref, kr_out_ref, st_out_ref = refs[pos:pos + 3]
        pos += 3
    if fused_ffn:
        ffn_out_ref = refs[pos]
        pos += 1
    u_scr, qt_scr, k_scr, vt_scr, ymt_scr, ycat_scr = refs[pos:pos + 6]
    if latent:
        st_scr = refs[pos + 6]

    n_chunks = T // ROW_CHUNK
    lane = lax.broadcasted_iota(jnp.int32, (1, LANES), 1)
    rope_lanes = (lane >= ROPE_LO) & (lane < ROPE_LO + QK_ROPE)

    sh1 = mod_ref[:, 0:D_MODEL]
    sc1 = mod_ref[:, D_MODEL:2 * D_MODEL]
    g1 = mod_ref[:, 2 * D_MODEL:3 * D_MODEL]
    gain1 = n1g_ref[...] * (1.0 + sc1)

    u_scr[0:PAD_ROWS, :] = jnp.zeros((PAD_ROWS, U_W), F32)
    u_scr[PAD_ROWS + T:2 * PAD_ROWS + T, :] = jnp.zeros((PAD_ROWS, U_W), F32)

    def phase_a(c, n=1):
        r0 = c * ROW_CHUNK
        nrows = n * ROW_CHUNK
        x = x_ref[r0:r0 + nrows, :]
        hn = (x * _rms_scale(x, D_MODEL) * gain1 + sh1).astype(BF16)
        yield
        urows = slice(PAD_ROWS + r0, PAD_ROWS + r0 + nrows)
        for s0 in range(0, C_CKV, PROJ_SLAB):
            u_scr[urows, s0:s0 + PROJ_SLAB] = _dot(hn, win_ref[:, s0:s0 + PROJ_SLAB])
            yield
        u_scr[urows, C_CKV:U_W] = _dot(hn, wlast_ref[...])
        yield
        u_scr[urows, C_CG:C_CG + CONV_W] = (
            u_scr[urows, C_CG:C_CG + CONV_W] * u_scr[urows, C_H:C_H + CONV_W])
        yield

    vrow = lax.broadcasted_iota(jnp.int32, (MLA_HEADS * V_EXT, 1), 0)
    ones_rows = (vrow % V_EXT) >= V_DIM

    def values_t(cb):
        return jnp.where(ones_rows, 1.0, _dot_nt(wvt_ref[...], cb)).astype(BF16)

    def phase_b(c):
        r0 = c * ROW_CHUNK
        rows = slice(PAD_ROWS + r0, PAD_ROWS + r0 + ROW_CHUNK)
        uq = u_scr[rows, C_Q:C_Q + Q_LORA]
        qn = (uq * _rms_scale(uq, Q_LORA) * qng_ref[...]).astype(BF16)
        qa_t = _dot_nt(wqt_ref[...], qn)
        if latent:
            qpa_t = _dot_nt(wqpt_ref[...], qn)
            q_tc = ropect_ref[c] * qgt_ref[...]
            q_ts = ropest_ref[c] * qgpt_ref[...]
        yield
        for h in range(MLA_HEADS):
            sl = slice(h * HEAD_PAD, (h + 1) * HEAD_PAD)
            qh = qa_t[sl, :]
            rq = lax.rsqrt(jnp.sum(qh * qh, axis=0, keepdims=True) * (1.0 / QK_DIM) + EPS)
            if latent:
                qh = (qh * q_tc + qpa_t[sl, :] * q_ts) * rq
            else:
                qh = qh * qgt_ref[...] * rq
            qt_scr[h, c] = qh.astype(BF16)
            if h % 2:
                yield
        uc = u_scr[rows, C_CKV:C_CKV + KV_LORA]
        ckv = uc * _rms_scale(uc, KV_LORA) * kvng_ref[...]
        krb = u_scr[rows, C_KR:C_KR + LANES]
        kr_rolled = pltpu.roll(krb, 64, 1)
        if not latent:
            ckv_out_ref[r0:r0 + ROW_CHUNK, :] = ckv
            kr_out_ref[r0:r0 + ROW_CHUNK, :] = kr_rolled[:, 0:QK_ROPE]
        krm = jnp.where(rope_lanes, krb, 0.0)
        cb = ckv.astype(BF16)
        ka = _dot(cb, wk_ref[...])
        vt_scr[c] = values_t(cb)
        if latent:
            k_tc = ropec_ref[r0:r0 + ROW_CHUNK, :] * kg_ref[...]
            k_ts = ropes_ref[r0:r0 + ROW_CHUNK, :] * kgp_ref[...]
        yield
        for h in range(MLA_HEADS):
            sl = slice(h * HEAD_PAD, (h + 1) * HEAD_PAD)
            kpre = ka[:, sl] + krm
            rk = _rms_scale(kpre, QK_DIM)
            if latent:
                kh = (kpre * k_tc + kr_rolled * k_ts) * rk
            else:
                kh = kpre * kg_ref[...] * rk
            k_scr[h, r0:r0 + ROW_CHUNK, :] = kh.astype(BF16)
            if h % 2:
                yield

    if latent:
        cc = cckv_ref[...].astype(BF16)
        ka = _dot(cc, wk_ref[...])
        vt_scr[n_chunks] = values_t(cc)
        krc = ckr_ref[...]
        for h in range(MLA_HEADS):
            kpre = ka[:, h * HEAD_PAD:(h + 1) * HEAD_PAD] + krc
            kh = kpre * kg_ref[...] * _rms_scale(kpre, QK_DIM)
            k_scr[h, seq:seq + n_ctx, :] = kh.astype(BF16)

    neg_lam = -lam_ref[...]
    sp = jnp.maximum(neg_lam, 0.0) + jnp.log(1.0 + jnp.exp(-jnp.abs(neg_lam)))
    log2a_coef = (-0.5 * LRU_C * math.log2(math.e)) * sp
    row_rc = lax.broadcasted_iota(jnp.int32, (ROW_CHUNK, 1), 0)

    def phase_c(ci):
        r0 = ci * ROW_CHUNK
        base = PAD_ROWS + r0
        first = (r0 % seq) == 0
        last = ((r0 + ROW_CHUNK) % seq) == 0

        def win(col, shift, width=CONV_W):
            w = u_scr[base + shift:base + shift + ROW_CHUNK, col:col + width]
            if shift < 0 and first:
                w = jnp.where(row_rc < -shift, 0.0, w)
            if shift > 0 and last:
                w = jnp.where(row_rc >= ROW_CHUNK - shift, 0.0, w)
            return w

        z_m, z_0, z_p = win(C_CG, -1), win(C_CG, 0), win(C_CG, 1)
        conv = (z_m * convw_ref[0:1, :] + z_0 * convw_ref[1:2, :]
                + z_p * convw_ref[2:3, :] + convb_ref[...])
        yc = u_scr[base:base + ROW_CHUNK, C_BG:C_BG + CONV_W] * conv
        ycn = yc * _rms_scale(yc, CONV_W) * gnc_ref[...]
        ycat_scr[r0:r0 + ROW_CHUNK, 0:CONV_W] = ycn.astype(BF16)
        yield

        xc = (win(C_XB, -2) * lcw_ref[0:1, :] + win(C_XB, -1) * lcw_ref[1:2, :]
              + win(C_XB, 0) * lcw_ref[2:3, :] + win(C_XB, 1) * lcw_ref[3:4, :]
              + lcb_ref[...])
        half_gates = _dot(xc.astype(BF16), wg_ref[...]) + bg_ref[...]
        half_xc = 0.5 * xc
        yield
        for d in range(2):
            t_r = jnp.tanh(half_gates[:, (2 * d) * LRU_W:(2 * d + 1) * LRU_W])
            t_i = jnp.tanh(half_gates[:, (2 * d + 1) * LRU_W:(2 * d + 2) * LRU_W])
            a = jnp.exp2(log2a_coef[d:d + 1, :] * (t_r + 1.0))
            v = 1.0 - a * a
            mult = v * lax.rsqrt(jnp.maximum(v, 1e-30))
            u_scr[base:base + ROW_CHUNK, SCAN_A[d]:SCAN_A[d] + LRU_W] = a
            u_scr[base:base + ROW_CHUNK, SCAN_B[d]:SCAN_B[d] + LRU_W] = (
                mult * (half_xc * (t_i + 1.0)))
            yield

    n_sc = seq // SCAN_CHUNK
    row_sc = lax.broadcasted_iota(jnp.int32, (SCAN_CHUNK, LRU_W), 0)
    steps = [1 << i for i in range(int(math.log2(SCAN_CHUNK)))]

    def scan_step(rf, rb, cf, cb_):
        rf = pl.multiple_of(rf + PAD_ROWS, SUBLANES)
        rb = pl.multiple_of(rb + PAD_ROWS, SUBLANES)
        a = u_scr[pl.ds(rf, SCAN_CHUNK), SCAN_A[0]:SCAN_A[0] + LRU_W]
        b = u_scr[pl.ds(rf, SCAN_CHUNK), SCAN_B[0]:SCAN_B[0] + LRU_W]
        b = b + jnp.where(row_sc == 0, a * cf, 0.0)
        for d in steps:
            b = b + a * jnp.where(row_sc >= d, pltpu.roll(b, d, 0), 0.0)
            if d != steps[-1]:
                a = a * jnp.where(row_sc >= d, pltpu.roll(a, d, 0), 1.0)
        u_scr[pl.ds(rf, SCAN_CHUNK), SCAN_B[0]:SCAN_B[0] + LRU_W] = b
        cf = b[SCAN_CHUNK - 1:SCAN_CHUNK, :]
        a = u_scr[pl.ds(rb, SCAN_CHUNK), SCAN_A[1]:SCAN_A[1] + LRU_W]
        b = u_scr[pl.ds(rb, SCAN_CHUNK), SCAN_B[1]:SCAN_B[1] + LRU_W]
        b = b + jnp.where(row_sc == SCAN_CHUNK - 1, a * cb_, 0.0)
        for d in steps:
            keep = row_sc < SCAN_CHUNK - d
            b = b + a * jnp.where(keep, pltpu.roll(b, SCAN_CHUNK - d, 0), 0.0)
            if d != steps[-1]:
                a = a * jnp.where(keep, pltpu.roll(a, SCAN_CHUNK - d, 0), 1.0)
        u_scr[pl.ds(rb, SCAN_CHUNK), SCAN_B[1]:SCAN_B[1] + LRU_W] = b
        cb_ = b[0:1, :]
        return cf, cb_

    cps = seq // ROW_CHUNK

    def softmax_pv(st, vts):
        m = jnp.max(st, axis=0, keepdims=True)
        pt = jnp.exp2(st - m).astype(BF16)
        ot = None
        for i, vt in enumerate(vts):
            part = _dot(vt, pt[i * ROW_CHUNK:(i + 1) * ROW_CHUNK, :])
            ot = part if ot is None else ot + part
        return ot[0:V_DIM, :] * (1.0 / ot[V_DIM:V_DIM + 1, :])

    def f_build(c):
        r0 = c * ROW_CHUNK
        rows = slice(r0, r0 + ROW_CHUNK)
        urows = slice(PAD_ROWS + r0, PAD_ROWS + r0 + ROW_CHUNK)
        hf = u_scr[urows, SCAN_B[0]:SCAN_B[0] + LRU_W]
        hb = u_scr[urows, SCAN_B[1]:SCAN_B[1] + LRU_W]
        yb = u_scr[urows, C_YB:C_YB + LRU_W]
        yl = (hf + hb) * _gelu_tanh(yb)
        yln = yl * _rms_scale(yl, LRU_W) * gnl_ref[...]
        ycat_scr[rows, CONV_W:CONV_W + LRU_W] = yln.astype(BF16)
        yield
        ymt = ymt_scr[c]
        ssq = jnp.sum(ymt * ymt, axis=0, keepdims=True)
        ynt = ymt * lax.rsqrt(ssq * (1.0 / (MLA_HEADS * V_DIM)) + EPS)
        ycat_scr[rows, CONV_W + LRU_W:D_MODEL] = (ynt.T * gnm_ref[...]).astype(BF16)
        yield

    def f_project(c):
        rows = slice(c * ROW_CHUNK, (c + 1) * ROW_CHUNK)
        for n0 in range(0, D_MODEL, PROJ_SLAB):
            cols = slice(n0, n0 + PROJ_SLAB)
            y = _dot(ycat_scr[rows, :], wout_ref[:, cols])
            h_out_ref[rows, cols] = x_ref[rows, cols] + g1[:, cols] * y
            yield

    if latent:
        groups = [list(range(g0, g0 + PROJ_CHUNKS)) for g0 in range(0, n_chunks, PROJ_CHUNKS)]
        _interleave(phase_a(0, PROJ_CHUNKS))
        for gi, group in enumerate(groups):
            dependent = []
            for c in group:
                dependent.append(phase_b(c))
                if c:
                    dependent.append(phase_c(c - 1))
            ahead = [phase_a(groups[gi + 1][0], PROJ_CHUNKS)] if gi + 1 < len(groups) else []
            _interleave(*ahead, itertools.chain(*dependent))
        _interleave(phase_c(n_chunks - 1))

        key_chunks = list(range(cps)) + [n_chunks]
        n_blk = cps * (MLA_HEADS // 2)
        trips = n_blk // 2
        scans_per_trip = n_sc // trips

        def heads_of(idx):
            qc = idx // (MLA_HEADS // 2)
            j = idx % (MLA_HEADS // 2)
            return qc, [2 * j, 2 * j + 1]

        def scores(idx, slot):
            qc, hs = heads_of(idx)
            for par, h in enumerate(hs):
                st_scr[slot, par] = _dot(k_scr[h], qt_scr[h, qc])

        def finish(idx, slot):
            qc, hs = heads_of(idx)
            for par, h in enumerate(hs):
                hr = pl.ds(pl.multiple_of(h * V_DIM, V_DIM), V_DIM)
                vr = pl.ds(pl.multiple_of(h * V_EXT, V_EXT - V_DIM), V_EXT)
                vts = [vt_scr[kc, vr, :] for kc in key_chunks]
                ymt_scr[qc, hr, :] = softmax_pv(st_scr[slot, par], vts)

        scores(0, 0)

        def att_body(i, carry):
            b0 = 2 * i
            attention = [
                lambda: scores(b0 + 1, 1),
                lambda: finish(b0, 0),
                lambda: scores(jnp.minimum(b0 + 2, n_blk - 1), 0),
                lambda: finish(b0 + 1, 1),
            ]
            for k in range(max(len(attention), scans_per_trip)):
                if k < len(attention):
                    attention[k]()
                if k < scans_per_trip:
                    c = i * scans_per_trip + k
                    rf = pl.multiple_of(c * SCAN_CHUNK, SCAN_CHUNK)
                    rb = pl.multiple_of((n_sc - 1 - c) * SCAN_CHUNK, SCAN_CHUNK)
                    carry = scan_step(rf, rb, *carry)
            return carry

        lax.fori_loop(0, trips, att_body, (h0_ref[0:1, :], h0_ref[1:2, :]))

        _interleave(f_build(0))
        for c in range(n_chunks):
            ahead = [f_build(c + 1)] if c + 1 < n_chunks else []
            _interleave(f_project(c), *ahead)
    else:
        def phase_d(s):
            carry = (jnp.zeros((1, LRU_W), F32), jnp.zeros((1, LRU_W), F32))
            for c in range(n_sc):
                carry = scan_step(s * seq + c * SCAN_CHUNK,
                                  s * seq + (n_sc - 1 - c) * SCAN_CHUNK, *carry)
                yield
            st_out_ref[s, 0:1, :] = carry[0]
            st_out_ref[s, 1:2, :] = carry[1]

        def phase_e(s):
            blocks = [(qi, h) for qi in range(cps) for h in range(MLA_HEADS)]

            def scores(blk):
                qi, h = blk
                return _dot(k_scr[h, s * seq:(s + 1) * seq, :], qt_scr[h, s * cps + qi])

            st = scores(blocks[0])
            for i, (qi, h) in enumerate(blocks):
                nxt = scores(blocks[i + 1]) if i + 1 < len(blocks) else None
                hr = slice(h * V_DIM, (h + 1) * V_DIM)
                vr = slice(h * V_EXT, (h + 1) * V_EXT)
                vts = [vt_scr[s * cps + j, vr, :] for j in range(cps)]
                ymt_scr[s * cps + qi, hr, :] = softmax_pv(st, vts)
                st = nxt
                yield

        def sequence(s):
            chunks = range(s * cps, (s + 1) * cps)
            if s == 0:
                yield from phase_a(0, n_chunks)
            for c in chunks:
                yield from phase_b(c)
            for c in chunks:
                yield from phase_c(c)
            yield from phase_d(s)
            yield from phase_e(s)
            for c in chunks:
                yield from f_build(c)
                yield from f_project(c)

        side = _ffn_steps(*ffn_in_refs, ffn_out_ref) if fused_ffn else None
        _interleave_staggered([sequence(s) for s in range(n_seq)], SEQ_STAGGER,
                              side=side, side_period=FUSED_FFN_PERIOD)


def _const_spec(shape):
    nd = len(shape)
    return pl.BlockSpec(shape, lambda i, _n=nd: (0,) * _n,
                        pipeline_mode=pl.Buffered(1))


def _layer_spec(arr, layer):
    nd = arr.ndim - 1
    return pl.BlockSpec((None,) + arr.shape[1:], lambda i, _n=nd: (layer,) + (0,) * _n,
                        pipeline_mode=pl.Buffered(1))


def _mix_call(x2d, mods, layer, lw, *, seq, n_seq, latent, extra=None, ffn_h2d=None,
              ffn_rows_per_mod=None):
    n_tok = x2d.shape[0]
    T = seq * n_seq
    grid = (n_tok // T,)
    fused_ffn = ffn_h2d is not None
    n_keys_buf = T if not latent else seq + 256

    if latent:
        mod_map = lambda i: (layer, i, 0, 0)
    else:
        mod_map = lambda i: (layer, 4, 0, 0)

    consts = [lw["norm1_g"], lw["w_in"], lw["w_in_last"], lw["conv_w"], lw["conv_b"], lw["lru_conv_w"],
              lw["lru_conv_b"], lw["w_gates"], lw["b_gates"], lw["lru_lambda"],
              lw["mla_qnorm_g"], lw["wq_t"], lw["mla_kvnorm_g"], lw["wk"], lw["wv_t"],
              lw["qg_t"], lw["kg"], lw["gnorm_conv"], lw["gnorm_lru"], lw["gnorm_mla"],
              lw["w_out"]]
    args = [x2d, mods] + consts
    in_specs = [pl.BlockSpec((T, D_MODEL), lambda i: (i, 0)),
                pl.BlockSpec((None, None, 1, 6 * D_MODEL), mod_map)]
    in_specs += [_layer_spec(a, layer) for a in consts]
    if latent:
        lconsts = [lw["wq_partner_t"], lw["qg_partner_t"], lw["kg_partner"]]
        tables = [extra["rope_ct"], extra["rope_st"], extra["rope_c"], extra["rope_s"]]
        args += lconsts + tables
        in_specs += [_layer_spec(a, layer) for a in lconsts]
        in_specs += [_const_spec(a.shape) for a in tables]
        args += [extra["cache_ckv"], extra["cache_kr"], extra["state"]]
        in_specs += [
            pl.BlockSpec((None, None, 256, KV_LORA), lambda i: (i, layer, 0, 0)),
            pl.BlockSpec((None, None, 256, LANES), lambda i: (i, layer, 0, 0)),
            pl.BlockSpec((None, None, 2, LRU_W), lambda i: (i, layer, 0, 0)),
        ]

    if fused_ffn:
        ffn_rows = ffn_h2d.shape[0] // grid[0]
        tiles_per_mod = ffn_rows_per_mod // ffn_rows
        ffn_consts = [lw["norm2_g"], lw["w_gate"], lw["w_up"], lw["w_down"]]
        args += [ffn_h2d, mods] + ffn_consts
        in_specs += [pl.BlockSpec((ffn_rows, D_MODEL), lambda i: (i, 0)),
                     pl.BlockSpec((None, None, 1, 6 * D_MODEL),
                                  lambda i: (layer, i // tiles_per_mod, 0, 0))]
        in_specs += [_layer_spec(a, layer) for a in ffn_consts]

    out_shape = [jax.ShapeDtypeStruct((n_tok, D_MODEL), F32)]
    out_specs = [pl.BlockSpec((T, D_MODEL), lambda i: (i, 0))]
    if not latent:
        out_shape += [jax.ShapeDtypeStruct((n_tok, KV_LORA), F32),
                      jax.ShapeDtypeStruct((n_tok, QK_ROPE), F32),
                      jax.ShapeDtypeStruct((n_tok // seq, 2, LRU_W), F32)]
        out_specs += [pl.BlockSpec((T, KV_LORA), lambda i: (i, 0)),
                      pl.BlockSpec((T, QK_ROPE), lambda i: (i, 0)),
                      pl.BlockSpec((n_seq, 2, LRU_W), lambda i: (i, 0, 0))]
    if fused_ffn:
        out_shape.append(jax.ShapeDtypeStruct(ffn_h2d.shape, F32))
        out_specs.append(pl.BlockSpec((ffn_rows, D_MODEL), lambda i: (i, 0)))

    scratch = [
        pltpu.VMEM((T + 2 * PAD_ROWS, U_W), F32),
        pltpu.VMEM((MLA_HEADS, T // ROW_CHUNK, HEAD_PAD, ROW_CHUNK), BF16),
        pltpu.VMEM((MLA_HEADS, n_keys_buf, HEAD_PAD), BF16),
        pltpu.VMEM((n_keys_buf // ROW_CHUNK, MLA_HEADS * V_EXT, ROW_CHUNK), BF16),
        pltpu.VMEM((T // ROW_CHUNK, MLA_HEADS * V_DIM, ROW_CHUNK), F32),
        pltpu.VMEM((T, D_MODEL), BF16),
    ]
    if latent:
        scratch.append(pltpu.VMEM((2, 2, n_keys_buf, ROW_CHUNK), F32))
    return pl.pallas_call(
        functools.partial(_mix_kernel, seq=seq, n_seq=n_seq, latent=latent,
                          fused_ffn=fused_ffn),
        grid=grid,
        in_specs=in_specs,
        out_specs=out_specs,
        out_shape=out_shape,
        scratch_shapes=scratch,
        compiler_params=pltpu.CompilerParams(
            dimension_semantics=("arbitrary",),
            vmem_limit_bytes=VMEM_LIMIT_BYTES),
        name="mix_latent" if latent else ("mix_context_swiglu" if fused_ffn else "mix_context"),
    )(*args)


def _ffn_steps(h_ref, mod_ref, n2g_ref, wg_ref, wu_ref, wd_ref, o_ref):
    sh2 = mod_ref[:, 3 * D_MODEL:4 * D_MODEL]
    sc2 = mod_ref[:, 4 * D_MODEL:5 * D_MODEL]
    g2 = mod_ref[:, 5 * D_MODEL:6 * D_MODEL]
    h = h_ref[...]
    hn = (h * _rms_scale(h, D_MODEL) * (n2g_ref[...] * (1.0 + sc2)) + sh2).astype(BF16)
    yield
    acc = None
    for c0 in range(0, FF, FF_CHUNK):
        g = _dot(hn, wg_ref[:, c0:c0 + FF_CHUNK])
        u = _dot(hn, wu_ref[:, c0:c0 + FF_CHUNK])
        act = (g * _sigmoid(g) * u).astype(BF16)
        part = _dot(act, wd_ref[c0:c0 + FF_CHUNK, :])
        acc = part if acc is None else acc + part
        yield
    o_ref[...] = h_ref[...] + g2 * acc
    yield


def _ffn_kernel(*refs):
    _interleave(_ffn_steps(*refs))


def _ffn_call(h2d, mods, layer, lw, *, rows_per_mod, mod_base):
    n_tok = h2d.shape[0]
    if mod_base is None:
        tiles_per_mod = rows_per_mod // FFN_ROWS
        mod_map = lambda i: (layer, i // tiles_per_mod, 0, 0)
    else:
        mod_map = lambda i: (layer, mod_base, 0, 0)
    consts = [lw["norm2_g"], lw["w_gate"], lw["w_up"], lw["w_down"]]
    return pl.pallas_call(
        _ffn_kernel,
        grid=(n_tok // FFN_ROWS,),
        in_specs=[pl.BlockSpec((FFN_ROWS, D_MODEL), lambda i: (i, 0)),
                  pl.BlockSpec((None, None, 1, 6 * D_MODEL), mod_map)]
        + [_layer_spec(a, layer) for a in consts],
        out_specs=pl.BlockSpec((FFN_ROWS, D_MODEL), lambda i: (i, 0)),
        out_shape=jax.ShapeDtypeStruct((n_tok, D_MODEL), F32),
        compiler_params=pltpu.CompilerParams(
            dimension_semantics=("arbitrary",),
            vmem_limit_bytes=VMEM_LIMIT_BYTES),
        name="swiglu",
    )(h2d, mods, *consts)


def _rope_partner_index():
    return [d + 8 if (d % 16) < 8 else d - 8 for d in range(QK_ROPE)]


def _prep_weights(p):
    partner = jnp.array(_rope_partner_index(), jnp.int32)
    q_scale = QK_DIM ** -0.5 * math.log2(math.e)
    w_in = p["w_in"].astype(BF16)
    w_kr = w_in[:, :, C_KR:C_KR + QK_ROPE]
    zeros32 = jnp.zeros((DEPTH, D_MODEL, QK_ROPE), BF16)
    w_in_last = jnp.concatenate(
        [w_in[:, :, C_CKV:C_KR], w_kr[:, :, partner], zeros32, w_kr, zeros32], axis=2)

    wq_h = p["mla_wq_up"].reshape(DEPTH, Q_LORA, MLA_HEADS, QK_DIM)
    wq_pad = jnp.pad(wq_h, ((0, 0), (0, 0), (0, 0), (0, HEAD_PAD - QK_DIM)))
    wq_t = wq_pad.reshape(DEPTH, Q_LORA, MLA_HEADS * HEAD_PAD).transpose(0, 2, 1)
    rope_pad = ((0, 0), (0, 0), (0, 0), (ROPE_LO, HEAD_PAD - ROPE_LO - QK_ROPE))
    wq_partner = jnp.pad(wq_h[:, :, :, QK_NOPE:][:, :, :, partner], rope_pad)
    wq_partner_t = wq_partner.reshape(DEPTH, Q_LORA, MLA_HEADS * HEAD_PAD).transpose(0, 2, 1)

    wkv = p["mla_wkv_up"].reshape(DEPTH, KV_LORA, MLA_HEADS, QK_NOPE + V_DIM)
    wk_pad = jnp.pad(wkv[:, :, :, :QK_NOPE], ((0, 0), (0, 0), (0, 0), (0, HEAD_PAD - QK_NOPE)))
    wk_pad = wk_pad.reshape(DEPTH, KV_LORA, MLA_HEADS * HEAD_PAD)
    wv_ext = jnp.pad(wkv[:, :, :, QK_NOPE:], ((0, 0), (0, 0), (0, 0), (0, V_EXT - V_DIM)))
    wv_t = wv_ext.reshape(DEPTH, KV_LORA, MLA_HEADS * V_EXT).transpose(0, 2, 1)

    def pad_gain(g):
        return jnp.pad(g, ((0, 0), (0, HEAD_PAD - QK_DIM)))[:, None, :]

    def partner_gain(g):
        gp = g[:, QK_NOPE:][:, partner]
        return jnp.pad(gp, ((0, 0), (ROPE_LO, HEAD_PAD - ROPE_LO - QK_ROPE)))[:, None, :]

    def on_sublanes(g):
        return jnp.broadcast_to(g.transpose(0, 2, 1), (DEPTH, HEAD_PAD, ROW_CHUNK))

    qg, kg = p["q_norm_g"], p["k_norm_g"]
    eye = jnp.eye(LRU_BLOCKS, dtype=F32)
    gate_w = jnp.stack([p["lru_wa"][:, 0], p["lru_wi"][:, 0],
                        p["lru_wa"][:, 1], p["lru_wi"][:, 1]], axis=1)
    w_gates = (gate_w[:, :, :, :, None, :] * eye[None, None, :, None, :, None])
    w_gates = w_gates.transpose(0, 2, 3, 1, 4, 5).reshape(DEPTH, LRU_W, 4 * LRU_W)
    b_gates = jnp.stack([p["lru_ba"][:, 0], p["lru_bi"][:, 0],
                         p["lru_ba"][:, 1], p["lru_bi"][:, 1]], axis=1)

    def row(v):
        return v[:, None, :]

    return {
        "norm1_g": row(p["norm1_g"]),
        "w_in": w_in,
        "w_in_last": w_in_last,
        "conv_w": p["conv_w"],
        "conv_b": row(p["conv_b"]),
        "lru_conv_w": p["lru_conv_w"],
        "lru_conv_b": row(p["lru_conv_b"]),
        "w_gates": (0.5 * w_gates).astype(BF16),
        "b_gates": 0.5 * b_gates.reshape(DEPTH, 1, 4 * LRU_W),
        "lru_lambda": p["lru_lambda"],
        "mla_qnorm_g": row(p["mla_qnorm_g"]),
        "wq_t": wq_t.astype(BF16),
        "wq_partner_t": wq_partner_t.astype(BF16),
        "mla_kvnorm_g": row(p["mla_kvnorm_g"]),
        "wk": wk_pad.astype(BF16),
        "wv_t": wv_t.astype(BF16),
        "qg_t": on_sublanes(pad_gain(qg) * q_scale),
        "qg_partner_t": on_sublanes(partner_gain(qg) * q_scale),
        "kg": pad_gain(kg),
        "kg_partner": partner_gain(kg),
        "gnorm_conv": row(p["gnorm_conv"]),
        "gnorm_lru": row(p["gnorm_lru"]),
        "gnorm_mla": row(p["gnorm_mla"]),
        "w_out": p["w_out"].astype(BF16),
        "norm2_g": row(p["norm2_g"]),
        "w_gate": p["w_gate"].astype(BF16),
        "w_up": p["w_up"].astype(BF16),
        "w_down": p["w_down"].astype(BF16),
    }


def _rope_tables(n_tokens):
    n_rows = n_tokens // GRID_W
    row = np.repeat(np.arange(n_rows, dtype=np.float32), GRID_W)
    col = np.tile(np.arange(GRID_W, dtype=np.float32), n_rows)
    n_freq = QK_ROPE // 4
    inv_freq = np.power(np.float32(ROPE_THETA),
                        -np.arange(n_freq, dtype=np.float32) / np.float32(n_freq))
    ang_r = (row[:, None] * inv_freq).astype(np.float32)
    ang_c = (col[:, None] * inv_freq).astype(np.float32)
    cos = np.concatenate([np.cos(ang_r), np.cos(ang_r), np.cos(ang_c), np.cos(ang_c)], 1)
    sin = np.concatenate([-np.sin(ang_r), np.sin(ang_r), -np.sin(ang_c), np.sin(ang_c)], 1)
    ones = np.ones((n_tokens, QK_NOPE), np.float32)
    tail = np.zeros((n_tokens, HEAD_PAD - QK_DIM), np.float32)
    rope_c = np.concatenate([ones, cos, tail], axis=1).astype(np.float32)
    rope_s = np.concatenate([np.zeros((n_tokens, QK_NOPE), np.float32), sin, tail],
                            axis=1).astype(np.float32)
    return rope_c, rope_s


def kernel(x_prompt, x_sample, cache_ckv, cache_krope, state_lru, c, c_ctx, norm1_g, ada_w, ada_b, w_in, conv_w, conv_b, lru_conv_w, lru_conv_b, lru_wa, lru_ba, lru_wi, lru_bi, lru_lambda, mla_qnorm_g, mla_wq_up, mla_kvnorm_g, mla_wkv_up, q_norm_g, k_norm_g, gnorm_conv, gnorm_lru, gnorm_mla, w_out, norm2_g, w_gate, w_up, w_down):
    params = dict(norm1_g=norm1_g, w_in=w_in, conv_w=conv_w, conv_b=conv_b,
                  lru_conv_w=lru_conv_w, lru_conv_b=lru_conv_b, lru_wa=lru_wa,
                  lru_ba=lru_ba, lru_wi=lru_wi, lru_bi=lru_bi, lru_lambda=lru_lambda,
                  mla_qnorm_g=mla_qnorm_g, mla_wq_up=mla_wq_up,
                  mla_kvnorm_g=mla_kvnorm_g, mla_wkv_up=mla_wkv_up, q_norm_g=q_norm_g,
                  k_norm_g=k_norm_g, gnorm_conv=gnorm_conv, gnorm_lru=gnorm_lru,
                  gnorm_mla=gnorm_mla, w_out=w_out, norm2_g=norm2_g, w_gate=w_gate,
                  w_up=w_up, w_down=w_down)
    batch, seq_p, _ = x_prompt.shape
    dec_batch, seq_s, _ = x_sample.shape
    assert dec_batch == 4 and cache_ckv.shape[2] == 256

    cond8 = jnp.concatenate(
        [c, c_ctx[None, :], jnp.zeros((8 - dec_batch - 1, D_MODEL), F32)], axis=0)
    mods = _ada_call(cond8, ada_w, ada_b).reshape(DEPTH, 8, 1, 6 * D_MODEL)

    rope_c, rope_s = _rope_tables(seq_s)
    cache_kr_pad = jnp.pad(
        cache_krope, ((0, 0), (0, 0), (0, 0), (ROPE_LO, HEAD_PAD - ROPE_LO - QK_ROPE)))

    def chunked_t(tab):
        return jnp.asarray(np.ascontiguousarray(
            tab.reshape(seq_s // ROW_CHUNK, ROW_CHUNK, HEAD_PAD).transpose(0, 2, 1)))

    extra = dict(rope_c=jnp.asarray(rope_c), rope_s=jnp.asarray(rope_s),
                 rope_ct=chunked_t(rope_c), rope_st=chunked_t(rope_s), cache_ckv=cache_ckv,
                 cache_kr=cache_kr_pad, state=state_lru)

    xp = x_prompt.reshape(batch * seq_p, D_MODEL)
    xs = x_sample.reshape(dec_batch * seq_s, D_MODEL)
    ckv_l, kr_l, st_l = [], [], []
    lw = _prep_weights(params)
    for l in range(DEPTH):
        (hs,) = _mix_call(xs, mods, l, lw, seq=seq_s, n_seq=1, latent=True, extra=extra)
        hp, ckv, kr, st, xs = _mix_call(xp, mods, l, lw, seq=seq_p, n_seq=CTX_SEQS_PER_STEP,
                                        latent=False, ffn_h2d=hs, ffn_rows_per_mod=seq_s)
        xp = _ffn_call(hp, mods, l, lw, rows_per_mod=None, mod_base=4)
        ckv_l.append(ckv.reshape(batch, seq_p, KV_LORA))
        kr_l.append(kr.reshape(batch, seq_p, QK_ROPE))
        st_l.append(st)
    return (xp.reshape(batch, seq_p, D_MODEL),
            xs.reshape(dec_batch, seq_s, D_MODEL),
            jnp.stack(ckv_l, axis=1),
            jnp.stack(kr_l, axis=1),
            jnp.stack(st_l, axis=1))
```

```python
import functools
import itertools
import math

import jax
import jax.numpy as jnp
import numpy as np
from jax import lax
from jax.experimental import pallas as pl
from jax.experimental.pallas import tpu as pltpu

F32 = jnp.float32
BF16 = jnp.bfloat16

D_MODEL = 1024
DEPTH = 2
GRID_W = 64
CONV_W = 256
LRU_W = 256
LRU_BLOCKS = 4
LRU_BLK = 64
LRU_C = 8.0
MLA_HEADS = 8
QK_NOPE = 64
QK_ROPE = 32
V_DIM = 64
V_EXT = V_DIM + 16
QK_DIM = QK_NOPE + QK_ROPE
Q_LORA = 256
KV_LORA = 128
ROPE_THETA = 10000.0
FF = 2816
EPS = 1e-6

LANES = 128
SUBLANES = 8
HEAD_PAD = LANES
VMEM_LIMIT_BYTES = 60 * 1024 * 1024

C_BG, C_CG, C_H = 0, 256, 512
C_XB, C_YB = 768, 1024
C_Q = 1280
C_CKV = 1536
C_KR = 1664
U_W = 1792
SCAN_A = (C_H, C_CKV)
SCAN_B = (C_Q, C_BG)
ROPE_LO = QK_NOPE

ROW_CHUNK = 256
SCAN_CHUNK = 32
Q_CHUNK = 256
FF_CHUNK = 256
PROJ_SLAB = 256
PROJ_CHUNKS = 1
CTX_SEQS_PER_STEP = 2
FUSED_FFN_PERIOD = 4
SEQ_STAGGER = 10
FFN_ROWS = 1024
PAD_ROWS = SUBLANES


def _rms_scale(x, n):
    ms = jnp.sum(x * x, axis=-1, keepdims=True) * (1.0 / n)
    return lax.rsqrt(ms + EPS)


def _sigmoid(x):
    return 0.5 * jnp.tanh(0.5 * x) + 0.5


def _gelu_tanh(x):
    c = math.sqrt(2.0 / math.pi)
    return 0.5 * x * (1.0 + jnp.tanh(c * (x + 0.044715 * (x * x * x))))


def _dot(a, b):
    return jnp.dot(a, b, preferred_element_type=F32)


def _interleave(*gens):
    live = list(gens)
    while live:
        for g in list(live):
            try:
                next(g)
            except StopIteration:
                live.remove(g)


def _interleave_staggered(gens, lag, side=None, side_period=1):
    pending = list(gens)
    live = []
    tick = 0
    while live or pending or side is not None:
        if pending and tick % lag == 0:
            live.append(pending.pop(0))
        if side is not None and (tick % side_period == 0 or not (live or pending)):
            try:
                next(side)
            except StopIteration:
                side = None
        for g in list(live):
            try:
                next(g)
            except StopIteration:
                live.remove(g)
        tick += 1


def _dot_nt(a, b):
    return lax.dot_general(a, b, (((1,), (1,)), ((), ())), preferred_element_type=F32)


ADA_TN = 1536


def _ada_kernel(cond_ref, w_ref, b_ref, o_ref):
    cnd = cond_ref[...]
    s = (cnd * _sigmoid(cnd)).astype(BF16)
    o_ref[...] = _dot(s, w_ref[...].astype(BF16)) + b_ref[...]


def _ada_call(cond8, ada_w, ada_b):
    n_out = 6 * D_MODEL
    return pl.pallas_call(
        _ada_kernel,
        grid=(DEPTH, n_out // ADA_TN),
        in_specs=[
            pl.BlockSpec((8, D_MODEL), lambda l, j: (0, 0)),
            pl.BlockSpec((None, D_MODEL, ADA_TN), lambda l, j: (l, 0, j)),
            pl.BlockSpec((None, 1, ADA_TN), lambda l, j: (l, 0, j)),
        ],
        out_specs=pl.BlockSpec((None, 8, ADA_TN), lambda l, j: (l, 0, j)),
        out_shape=jax.ShapeDtypeStruct((DEPTH, 8, n_out), F32),
        compiler_params=pltpu.CompilerParams(
            dimension_semantics=("arbitrary", "arbitrary"),
            vmem_limit_bytes=VMEM_LIMIT_BYTES),
        name="ada_mod",
    )(cond8, ada_w, ada_b.reshape(DEPTH, 1, n_out))


def _mix_kernel(*refs, seq, n_seq, latent, fused_ffn):
    T = seq * n_seq
    n_ctx = 256 if latent else 0
    (x_ref, mod_ref, n1g_ref, win_ref, wlast_ref, convw_ref, convb_ref, lcw_ref, lcb_ref,
     wg_ref, bg_ref, lam_ref, qng_ref, wqt_ref, kvng_ref, wk_ref, wvt_ref,
     qgt_ref, kg_ref, gnc_ref, gnl_ref, gnm_ref, wout_ref) = refs[:23]
    pos = 23
    if latent:
        (wqpt_ref, qgpt_ref, kgp_ref, ropect_ref, ropest_ref, ropec_ref, ropes_ref,
         cckv_ref, ckr_ref, h0_ref) = refs[pos:pos + 10]
        pos += 10
    if fused_ffn:
        ffn_in_refs = refs[pos:pos + 6]
        pos += 6
    h_out_ref = refs[pos]
    pos += 1
    if not latent:
        ckv_out_ref, kr_out_ref, st_out_ref = refs[pos:pos + 3]
        pos += 3
    if fused_ffn:
        ffn_out_ref = refs[pos]
        pos += 1
    u_scr, qt_scr, k_scr, vt_scr, ymt_scr, ycat_scr = refs[pos:pos + 6]
    if latent:
        st_scr = refs[pos + 6]

    n_chunks = T // ROW_CHUNK
    lane = lax.broadcasted_iota(jnp.int32, (1, LANES), 1)
    rope_lanes = (lane >= ROPE_LO) & (lane < ROPE_LO + QK_ROPE)

    sh1 = mod_ref[:, 0:D_MODEL]
    sc1 = mod_ref[:, D_MODEL:2 * D_MODEL]
    g1 = mod_ref[:, 2 * D_MODEL:3 * D_MODEL]
    gain1 = n1g_ref[...] * (1.0 + sc1)

    u_scr[0:PAD_ROWS, :] = jnp.zeros((PAD_ROWS, U_W), F32)
    u_scr[PAD_ROWS + T:2 * PAD_ROWS + T, :] = jnp.zeros((PAD_ROWS, U_W), F32)

    def phase_a(c, n=1):
        r0 = c * ROW_CHUNK
        nrows = n * ROW_CHUNK
        x = x_ref[r0:r0 + nrows, :]
        hn = (x * _rms_scale(x, D_MODEL) * gain1 + sh1).astype(BF16)
        yield
        urows = slice(PAD_ROWS + r0, PAD_ROWS + r0 + nrows)
        for s0 in range(0, C_CKV, PROJ_SLAB):
            u_scr[urows, s0:s0 + PROJ_SLAB] = _dot(hn, win_ref[:, s0:s0 + PROJ_SLAB])
            yield
        u_scr[urows, C_CKV:U_W] = _dot(hn, wlast_ref[...])
        yield
        u_scr[urows, C_CG:C_CG + CONV_W] = (
            u_scr[urows, C_CG:C_CG + CONV_W] * u_scr[urows, C_H:C_H + CONV_W])
        yield

    vrow = lax.broadcasted_iota(jnp.int32, (MLA_HEADS * V_EXT, 1), 0)
    ones_rows = (vrow % V_EXT) >= V_DIM

    def values_t(cb):
        return jnp.where(ones_rows, 1.0, _dot_nt(wvt_ref[...], cb)).astype(BF16)

    def phase_b(c):
        r0 = c * ROW_CHUNK
        rows = slice(PAD_ROWS + r0, PAD_ROWS + r0 + ROW_CHUNK)
        uq = u_scr[rows, C_Q:C_Q + Q_LORA]
        qn = (uq * _rms_scale(uq, Q_LORA) * qng_ref[...]).astype(BF16)
        qa_t = _dot_nt(wqt_ref[...], qn)
        if latent:
            qpa_t = _dot_nt(wqpt_ref[...], qn)
            q_tc = ropect_ref[c] * qgt_ref[...]
            q_ts = ropest_ref[c] * qgpt_ref[...]
        yield
        for h in range(MLA_HEADS):
            sl = slice(h * HEAD_PAD, (h + 1) * HEAD_PAD)
            qh = qa_t[sl, :]
            rq = lax.rsqrt(jnp.sum(qh * qh, axis=0, keepdims=True) * (1.0 / QK_DIM) + EPS)
            if latent:
                qh = (qh * q_tc + qpa_t[sl, :] * q_ts) * rq
            else:
                qh = qh * qgt_ref[...] * rq
            qt_scr[h, c] = qh.astype(BF16)
            if h % 2:
                yield
        uc = u_scr[rows, C_CKV:C_CKV + KV_LORA]
        ckv = uc * _rms_scale(uc, KV_LORA) * kvng_ref[...]
        krb = u_scr[rows, C_KR:C_KR + LANES]
        kr_rolled = pltpu.roll(krb, 64, 1)
        if not latent:
            ckv_out_ref[r0:r0 + ROW_CHUNK, :] = ckv
            kr_out_ref[r0:r0 + ROW_CHUNK, :] = kr_rolled[:, 0:QK_ROPE]
        krm = jnp.where(rope_lanes, krb, 0.0)
        cb = ckv.astype(BF16)
        ka = _dot(cb, wk_ref[...])
        vt_scr[c] = values_t(cb)
        if latent:
            k_tc = ropec_ref[r0:r0 + ROW_CHUNK, :] * kg_ref[...]
            k_ts = ropes_ref[r0:r0 + ROW_CHUNK, :] * kgp_ref[...]
        yield
        for h in range(MLA_HEADS):
            sl = slice(h * HEAD_PAD, (h + 1) * HEAD_PAD)
            kpre = ka[:, sl] + krm
            rk = _rms_scale(kpre, QK_DIM)
            if latent:
                kh = (kpre * k_tc + kr_rolled * k_ts) * rk
            else:
                kh = kpre * kg_ref[...] * rk
            k_scr[h, r0:r0 + ROW_CHUNK, :] = kh.astype(BF16)
            if h % 2:
                yield

    if latent:
        cc = cckv_ref[...].astype(BF16)
        ka = _dot(cc, wk_ref[...])
        vt_scr[n_chunks] = values_t(cc)
        krc = ckr_ref[...]
        for h in range(MLA_HEADS):
            kpre = ka[:, h * HEAD_PAD:(h + 1) * HEAD_PAD] + krc
            kh = kpre * kg_ref[...] * _rms_scale(kpre, QK_DIM)
            k_scr[h, seq:seq + n_ctx, :] = kh.astype(BF16)

    neg_lam = -lam_ref[...]
    sp = jnp.maximum(neg_lam, 0.0) + jnp.log(1.0 + jnp.exp(-jnp.abs(neg_lam)))
    log2a_coef = (-0.5 * LRU_C * math.log2(math.e)) * sp
    row_rc = lax.broadcasted_iota(jnp.int32, (ROW_CHUNK, 1), 0)

    def phase_c(ci):
        r0 = ci * ROW_CHUNK
        base = PAD_ROWS + r0
        first = (r0 % seq) == 0
        last = ((r0 + ROW_CHUNK) % seq) == 0

        def win(col, shift, width=CONV_W):
            w = u_scr[base + shift:base + shift + ROW_CHUNK, col:col + width]
            if shift < 0 and first:
                w = jnp.where(row_rc < -shift, 0.0, w)
            if shift > 0 and last:
                w = jnp.where(row_rc >= ROW_CHUNK - shift, 0.0, w)
            return w

        z_m, z_0, z_p = win(C_CG, -1), win(C_CG, 0), win(C_CG, 1)
        conv = (z_m * convw_ref[0:1, :] + z_0 * convw_ref[1:2, :]
                + z_p * convw_ref[2:3, :] + convb_ref[...])
        yc = u_scr[base:base + ROW_CHUNK, C_BG:C_BG + CONV_W] * conv
        ycn = yc * _rms_scale(yc, CONV_W) * gnc_ref[...]
        ycat_scr[r0:r0 + ROW_CHUNK, 0:CONV_W] = ycn.astype(BF16)
        yield

        xc = (win(C_XB, -2) * lcw_ref[0:1, :] + win(C_XB, -1) * lcw_ref[1:2, :]
              + win(C_XB, 0) * lcw_ref[2:3, :] + win(C_XB, 1) * lcw_ref[3:4, :]
              + lcb_ref[...])
        half_gates = _dot(xc.astype(BF16), wg_ref[...]) + bg_ref[...]
        half_xc = 0.5 * xc
        yield
        for d in range(2):
            t_r = jnp.tanh(half_gates[:, (2 * d) * LRU_W:(2 * d + 1) * LRU_W])
            t_i = jnp.tanh(half_gates[:, (2 * d + 1) * LRU_W:(2 * d + 2) * LRU_W])
            a = jnp.exp2(log2a_coef[d:d + 1, :] * (t_r + 1.0))
            v = 1.0 - a * a
            mult = v * lax.rsqrt(jnp.maximum(v, 1e-30))
            u_scr[base:base + ROW_CHUNK, SCAN_A[d]:SCAN_A[d] + LRU_W] = a
            u_scr[base:base + ROW_CHUNK, SCAN_B[d]:SCAN_B[d] + LRU_W] = (
                mult * (half_xc * (t_i + 1.0)))
            yield

    n_sc = seq // SCAN_CHUNK
    row_sc = lax.broadcasted_iota(jnp.int32, (SCAN_CHUNK, LRU_W), 0)
    steps = [1 << i for i in range(int(math.log2(SCAN_CHUNK)))]

    def scan_step(rf, rb, cf, cb_):
        rf = pl.multiple_of(rf + PAD_ROWS, SUBLANES)
        rb = pl.multiple_of(rb + PAD_ROWS, SUBLANES)
        a = u_scr[pl.ds(rf, SCAN_CHUNK), SCAN_A[0]:SCAN_A[0] + LRU_W]
        b = u_scr[pl.ds(rf, SCAN_CHUNK), SCAN_B[0]:SCAN_B[0] + LRU_W]
        b = b + jnp.where(row_sc == 0, a * cf, 0.0)
        for d in steps:
            b = b + a * jnp.where(row_sc >= d, pltpu.roll(b, d, 0), 0.0)
            if d != steps[-1]:
                a = a * jnp.where(row_sc >= d, pltpu.roll(a, d, 0), 1.0)
        u_scr[pl.ds(rf, SCAN_CHUNK), SCAN_B[0]:SCAN_B[0] + LRU_W] = b
        cf = b[SCAN_CHUNK - 1:SCAN_CHUNK, :]
        a = u_scr[pl.ds(rb, SCAN_CHUNK), SCAN_A[1]:SCAN_A[1] + LRU_W]
        b = u_scr[pl.ds(rb, SCAN_CHUNK), SCAN_B[1]:SCAN_B[1] + LRU_W]
        b = b + jnp.where(row_sc == SCAN_CHUNK - 1, a * cb_, 0.0)
        for d in steps:
            keep = row_sc < SCAN_CHUNK - d
            b = b + a * jnp.where(keep, pltpu.roll(b, SCAN_CHUNK - d, 0), 0.0)
            if d != steps[-1]:
                a = a * jnp.where(keep, pltpu.roll(a, SCAN_CHUNK - d, 0), 1.0)
        u_scr[pl.ds(rb, SCAN_CHUNK), SCAN_B[1]:SCAN_B[1] + LRU_W] = b
        cb_ = b[0:1, :]
        return cf, cb_

    cps = seq // ROW_CHUNK

    def softmax_pv(st, vts):
        m = jnp.max(st, axis=0, keepdims=True)
        pt = jnp.exp2(st - m).astype(BF16)
        ot = None
        for i, vt in enumerate(vts):
            part = _dot(vt, pt[i * ROW_CHUNK:(i + 1) * ROW_CHUNK, :])
            ot = part if ot is None else ot + part
        return ot[0:V_DIM, :] * (1.0 / ot[V_DIM:V_DIM + 1, :])

    def f_build(c):
        r0 = c * ROW_CHUNK
        rows = slice(r0, r0 + ROW_CHUNK)
        urows = slice(PAD_ROWS + r0, PAD_ROWS + r0 + ROW_CHUNK)
        hf = u_scr[urows, SCAN_B[0]:SCAN_B[0] + LRU_W]
        hb = u_scr[urows, SCAN_B[1]:SCAN_B[1] + LRU_W]
        yb = u_scr[urows, C_YB:C_YB + LRU_W]
        yl = (hf + hb) * _gelu_tanh(yb)
        yln = yl * _rms_scale(yl, LRU_W) * gnl_ref[...]
        ycat_scr[rows, CONV_W:CONV_W + LRU_W] = yln.astype(BF16)
        yield
        ymt = ymt_scr[c]
        ssq = jnp.sum(ymt * ymt, axis=0, keepdims=True)
        ynt = ymt * lax.rsqrt(ssq * (1.0 / (MLA_HEADS * V_DIM)) + EPS)
        ycat_scr[rows, CONV_W + LRU_W:D_MODEL] = (ynt.T * gnm_ref[...]).astype(BF16)
        yield

    def f_project(c):
        rows = slice(c * ROW_CHUNK, (c + 1) * ROW_CHUNK)
        for n0 in range(0, D_MODEL, PROJ_SLAB):
            cols = slice(n0, n0 + PROJ_SLAB)
            y = _dot(ycat_scr[rows, :], wout_ref[:, cols])
            h_out_ref[rows, cols] = x_ref[rows, cols] + g1[:, cols] * y
            yield

    if latent:
        groups = [list(range(g0, g0 + PROJ_CHUNKS)) for g0 in range(0, n_chunks, PROJ_CHUNKS)]
        _interleave(phase_a(0, PROJ_CHUNKS))
        for gi, group in enumerate(groups):
            dependent = []
            for c in group:
                dependent.append(phase_b(c))
                if c:
                    dependent.append(phase_c(c - 1))
            ahead = [phase_a(groups[gi + 1][0], PROJ_CHUNKS)] if gi + 1 < len(groups) else []
            _interleave(*ahead, itertools.chain(*dependent))
        _interleave(phase_c(n_chunks - 1))

        key_chunks = list(range(cps)) + [n_chunks]
        n_blk = cps * (MLA_HEADS // 2)
        trips = n_blk // 2
        scans_per_trip = n_sc // trips

        def heads_of(idx):
            qc = idx // (MLA_HEADS // 2)
            j = idx % (MLA_HEADS // 2)
            return qc, [2 * j, 2 * j + 1]

        def scores(idx, slot):
            qc, hs = heads_of(idx)
            for par, h in enumerate(hs):
                st_scr[slot, par] = _dot(k_scr[h], qt_scr[h, qc])

        def finish(idx, slot):
            qc, hs = heads_of(idx)
            for par, h in enumerate(hs):
                hr = pl.ds(pl.multiple_of(h * V_DIM, V_DIM), V_DIM)
                vr = pl.ds(pl.multiple_of(h * V_EXT, V_EXT - V_DIM), V_EXT)
                vts = [vt_scr[kc, vr, :] for kc in key_chunks]
                ymt_scr[qc, hr, :] = softmax_pv(st_scr[slot, par], vts)

        scores(0, 0)

        def att_body(i, carry):
            b0 = 2 * i
            attention = [
                lambda: scores(b0 + 1, 1),
                lambda: finish(b0, 0),
                lambda: scores(jnp.minimum(b0 + 2, n_blk - 1), 0),
                lambda: finish(b0 + 1, 1),
            ]
            for k in range(max(len(attention), scans_per_trip)):
                if k < len(attention):
                    attention[k]()
                if k < scans_per_trip:
                    c = i * scans_per_trip + k
                    rf = pl.multiple_of(c * SCAN_CHUNK, SCAN_CHUNK)
                    rb = pl.multiple_of((n_sc - 1 - c) * SCAN_CHUNK, SCAN_CHUNK)
                    carry = scan_step(rf, rb, *carry)
            return carry

        lax.fori_loop(0, trips, att_body, (h0_ref[0:1, :], h0_ref[1:2, :]))

        _interleave(f_build(0))
        for c in range(n_chunks):
            ahead = [f_build(c + 1)] if c + 1 < n_chunks else []
            _interleave(f_project(c), *ahead)
    else:
        def phase_d(s):
            carry = (jnp.zeros((1, LRU_W), F32), jnp.zeros((1, LRU_W), F32))
            for c in range(n_sc):
                carry = scan_step(s * seq + c * SCAN_CHUNK,
                                  s * seq + (n_sc - 1 - c) * SCAN_CHUNK, *carry)
                yield
            st_out_ref[s, 0:1, :] = carry[0]
            st_out_ref[s, 1:2, :] = carry[1]

        def phase_e(s):
            blocks = [(qi, h) for qi in range(cps) for h in range(MLA_HEADS)]

            def scores(blk):
                qi, h = blk
                return _dot(k_scr[h, s * seq:(s + 1) * seq, :], qt_scr[h, s * cps + qi])

            st = scores(blocks[0])
            for i, (qi, h) in enumerate(blocks):
                nxt = scores(blocks[i + 1]) if i + 1 < len(blocks) else None
                hr = slice(h * V_DIM, (h + 1) * V_DIM)
                vr = slice(h * V_EXT, (h + 1) * V_EXT)
                vts = [vt_scr[s * cps + j, vr, :] for j in range(cps)]
                ymt_scr[s * cps + qi, hr, :] = softmax_pv(st, vts)
                st = nxt
                yield

        def sequence(s):
            chunks = range(s * cps, (s + 1) * cps)
            for c in chunks:
                yield from phase_a(c)
            for c in chunks:
                yield from phase_b(c)
            for c in chunks:
                yield from phase_c(c)
            yield from phase_d(s)
            yield from phase_e(s)
            for c in chunks:
                yield from f_build(c)
                yield from f_project(c)

        side = _ffn_steps(*ffn_in_refs, ffn_out_ref) if fused_ffn else None
        _interleave_staggered([sequence(s) for s in range(n_seq)], SEQ_STAGGER,
                              side=side, side_period=FUSED_FFN_PERIOD)


def _const_spec(shape):
    nd = len(shape)
    return pl.BlockSpec(shape, lambda i, _n=nd: (0,) * _n,
                        pipeline_mode=pl.Buffered(1))


def _layer_spec(arr, layer):
    nd = arr.ndim - 1
    return pl.BlockSpec((None,) + arr.shape[1:], lambda i, _n=nd: (layer,) + (0,) * _n,
                        pipeline_mode=pl.Buffered(1))


def _mix_call(x2d, mods, layer, lw, *, seq, n_seq, latent, extra=None, ffn_h2d=None,
              ffn_rows_per_mod=None):
    n_tok = x2d.shape[0]
    T = seq * n_seq
    grid = (n_tok // T,)
    fused_ffn = ffn_h2d is not None
    n_keys_buf = T if not latent else seq + 256

    if latent:
        mod_map = lambda i: (layer, i, 0, 0)
    else:
        mod_map = lambda i: (layer, 4, 0, 0)

    consts = [lw["norm1_g"], lw["w_in"], lw["w_in_last"], lw["conv_w"], lw["conv_b"], lw["lru_conv_w"],
              lw["lru_conv_b"], lw["w_gates"], lw["b_gates"], lw["lru_lambda"],
              lw["mla_qnorm_g"], lw["wq_t"], lw["mla_kvnorm_g"], lw["wk"], lw["wv_t"],
              lw["qg_t"], lw["kg"], lw["gnorm_conv"], lw["gnorm_lru"], lw["gnorm_mla"],
              lw["w_out"]]
    args = [x2d, mods] + consts
    in_specs = [pl.BlockSpec((T, D_MODEL), lambda i: (i, 0)),
                pl.BlockSpec((None, None, 1, 6 * D_MODEL), mod_map)]
    in_specs += [_layer_spec(a, layer) for a in consts]
    if latent:
        lconsts = [lw["wq_partner_t"], lw["qg_partner_t"], lw["kg_partner"]]
        tables = [extra["rope_ct"], extra["rope_st"], extra["rope_c"], extra["rope_s"]]
        args += lconsts + tables
        in_specs += [_layer_spec(a, layer) for a in lconsts]
        in_specs += [_const_spec(a.shape) for a in tables]
        args += [extra["cache_ckv"], extra["cache_kr"], extra["state"]]
        in_specs += [
            pl.BlockSpec((None, None, 256, KV_LORA), lambda i: (i, layer, 0, 0)),
            pl.BlockSpec((None, None, 256, LANES), lambda i: (i, layer, 0, 0)),
            pl.BlockSpec((None, None, 2, LRU_W), lambda i: (i, layer, 0, 0)),
        ]

    if fused_ffn:
        ffn_rows = ffn_h2d.shape[0] // grid[0]
        tiles_per_mod = ffn_rows_per_mod // ffn_rows
        ffn_consts = [lw["norm2_g"], lw["w_gate"], lw["w_up"], lw["w_down"]]
        args += [ffn_h2d, mods] + ffn_consts
        in_specs += [pl.BlockSpec((ffn_rows, D_MODEL), lambda i: (i, 0)),
                     pl.BlockSpec((None, None, 1, 6 * D_MODEL),
                                  lambda i: (layer, i // tiles_per_mod, 0, 0))]
        in_specs += [_layer_spec(a, layer) for a in ffn_consts]

    out_shape = [jax.ShapeDtypeStruct((n_tok, D_MODEL), F32)]
    out_specs = [pl.BlockSpec((T, D_MODEL), lambda i: (i, 0))]
    if not latent:
        out_shape += [jax.ShapeDtypeStruct((n_tok, KV_LORA), F32),
                      jax.ShapeDtypeStruct((n_tok, QK_ROPE), F32),
                      jax.ShapeDtypeStruct((n_tok // seq, 2, LRU_W), F32)]
        out_specs += [pl.BlockSpec((T, KV_LORA), lambda i: (i, 0)),
                      pl.BlockSpec((T, QK_ROPE), lambda i: (i, 0)),
                      pl.BlockSpec((n_seq, 2, LRU_W), lambda i: (i, 0, 0))]
    if fused_ffn:
        out_shape.append(jax.ShapeDtypeStruct(ffn_h2d.shape, F32))
        out_specs.append(pl.BlockSpec((ffn_rows, D_MODEL), lambda i: (i, 0)))

    scratch = [
        pltpu.VMEM((T + 2 * PAD_ROWS, U_W), F32),
        pltpu.VMEM((MLA_HEADS, T // ROW_CHUNK, HEAD_PAD, ROW_CHUNK), BF16),
        pltpu.VMEM((MLA_HEADS, n_keys_buf, HEAD_PAD), BF16),
        pltpu.VMEM((n_keys_buf // ROW_CHUNK, MLA_HEADS * V_EXT, ROW_CHUNK), BF16),
        pltpu.VMEM((T // ROW_CHUNK, MLA_HEADS * V_DIM, ROW_CHUNK), F32),
        pltpu.VMEM((T, D_MODEL), BF16),
    ]
    if latent:
        scratch.append(pltpu.VMEM((2, 2, n_keys_buf, ROW_CHUNK), F32))
    return pl.pallas_call(
        functools.partial(_mix_kernel, seq=seq, n_seq=n_seq, latent=latent,
                          fused_ffn=fused_ffn),
        grid=grid,
        in_specs=in_specs,
        out_specs=out_specs,
        out_shape=out_shape,
        scratch_shapes=scratch,
        compiler_params=pltpu.CompilerParams(
            dimension_semantics=("arbitrary",),
            vmem_limit_bytes=VMEM_LIMIT_BYTES),
        name="mix_latent" if latent else ("mix_context_swiglu" if fused_ffn else "mix_context"),
    )(*args)


def _ffn_steps(h_ref, mod_ref, n2g_ref, wg_ref, wu_ref, wd_ref, o_ref):
    sh2 = mod_ref[:, 3 * D_MODEL:4 * D_MODEL]
    sc2 = mod_ref[:, 4 * D_MODEL:5 * D_MODEL]
    g2 = mod_ref[:, 5 * D_MODEL:6 * D_MODEL]
    h = h_ref[...]
    hn = (h * _rms_scale(h, D_MODEL) * (n2g_ref[...] * (1.0 + sc2)) + sh2).astype(BF16)
    yield
    acc = None
    for c0 in range(0, FF, FF_CHUNK):
        g = _dot(hn, wg_ref[:, c0:c0 + FF_CHUNK])
        u = _dot(hn, wu_ref[:, c0:c0 + FF_CHUNK])
        act = (g * _sigmoid(g) * u).astype(BF16)
        part = _dot(act, wd_ref[c0:c0 + FF_CHUNK, :])
        acc = part if acc is None else acc + part
        yield
    o_ref[...] = h_ref[...] + g2 * acc
    yield


def _ffn_kernel(*refs):
    _interleave(_ffn_steps(*refs))


def _ffn_call(h2d, mods, layer, lw, *, rows_per_mod, mod_base):
    n_tok = h2d.shape[0]
    if mod_base is None:
        tiles_per_mod = rows_per_mod // FFN_ROWS
        mod_map = lambda i: (layer, i // tiles_per_mod, 0, 0)
    else:
        mod_map = lambda i: (layer, mod_base, 0, 0)
    consts = [lw["norm2_g"], lw["w_gate"], lw["w_up"], lw["w_down"]]
    return pl.pallas_call(
        _ffn_kernel,
        grid=(n_tok // FFN_ROWS,),
        in_specs=[pl.BlockSpec((FFN_ROWS, D_MODEL), lambda i: (i, 0)),
                  pl.BlockSpec((None, None, 1, 6 * D_MODEL), mod_map)]
        + [_layer_spec(a, layer) for a in consts],
        out_specs=pl.BlockSpec((FFN_ROWS, D_MODEL), lambda i: (i, 0)),
        out_shape=jax.ShapeDtypeStruct((n_tok, D_MODEL), F32),
        compiler_params=pltpu.CompilerParams(
            dimension_semantics=("arbitrary",),
            vmem_limit_bytes=VMEM_LIMIT_BYTES),
        name="swiglu",
    )(h2d, mods, *consts)


def _rope_partner_index():
    return [d + 8 if (d % 16) < 8 else d - 8 for d in range(QK_ROPE)]


def _prep_weights(p):
    partner = jnp.array(_rope_partner_index(), jnp.int32)
    q_scale = QK_DIM ** -0.5 * math.log2(math.e)
    w_in = p["w_in"][:, :, :C_CKV].astype(BF16)
    w_tail = p["w_in"][:, :, C_CKV:].astype(BF16)
    w_kr = w_tail[:, :, KV_LORA:KV_LORA + QK_ROPE]
    zeros32 = jnp.zeros((DEPTH, D_MODEL, QK_ROPE), BF16)
    w_in_last = jnp.concatenate(
        [w_tail[:, :, :KV_LORA], w_kr[:, :, partner], zeros32, w_kr, zeros32], axis=2)

    wq_h = p["mla_wq_up"].reshape(DEPTH, Q_LORA, MLA_HEADS, QK_DIM)
    wq_pad = jnp.pad(wq_h, ((0, 0), (0, 0), (0, 0), (0, HEAD_PAD - QK_DIM)))
    wq_t = wq_pad.reshape(DEPTH, Q_LORA, MLA_HEADS * HEAD_PAD).transpose(0, 2, 1)
    rope_pad = ((0, 0), (0, 0), (0, 0), (ROPE_LO, HEAD_PAD - ROPE_LO - QK_ROPE))
    wq_partner = jnp.pad(wq_h[:, :, :, QK_NOPE:][:, :, :, partner], rope_pad)
    wq_partner_t = wq_partner.reshape(DEPTH, Q_LORA, MLA_HEADS * HEAD_PAD).transpose(0, 2, 1)

    wkv = p["mla_wkv_up"].reshape(DEPTH, KV_LORA, MLA_HEADS, QK_NOPE + V_DIM)
    wk_pad = jnp.pad(wkv[:, :, :, :QK_NOPE], ((0, 0), (0, 0), (0, 0), (0, HEAD_PAD - QK_NOPE)))
    wk_pad = wk_pad.reshape(DEPTH, KV_LORA, MLA_HEADS * HEAD_PAD)
    wv_ext = jnp.pad(wkv[:, :, :, QK_NOPE:], ((0, 0), (0, 0), (0, 0), (0, V_EXT - V_DIM)))
    wv_t = wv_ext.reshape(DEPTH, KV_LORA, MLA_HEADS * V_EXT).transpose(0, 2, 1)

    def pad_gain(g):
        return jnp.pad(g, ((0, 0), (0, HEAD_PAD - QK_DIM)))[:, None, :]

    def partner_gain(g):
        gp = g[:, QK_NOPE:][:, partner]
        return jnp.pad(gp, ((0, 0), (ROPE_LO, HEAD_PAD - ROPE_LO - QK_ROPE)))[:, None, :]

    def on_sublanes(g):
        return jnp.broadcast_to(g.transpose(0, 2, 1), (DEPTH, HEAD_PAD, ROW_CHUNK))

    qg, kg = p["q_norm_g"], p["k_norm_g"]
    eye = jnp.eye(LRU_BLOCKS, dtype=F32)
    gate_w = jnp.stack([p["lru_wa"][:, 0], p["lru_wi"][:, 0],
                        p["lru_wa"][:, 1], p["lru_wi"][:, 1]], axis=1)
    w_gates = (gate_w[:, :, :, :, None, :] * eye[None, None, :, None, :, None])
    w_gates = w_gates.transpose(0, 2, 3, 1, 4, 5).reshape(DEPTH, LRU_W, 4 * LRU_W)
    b_gates = jnp.stack([p["lru_ba"][:, 0], p["lru_bi"][:, 0],
                         p["lru_ba"][:, 1], p["lru_bi"][:, 1]], axis=1)

    def row(v):
        return v[:, None, :]

    return {
        "norm1_g": row(p["norm1_g"]),
        "w_in": w_in,
        "w_in_last": w_in_last,
        "conv_w": p["conv_w"],
        "conv_b": row(p["conv_b"]),
        "lru_conv_w": p["lru_conv_w"],
        "lru_conv_b": row(p["lru_conv_b"]),
        "w_gates": (0.5 * w_gates).astype(BF16),
        "b_gates": 0.5 * b_gates.reshape(DEPTH, 1, 4 * LRU_W),
        "lru_lambda": p["lru_lambda"],
        "mla_qnorm_g": row(p["mla_qnorm_g"]),
        "wq_t": wq_t.astype(BF16),
        "wq_partner_t": wq_partner_t.astype(BF16),
        "mla_kvnorm_g": row(p["mla_kvnorm_g"]),
        "wk": wk_pad.astype(BF16),
        "wv_t": wv_t.astype(BF16),
        "qg_t": on_sublanes(pad_gain(qg) * q_scale),
        "qg_partner_t": on_sublanes(partner_gain(qg) * q_scale),
        "kg": pad_gain(kg),
        "kg_partner": partner_gain(kg),
        "gnorm_conv": row(p["gnorm_conv"]),
        "gnorm_lru": row(p["gnorm_lru"]),
        "gnorm_mla": row(p["gnorm_mla"]),
        "w_out": p["w_out"].astype(BF16),
        "norm2_g": row(p["norm2_g"]),
        "w_gate": p["w_gate"].astype(BF16),
        "w_up": p["w_up"].astype(BF16),
        "w_down": p["w_down"].astype(BF16),
    }


def _rope_tables(n_tokens):
    n_rows = n_tokens // GRID_W
    row = np.repeat(np.arange(n_rows, dtype=np.float32), GRID_W)
    col = np.tile(np.arange(GRID_W, dtype=np.float32), n_rows)
    n_freq = QK_ROPE // 4
    inv_freq = np.power(np.float32(ROPE_THETA),
                        -np.arange(n_freq, dtype=np.float32) / np.float32(n_freq))
    ang_r = (row[:, None] * inv_freq).astype(np.float32)
    ang_c = (col[:, None] * inv_freq).astype(np.float32)
    cos = np.concatenate([np.cos(ang_r), np.cos(ang_r), np.cos(ang_c), np.cos(ang_c)], 1)
    sin = np.concatenate([-np.sin(ang_r), np.sin(ang_r), -np.sin(ang_c), np.sin(ang_c)], 1)
    ones = np.ones((n_tokens, QK_NOPE), np.float32)
    tail = np.zeros((n_tokens, HEAD_PAD - QK_DIM), np.float32)
    rope_c = np.concatenate([ones, cos, tail], axis=1).astype(np.float32)
    rope_s = np.concatenate([np.zeros((n_tokens, QK_NOPE), np.float32), sin, tail],
                            axis=1).astype(np.float32)
    return rope_c, rope_s


def kernel(x_prompt, x_sample, cache_ckv, cache_krope, state_lru, c, c_ctx, norm1_g, ada_w, ada_b, w_in, conv_w, conv_b, lru_conv_w, lru_conv_b, lru_wa, lru_ba, lru_wi, lru_bi, lru_lambda, mla_qnorm_g, mla_wq_up, mla_kvnorm_g, mla_wkv_up, q_norm_g, k_norm_g, gnorm_conv, gnorm_lru, gnorm_mla, w_out, norm2_g, w_gate, w_up, w_down):
    params = dict(norm1_g=norm1_g, w_in=w_in, conv_w=conv_w, conv_b=conv_b,
                  lru_conv_w=lru_conv_w, lru_conv_b=lru_conv_b, lru_wa=lru_wa,
                  lru_ba=lru_ba, lru_wi=lru_wi, lru_bi=lru_bi, lru_lambda=lru_lambda,
                  mla_qnorm_g=mla_qnorm_g, mla_wq_up=mla_wq_up,
                  mla_kvnorm_g=mla_kvnorm_g, mla_wkv_up=mla_wkv_up, q_norm_g=q_norm_g,
                  k_norm_g=k_norm_g, gnorm_conv=gnorm_conv, gnorm_lru=gnorm_lru,
                  gnorm_mla=gnorm_mla, w_out=w_out, norm2_g=norm2_g, w_gate=w_gate,
                  w_up=w_up, w_down=w_down)
    batch, seq_p, _ = x_prompt.shape
    dec_batch, seq_s, _ = x_sample.shape
    assert dec_batch == 4 and cache_ckv.shape[2] == 256

    cond8 = jnp.concatenate(
        [c, c_ctx[None, :], jnp.zeros((8 - dec_batch - 1, D_MODEL), F32)], axis=0)
    mods = _ada_call(cond8, ada_w, ada_b).reshape(DEPTH, 8, 1, 6 * D_MODEL)

    rope_c, rope_s = _rope_tables(seq_s)
    cache_kr_pad = jnp.pad(
        cache_krope, ((0, 0), (0, 0), (0, 0), (ROPE_LO, HEAD_PAD - ROPE_LO - QK_ROPE)))

    def chunked_t(tab):
        return jnp.asarray(np.ascontiguousarray(
            tab.reshape(seq_s // ROW_CHUNK, ROW_CHUNK, HEAD_PAD).transpose(0, 2, 1)))

    extra = dict(rope_c=jnp.asarray(rope_c), rope_s=jnp.asarray(rope_s),
                 rope_ct=chunked_t(rope_c), rope_st=chunked_t(rope_s), cache_ckv=cache_ckv,
                 cache_kr=cache_kr_pad, state=state_lru)

    xp = x_prompt.reshape(batch * seq_p, D_MODEL)
    xs = x_sample.reshape(dec_batch * seq_s, D_MODEL)
    ckv_l, kr_l, st_l = [], [], []
    lw = _prep_weights(params)
    for l in range(DEPTH):
        (hs,) = _mix_call(xs, mods, l, lw, seq=seq_s, n_seq=1, latent=True, extra=extra)
        hp, ckv, kr, st, xs = _mix_call(xp, mods, l, lw, seq=seq_p, n_seq=CTX_SEQS_PER_STEP,
                                        latent=False, ffn_h2d=hs, ffn_rows_per_mod=seq_s)
        xp = _ffn_call(hp, mods, l, lw, rows_per_mod=None, mod_base=4)
        ckv_l.append(ckv.reshape(batch, seq_p, KV_LORA))
        kr_l.append(kr.reshape(batch, seq_p, QK_ROPE))
        st_l.append(st)
    return (xp.reshape(batch, seq_p, D_MODEL),
            xs.reshape(dec_batch, seq_s, D_MODEL),
            jnp.stack(ckv_l, axis=1),
            jnp.stack(kr_l, axis=1),
            jnp.stack(st_l, axis=1))
```

```python
import functools
import itertools
import math

import jax
import jax.numpy as jnp
import numpy as np
from jax import lax
from jax.experimental import pallas as pl
from jax.experimental.pallas import tpu as pltpu

F32 = jnp.float32
BF16 = jnp.bfloat16

D_MODEL = 1024
DEPTH = 2
GRID_W = 64
CONV_W = 256
LRU_W = 256
LRU_BLOCKS = 4
LRU_BLK = 64
LRU_C = 8.0
MLA_HEADS = 8
QK_NOPE = 64
QK_ROPE = 32
V_DIM = 64
V_EXT = V_DIM + 16
QK_DIM = QK_NOPE + QK_ROPE
Q_LORA = 256
KV_LORA = 128
ROPE_THETA = 10000.0
FF = 2816
EPS = 1e-6

LANES = 128
SUBLANES = 8
HEAD_PAD = LANES
VMEM_LIMIT_BYTES = 63 * 1024 * 1024

C_BG, C_CG, C_H = 0, 256, 512
C_XB, C_YB = 768, 1024
C_Q = 1280
C_CKV = 1536
C_KR = 1664
U_W = 1792
SCAN_A = (C_H, C_CKV)
SCAN_B = (C_Q, C_BG)
ROPE_LO = QK_NOPE

ROW_CHUNK = 256
SCAN_CHUNK = 32
Q_CHUNK = 256
FF_CHUNK = 256
PROJ_SLAB = 256
PROJ_CHUNKS = 1
CTX_SEQS_PER_STEP = 2
FUSED_FFN_PERIOD = 4
SEQ_STAGGER = 10
FFN_ROWS = 1024
PAD_ROWS = SUBLANES


def _rms_scale(x, n):
    ms = jnp.sum(x * x, axis=-1, keepdims=True) * (1.0 / n)
    return lax.rsqrt(ms + EPS)


def _sigmoid(x):
    return 0.5 * jnp.tanh(0.5 * x) + 0.5


def _gelu_tanh(x):
    c = math.sqrt(2.0 / math.pi)
    return 0.5 * x * (1.0 + jnp.tanh(c * (x + 0.044715 * (x * x * x))))


def _dot(a, b):
    return jnp.dot(a, b, preferred_element_type=F32)


def _interleave(*gens):
    live = list(gens)
    while live:
        for g in list(live):
            try:
                next(g)
            except StopIteration:
                live.remove(g)


def _interleave_staggered(gens, lag, side=None, side_period=1):
    pending = list(gens)
    live = []
    tick = 0
    while live or pending or side is not None:
        if pending and tick % lag == 0:
            live.append(pending.pop(0))
        if side is not None and (tick % side_period == 0 or not (live or pending)):
            try:
                next(side)
            except StopIteration:
                side = None
        for g in list(live):
            try:
                next(g)
            except StopIteration:
                live.remove(g)
        tick += 1


def _dot_nt(a, b):
    return lax.dot_general(a, b, (((1,), (1,)), ((), ())), preferred_element_type=F32)


ADA_TN = 1536


def _ada_kernel(cond_ref, w_ref, b_ref, o_ref):
    cnd = cond_ref[...]
    s = (cnd * _sigmoid(cnd)).astype(BF16)
    o_ref[...] = _dot(s, w_ref[...].astype(BF16)) + b_ref[...]


def _ada_call(cond8, ada_w, ada_b):
    n_out = 6 * D_MODEL
    return pl.pallas_call(
        _ada_kernel,
        grid=(DEPTH, n_out // ADA_TN),
        in_specs=[
            pl.BlockSpec((8, D_MODEL), lambda l, j: (0, 0)),
            pl.BlockSpec((None, D_MODEL, ADA_TN), lambda l, j: (l, 0, j)),
            pl.BlockSpec((None, 1, ADA_TN), lambda l, j: (l, 0, j)),
        ],
        out_specs=pl.BlockSpec((None, 8, ADA_TN), lambda l, j: (l, 0, j)),
        out_shape=jax.ShapeDtypeStruct((DEPTH, 8, n_out), F32),
        compiler_params=pltpu.CompilerParams(
            dimension_semantics=("arbitrary", "arbitrary"),
            vmem_limit_bytes=VMEM_LIMIT_BYTES),
        name="ada_mod",
    )(cond8, ada_w, ada_b.reshape(DEPTH, 1, n_out))


def _mix_kernel(*refs, seq, n_seq, latent, fused_ffn):
    T = seq * n_seq
    n_ctx = 256 if latent else 0
    (x_ref, mod_ref, n1g_ref, win_ref, wlast_ref, convw_ref, convb_ref, lcw_ref, lcb_ref,
     wg_ref, bg_ref, lam_ref, qng_ref, wqt_ref, kvng_ref, wk_ref, wvt_ref,
     qgt_ref, kg_ref, gnc_ref, gnl_ref, gnm_ref, wout_ref) = refs[:23]
    pos = 23
    if latent:
        (wqpt_ref, qgpt_ref, kgp_ref, ropect_ref, ropest_ref, ropec_ref, ropes_ref,
         cckv_ref, ckr_ref, h0_ref) = refs[pos:pos + 10]
        pos += 10
    if fused_ffn:
        ffn_in_refs = refs[pos:pos + 6]
        pos += 6
    h_out_ref = refs[pos]
    pos += 1
    if not latent:
        ckv_out_ref, kr_out_ref, st_out_ref = refs[pos:pos + 3]
        pos += 3
    if fused_ffn:
        ffn_out_ref = refs[pos]
        pos += 1
    u_scr, qt_scr, k_scr, vt_scr, ymt_scr, ycat_scr = refs[pos:pos + 6]
    if latent:
        st_scr = refs[pos + 6]

    n_chunks = T // ROW_CHUNK
    lane = lax.broadcasted_iota(jnp.int32, (1, LANES), 1)
    rope_lanes = (lane >= ROPE_LO) & (lane < ROPE_LO + QK_ROPE)

    sh1 = mod_ref[:, 0:D_MODEL]
    sc1 = mod_ref[:, D_MODEL:2 * D_MODEL]
    g1 = mod_ref[:, 2 * D_MODEL:3 * D_MODEL]
    gain1 = n1g_ref[...] * (1.0 + sc1)

    u_scr[0:PAD_ROWS, :] = jnp.zeros((PAD_ROWS, U_W), F32)
    u_scr[PAD_ROWS + T:2 * PAD_ROWS + T, :] = jnp.zeros((PAD_ROWS, U_W), F32)

    def phase_a(c, n=1):
        r0 = c * ROW_CHUNK
        nrows = n * ROW_CHUNK
        x = x_ref[r0:r0 + nrows, :]
        hn = (x * _rms_scale(x, D_MODEL) * gain1 + sh1).astype(BF16)
        yield
        urows = slice(PAD_ROWS + r0, PAD_ROWS + r0 + nrows)
        for s0 in range(0, C_CKV, PROJ_SLAB):
            u_scr[urows, s0:s0 + PROJ_SLAB] = _dot(
                hn, win_ref[:, s0:s0 + PROJ_SLAB].astype(BF16))
            yield
        u_scr[urows, C_CKV:U_W] = _dot(hn, wlast_ref[...])
        yield
        u_scr[urows, C_CG:C_CG + CONV_W] = (
            u_scr[urows, C_CG:C_CG + CONV_W] * u_scr[urows, C_H:C_H + CONV_W])
        yield

    vrow = lax.broadcasted_iota(jnp.int32, (MLA_HEADS * V_EXT, 1), 0)
    ones_rows = (vrow % V_EXT) >= V_DIM

    def values_t(cb):
        return jnp.where(ones_rows, 1.0, _dot_nt(wvt_ref[...], cb)).astype(BF16)

    def phase_b(c):
        r0 = c * ROW_CHUNK
        rows = slice(PAD_ROWS + r0, PAD_ROWS + r0 + ROW_CHUNK)
        uq = u_scr[rows, C_Q:C_Q + Q_LORA]
        qn = (uq * _rms_scale(uq, Q_LORA) * qng_ref[...]).astype(BF16)
        qa_t = _dot_nt(wqt_ref[...], qn)
        if latent:
            qpa_t = _dot_nt(wqpt_ref[...], qn)
            q_tc = ropect_ref[c] * qgt_ref[...]
            q_ts = ropest_ref[c] * qgpt_ref[...]
        yield
        for h in range(MLA_HEADS):
            sl = slice(h * HEAD_PAD, (h + 1) * HEAD_PAD)
            qh = qa_t[sl, :]
            rq = lax.rsqrt(jnp.sum(qh * qh, axis=0, keepdims=True) * (1.0 / QK_DIM) + EPS)
            if latent:
                qh = (qh * q_tc + qpa_t[sl, :] * q_ts) * rq
            else:
                qh = qh * qgt_ref[...] * rq
            qt_scr[h, c] = qh.astype(BF16)
            if h % 2:
                yield
        uc = u_scr[rows, C_CKV:C_CKV + KV_LORA]
        ckv = uc * _rms_scale(uc, KV_LORA) * kvng_ref[...]
        krb = u_scr[rows, C_KR:C_KR + LANES]
        kr_rolled = pltpu.roll(krb, 64, 1)
        if not latent:
            ckv_out_ref[r0:r0 + ROW_CHUNK, :] = ckv
            kr_out_ref[r0:r0 + ROW_CHUNK, :] = kr_rolled[:, 0:QK_ROPE]
        krm = jnp.where(rope_lanes, krb, 0.0)
        cb = ckv.astype(BF16)
        ka = _dot(cb, wk_ref[...])
        vt_scr[c] = values_t(cb)
        if latent:
            k_tc = ropec_ref[r0:r0 + ROW_CHUNK, :] * kg_ref[...]
            k_ts = ropes_ref[r0:r0 + ROW_CHUNK, :] * kgp_ref[...]
        yield
        for h in range(MLA_HEADS):
            sl = slice(h * HEAD_PAD, (h + 1) * HEAD_PAD)
            kpre = ka[:, sl] + krm
            rk = _rms_scale(kpre, QK_DIM)
            if latent:
                kh = (kpre * k_tc + kr_rolled * k_ts) * rk
            else:
                kh = kpre * kg_ref[...] * rk
            k_scr[h, r0:r0 + ROW_CHUNK, :] = kh.astype(BF16)
            if h % 2:
                yield

    if latent:
        cc = cckv_ref[...].astype(BF16)
        ka = _dot(cc, wk_ref[...])
        vt_scr[n_chunks] = values_t(cc)
        krc = ckr_ref[...]
        for h in range(MLA_HEADS):
            kpre = ka[:, h * HEAD_PAD:(h + 1) * HEAD_PAD] + krc
            kh = kpre * kg_ref[...] * _rms_scale(kpre, QK_DIM)
            k_scr[h, seq:seq + n_ctx, :] = kh.astype(BF16)

    neg_lam = -lam_ref[...]
    sp = jnp.maximum(neg_lam, 0.0) + jnp.log(1.0 + jnp.exp(-jnp.abs(neg_lam)))
    log2a_coef = (-0.5 * LRU_C * math.log2(math.e)) * sp
    row_rc = lax.broadcasted_iota(jnp.int32, (ROW_CHUNK, 1), 0)

    def phase_c(ci):
        r0 = ci * ROW_CHUNK
        base = PAD_ROWS + r0
        first = (r0 % seq) == 0
        last = ((r0 + ROW_CHUNK) % seq) == 0

        def win(col, shift, width=CONV_W):
            w = u_scr[base + shift:base + shift + ROW_CHUNK, col:col + width]
            if shift < 0 and first:
                w = jnp.where(row_rc < -shift, 0.0, w)
            if shift > 0 and last:
                w = jnp.where(row_rc >= ROW_CHUNK - shift, 0.0, w)
            return w

        z_m, z_0, z_p = win(C_CG, -1), win(C_CG, 0), win(C_CG, 1)
        conv = (z_m * convw_ref[0:1, :] + z_0 * convw_ref[1:2, :]
                + z_p * convw_ref[2:3, :] + convb_ref[...])
        yc = u_scr[base:base + ROW_CHUNK, C_BG:C_BG + CONV_W] * conv
        ycn = yc * _rms_scale(yc, CONV_W) * gnc_ref[...]
        ycat_scr[r0:r0 + ROW_CHUNK, 0:CONV_W] = ycn.astype(BF16)
        yield

        xc = (win(C_XB, -2) * lcw_ref[0:1, :] + win(C_XB, -1) * lcw_ref[1:2, :]
              + win(C_XB, 0) * lcw_ref[2:3, :] + win(C_XB, 1) * lcw_ref[3:4, :]
              + lcb_ref[...])
        half_gates = _dot(xc.astype(BF16), wg_ref[...]) + bg_ref[...]
        half_xc = 0.5 * xc
        yield
        for d in range(2):
            t_r = jnp.tanh(half_gates[:, (2 * d) * LRU_W:(2 * d + 1) * LRU_W])
            t_i = jnp.tanh(half_gates[:, (2 * d + 1) * LRU_W:(2 * d + 2) * LRU_W])
            a = jnp.exp2(log2a_coef[d:d + 1, :] * (t_r + 1.0))
            v = 1.0 - a * a
            mult = v * lax.rsqrt(jnp.maximum(v, 1e-30))
            u_scr[base:base + ROW_CHUNK, SCAN_A[d]:SCAN_A[d] + LRU_W] = a
            u_scr[base:base + ROW_CHUNK, SCAN_B[d]:SCAN_B[d] + LRU_W] = (
                mult * (half_xc * (t_i + 1.0)))
            yield

    n_sc = seq // SCAN_CHUNK
    row_sc = lax.broadcasted_iota(jnp.int32, (SCAN_CHUNK, LRU_W), 0)
    steps = [1 << i for i in range(int(math.log2(SCAN_CHUNK)))]

    def scan_step(rf, rb, cf, cb_):
        rf = pl.multiple_of(rf + PAD_ROWS, SUBLANES)
        rb = pl.multiple_of(rb + PAD_ROWS, SUBLANES)
        a = u_scr[pl.ds(rf, SCAN_CHUNK), SCAN_A[0]:SCAN_A[0] + LRU_W]
        b = u_scr[pl.ds(rf, SCAN_CHUNK), SCAN_B[0]:SCAN_B[0] + LRU_W]
        b = b + jnp.where(row_sc == 0, a * cf, 0.0)
        for d in steps:
            b = b + a * jnp.where(row_sc >= d, pltpu.roll(b, d, 0), 0.0)
            if d != steps[-1]:
                a = a * jnp.where(row_sc >= d, pltpu.roll(a, d, 0), 1.0)
        u_scr[pl.ds(rf, SCAN_CHUNK), SCAN_B[0]:SCAN_B[0] + LRU_W] = b
        cf = b[SCAN_CHUNK - 1:SCAN_CHUNK, :]
        a = u_scr[pl.ds(rb, SCAN_CHUNK), SCAN_A[1]:SCAN_A[1] + LRU_W]
        b = u_scr[pl.ds(rb, SCAN_CHUNK), SCAN_B[1]:SCAN_B[1] + LRU_W]
        b = b + jnp.where(row_sc == SCAN_CHUNK - 1, a * cb_, 0.0)
        for d in steps:
            keep = row_sc < SCAN_CHUNK - d
            b = b + a * jnp.where(keep, pltpu.roll(b, SCAN_CHUNK - d, 0), 0.0)
            if d != steps[-1]:
                a = a * jnp.where(keep, pltpu.roll(a, SCAN_CHUNK - d, 0), 1.0)
        u_scr[pl.ds(rb, SCAN_CHUNK), SCAN_B[1]:SCAN_B[1] + LRU_W] = b
        cb_ = b[0:1, :]
        return cf, cb_

    cps = seq // ROW_CHUNK

    def softmax_pv(st, vts):
        m = jnp.max(st, axis=0, keepdims=True)
        pt = jnp.exp2(st - m).astype(BF16)
        ot = None
        for i, vt in enumerate(vts):
            part = _dot(vt, pt[i * ROW_CHUNK:(i + 1) * ROW_CHUNK, :])
            ot = part if ot is None else ot + part
        return ot[0:V_DIM, :] * (1.0 / ot[V_DIM:V_DIM + 1, :])

    def f_build(c):
        r0 = c * ROW_CHUNK
        rows = slice(r0, r0 + ROW_CHUNK)
        urows = slice(PAD_ROWS + r0, PAD_ROWS + r0 + ROW_CHUNK)
        hf = u_scr[urows, SCAN_B[0]:SCAN_B[0] + LRU_W]
        hb = u_scr[urows, SCAN_B[1]:SCAN_B[1] + LRU_W]
        yb = u_scr[urows, C_YB:C_YB + LRU_W]
        yl = (hf + hb) * _gelu_tanh(yb)
        yln = yl * _rms_scale(yl, LRU_W) * gnl_ref[...]
        ycat_scr[rows, CONV_W:CONV_W + LRU_W] = yln.astype(BF16)
        yield
        ymt = ymt_scr[c]
        ssq = jnp.sum(ymt * ymt, axis=0, keepdims=True)
        ynt = ymt * lax.rsqrt(ssq * (1.0 / (MLA_HEADS * V_DIM)) + EPS)
        ycat_scr[rows, CONV_W + LRU_W:D_MODEL] = (ynt.T * gnm_ref[...]).astype(BF16)
        yield

    def f_project(c):
        rows = slice(c * ROW_CHUNK, (c + 1) * ROW_CHUNK)
        for n0 in range(0, D_MODEL, PROJ_SLAB):
            cols = slice(n0, n0 + PROJ_SLAB)
            y = _dot(ycat_scr[rows, :], wout_ref[:, cols])
            h_out_ref[rows, cols] = x_ref[rows, cols] + g1[:, cols] * y
            yield

    if latent:
        groups = [list(range(g0, g0 + PROJ_CHUNKS)) for g0 in range(0, n_chunks, PROJ_CHUNKS)]
        _interleave(phase_a(0, PROJ_CHUNKS))
        for gi, group in enumerate(groups):
            dependent = []
            for c in group:
                dependent.append(phase_b(c))
                if c:
                    dependent.append(phase_c(c - 1))
            ahead = [phase_a(groups[gi + 1][0], PROJ_CHUNKS)] if gi + 1 < len(groups) else []
            _interleave(*ahead, itertools.chain(*dependent))
        _interleave(phase_c(n_chunks - 1))

        key_chunks = list(range(cps)) + [n_chunks]
        n_blk = cps * (MLA_HEADS // 2)
        trips = n_blk // 2
        scans_per_trip = n_sc // trips

        def heads_of(idx):
            qc = idx // (MLA_HEADS // 2)
            j = idx % (MLA_HEADS // 2)
            return qc, [2 * j, 2 * j + 1]

        def scores(idx, slot):
            qc, hs = heads_of(idx)
            for par, h in enumerate(hs):
                st_scr[slot, par] = _dot(k_scr[h], qt_scr[h, qc])

        def finish(idx, slot):
            qc, hs = heads_of(idx)
            for par, h in enumerate(hs):
                hr = pl.ds(pl.multiple_of(h * V_DIM, V_DIM), V_DIM)
                vr = pl.ds(pl.multiple_of(h * V_EXT, V_EXT - V_DIM), V_EXT)
                vts = [vt_scr[kc, vr, :] for kc in key_chunks]
                ymt_scr[qc, hr, :] = softmax_pv(st_scr[slot, par], vts)

        scores(0, 0)

        def att_body(i, carry):
            b0 = 2 * i
            attention = [
                lambda: scores(b0 + 1, 1),
                lambda: finish(b0, 0),
                lambda: scores(jnp.minimum(b0 + 2, n_blk - 1), 0),
                lambda: finish(b0 + 1, 1),
            ]
            for k in range(max(len(attention), scans_per_trip)):
                if k < len(attention):
                    attention[k]()
                if k < scans_per_trip:
                    c = i * scans_per_trip + k
                    rf = pl.multiple_of(c * SCAN_CHUNK, SCAN_CHUNK)
                    rb = pl.multiple_of((n_sc - 1 - c) * SCAN_CHUNK, SCAN_CHUNK)
                    carry = scan_step(rf, rb, *carry)
            return carry

        lax.fori_loop(0, trips, att_body, (h0_ref[0:1, :], h0_ref[1:2, :]))

        _interleave(f_build(0))
        for c in range(n_chunks):
            ahead = [f_build(c + 1)] if c + 1 < n_chunks else []
            _interleave(f_project(c), *ahead)
    else:
        def phase_d(s):
            carry = (jnp.zeros((1, LRU_W), F32), jnp.zeros((1, LRU_W), F32))
            for c in range(n_sc):
                carry = scan_step(s * seq + c * SCAN_CHUNK,
                                  s * seq + (n_sc - 1 - c) * SCAN_CHUNK, *carry)
                yield
            st_out_ref[s, 0:1, :] = carry[0]
            st_out_ref[s, 1:2, :] = carry[1]

        def phase_e(s):
            blocks = [(qi, h) for qi in range(cps) for h in range(MLA_HEADS)]

            def scores(blk):
                qi, h = blk
                return _dot(k_scr[h, s * seq:(s + 1) * seq, :], qt_scr[h, s * cps + qi])

            st = scores(blocks[0])
            for i, (qi, h) in enumerate(blocks):
                nxt = scores(blocks[i + 1]) if i + 1 < len(blocks) else None
                hr = slice(h * V_DIM, (h + 1) * V_DIM)
                vr = slice(h * V_EXT, (h + 1) * V_EXT)
                vts = [vt_scr[s * cps + j, vr, :] for j in range(cps)]
                ymt_scr[s * cps + qi, hr, :] = softmax_pv(st, vts)
                st = nxt
                yield

        def sequence(s):
            chunks = range(s * cps, (s + 1) * cps)
            for c in chunks:
                yield from phase_a(c)
            for c in chunks:
                yield from phase_b(c)
            for c in chunks:
                yield from phase_c(c)
            yield from phase_d(s)
            yield from phase_e(s)
            for c in chunks:
                yield from f_build(c)
                yield from f_project(c)

        side = _ffn_steps(*ffn_in_refs, ffn_out_ref) if fused_ffn else None
        _interleave_staggered([sequence(s) for s in range(n_seq)], SEQ_STAGGER,
                              side=side, side_period=FUSED_FFN_PERIOD)


def _const_spec(shape):
    nd = len(shape)
    return pl.BlockSpec(shape, lambda i, _n=nd: (0,) * _n,
                        pipeline_mode=pl.Buffered(1))


def _layer_spec(arr, layer):
    nd = arr.ndim - 1
    return pl.BlockSpec((None,) + arr.shape[1:], lambda i, _n=nd: (layer,) + (0,) * _n,
                        pipeline_mode=pl.Buffered(1))


def _mix_call(x2d, mods, layer, lw, *, seq, n_seq, latent, extra=None, ffn_h2d=None,
              ffn_rows_per_mod=None):
    n_tok = x2d.shape[0]
    T = seq * n_seq
    grid = (n_tok // T,)
    fused_ffn = ffn_h2d is not None
    n_keys_buf = T if not latent else seq + 256

    if latent:
        mod_map = lambda i: (layer, i, 0, 0)
    else:
        mod_map = lambda i: (layer, 4, 0, 0)

    consts = [lw["norm1_g"], lw["w_in"], lw["w_in_last"], lw["conv_w"], lw["conv_b"], lw["lru_conv_w"],
              lw["lru_conv_b"], lw["w_gates"], lw["b_gates"], lw["lru_lambda"],
              lw["mla_qnorm_g"], lw["wq_t"], lw["mla_kvnorm_g"], lw["wk"], lw["wv_t"],
              lw["qg_t"], lw["kg"], lw["gnorm_conv"], lw["gnorm_lru"], lw["gnorm_mla"],
              lw["w_out"]]
    args = [x2d, mods] + consts
    in_specs = [pl.BlockSpec((T, D_MODEL), lambda i: (i, 0)),
                pl.BlockSpec((None, None, 1, 6 * D_MODEL), mod_map)]
    in_specs += [_layer_spec(a, layer) for a in consts]
    if latent:
        lconsts = [lw["wq_partner_t"], lw["qg_partner_t"], lw["kg_partner"]]
        tables = [extra["rope_ct"], extra["rope_st"], extra["rope_c"], extra["rope_s"]]
        args += lconsts + tables
        in_specs += [_layer_spec(a, layer) for a in lconsts]
        in_specs += [_const_spec(a.shape) for a in tables]
        args += [extra["cache_ckv"], extra["cache_kr"], extra["state"]]
        in_specs += [
            pl.BlockSpec((None, None, 256, KV_LORA), lambda i: (i, layer, 0, 0)),
            pl.BlockSpec((None, None, 256, LANES), lambda i: (i, layer, 0, 0)),
            pl.BlockSpec((None, None, 2, LRU_W), lambda i: (i, layer, 0, 0)),
        ]

    if fused_ffn:
        ffn_rows = ffn_h2d.shape[0] // grid[0]
        tiles_per_mod = ffn_rows_per_mod // ffn_rows
        ffn_consts = [lw["norm2_g"], lw["w_gate"], lw["w_up"], lw["w_down"]]
        args += [ffn_h2d, mods] + ffn_consts
        in_specs += [pl.BlockSpec((ffn_rows, D_MODEL), lambda i: (i, 0)),
                     pl.BlockSpec((None, None, 1, 6 * D_MODEL),
                                  lambda i: (layer, i // tiles_per_mod, 0, 0))]
        in_specs += [_layer_spec(a, layer) for a in ffn_consts]

    out_shape = [jax.ShapeDtypeStruct((n_tok, D_MODEL), F32)]
    out_specs = [pl.BlockSpec((T, D_MODEL), lambda i: (i, 0))]
    if not latent:
        out_shape += [jax.ShapeDtypeStruct((n_tok, KV_LORA), F32),
                      jax.ShapeDtypeStruct((n_tok, QK_ROPE), F32),
                      jax.ShapeDtypeStruct((n_tok // seq, 2, LRU_W), F32)]
        out_specs += [pl.BlockSpec((T, KV_LORA), lambda i: (i, 0)),
                      pl.BlockSpec((T, QK_ROPE), lambda i: (i, 0)),
                      pl.BlockSpec((n_seq, 2, LRU_W), lambda i: (i, 0, 0))]
    if fused_ffn:
        out_shape.append(jax.ShapeDtypeStruct(ffn_h2d.shape, F32))
        out_specs.append(pl.BlockSpec((ffn_rows, D_MODEL), lambda i: (i, 0)))

    scratch = [
        pltpu.VMEM((T + 2 * PAD_ROWS, U_W), F32),
        pltpu.VMEM((MLA_HEADS, T // ROW_CHUNK, HEAD_PAD, ROW_CHUNK), BF16),
        pltpu.VMEM((MLA_HEADS, n_keys_buf, HEAD_PAD), BF16),
        pltpu.VMEM((n_keys_buf // ROW_CHUNK, MLA_HEADS * V_EXT, ROW_CHUNK), BF16),
        pltpu.VMEM((T // ROW_CHUNK, MLA_HEADS * V_DIM, ROW_CHUNK), F32),
        pltpu.VMEM((T, D_MODEL), BF16),
    ]
    if latent:
        scratch.append(pltpu.VMEM((2, 2, n_keys_buf, ROW_CHUNK), F32))
    return pl.pallas_call(
        functools.partial(_mix_kernel, seq=seq, n_seq=n_seq, latent=latent,
                          fused_ffn=fused_ffn),
        grid=grid,
        in_specs=in_specs,
        out_specs=out_specs,
        out_shape=out_shape,
        scratch_shapes=scratch,
        compiler_params=pltpu.CompilerParams(
            dimension_semantics=("arbitrary",),
            vmem_limit_bytes=VMEM_LIMIT_BYTES),
        name="mix_latent" if latent else ("mix_context_swiglu" if fused_ffn else "mix_context"),
    )(*args)


def _ffn_steps(h_ref, mod_ref, n2g_ref, wg_ref, wu_ref, wd_ref, o_ref):
    sh2 = mod_ref[:, 3 * D_MODEL:4 * D_MODEL]
    sc2 = mod_ref[:, 4 * D_MODEL:5 * D_MODEL]
    g2 = mod_ref[:, 5 * D_MODEL:6 * D_MODEL]
    h = h_ref[...]
    hn = (h * _rms_scale(h, D_MODEL) * (n2g_ref[...] * (1.0 + sc2)) + sh2).astype(BF16)
    yield
    acc = None
    for c0 in range(0, FF, FF_CHUNK):
        g = _dot(hn, wg_ref[:, c0:c0 + FF_CHUNK])
        u = _dot(hn, wu_ref[:, c0:c0 + FF_CHUNK])
        act = (g * _sigmoid(g) * u).astype(BF16)
        part = _dot(act, wd_ref[c0:c0 + FF_CHUNK, :])
        acc = part if acc is None else acc + part
        yield
    o_ref[...] = h_ref[...] + g2 * acc
    yield


def _ffn_kernel(*refs):
    _interleave(_ffn_steps(*refs))


def _ffn_call(h2d, mods, layer, lw, *, rows_per_mod, mod_base):
    n_tok = h2d.shape[0]
    if mod_base is None:
        tiles_per_mod = rows_per_mod // FFN_ROWS
        mod_map = lambda i: (layer, i // tiles_per_mod, 0, 0)
    else:
        mod_map = lambda i: (layer, mod_base, 0, 0)
    consts = [lw["norm2_g"], lw["w_gate"], lw["w_up"], lw["w_down"]]
    return pl.pallas_call(
        _ffn_kernel,
        grid=(n_tok // FFN_ROWS,),
        in_specs=[pl.BlockSpec((FFN_ROWS, D_MODEL), lambda i: (i, 0)),
                  pl.BlockSpec((None, None, 1, 6 * D_MODEL), mod_map)]
        + [_layer_spec(a, layer) for a in consts],
        out_specs=pl.BlockSpec((FFN_ROWS, D_MODEL), lambda i: (i, 0)),
        out_shape=jax.ShapeDtypeStruct((n_tok, D_MODEL), F32),
        compiler_params=pltpu.CompilerParams(
            dimension_semantics=("arbitrary",),
            vmem_limit_bytes=VMEM_LIMIT_BYTES),
        name="swiglu",
    )(h2d, mods, *consts)


def _rope_partner_index():
    return [d + 8 if (d % 16) < 8 else d - 8 for d in range(QK_ROPE)]


def _prep_weights(p):
    partner = jnp.array(_rope_partner_index(), jnp.int32)
    q_scale = QK_DIM ** -0.5 * math.log2(math.e)
    w_in = p["w_in"]
    w_tail = w_in[:, :, C_CKV:].astype(BF16)
    w_kr = w_tail[:, :, KV_LORA:KV_LORA + QK_ROPE]
    zeros32 = jnp.zeros((DEPTH, D_MODEL, QK_ROPE), BF16)
    w_in_last = jnp.concatenate(
        [w_tail[:, :, :KV_LORA], w_kr[:, :, partner], zeros32, w_kr, zeros32], axis=2)

    wq_h = p["mla_wq_up"].reshape(DEPTH, Q_LORA, MLA_HEADS, QK_DIM)
    wq_pad = jnp.pad(wq_h, ((0, 0), (0, 0), (0, 0), (0, HEAD_PAD - QK_DIM)))
    wq_t = wq_pad.reshape(DEPTH, Q_LORA, MLA_HEADS * HEAD_PAD).transpose(0, 2, 1)
    rope_pad = ((0, 0), (0, 0), (0, 0), (ROPE_LO, HEAD_PAD - ROPE_LO - QK_ROPE))
    wq_partner = jnp.pad(wq_h[:, :, :, QK_NOPE:][:, :, :, partner], rope_pad)
    wq_partner_t = wq_partner.reshape(DEPTH, Q_LORA, MLA_HEADS * HEAD_PAD).transpose(0, 2, 1)

    wkv = p["mla_wkv_up"].reshape(DEPTH, KV_LORA, MLA_HEADS, QK_NOPE + V_DIM)
    wk_pad = jnp.pad(wkv[:, :, :, :QK_NOPE], ((0, 0), (0, 0), (0, 0), (0, HEAD_PAD - QK_NOPE)))
    wk_pad = wk_pad.reshape(DEPTH, KV_LORA, MLA_HEADS * HEAD_PAD)
    wv_ext = jnp.pad(wkv[:, :, :, QK_NOPE:], ((0, 0), (0, 0), (0, 0), (0, V_EXT - V_DIM)))
    wv_t = wv_ext.reshape(DEPTH, KV_LORA, MLA_HEADS * V_EXT).transpose(0, 2, 1)

    def pad_gain(g):
        return jnp.pad(g, ((0, 0), (0, HEAD_PAD - QK_DIM)))[:, None, :]

    def partner_gain(g):
        gp = g[:, QK_NOPE:][:, partner]
        return jnp.pad(gp, ((0, 0), (ROPE_LO, HEAD_PAD - ROPE_LO - QK_ROPE)))[:, None, :]

    def on_sublanes(g):
        return jnp.broadcast_to(g.transpose(0, 2, 1), (DEPTH, HEAD_PAD, ROW_CHUNK))

    qg, kg = p["q_norm_g"], p["k_norm_g"]
    eye = jnp.eye(LRU_BLOCKS, dtype=F32)
    gate_w = jnp.stack([p["lru_wa"][:, 0], p["lru_wi"][:, 0],
                        p["lru_wa"][:, 1], p["lru_wi"][:, 1]], axis=1)
    w_gates = (gate_w[:, :, :, :, None, :] * eye[None, None, :, None, :, None])
    w_gates = w_gates.transpose(0, 2, 3, 1, 4, 5).reshape(DEPTH, LRU_W, 4 * LRU_W)
    b_gates = jnp.stack([p["lru_ba"][:, 0], p["lru_bi"][:, 0],
                         p["lru_ba"][:, 1], p["lru_bi"][:, 1]], axis=1)

    def row(v):
        return v[:, None, :]

    return {
        "norm1_g": row(p["norm1_g"]),
        "w_in": w_in,
        "w_in_last": w_in_last,
        "conv_w": p["conv_w"],
        "conv_b": row(p["conv_b"]),
        "lru_conv_w": p["lru_conv_w"],
        "lru_conv_b": row(p["lru_conv_b"]),
        "w_gates": (0.5 * w_gates).astype(BF16),
        "b_gates": 0.5 * b_gates.reshape(DEPTH, 1, 4 * LRU_W),
        "lru_lambda": p["lru_lambda"],
        "mla_qnorm_g": row(p["mla_qnorm_g"]),
        "wq_t": wq_t.astype(BF16),
        "wq_partner_t": wq_partner_t.astype(BF16),
        "mla_kvnorm_g": row(p["mla_kvnorm_g"]),
        "wk": wk_pad.astype(BF16),
        "wv_t": wv_t.astype(BF16),
        "qg_t": on_sublanes(pad_gain(qg) * q_scale),
        "qg_partner_t": on_sublanes(partner_gain(qg) * q_scale),
        "kg": pad_gain(kg),
        "kg_partner": partner_gain(kg),
        "gnorm_conv": row(p["gnorm_conv"]),
        "gnorm_lru": row(p["gnorm_lru"]),
        "gnorm_mla": row(p["gnorm_mla"]),
        "w_out": p["w_out"].astype(BF16),
        "norm2_g": row(p["norm2_g"]),
        "w_gate": p["w_gate"].astype(BF16),
        "w_up": p["w_up"].astype(BF16),
        "w_down": p["w_down"].astype(BF16),
    }


def _rope_tables(n_tokens):
    n_rows = n_tokens // GRID_W
    row = np.repeat(np.arange(n_rows, dtype=np.float32), GRID_W)
    col = np.tile(np.arange(GRID_W, dtype=np.float32), n_rows)
    n_freq = QK_ROPE // 4
    inv_freq = np.power(np.float32(ROPE_THETA),
                        -np.arange(n_freq, dtype=np.float32) / np.float32(n_freq))
    ang_r = (row[:, None] * inv_freq).astype(np.float32)
    ang_c = (col[:, None] * inv_freq).astype(np.float32)
    cos = np.concatenate([np.cos(ang_r), np.cos(ang_r), np.cos(ang_c), np.cos(ang_c)], 1)
    sin = np.concatenate([-np.sin(ang_r), np.sin(ang_r), -np.sin(ang_c), np.sin(ang_c)], 1)
    ones = np.ones((n_tokens, QK_NOPE), np.float32)
    tail = np.zeros((n_tokens, HEAD_PAD - QK_DIM), np.float32)
    rope_c = np.concatenate([ones, cos, tail], axis=1).astype(np.float32)
    rope_s = np.concatenate([np.zeros((n_tokens, QK_NOPE), np.float32), sin, tail],
                            axis=1).astype(np.float32)
    return rope_c, rope_s


def kernel(x_prompt, x_sample, cache_ckv, cache_krope, state_lru, c, c_ctx, norm1_g, ada_w, ada_b, w_in, conv_w, conv_b, lru_conv_w, lru_conv_b, lru_wa, lru_ba, lru_wi, lru_bi, lru_lambda, mla_qnorm_g, mla_wq_up, mla_kvnorm_g, mla_wkv_up, q_norm_g, k_norm_g, gnorm_conv, gnorm_lru, gnorm_mla, w_out, norm2_g, w_gate, w_up, w_down):
    params = dict(norm1_g=norm1_g, w_in=w_in, conv_w=conv_w, conv_b=conv_b,
                  lru_conv_w=lru_conv_w, lru_conv_b=lru_conv_b, lru_wa=lru_wa,
                  lru_ba=lru_ba, lru_wi=lru_wi, lru_bi=lru_bi, lru_lambda=lru_lambda,
                  mla_qnorm_g=mla_qnorm_g, mla_wq_up=mla_wq_up,
                  mla_kvnorm_g=mla_kvnorm_g, mla_wkv_up=mla_wkv_up, q_norm_g=q_norm_g,
                  k_norm_g=k_norm_g, gnorm_conv=gnorm_conv, gnorm_lru=gnorm_lru,
                  gnorm_mla=gnorm_mla, w_out=w_out, norm2_g=norm2_g, w_gate=w_gate,
                  w_up=w_up, w_down=w_down)
    batch, seq_p, _ = x_prompt.shape
    dec_batch, seq_s, _ = x_sample.shape
    assert dec_batch == 4 and cache_ckv.shape[2] == 256

    cond8 = jnp.concatenate(
        [c, c_ctx[None, :], jnp.zeros((8 - dec_batch - 1, D_MODEL), F32)], axis=0)
    mods = _ada_call(cond8, ada_w, ada_b).reshape(DEPTH, 8, 1, 6 * D_MODEL)

    rope_c, rope_s = _rope_tables(seq_s)
    cache_kr_pad = jnp.pad(
        cache_krope, ((0, 0), (0, 0), (0, 0), (ROPE_LO, HEAD_PAD - ROPE_LO - QK_ROPE)))

    def chunked_t(tab):
        return jnp.asarray(np.ascontiguousarray(
            tab.reshape(seq_s // ROW_CHUNK, ROW_CHUNK, HEAD_PAD).transpose(0, 2, 1)))

    extra = dict(rope_c=jnp.asarray(rope_c), rope_s=jnp.asarray(rope_s),
                 rope_ct=chunked_t(rope_c), rope_st=chunked_t(rope_s), cache_ckv=cache_ckv,
                 cache_kr=cache_kr_pad, state=state_lru)

    xp = x_prompt.reshape(batch * seq_p, D_MODEL)
    xs = x_sample.reshape(dec_batch * seq_s, D_MODEL)
    ckv_l, kr_l, st_l = [], [], []
    lw = _prep_weights(params)
    for l in range(DEPTH):
        (hs,) = _mix_call(xs, mods, l, lw, seq=seq_s, n_seq=1, latent=True, extra=extra)
        hp, ckv, kr, st, xs = _mix_call(xp, mods, l, lw, seq=seq_p, n_seq=CTX_SEQS_PER_STEP,
                                        latent=False, ffn_h2d=hs, ffn_rows_per_mod=seq_s)
        xp = _ffn_call(hp, mods, l, lw, rows_per_mod=None, mod_base=4)
        ckv_l.append(ckv.reshape(batch, seq_p, KV_LORA))
        kr_l.append(kr.reshape(batch, seq_p, QK_ROPE))
        st_l.append(st)
    return (xp.reshape(batch, seq_p, D_MODEL),
            xs.reshape(dec_batch, seq_s, D_MODEL),
            jnp.stack(ckv_l, axis=1),
            jnp.stack(kr_l, axis=1),
            jnp.stack(st_l, axis=1))
```

```python
import functools
import itertools
import math

import jax
import jax.numpy as jnp
import numpy as np
from jax import lax
from jax.experimental import pallas as pl
from jax.experimental.pallas import tpu as pltpu

F32 = jnp.float32
BF16 = jnp.bfloat16

D_MODEL = 1024
DEPTH = 2
GRID_W = 64
CONV_W = 256
LRU_W = 256
LRU_BLOCKS = 4
LRU_BLK = 64
LRU_C = 8.0
MLA_HEADS = 8
QK_NOPE = 64
QK_ROPE = 32
V_DIM = 64
V_EXT = V_DIM + 16
QK_DIM = QK_NOPE + QK_ROPE
Q_LORA = 256
KV_LORA = 128
ROPE_THETA = 10000.0
FF = 2816
EPS = 1e-6

LANES = 128
SUBLANES = 8
HEAD_PAD = LANES
VMEM_LIMIT_BYTES = 60 * 1024 * 1024

C_BG, C_CG, C_H = 0, 256, 512
C_XB, C_YB = 768, 1024
C_Q = 1280
C_CKV = 1536
C_KR = 1664
U_W = 1792
SCAN_A = (C_H, C_CKV)
SCAN_B = (C_Q, C_BG)
ROPE_LO = QK_NOPE

ROW_CHUNK = 256
SCAN_CHUNK = 32
Q_CHUNK = 256
FF_CHUNK = 256
PROJ_SLAB = 256
PROJ_CHUNKS = 1
CTX_SEQS_PER_STEP = 2
FUSED_FFN_PERIOD = 4
SEQ_STAGGER = 10
FFN_ROWS = 1024
PAD_ROWS = SUBLANES


def _rms_scale(x, n):
    ms = jnp.sum(x * x, axis=-1, keepdims=True) * (1.0 / n)
    return lax.rsqrt(ms + EPS)


def _sigmoid(x):
    return 0.5 * jnp.tanh(0.5 * x) + 0.5


def _gelu_tanh(x):
    c = math.sqrt(2.0 / math.pi)
    return 0.5 * x * (1.0 + jnp.tanh(c * (x + 0.044715 * (x * x * x))))


def _dot(a, b):
    return jnp.dot(a, b, preferred_element_type=F32)


def _interleave(*gens):
    live = list(gens)
    while live:
        for g in list(live):
            try:
                next(g)
            except StopIteration:
                live.remove(g)


def _interleave_staggered(gens, lag, side=None, side_period=1):
    pending = list(gens)
    live = []
    tick = 0
    while live or pending or side is not None:
        if pending and tick % lag == 0:
            live.append(pending.pop(0))
        if side is not None and (tick % side_period == 0 or not (live or pending)):
            try:
                next(side)
            except StopIteration:
                side = None
        for g in list(live):
            try:
                next(g)
            except StopIteration:
                live.remove(g)
        tick += 1


def _dot_nt(a, b):
    return lax.dot_general(a, b, (((1,), (1,)), ((), ())), preferred_element_type=F32)


ADA_TN = 1536


def _ada_kernel(cond_ref, w_ref, b_ref, o_ref):
    cnd = cond_ref[...]
    s = (cnd * _sigmoid(cnd)).astype(BF16)
    o_ref[...] = _dot(s, w_ref[...].astype(BF16)) + b_ref[...]


def _ada_call(cond8, ada_w, ada_b):
    n_out = 6 * D_MODEL
    return pl.pallas_call(
        _ada_kernel,
        grid=(DEPTH, n_out // ADA_TN),
        in_specs=[
            pl.BlockSpec((8, D_MODEL), lambda l, j: (0, 0)),
            pl.BlockSpec((None, D_MODEL, ADA_TN), lambda l, j: (l, 0, j)),
            pl.BlockSpec((None, 1, ADA_TN), lambda l, j: (l, 0, j)),
        ],
        out_specs=pl.BlockSpec((None, 8, ADA_TN), lambda l, j: (l, 0, j)),
        out_shape=jax.ShapeDtypeStruct((DEPTH, 8, n_out), F32),
        compiler_params=pltpu.CompilerParams(
            dimension_semantics=("arbitrary", "arbitrary"),
            vmem_limit_bytes=VMEM_LIMIT_BYTES),
        name="ada_mod",
    )(cond8, ada_w, ada_b.reshape(DEPTH, 1, n_out))


def _mix_kernel(*refs, seq, n_seq, latent, fused_ffn):
    T = seq * n_seq
    n_ctx = 256 if latent else 0
    (x_ref, mod_ref, n1g_ref, win_ref, wlast_ref, convw_ref, convb_ref, lcw_ref, lcb_ref,
     wg_ref, bg_ref, lam_ref, qng_ref, wqt_ref, kvng_ref, wk_ref, wvt_ref,
     qgt_ref, kg_ref, gnc_ref, gnl_ref, gnm_ref, wout_ref) = refs[:23]
    pos = 23
    if latent:
        (wqpt_ref, qgpt_ref, kgp_ref, ropect_ref, ropest_ref, ropec_ref, ropes_ref,
         cckv_ref, ckr_ref, h0_ref) = refs[pos:pos + 10]
        pos += 10
    if fused_ffn:
        ffn_in_refs = refs[pos:pos + 6]
        pos += 6
    h_out_ref = refs[pos]
    pos += 1
    if not latent:
        ckv_out_ref, kr_out_ref, st_out_ref = refs[pos:pos + 3]
        pos += 3
    if fused_ffn:
        ffn_out_ref = refs[pos]
        pos += 1
    u_scr, qt_scr, k_scr, vt_scr, ymt_scr, ycat_scr = refs[pos:pos + 6]
    if latent:
        st_scr, m_scr = refs[pos + 6:pos + 8]

    n_chunks = T // ROW_CHUNK
    lane = lax.broadcasted_iota(jnp.int32, (1, LANES), 1)
    rope_lanes = (lane >= ROPE_LO) & (lane < ROPE_LO + QK_ROPE)

    sh1 = mod_ref[:, 0:D_MODEL]
    sc1 = mod_ref[:, D_MODEL:2 * D_MODEL]
    g1 = mod_ref[:, 2 * D_MODEL:3 * D_MODEL]
    gain1 = n1g_ref[...] * (1.0 + sc1)

    u_scr[0:PAD_ROWS, :] = jnp.zeros((PAD_ROWS, U_W), F32)
    u_scr[PAD_ROWS + T:2 * PAD_ROWS + T, :] = jnp.zeros((PAD_ROWS, U_W), F32)

    def phase_a(c, n=1):
        r0 = c * ROW_CHUNK
        nrows = n * ROW_CHUNK
        x = x_ref[r0:r0 + nrows, :]
        hn = (x * _rms_scale(x, D_MODEL) * gain1 + sh1).astype(BF16)
        yield
        urows = slice(PAD_ROWS + r0, PAD_ROWS + r0 + nrows)
        for s0 in range(0, C_CKV, PROJ_SLAB):
            u_scr[urows, s0:s0 + PROJ_SLAB] = _dot(hn, win_ref[:, s0:s0 + PROJ_SLAB])
            yield
        u_scr[urows, C_CKV:U_W] = _dot(hn, wlast_ref[...])
        yield
        u_scr[urows, C_CG:C_CG + CONV_W] = (
            u_scr[urows, C_CG:C_CG + CONV_W] * u_scr[urows, C_H:C_H + CONV_W])
        yield

    vrow = lax.broadcasted_iota(jnp.int32, (MLA_HEADS * V_EXT, 1), 0)
    ones_rows = (vrow % V_EXT) >= V_DIM

    def values_t(cb):
        return jnp.where(ones_rows, 1.0, _dot_nt(wvt_ref[...], cb)).astype(BF16)

    def phase_b(c):
        r0 = c * ROW_CHUNK
        rows = slice(PAD_ROWS + r0, PAD_ROWS + r0 + ROW_CHUNK)
        uq = u_scr[rows, C_Q:C_Q + Q_LORA]
        qn = (uq * _rms_scale(uq, Q_LORA) * qng_ref[...]).astype(BF16)
        qa_t = _dot_nt(wqt_ref[...], qn)
        if latent:
            qpa_t = _dot_nt(wqpt_ref[...], qn)
            q_tc = ropect_ref[c] * qgt_ref[...]
            q_ts = ropest_ref[c] * qgpt_ref[...]
        yield
        for h in range(MLA_HEADS):
            sl = slice(h * HEAD_PAD, (h + 1) * HEAD_PAD)
            qh = qa_t[sl, :]
            rq = lax.rsqrt(jnp.sum(qh * qh, axis=0, keepdims=True) * (1.0 / QK_DIM) + EPS)
            if latent:
                qh = (qh * q_tc + qpa_t[sl, :] * q_ts) * rq
            else:
                qh = qh * qgt_ref[...] * rq
            qt_scr[h, c] = qh.astype(BF16)
            if h % 2:
                yield
        uc = u_scr[rows, C_CKV:C_CKV + KV_LORA]
        ckv = uc * _rms_scale(uc, KV_LORA) * kvng_ref[...]
        krb = u_scr[rows, C_KR:C_KR + LANES]
        kr_rolled = pltpu.roll(krb, 64, 1)
        if not latent:
            ckv_out_ref[r0:r0 + ROW_CHUNK, :] = ckv
            kr_out_ref[r0:r0 + ROW_CHUNK, :] = kr_rolled[:, 0:QK_ROPE]
        krm = jnp.where(rope_lanes, krb, 0.0)
        cb = ckv.astype(BF16)
        ka = _dot(cb, wk_ref[...])
        vt_scr[c] = values_t(cb)
        if latent:
            k_tc = ropec_ref[r0:r0 + ROW_CHUNK, :] * kg_ref[...]
            k_ts = ropes_ref[r0:r0 + ROW_CHUNK, :] * kgp_ref[...]
        yield
        for h in range(MLA_HEADS):
            sl = slice(h * HEAD_PAD, (h + 1) * HEAD_PAD)
            kpre = ka[:, sl] + krm
            rk = _rms_scale(kpre, QK_DIM)
            if latent:
                kh = (kpre * k_tc + kr_rolled * k_ts) * rk
            else:
                kh = kpre * kg_ref[...] * rk
            k_scr[h, r0:r0 + ROW_CHUNK, :] = kh.astype(BF16)
            if h % 2:
                yield

    if latent:
        cc = cckv_ref[...].astype(BF16)
        ka = _dot(cc, wk_ref[...])
        vt_scr[n_chunks] = values_t(cc)
        krc = ckr_ref[...]
        for h in range(MLA_HEADS):
            kpre = ka[:, h * HEAD_PAD:(h + 1) * HEAD_PAD] + krc
            kh = kpre * kg_ref[...] * _rms_scale(kpre, QK_DIM)
            k_scr[h, seq:seq + n_ctx, :] = kh.astype(BF16)

    neg_lam = -lam_ref[...]
    sp = jnp.maximum(neg_lam, 0.0) + jnp.log(1.0 + jnp.exp(-jnp.abs(neg_lam)))
    log2a_coef = (-0.5 * LRU_C * math.log2(math.e)) * sp
    row_rc = lax.broadcasted_iota(jnp.int32, (ROW_CHUNK, 1), 0)

    def phase_c(ci):
        r0 = ci * ROW_CHUNK
        base = PAD_ROWS + r0
        first = (r0 % seq) == 0
        last = ((r0 + ROW_CHUNK) % seq) == 0

        def win(col, shift, width=CONV_W):
            w = u_scr[base + shift:base + shift + ROW_CHUNK, col:col + width]
            if shift < 0 and first:
                w = jnp.where(row_rc < -shift, 0.0, w)
            if shift > 0 and last:
                w = jnp.where(row_rc >= ROW_CHUNK - shift, 0.0, w)
            return w

        z_m, z_0, z_p = win(C_CG, -1), win(C_CG, 0), win(C_CG, 1)
        conv = (z_m * convw_ref[0:1, :] + z_0 * convw_ref[1:2, :]
                + z_p * convw_ref[2:3, :] + convb_ref[...])
        yc = u_scr[base:base + ROW_CHUNK, C_BG:C_BG + CONV_W] * conv
        ycn = yc * _rms_scale(yc, CONV_W) * gnc_ref[...]
        ycat_scr[r0:r0 + ROW_CHUNK, 0:CONV_W] = ycn.astype(BF16)
        yield

        xc = (win(C_XB, -2) * lcw_ref[0:1, :] + win(C_XB, -1) * lcw_ref[1:2, :]
              + win(C_XB, 0) * lcw_ref[2:3, :] + win(C_XB, 1) * lcw_ref[3:4, :]
              + lcb_ref[...])
        half_gates = _dot(xc.astype(BF16), wg_ref[...]) + bg_ref[...]
        half_xc = 0.5 * xc
        yield
        for d in range(2):
            t_r = jnp.tanh(half_gates[:, (2 * d) * LRU_W:(2 * d + 1) * LRU_W])
            t_i = jnp.tanh(half_gates[:, (2 * d + 1) * LRU_W:(2 * d + 2) * LRU_W])
            a = jnp.exp2(log2a_coef[d:d + 1, :] * (t_r + 1.0))
            v = 1.0 - a * a
            mult = v * lax.rsqrt(jnp.maximum(v, 1e-30))
            u_scr[base:base + ROW_CHUNK, SCAN_A[d]:SCAN_A[d] + LRU_W] = a
            u_scr[base:base + ROW_CHUNK, SCAN_B[d]:SCAN_B[d] + LRU_W] = (
                mult * (half_xc * (t_i + 1.0)))
            yield

    n_sc = seq // SCAN_CHUNK
    row_sc = lax.broadcasted_iota(jnp.int32, (SCAN_CHUNK, LRU_W), 0)
    steps = [1 << i for i in range(int(math.log2(SCAN_CHUNK)))]

    def scan_step(rf, rb, cf, cb_):
        rf = pl.multiple_of(rf + PAD_ROWS, SUBLANES)
        rb = pl.multiple_of(rb + PAD_ROWS, SUBLANES)
        a = u_scr[pl.ds(rf, SCAN_CHUNK), SCAN_A[0]:SCAN_A[0] + LRU_W]
        b = u_scr[pl.ds(rf, SCAN_CHUNK), SCAN_B[0]:SCAN_B[0] + LRU_W]
        b = b + jnp.where(row_sc == 0, a * cf, 0.0)
        for d in steps:
            if d % SUBLANES == 0:
                b = jnp.concatenate([b[:d], b[d:] + a[d:] * b[:-d]], axis=0)
                if d != steps[-1]:
                    a = jnp.concatenate([a[:d], a[d:] * a[:-d]], axis=0)
                continue
            b = b + a * jnp.where(row_sc >= d, pltpu.roll(b, d, 0), 0.0)
            if d != steps[-1]:
                a = a * jnp.where(row_sc >= d, pltpu.roll(a, d, 0), 1.0)
        u_scr[pl.ds(rf, SCAN_CHUNK), SCAN_B[0]:SCAN_B[0] + LRU_W] = b
        cf = b[SCAN_CHUNK - 1:SCAN_CHUNK, :]
        a = u_scr[pl.ds(rb, SCAN_CHUNK), SCAN_A[1]:SCAN_A[1] + LRU_W]
        b = u_scr[pl.ds(rb, SCAN_CHUNK), SCAN_B[1]:SCAN_B[1] + LRU_W]
        b = b + jnp.where(row_sc == SCAN_CHUNK - 1, a * cb_, 0.0)
        for d in steps:
            if d % SUBLANES == 0:
                b = jnp.concatenate([b[:-d] + a[:-d] * b[d:], b[-d:]], axis=0)
                if d != steps[-1]:
                    a = jnp.concatenate([a[:-d] * a[d:], a[-d:]], axis=0)
                continue
            keep = row_sc < SCAN_CHUNK - d
            b = b + a * jnp.where(keep, pltpu.roll(b, SCAN_CHUNK - d, 0), 0.0)
            if d != steps[-1]:
                a = a * jnp.where(keep, pltpu.roll(a, SCAN_CHUNK - d, 0), 1.0)
        u_scr[pl.ds(rb, SCAN_CHUNK), SCAN_B[1]:SCAN_B[1] + LRU_W] = b
        cb_ = b[0:1, :]
        return cf, cb_

    cps = seq // ROW_CHUNK

    def softmax_pv(st, vts):
        m = jnp.max(st, axis=0, keepdims=True)
        pt = jnp.exp2(st - m).astype(BF16)
        ot = None
        for i, vt in enumerate(vts):
            part = _dot(vt, pt[i * ROW_CHUNK:(i + 1) * ROW_CHUNK, :])
            ot = part if ot is None else ot + part
        return ot[0:V_DIM, :] * (1.0 / ot[V_DIM:V_DIM + 1, :])

    def f_build(c):
        r0 = c * ROW_CHUNK
        rows = slice(r0, r0 + ROW_CHUNK)
        urows = slice(PAD_ROWS + r0, PAD_ROWS + r0 + ROW_CHUNK)
        hf = u_scr[urows, SCAN_B[0]:SCAN_B[0] + LRU_W]
        hb = u_scr[urows, SCAN_B[1]:SCAN_B[1] + LRU_W]
        yb = u_scr[urows, C_YB:C_YB + LRU_W]
        yl = (hf + hb) * _gelu_tanh(yb)
        yln = yl * _rms_scale(yl, LRU_W) * gnl_ref[...]
        ycat_scr[rows, CONV_W:CONV_W + LRU_W] = yln.astype(BF16)
        yield
        ymt = ymt_scr[c]
        ssq = jnp.sum(ymt * ymt, axis=0, keepdims=True)
        ynt = ymt * lax.rsqrt(ssq * (1.0 / (MLA_HEADS * V_DIM)) + EPS)
        ycat_scr[rows, CONV_W + LRU_W:D_MODEL] = (ynt.T * gnm_ref[...]).astype(BF16)
        yield

    def f_project(c):
        rows = slice(c * ROW_CHUNK, (c + 1) * ROW_CHUNK)
        for n0 in range(0, D_MODEL, PROJ_SLAB):
            cols = slice(n0, n0 + PROJ_SLAB)
            y = _dot(ycat_scr[rows, :], wout_ref[:, cols])
            h_out_ref[rows, cols] = x_ref[rows, cols] + g1[:, cols] * y
            yield

    if latent:
        groups = [list(range(g0, g0 + PROJ_CHUNKS)) for g0 in range(0, n_chunks, PROJ_CHUNKS)]
        _interleave(phase_a(0, PROJ_CHUNKS))
        for gi, group in enumerate(groups):
            dependent = []
            for c in group:
                dependent.append(phase_b(c))
                if c:
                    dependent.append(phase_c(c - 1))
            ahead = [phase_a(groups[gi + 1][0], PROJ_CHUNKS)] if gi + 1 < len(groups) else []
            _interleave(*ahead, itertools.chain(*dependent))
        _interleave(phase_c(n_chunks - 1))

        key_chunks = list(range(cps)) + [n_chunks]
        n_blk = cps * (MLA_HEADS // 2)
        trips = n_blk // 2
        scans_per_trip = n_sc // trips

        def heads_of(idx):
            qc = idx // (MLA_HEADS // 2)
            j = idx % (MLA_HEADS // 2)
            return qc, [2 * j, 2 * j + 1]

        def scores(idx, slot):
            qc, hs = heads_of(idx)
            for par, h in enumerate(hs):
                qt = qt_scr[h, qc]
                m = None
                for i in range(len(key_chunks)):
                    rows = slice(i * ROW_CHUNK, (i + 1) * ROW_CHUNK)
                    st = _dot(k_scr[h, rows, :], qt)
                    st_scr[slot, par, rows, :] = st
                    mc = jnp.max(st, axis=0, keepdims=True)
                    m = mc if m is None else jnp.maximum(m, mc)
                    yield
                m_scr[slot, par] = m

        def finish(idx, slot):
            qc, hs = heads_of(idx)
            for par, h in enumerate(hs):
                hr = pl.ds(pl.multiple_of(h * V_DIM, V_DIM), V_DIM)
                vr = pl.ds(pl.multiple_of(h * V_EXT, V_EXT - V_DIM), V_EXT)
                m = m_scr[slot, par]
                ot = None
                for i, kc in enumerate(key_chunks):
                    rows = slice(i * ROW_CHUNK, (i + 1) * ROW_CHUNK)
                    pt = jnp.exp2(st_scr[slot, par, rows, :] - m).astype(BF16)
                    part = _dot(vt_scr[kc, vr, :], pt)
                    ot = part if ot is None else ot + part
                    yield
                ymt_scr[qc, hr, :] = ot[0:V_DIM, :] * (1.0 / ot[V_DIM:V_DIM + 1, :])

        _interleave(scores(0, 0))

        def att_body(i, carry):
            b0 = 2 * i
            state = {"carry": carry, "c": i * scans_per_trip}

            def scans(n):
                for _ in range(n):
                    c = state["c"]
                    rf = pl.multiple_of(c * SCAN_CHUNK, SCAN_CHUNK)
                    rb = pl.multiple_of((n_sc - 1 - c) * SCAN_CHUNK, SCAN_CHUNK)
                    state["carry"] = scan_step(rf, rb, *state["carry"])
                    state["c"] = c + 1
                    yield
                    yield
                    yield

            half = scans_per_trip // 2
            _interleave(scores(b0 + 1, 1), finish(b0, 0), scans(half))
            _interleave(scores(jnp.minimum(b0 + 2, n_blk - 1), 0), finish(b0 + 1, 1),
                        scans(scans_per_trip - half))
            return state["carry"]

        lax.fori_loop(0, trips, att_body, (h0_ref[0:1, :], h0_ref[1:2, :]))

        _interleave(f_build(0))
        for c in range(n_chunks):
            ahead = [f_build(c + 1)] if c + 1 < n_chunks else []
            _interleave(f_project(c), *ahead)
    else:
        def phase_d(s):
            carry = (jnp.zeros((1, LRU_W), F32), jnp.zeros((1, LRU_W), F32))
            for c in range(n_sc):
                carry = scan_step(s * seq + c * SCAN_CHUNK,
                                  s * seq + (n_sc - 1 - c) * SCAN_CHUNK, *carry)
                yield
            st_out_ref[s, 0:1, :] = carry[0]
            st_out_ref[s, 1:2, :] = carry[1]

        def phase_e(s):
            blocks = [(qi, h) for qi in range(cps) for h in range(MLA_HEADS)]

            def scores(blk):
                qi, h = blk
                return _dot(k_scr[h, s * seq:(s + 1) * seq, :], qt_scr[h, s * cps + qi])

            st = scores(blocks[0])
            for i, (qi, h) in enumerate(blocks):
                nxt = scores(blocks[i + 1]) if i + 1 < len(blocks) else None
                hr = slice(h * V_DIM, (h + 1) * V_DIM)
                vr = slice(h * V_EXT, (h + 1) * V_EXT)
                vts = [vt_scr[s * cps + j, vr, :] for j in range(cps)]
                ymt_scr[s * cps + qi, hr, :] = softmax_pv(st, vts)
                st = nxt
                yield

        def sequence(s):
            chunks = range(s * cps, (s + 1) * cps)
            for c in chunks:
                yield from phase_a(c)
            for c in chunks:
                yield from phase_b(c)
            for c in chunks:
                yield from phase_c(c)
            yield from phase_d(s)
            yield from phase_e(s)
            for c in chunks:
                yield from f_build(c)
                yield from f_project(c)

        side = _ffn_steps(*ffn_in_refs, ffn_out_ref) if fused_ffn else None
        _interleave_staggered([sequence(s) for s in range(n_seq)], SEQ_STAGGER,
                              side=side, side_period=FUSED_FFN_PERIOD)


def _const_spec(shape):
    nd = len(shape)
    return pl.BlockSpec(shape, lambda i, _n=nd: (0,) * _n,
                        pipeline_mode=pl.Buffered(1))


def _layer_spec(arr, layer):
    nd = arr.ndim - 1
    return pl.BlockSpec((None,) + arr.shape[1:], lambda i, _n=nd: (layer,) + (0,) * _n,
                        pipeline_mode=pl.Buffered(1))


def _mix_call(x2d, mods, layer, lw, *, seq, n_seq, latent, extra=None, ffn_h2d=None,
              ffn_rows_per_mod=None):
    n_tok = x2d.shape[0]
    T = seq * n_seq
    grid = (n_tok // T,)
    fused_ffn = ffn_h2d is not None
    n_keys_buf = T if not latent else seq + 256

    if latent:
        mod_map = lambda i: (layer, i, 0, 0)
    else:
        mod_map = lambda i: (layer, 4, 0, 0)

    consts = [lw["norm1_g"], lw["w_in"], lw["w_in_last"], lw["conv_w"], lw["conv_b"], lw["lru_conv_w"],
              lw["lru_conv_b"], lw["w_gates"], lw["b_gates"], lw["lru_lambda"],
              lw["mla_qnorm_g"], lw["wq_t"], lw["mla_kvnorm_g"], lw["wk"], lw["wv_t"],
              lw["qg_t"], lw["kg"], lw["gnorm_conv"], lw["gnorm_lru"], lw["gnorm_mla"],
              lw["w_out"]]
    args = [x2d, mods] + consts
    in_specs = [pl.BlockSpec((T, D_MODEL), lambda i: (i, 0)),
                pl.BlockSpec((None, None, 1, 6 * D_MODEL), mod_map)]
    in_specs += [_layer_spec(a, layer) for a in consts]
    if latent:
        lconsts = [lw["wq_partner_t"], lw["qg_partner_t"], lw["kg_partner"]]
        tables = [extra["rope_ct"], extra["rope_st"], extra["rope_c"], extra["rope_s"]]
        args += lconsts + tables
        in_specs += [_layer_spec(a, layer) for a in lconsts]
        in_specs += [_const_spec(a.shape) for a in tables]
        args += [extra["cache_ckv"], extra["cache_kr"], extra["state"]]
        in_specs += [
            pl.BlockSpec((None, None, 256, KV_LORA), lambda i: (i, layer, 0, 0)),
            pl.BlockSpec((None, None, 256, LANES), lambda i: (i, layer, 0, 0)),
            pl.BlockSpec((None, None, 2, LRU_W), lambda i: (i, layer, 0, 0)),
        ]

    if fused_ffn:
        ffn_rows = ffn_h2d.shape[0] // grid[0]
        tiles_per_mod = ffn_rows_per_mod // ffn_rows
        ffn_consts = [lw["norm2_g"], lw["w_gate"], lw["w_up"], lw["w_down"]]
        args += [ffn_h2d, mods] + ffn_consts
        in_specs += [pl.BlockSpec((ffn_rows, D_MODEL), lambda i: (i, 0)),
                     pl.BlockSpec((None, None, 1, 6 * D_MODEL),
                                  lambda i: (layer, i // tiles_per_mod, 0, 0))]
        in_specs += [_layer_spec(a, layer) for a in ffn_consts]

    out_shape = [jax.ShapeDtypeStruct((n_tok, D_MODEL), F32)]
    out_specs = [pl.BlockSpec((T, D_MODEL), lambda i: (i, 0))]
    if not latent:
        out_shape += [jax.ShapeDtypeStruct((n_tok, KV_LORA), F32),
                      jax.ShapeDtypeStruct((n_tok, QK_ROPE), F32),
                      jax.ShapeDtypeStruct((n_tok // seq, 2, LRU_W), F32)]
        out_specs += [pl.BlockSpec((T, KV_LORA), lambda i: (i, 0)),
                      pl.BlockSpec((T, QK_ROPE), lambda i: (i, 0)),
                      pl.BlockSpec((n_seq, 2, LRU_W), lambda i: (i, 0, 0))]
    if fused_ffn:
        out_shape.append(jax.ShapeDtypeStruct(ffn_h2d.shape, F32))
        out_specs.append(pl.BlockSpec((ffn_rows, D_MODEL), lambda i: (i, 0)))

    scratch = [
        pltpu.VMEM((T + 2 * PAD_ROWS, U_W), F32),
        pltpu.VMEM((MLA_HEADS, T // ROW_CHUNK, HEAD_PAD, ROW_CHUNK), BF16),
        pltpu.VMEM((MLA_HEADS, n_keys_buf, HEAD_PAD), BF16),
        pltpu.VMEM((n_keys_buf // ROW_CHUNK, MLA_HEADS * V_EXT, ROW_CHUNK), BF16),
        pltpu.VMEM((T // ROW_CHUNK, MLA_HEADS * V_DIM, ROW_CHUNK), F32),
        pltpu.VMEM((T, D_MODEL), BF16),
    ]
    if latent:
        scratch.append(pltpu.VMEM((2, 2, n_keys_buf, ROW_CHUNK), F32))
        scratch.append(pltpu.VMEM((2, 2, 1, ROW_CHUNK), F32))
    return pl.pallas_call(
        functools.partial(_mix_kernel, seq=seq, n_seq=n_seq, latent=latent,
                          fused_ffn=fused_ffn),
        grid=grid,
        in_specs=in_specs,
        out_specs=out_specs,
        out_shape=out_shape,
        scratch_shapes=scratch,
        compiler_params=pltpu.CompilerParams(
            dimension_semantics=("arbitrary",),
            vmem_limit_bytes=VMEM_LIMIT_BYTES),
        name="mix_latent" if latent else ("mix_context_swiglu" if fused_ffn else "mix_context"),
    )(*args)


def _ffn_steps(h_ref, mod_ref, n2g_ref, wg_ref, wu_ref, wd_ref, o_ref):
    sh2 = mod_ref[:, 3 * D_MODEL:4 * D_MODEL]
    sc2 = mod_ref[:, 4 * D_MODEL:5 * D_MODEL]
    g2 = mod_ref[:, 5 * D_MODEL:6 * D_MODEL]
    h = h_ref[...]
    hn = (h * _rms_scale(h, D_MODEL) * (n2g_ref[...] * (1.0 + sc2)) + sh2).astype(BF16)
    yield
    acc = None
    for c0 in range(0, FF, FF_CHUNK):
        g = _dot(hn, wg_ref[:, c0:c0 + FF_CHUNK])
        u = _dot(hn, wu_ref[:, c0:c0 + FF_CHUNK])
        act = (g * _sigmoid(g) * u).astype(BF16)
        part = _dot(act, wd_ref[c0:c0 + FF_CHUNK, :])
        acc = part if acc is None else acc + part
        yield
    o_ref[...] = h_ref[...] + g2 * acc
    yield


def _ffn_kernel(*refs):
    _interleave(_ffn_steps(*refs))


def _ffn_call(h2d, mods, layer, lw, *, rows_per_mod, mod_base):
    n_tok = h2d.shape[0]
    if mod_base is None:
        tiles_per_mod = rows_per_mod // FFN_ROWS
        mod_map = lambda i: (layer, i // tiles_per_mod, 0, 0)
    else:
        mod_map = lambda i: (layer, mod_base, 0, 0)
    consts = [lw["norm2_g"], lw["w_gate"], lw["w_up"], lw["w_down"]]
    return pl.pallas_call(
        _ffn_kernel,
        grid=(n_tok // FFN_ROWS,),
        in_specs=[pl.BlockSpec((FFN_ROWS, D_MODEL), lambda i: (i, 0)),
                  pl.BlockSpec((None, None, 1, 6 * D_MODEL), mod_map)]
        + [_layer_spec(a, layer) for a in consts],
        out_specs=pl.BlockSpec((FFN_ROWS, D_MODEL), lambda i: (i, 0)),
        out_shape=jax.ShapeDtypeStruct((n_tok, D_MODEL), F32),
        compiler_params=pltpu.CompilerParams(
            dimension_semantics=("arbitrary",),
            vmem_limit_bytes=VMEM_LIMIT_BYTES),
        name="swiglu",
    )(h2d, mods, *consts)


def _rope_partner_index():
    return [d + 8 if (d % 16) < 8 else d - 8 for d in range(QK_ROPE)]


def _prep_weights(p):
    partner = jnp.array(_rope_partner_index(), jnp.int32)
    q_scale = QK_DIM ** -0.5 * math.log2(math.e)
    w_in = p["w_in"][:, :, :C_CKV].astype(BF16)
    w_tail = p["w_in"][:, :, C_CKV:].astype(BF16)
    w_kr = w_tail[:, :, KV_LORA:KV_LORA + QK_ROPE]
    zeros32 = jnp.zeros((DEPTH, D_MODEL, QK_ROPE), BF16)
    w_in_last = jnp.concatenate(
        [w_tail[:, :, :KV_LORA], w_kr[:, :, partner], zeros32, w_kr, zeros32], axis=2)

    wq_h = p["mla_wq_up"].reshape(DEPTH, Q_LORA, MLA_HEADS, QK_DIM)
    wq_pad = jnp.pad(wq_h, ((0, 0), (0, 0), (0, 0), (0, HEAD_PAD - QK_DIM)))
    wq_t = wq_pad.reshape(DEPTH, Q_LORA, MLA_HEADS * HEAD_PAD).transpose(0, 2, 1)
    rope_pad = ((0, 0), (0, 0), (0, 0), (ROPE_LO, HEAD_PAD - ROPE_LO - QK_ROPE))
    wq_partner = jnp.pad(wq_h[:, :, :, QK_NOPE:][:, :, :, partner], rope_pad)
    wq_partner_t = wq_partner.reshape(DEPTH, Q_LORA, MLA_HEADS * HEAD_PAD).transpose(0, 2, 1)

    wkv = p["mla_wkv_up"].reshape(DEPTH, KV_LORA, MLA_HEADS, QK_NOPE + V_DIM)
    wk_pad = jnp.pad(wkv[:, :, :, :QK_NOPE], ((0, 0), (0, 0), (0, 0), (0, HEAD_PAD - QK_NOPE)))
    wk_pad = wk_pad.reshape(DEPTH, KV_LORA, MLA_HEADS * HEAD_PAD)
    wv_ext = jnp.pad(wkv[:, :, :, QK_NOPE:], ((0, 0), (0, 0), (0, 0), (0, V_EXT - V_DIM)))
    wv_t = wv_ext.reshape(DEPTH, KV_LORA, MLA_HEADS * V_EXT).transpose(0, 2, 1)

    def pad_gain(g):
        return jnp.pad(g, ((0, 0), (0, HEAD_PAD - QK_DIM)))[:, None, :]

    def partner_gain(g):
        gp = g[:, QK_NOPE:][:, partner]
        return jnp.pad(gp, ((0, 0), (ROPE_LO, HEAD_PAD - ROPE_LO - QK_ROPE)))[:, None, :]

    def on_sublanes(g):
        return jnp.broadcast_to(g.transpose(0, 2, 1), (DEPTH, HEAD_PAD, ROW_CHUNK))

    qg, kg = p["q_norm_g"], p["k_norm_g"]
    eye = jnp.eye(LRU_BLOCKS, dtype=F32)
    gate_w = jnp.stack([p["lru_wa"][:, 0], p["lru_wi"][:, 0],
                        p["lru_wa"][:, 1], p["lru_wi"][:, 1]], axis=1)
    w_gates = (gate_w[:, :, :, :, None, :] * eye[None, None, :, None, :, None])
    w_gates = w_gates.transpose(0, 2, 3, 1, 4, 5).reshape(DEPTH, LRU_W, 4 * LRU_W)
    b_gates = jnp.stack([p["lru_ba"][:, 0], p["lru_bi"][:, 0],
                         p["lru_ba"][:, 1], p["lru_bi"][:, 1]], axis=1)

    def row(v):
        return v[:, None, :]

    return {
        "norm1_g": row(p["norm1_g"]),
        "w_in": w_in,
        "w_in_last": w_in_last,
        "conv_w": p["conv_w"],
        "conv_b": row(p["conv_b"]),
        "lru_conv_w": p["lru_conv_w"],
        "lru_conv_b": row(p["lru_conv_b"]),
        "w_gates": (0.5 * w_gates).astype(BF16),
        "b_gates": 0.5 * b_gates.reshape(DEPTH, 1, 4 * LRU_W),
        "lru_lambda": p["lru_lambda"],
        "mla_qnorm_g": row(p["mla_qnorm_g"]),
        "wq_t": wq_t.astype(BF16),
        "wq_partner_t": wq_partner_t.astype(BF16),
        "mla_kvnorm_g": row(p["mla_kvnorm_g"]),
        "wk": wk_pad.astype(BF16),
        "wv_t": wv_t.astype(BF16),
        "qg_t": on_sublanes(pad_gain(qg) * q_scale),
        "qg_partner_t": on_sublanes(partner_gain(qg) * q_scale),
        "kg": pad_gain(kg),
        "kg_partner": partner_gain(kg),
        "gnorm_conv": row(p["gnorm_conv"]),
        "gnorm_lru": row(p["gnorm_lru"]),
        "gnorm_mla": row(p["gnorm_mla"]),
        "w_out": p["w_out"].astype(BF16),
        "norm2_g": row(p["norm2_g"]),
        "w_gate": p["w_gate"].astype(BF16),
        "w_up": p["w_up"].astype(BF16),
        "w_down": p["w_down"].astype(BF16),
    }


def _rope_tables(n_tokens):
    n_rows = n_tokens // GRID_W
    row = np.repeat(np.arange(n_rows, dtype=np.float32), GRID_W)
    col = np.tile(np.arange(GRID_W, dtype=np.float32), n_rows)
    n_freq = QK_ROPE // 4
    inv_freq = np.power(np.float32(ROPE_THETA),
                        -np.arange(n_freq, dtype=np.float32) / np.float32(n_freq))
    ang_r = (row[:, None] * inv_freq).astype(np.float32)
    ang_c = (col[:, None] * inv_freq).astype(np.float32)
    cos = np.concatenate([np.cos(ang_r), np.cos(ang_r), np.cos(ang_c), np.cos(ang_c)], 1)
    sin = np.concatenate([-np.sin(ang_r), np.sin(ang_r), -np.sin(ang_c), np.sin(ang_c)], 1)
    ones = np.ones((n_tokens, QK_NOPE), np.float32)
    tail = np.zeros((n_tokens, HEAD_PAD - QK_DIM), np.float32)
    rope_c = np.concatenate([ones, cos, tail], axis=1).astype(np.float32)
    rope_s = np.concatenate([np.zeros((n_tokens, QK_NOPE), np.float32), sin, tail],
                            axis=1).astype(np.float32)
    return rope_c, rope_s


def kernel(x_prompt, x_sample, cache_ckv, cache_krope, state_lru, c, c_ctx, norm1_g, ada_w, ada_b, w_in, conv_w, conv_b, lru_conv_w, lru_conv_b, lru_wa, lru_ba, lru_wi, lru_bi, lru_lambda, mla_qnorm_g, mla_wq_up, mla_kvnorm_g, mla_wkv_up, q_norm_g, k_norm_g, gnorm_conv, gnorm_lru, gnorm_mla, w_out, norm2_g, w_gate, w_up, w_down):
    params = dict(norm1_g=norm1_g, w_in=w_in, conv_w=conv_w, conv_b=conv_b,
                  lru_conv_w=lru_conv_w, lru_conv_b=lru_conv_b, lru_wa=lru_wa,
                  lru_ba=lru_ba, lru_wi=lru_wi, lru_bi=lru_bi, lru_lambda=lru_lambda,
                  mla_qnorm_g=mla_qnorm_g, mla_wq_up=mla_wq_up,
                  mla_kvnorm_g=mla_kvnorm_g, mla_wkv_up=mla_wkv_up, q_norm_g=q_norm_g,
                  k_norm_g=k_norm_g, gnorm_conv=gnorm_conv, gnorm_lru=gnorm_lru,
                  gnorm_mla=gnorm_mla, w_out=w_out, norm2_g=norm2_g, w_gate=w_gate,
                  w_up=w_up, w_down=w_down)
    batch, seq_p, _ = x_prompt.shape
    dec_batch, seq_s, _ = x_sample.shape
    assert dec_batch == 4 and cache_ckv.shape[2] == 256

    cond8 = jnp.concatenate(
        [c, c_ctx[None, :], jnp.zeros((8 - dec_batch - 1, D_MODEL), F32)], axis=0)
    mods = _ada_call(cond8, ada_w, ada_b).reshape(DEPTH, 8, 1, 6 * D_MODEL)

    rope_c, rope_s = _rope_tables(seq_s)
    cache_kr_pad = jnp.pad(
        cache_krope, ((0, 0), (0, 0), (0, 0), (ROPE_LO, HEAD_PAD - ROPE_LO - QK_ROPE)))

    def chunked_t(tab):
        return jnp.asarray(np.ascontiguousarray(
            tab.reshape(seq_s // ROW_CHUNK, ROW_CHUNK, HEAD_PAD).transpose(0, 2, 1)))

    extra = dict(rope_c=jnp.asarray(rope_c), rope_s=jnp.asarray(rope_s),
                 rope_ct=chunked_t(rope_c), rope_st=chunked_t(rope_s), cache_ckv=cache_ckv,
                 cache_kr=cache_kr_pad, state=state_lru)

    xp = x_prompt.reshape(batch * seq_p, D_MODEL)
    xs = x_sample.reshape(dec_batch * seq_s, D_MODEL)
    ckv_l, kr_l, st_l = [], [], []
    lw = _prep_weights(params)
    for l in range(DEPTH):
        (hs,) = _mix_call(xs, mods, l, lw, seq=seq_s, n_seq=1, latent=True, extra=extra)
        hp, ckv, kr, st, xs = _mix_call(xp, mods, l, lw, seq=seq_p, n_seq=CTX_SEQS_PER_STEP,
                                        latent=False, ffn_h2d=hs, ffn_rows_per_mod=seq_s)
        xp = _ffn_call(hp, mods, l, lw, rows_per_mod=None, mod_base=4)
        ckv_l.append(ckv.reshape(batch, seq_p, KV_LORA))
        kr_l.append(kr.reshape(batch, seq_p, QK_ROPE))
        st_l.append(st)
    return (xp.reshape(batch, seq_p, D_MODEL),
            xs.reshape(dec_batch, seq_s, D_MODEL),
            jnp.stack(ckv_l, axis=1),
            jnp.stack(kr_l, axis=1),
            jnp.stack(st_l, axis=1))
```

```python
import functools
import itertools
import math

import jax
import jax.numpy as jnp
import numpy as np
from jax import lax
from jax.experimental import pallas as pl
from jax.experimental.pallas import tpu as pltpu

F32 = jnp.float32
BF16 = jnp.bfloat16

D_MODEL = 1024
DEPTH = 2
GRID_W = 64
CONV_W = 256
LRU_W = 256
LRU_BLOCKS = 4
LRU_BLK = 64
LRU_C = 8.0
MLA_HEADS = 8
QK_NOPE = 64
QK_ROPE = 32
V_DIM = 64
V_EXT = V_DIM + 16
QK_DIM = QK_NOPE + QK_ROPE
Q_LORA = 256
KV_LORA = 128
ROPE_THETA = 10000.0
FF = 2816
EPS = 1e-6

LANES = 128
SUBLANES = 8
HEAD_PAD = LANES
VMEM_LIMIT_BYTES = 60 * 1024 * 1024

C_BG, C_CG, C_H = 0, 256, 512
C_XB, C_YB = 768, 1024
C_Q = 1280
C_CKV = 1536
C_KR = 1664
U_W = 1792
SCAN_A = (C_H, C_CKV)
SCAN_B = (C_Q, C_BG)
ROPE_LO = QK_NOPE

def _slots(widths):
    out, off = {}, 0
    for name, width in widths:
        out[name] = (off, width)
        off += width
    return out


VEC_SLOTS = _slots([
    ("norm1_g", D_MODEL), ("conv_b", CONV_W), ("lru_conv_b", LRU_W), ("b_gates", 4 * LRU_W),
    ("mla_qnorm_g", Q_LORA), ("mla_kvnorm_g", KV_LORA), ("kg", HEAD_PAD),
    ("kg_partner", HEAD_PAD), ("gnorm_conv", CONV_W), ("gnorm_lru", LRU_W),
    ("gnorm_mla", MLA_HEADS * V_DIM)])

ROW_CHUNK = 256
SCAN_CHUNK = 32
Q_CHUNK = 256
FF_CHUNK = 256
PROJ_SLAB = 256
PROJ_CHUNKS = 1
CTX_SEQS_PER_STEP = 2
FUSED_FFN_PERIOD = 4
SEQ_STAGGER = 10
FFN_ROWS = 1024
PAD_ROWS = SUBLANES


def _rms_scale(x, n):
    ms = jnp.sum(x * x, axis=-1, keepdims=True) * (1.0 / n)
    return lax.rsqrt(ms + EPS)


def _sigmoid(x):
    return 0.5 * jnp.tanh(0.5 * x) + 0.5


def _gelu_tanh(x):
    c = math.sqrt(2.0 / math.pi)
    return 0.5 * x * (1.0 + jnp.tanh(c * (x + 0.044715 * (x * x * x))))


def _dot(a, b):
    return jnp.dot(a, b, preferred_element_type=F32)


def _interleave(*gens):
    live = list(gens)
    while live:
        for g in list(live):
            try:
                next(g)
            except StopIteration:
                live.remove(g)


def _interleave_staggered(gens, lag, side=None, side_period=1):
    pending = list(gens)
    live = []
    tick = 0
    while live or pending or side is not None:
        if pending and tick % lag == 0:
            live.append(pending.pop(0))
        if side is not None and (tick % side_period == 0 or not (live or pending)):
            try:
                next(side)
            except StopIteration:
                side = None
        for g in list(live):
            try:
                next(g)
            except StopIteration:
                live.remove(g)
        tick += 1


def _dot_nt(a, b):
    return lax.dot_general(a, b, (((1,), (1,)), ((), ())), preferred_element_type=F32)


ADA_TN = 1536


def _ada_kernel(cond_ref, w_ref, b_ref, o_ref):
    cnd = cond_ref[...]
    s = (cnd * _sigmoid(cnd)).astype(BF16)
    o_ref[:, 0, :] = _dot(s, w_ref[...].astype(BF16)) + b_ref[...]


def _ada_call(cond8, ada_w, ada_b):
    n_out = 6 * D_MODEL
    return pl.pallas_call(
        _ada_kernel,
        grid=(DEPTH, n_out // ADA_TN),
        in_specs=[
            pl.BlockSpec((8, D_MODEL), lambda l, j: (0, 0)),
            pl.BlockSpec((None, D_MODEL, ADA_TN), lambda l, j: (l, 0, j)),
            pl.BlockSpec((None, 1, ADA_TN), lambda l, j: (l, 0, j)),
        ],
        out_specs=pl.BlockSpec((None, 8, 1, ADA_TN), lambda l, j: (l, 0, 0, j)),
        out_shape=jax.ShapeDtypeStruct((DEPTH, 8, 1, n_out), F32),
        compiler_params=pltpu.CompilerParams(
            dimension_semantics=("arbitrary", "arbitrary"),
            vmem_limit_bytes=VMEM_LIMIT_BYTES),
        name="ada_mod",
    )(cond8, ada_w, ada_b.reshape(DEPTH, 1, n_out))


def _mix_kernel(*refs, seq, n_seq, latent, fused_ffn):
    T = seq * n_seq
    n_ctx = 256 if latent else 0
    (x_ref, mod_ref, vec_ref, win_ref, wlast_ref, convw_ref, lcw_ref, wg_ref, lam_ref,
     wqt_ref, wk_ref, wvt_ref, qgt_ref, wout_ref) = refs[:14]
    pos = 14
    if latent:
        (wqpt_ref, qgpt_ref, ropect_ref, ropest_ref, ropec_ref, ropes_ref,
         cckv_ref, ckr_ref, h0_ref) = refs[pos:pos + 9]
        pos += 9

    def vec(name):
        off, width = VEC_SLOTS[name]
        return vec_ref[:, off:off + width]
    if fused_ffn:
        ffn_in_refs = refs[pos:pos + 6]
        pos += 6
    h_out_ref = refs[pos]
    pos += 1
    if not latent:
        ckv_out_ref, kr_out_ref, st_out_ref = refs[pos:pos + 3]
        pos += 3
    if fused_ffn:
        ffn_out_ref = refs[pos]
        pos += 1
    u_scr, qt_scr, k_scr, vt_scr, ymt_scr, ycat_scr = refs[pos:pos + 6]
    if latent:
        st_scr, m_scr = refs[pos + 6:pos + 8]

    n_chunks = T // ROW_CHUNK
    lane = lax.broadcasted_iota(jnp.int32, (1, LANES), 1)
    rope_lanes = (lane >= ROPE_LO) & (lane < ROPE_LO + QK_ROPE)

    sh1 = mod_ref[:, 0:D_MODEL]
    sc1 = mod_ref[:, D_MODEL:2 * D_MODEL]
    g1 = mod_ref[:, 2 * D_MODEL:3 * D_MODEL]
    gain1 = vec("norm1_g") * (1.0 + sc1)

    u_scr[0:PAD_ROWS, :] = jnp.zeros((PAD_ROWS, U_W), F32)
    u_scr[PAD_ROWS + T:2 * PAD_ROWS + T, :] = jnp.zeros((PAD_ROWS, U_W), F32)

    def phase_a(c, n=1):
        r0 = c * ROW_CHUNK
        nrows = n * ROW_CHUNK
        x = x_ref[r0:r0 + nrows, :]
        hn = (x * _rms_scale(x, D_MODEL) * gain1 + sh1).astype(BF16)
        yield
        urows = slice(PAD_ROWS + r0, PAD_ROWS + r0 + nrows)
        for s0 in range(0, C_CKV, PROJ_SLAB):
            u_scr[urows, s0:s0 + PROJ_SLAB] = _dot(hn, win_ref[:, s0:s0 + PROJ_SLAB])
            yield
        u_scr[urows, C_CKV:U_W] = _dot(hn, wlast_ref[...])
        yield
        u_scr[urows, C_CG:C_CG + CONV_W] = (
            u_scr[urows, C_CG:C_CG + CONV_W] * u_scr[urows, C_H:C_H + CONV_W])
        yield

    vrow = lax.broadcasted_iota(jnp.int32, (MLA_HEADS * V_EXT, 1), 0)
    ones_rows = (vrow % V_EXT) >= V_DIM

    def values_t(cb):
        return jnp.where(ones_rows, 1.0, _dot_nt(wvt_ref[...], cb)).astype(BF16)

    def phase_b(c):
        r0 = c * ROW_CHUNK
        rows = slice(PAD_ROWS + r0, PAD_ROWS + r0 + ROW_CHUNK)
        uq = u_scr[rows, C_Q:C_Q + Q_LORA]
        qn = (uq * _rms_scale(uq, Q_LORA) * vec("mla_qnorm_g")).astype(BF16)
        qa_t = _dot_nt(wqt_ref[...], qn)
        if latent:
            qpa_t = _dot_nt(wqpt_ref[...], qn)
            q_tc = ropect_ref[c] * qgt_ref[...]
            q_ts = ropest_ref[c] * qgpt_ref[...]
        yield
        for h in range(MLA_HEADS):
            sl = slice(h * HEAD_PAD, (h + 1) * HEAD_PAD)
            qh = qa_t[sl, :]
            rq = lax.rsqrt(jnp.sum(qh * qh, axis=0, keepdims=True) * (1.0 / QK_DIM) + EPS)
            if latent:
                qh = (qh * q_tc + qpa_t[sl, :] * q_ts) * rq
            else:
                qh = qh * qgt_ref[...] * rq
            qt_scr[h, c] = qh.astype(BF16)
            if h % 2:
                yield
        uc = u_scr[rows, C_CKV:C_CKV + KV_LORA]
        ckv = uc * _rms_scale(uc, KV_LORA) * vec("mla_kvnorm_g")
        krb = u_scr[rows, C_KR:C_KR + LANES]
        kr_rolled = pltpu.roll(krb, 64, 1)
        if not latent:
            ckv_out_ref[r0:r0 + ROW_CHUNK, :] = ckv
            kr_out_ref[r0:r0 + ROW_CHUNK, :] = kr_rolled[:, 0:QK_ROPE]
        krm = jnp.where(rope_lanes, krb, 0.0)
        cb = ckv.astype(BF16)
        ka = _dot(cb, wk_ref[...])
        vt_scr[c] = values_t(cb)
        if latent:
            k_tc = ropec_ref[r0:r0 + ROW_CHUNK, :] * vec("kg")
            k_ts = ropes_ref[r0:r0 + ROW_CHUNK, :] * vec("kg_partner")
        yield
        for h in range(MLA_HEADS):
            sl = slice(h * HEAD_PAD, (h + 1) * HEAD_PAD)
            kpre = ka[:, sl] + krm
            rk = _rms_scale(kpre, QK_DIM)
            if latent:
                kh = (kpre * k_tc + kr_rolled * k_ts) * rk
            else:
                kh = kpre * vec("kg") * rk
            k_scr[h, r0:r0 + ROW_CHUNK, :] = kh.astype(BF16)
            if h % 2:
                yield

    if latent:
        cc = cckv_ref[...].astype(BF16)
        ka = _dot(cc, wk_ref[...])
        vt_scr[n_chunks] = values_t(cc)
        krc = ckr_ref[...]
        for h in range(MLA_HEADS):
            kpre = ka[:, h * HEAD_PAD:(h + 1) * HEAD_PAD] + krc
            kh = kpre * vec("kg") * _rms_scale(kpre, QK_DIM)
            k_scr[h, seq:seq + n_ctx, :] = kh.astype(BF16)

    neg_lam = -lam_ref[...]
    sp = jnp.maximum(neg_lam, 0.0) + jnp.log(1.0 + jnp.exp(-jnp.abs(neg_lam)))
    log2a_coef = (-0.5 * LRU_C * math.log2(math.e)) * sp
    row_rc = lax.broadcasted_iota(jnp.int32, (ROW_CHUNK, 1), 0)

    def phase_c(ci):
        r0 = ci * ROW_CHUNK
        base = PAD_ROWS + r0
        first = (r0 % seq) == 0
        last = ((r0 + ROW_CHUNK) % seq) == 0

        def win(col, shift, width=CONV_W):
            w = u_scr[base + shift:base + shift + ROW_CHUNK, col:col + width]
            if shift < 0 and first:
                w = jnp.where(row_rc < -shift, 0.0, w)
            if shift > 0 and last:
                w = jnp.where(row_rc >= ROW_CHUNK - shift, 0.0, w)
            return w

        z_m, z_0, z_p = win(C_CG, -1), win(C_CG, 0), win(C_CG, 1)
        conv = (z_m * convw_ref[0:1, :] + z_0 * convw_ref[1:2, :]
                + z_p * convw_ref[2:3, :] + vec("conv_b"))
        yc = u_scr[base:base + ROW_CHUNK, C_BG:C_BG + CONV_W] * conv
        ycn = yc * _rms_scale(yc, CONV_W) * vec("gnorm_conv")
        ycat_scr[r0:r0 + ROW_CHUNK, 0:CONV_W] = ycn.astype(BF16)
        yield

        xc = (win(C_XB, -2) * lcw_ref[0:1, :] + win(C_XB, -1) * lcw_ref[1:2, :]
              + win(C_XB, 0) * lcw_ref[2:3, :] + win(C_XB, 1) * lcw_ref[3:4, :]
              + vec("lru_conv_b"))
        half_gates = _dot(xc.astype(BF16), wg_ref[...]) + vec("b_gates")
        half_xc = 0.5 * xc
        yield
        for d in range(2):
            t_r = jnp.tanh(half_gates[:, (2 * d) * LRU_W:(2 * d + 1) * LRU_W])
            t_i = jnp.tanh(half_gates[:, (2 * d + 1) * LRU_W:(2 * d + 2) * LRU_W])
            a = jnp.exp2(log2a_coef[d:d + 1, :] * (t_r + 1.0))
            v = 1.0 - a * a
            mult = v * lax.rsqrt(jnp.maximum(v, 1e-30))
            u_scr[base:base + ROW_CHUNK, SCAN_A[d]:SCAN_A[d] + LRU_W] = a
            u_scr[base:base + ROW_CHUNK, SCAN_B[d]:SCAN_B[d] + LRU_W] = (
                mult * (half_xc * (t_i + 1.0)))
            yield

    n_sc = seq // SCAN_CHUNK
    row_sc = lax.broadcasted_iota(jnp.int32, (SCAN_CHUNK, LRU_W), 0)
    steps = [1 << i for i in range(int(math.log2(SCAN_CHUNK)))]

    def scan_step(rf, rb, cf, cb_):
        rf = pl.multiple_of(rf + PAD_ROWS, SUBLANES)
        rb = pl.multiple_of(rb + PAD_ROWS, SUBLANES)
        a = u_scr[pl.ds(rf, SCAN_CHUNK), SCAN_A[0]:SCAN_A[0] + LRU_W]
        b = u_scr[pl.ds(rf, SCAN_CHUNK), SCAN_B[0]:SCAN_B[0] + LRU_W]
        b = b + jnp.where(row_sc == 0, a * cf, 0.0)
        for d in steps:
            if d % SUBLANES == 0:
                b = jnp.concatenate([b[:d], b[d:] + a[d:] * b[:-d]], axis=0)
                if d != steps[-1]:
                    a = jnp.concatenate([a[:d], a[d:] * a[:-d]], axis=0)
                continue
            b = b + a * jnp.where(row_sc >= d, pltpu.roll(b, d, 0), 0.0)
            if d != steps[-1]:
                a = a * jnp.where(row_sc >= d, pltpu.roll(a, d, 0), 1.0)
        u_scr[pl.ds(rf, SCAN_CHUNK), SCAN_B[0]:SCAN_B[0] + LRU_W] = b
        cf = b[SCAN_CHUNK - 1:SCAN_CHUNK, :]
        a = u_scr[pl.ds(rb, SCAN_CHUNK), SCAN_A[1]:SCAN_A[1] + LRU_W]
        b = u_scr[pl.ds(rb, SCAN_CHUNK), SCAN_B[1]:SCAN_B[1] + LRU_W]
        b = b + jnp.where(row_sc == SCAN_CHUNK - 1, a * cb_, 0.0)
        for d in steps:
            if d % SUBLANES == 0:
                b = jnp.concatenate([b[:-d] + a[:-d] * b[d:], b[-d:]], axis=0)
                if d != steps[-1]:
                    a = jnp.concatenate([a[:-d] * a[d:], a[-d:]], axis=0)
                continue
            keep = row_sc < SCAN_CHUNK - d
            b = b + a * jnp.where(keep, pltpu.roll(b, SCAN_CHUNK - d, 0), 0.0)
            if d != steps[-1]:
                a = a * jnp.where(keep, pltpu.roll(a, SCAN_CHUNK - d, 0), 1.0)
        u_scr[pl.ds(rb, SCAN_CHUNK), SCAN_B[1]:SCAN_B[1] + LRU_W] = b
        cb_ = b[0:1, :]
        return cf, cb_

    cps = seq // ROW_CHUNK

    def softmax_pv(st, vts):
        m = jnp.max(st, axis=0, keepdims=True)
        pt = jnp.exp2(st - m).astype(BF16)
        ot = None
        for i, vt in enumerate(vts):
            part = _dot(vt, pt[i * ROW_CHUNK:(i + 1) * ROW_CHUNK, :])
            ot = part if ot is None else ot + part
        return ot[0:V_DIM, :] * (1.0 / ot[V_DIM:V_DIM + 1, :])

    def f_build(c):
        r0 = c * ROW_CHUNK
        rows = slice(r0, r0 + ROW_CHUNK)
        urows = slice(PAD_ROWS + r0, PAD_ROWS + r0 + ROW_CHUNK)
        hf = u_scr[urows, SCAN_B[0]:SCAN_B[0] + LRU_W]
        hb = u_scr[urows, SCAN_B[1]:SCAN_B[1] + LRU_W]
        yb = u_scr[urows, C_YB:C_YB + LRU_W]
        yl = (hf + hb) * _gelu_tanh(yb)
        yln = yl * _rms_scale(yl, LRU_W) * vec("gnorm_lru")
        ycat_scr[rows, CONV_W:CONV_W + LRU_W] = yln.astype(BF16)
        yield
        ymt = ymt_scr[c]
        ssq = jnp.sum(ymt * ymt, axis=0, keepdims=True)
        ynt = ymt * lax.rsqrt(ssq * (1.0 / (MLA_HEADS * V_DIM)) + EPS)
        ycat_scr[rows, CONV_W + LRU_W:D_MODEL] = (ynt.T * vec("gnorm_mla")).astype(BF16)
        yield

    def f_project(c):
        rows = slice(c * ROW_CHUNK, (c + 1) * ROW_CHUNK)
        for n0 in range(0, D_MODEL, PROJ_SLAB):
            cols = slice(n0, n0 + PROJ_SLAB)
            y = _dot(ycat_scr[rows, :], wout_ref[:, cols])
            h_out_ref[rows, cols] = x_ref[rows, cols] + g1[:, cols] * y
            yield

    if latent:
        groups = [list(range(g0, g0 + PROJ_CHUNKS)) for g0 in range(0, n_chunks, PROJ_CHUNKS)]
        _interleave(phase_a(0, PROJ_CHUNKS))
        for gi, group in enumerate(groups):
            dependent = []
            for c in group:
                dependent.append(phase_b(c))
                if c:
                    dependent.append(phase_c(c - 1))
            ahead = [phase_a(groups[gi + 1][0], PROJ_CHUNKS)] if gi + 1 < len(groups) else []
            _interleave(*ahead, itertools.chain(*dependent))
        _interleave(phase_c(n_chunks - 1))

        key_chunks = list(range(cps)) + [n_chunks]
        n_blk = cps * (MLA_HEADS // 2)
        trips = n_blk // 2
        scans_per_trip = n_sc // trips

        def heads_of(idx):
            qc = idx // (MLA_HEADS // 2)
            j = idx % (MLA_HEADS // 2)
            return qc, [2 * j, 2 * j + 1]

        def scores(idx, slot):
            qc, hs = heads_of(idx)
            for par, h in enumerate(hs):
                qt = qt_scr[h, qc]
                m = None
                for i in range(len(key_chunks)):
                    rows = slice(i * ROW_CHUNK, (i + 1) * ROW_CHUNK)
                    st = _dot(k_scr[h, rows, :], qt)
                    st_scr[slot, par, rows, :] = st
                    mc = jnp.max(st, axis=0, keepdims=True)
                    m = mc if m is None else jnp.maximum(m, mc)
                    yield
                m_scr[slot, par] = m

        def finish(idx, slot):
            qc, hs = heads_of(idx)
            for par, h in enumerate(hs):
                hr = pl.ds(pl.multiple_of(h * V_DIM, V_DIM), V_DIM)
                vr = pl.ds(pl.multiple_of(h * V_EXT, V_EXT - V_DIM), V_EXT)
                m = m_scr[slot, par]
                ot = None
                for i, kc in enumerate(key_chunks):
                    rows = slice(i * ROW_CHUNK, (i + 1) * ROW_CHUNK)
                    pt = jnp.exp2(st_scr[slot, par, rows, :] - m).astype(BF16)
                    part = _dot(vt_scr[kc, vr, :], pt)
                    ot = part if ot is None else ot + part
                    yield
                ymt_scr[qc, hr, :] = ot[0:V_DIM, :] * (1.0 / ot[V_DIM:V_DIM + 1, :])

        _interleave(scores(0, 0))

        def att_body(i, carry):
            b0 = 2 * i
            state = {"carry": carry, "c": i * scans_per_trip}

            def scans(n):
                for _ in range(n):
                    c = state["c"]
                    rf = pl.multiple_of(c * SCAN_CHUNK, SCAN_CHUNK)
                    rb = pl.multiple_of((n_sc - 1 - c) * SCAN_CHUNK, SCAN_CHUNK)
                    state["carry"] = scan_step(rf, rb, *state["carry"])
                    state["c"] = c + 1
                    yield
                    yield
                    yield

            half = scans_per_trip // 2
            _interleave(scores(b0 + 1, 1), finish(b0, 0), scans(half))
            _interleave(scores(jnp.minimum(b0 + 2, n_blk - 1), 0), finish(b0 + 1, 1),
                        scans(scans_per_trip - half))
            return state["carry"]

        lax.fori_loop(0, trips, att_body, (h0_ref[0:1, :], h0_ref[1:2, :]))

        _interleave(f_build(0))
        for c in range(n_chunks):
            ahead = [f_build(c + 1)] if c + 1 < n_chunks else []
            _interleave(f_project(c), *ahead)
    else:
        def phase_d(s):
            carry = (jnp.zeros((1, LRU_W), F32), jnp.zeros((1, LRU_W), F32))
            for c in range(n_sc):
                carry = scan_step(s * seq + c * SCAN_CHUNK,
                                  s * seq + (n_sc - 1 - c) * SCAN_CHUNK, *carry)
                yield
            st_out_ref[s, 0:1, :] = carry[0]
            st_out_ref[s, 1:2, :] = carry[1]

        def phase_e(s):
            blocks = [(qi, h) for qi in range(cps) for h in range(MLA_HEADS)]

            def scores(blk):
                qi, h = blk
                return _dot(k_scr[h, s * seq:(s + 1) * seq, :], qt_scr[h, s * cps + qi])

            st = scores(blocks[0])
            for i, (qi, h) in enumerate(blocks):
                nxt = scores(blocks[i + 1]) if i + 1 < len(blocks) else None
                hr = slice(h * V_DIM, (h + 1) * V_DIM)
                vr = slice(h * V_EXT, (h + 1) * V_EXT)
                vts = [vt_scr[s * cps + j, vr, :] for j in range(cps)]
                ymt_scr[s * cps + qi, hr, :] = softmax_pv(st, vts)
                st = nxt
                yield

        def sequence(s):
            chunks = range(s * cps, (s + 1) * cps)
            for c in chunks:
                yield from phase_a(c)
            for c in chunks:
                yield from phase_b(c)
            for c in chunks:
                yield from phase_c(c)
            yield from phase_d(s)
            yield from phase_e(s)
            for c in chunks:
                yield from f_build(c)
                yield from f_project(c)

        side = _ffn_steps(*ffn_in_refs, ffn_out_ref) if fused_ffn else None
        _interleave_staggered([sequence(s) for s in range(n_seq)], SEQ_STAGGER,
                              side=side, side_period=FUSED_FFN_PERIOD)


def _const_spec(shape):
    nd = len(shape)
    return pl.BlockSpec(shape, lambda i, _n=nd: (0,) * _n,
                        pipeline_mode=pl.Buffered(1))


def _layer_spec(arr, layer):
    nd = arr.ndim - 1
    return pl.BlockSpec((None,) + arr.shape[1:], lambda i, _n=nd: (layer,) + (0,) * _n,
                        pipeline_mode=pl.Buffered(1))


def _mix_call(x2d, mods, layer, lw, *, seq, n_seq, latent, extra=None, ffn_h2d=None,
              ffn_rows_per_mod=None):
    n_tok = x2d.shape[0]
    T = seq * n_seq
    grid = (n_tok // T,)
    fused_ffn = ffn_h2d is not None
    n_keys_buf = T if not latent else seq + 256

    if latent:
        mod_map = lambda i: (layer, i, 0, 0)
    else:
        mod_map = lambda i: (layer, 4, 0, 0)

    consts = [lw["vecs"], lw["w_in"], lw["w_in_last"], lw["conv_w"], lw["lru_conv_w"],
              lw["w_gates"], lw["lru_lambda"], lw["wq_t"], lw["wk"], lw["wv_t"], lw["qg_t"],
              lw["w_out"]]
    args = [x2d, mods] + consts
    in_specs = [pl.BlockSpec((T, D_MODEL), lambda i: (i, 0)),
                pl.BlockSpec((None, None, 1, 6 * D_MODEL), mod_map)]
    in_specs += [_layer_spec(a, layer) for a in consts]
    if latent:
        lconsts = [lw["wq_partner_t"], lw["qg_partner_t"]]
        tables = [extra["rope_ct"], extra["rope_st"], extra["rope_c"], extra["rope_s"]]
        args += lconsts + tables
        in_specs += [_layer_spec(a, layer) for a in lconsts]
        in_specs += [_const_spec(a.shape) for a in tables]
        args += [extra["cache_ckv"], extra["cache_kr"], extra["state"]]
        in_specs += [
            pl.BlockSpec((None, None, 256, KV_LORA), lambda i: (i, layer, 0, 0)),
            pl.BlockSpec((None, None, 256, LANES), lambda i: (i, layer, 0, 0)),
            pl.BlockSpec((None, None, 2, LRU_W), lambda i: (i, layer, 0, 0)),
        ]

    if fused_ffn:
        ffn_rows = ffn_h2d.shape[0] // grid[0]
        tiles_per_mod = ffn_rows_per_mod // ffn_rows
        ffn_consts = [lw["norm2_g"], lw["w_gate"], lw["w_up"], lw["w_down"]]
        args += [ffn_h2d, mods] + ffn_consts
        in_specs += [pl.BlockSpec((ffn_rows, D_MODEL), lambda i: (i, 0)),
                     pl.BlockSpec((None, None, 1, 6 * D_MODEL),
                                  lambda i: (layer, i // tiles_per_mod, 0, 0))]
        in_specs += [_layer_spec(a, layer) for a in ffn_consts]

    out_shape = [jax.ShapeDtypeStruct((n_tok, D_MODEL), F32)]
    out_specs = [pl.BlockSpec((T, D_MODEL), lambda i: (i, 0))]
    if not latent:
        out_shape += [jax.ShapeDtypeStruct((n_tok, KV_LORA), F32),
                      jax.ShapeDtypeStruct((n_tok, QK_ROPE), F32),
                      jax.ShapeDtypeStruct((n_tok // seq, 2, LRU_W), F32)]
        out_specs += [pl.BlockSpec((T, KV_LORA), lambda i: (i, 0)),
                      pl.BlockSpec((T, QK_ROPE), lambda i: (i, 0)),
                      pl.BlockSpec((n_seq, 2, LRU_W), lambda i: (i, 0, 0))]
    if fused_ffn:
        out_shape.append(jax.ShapeDtypeStruct(ffn_h2d.shape, F32))
        out_specs.append(pl.BlockSpec((ffn_rows, D_MODEL), lambda i: (i, 0)))

    scratch = [
        pltpu.VMEM((T + 2 * PAD_ROWS, U_W), F32),
        pltpu.VMEM((MLA_HEADS, T // ROW_CHUNK, HEAD_PAD, ROW_CHUNK), BF16),
        pltpu.VMEM((MLA_HEADS, n_keys_buf, HEAD_PAD), BF16),
        pltpu.VMEM((n_keys_buf // ROW_CHUNK, MLA_HEADS * V_EXT, ROW_CHUNK), BF16),
        pltpu.VMEM((T // ROW_CHUNK, MLA_HEADS * V_DIM, ROW_CHUNK), F32),
        pltpu.VMEM((T, D_MODEL), BF16),
    ]
    if latent:
        scratch.append(pltpu.VMEM((2, 2, n_keys_buf, ROW_CHUNK), F32))
        scratch.append(pltpu.VMEM((2, 2, 1, ROW_CHUNK), F32))
    return pl.pallas_call(
        functools.partial(_mix_kernel, seq=seq, n_seq=n_seq, latent=latent,
                          fused_ffn=fused_ffn),
        grid=grid,
        in_specs=in_specs,
        out_specs=out_specs,
        out_shape=out_shape,
        scratch_shapes=scratch,
        compiler_params=pltpu.CompilerParams(
            dimension_semantics=("arbitrary",),
            vmem_limit_bytes=VMEM_LIMIT_BYTES),
        name="mix_latent" if latent else ("mix_context_swiglu" if fused_ffn else "mix_context"),
    )(*args)


def _ffn_steps(h_ref, mod_ref, n2g_ref, wg_ref, wu_ref, wd_ref, o_ref):
    sh2 = mod_ref[:, 3 * D_MODEL:4 * D_MODEL]
    sc2 = mod_ref[:, 4 * D_MODEL:5 * D_MODEL]
    g2 = mod_ref[:, 5 * D_MODEL:6 * D_MODEL]
    h = h_ref[...]
    hn = (h * _rms_scale(h, D_MODEL) * (n2g_ref[...] * (1.0 + sc2)) + sh2).astype(BF16)
    yield
    acc = None
    for c0 in range(0, FF, FF_CHUNK):
        g = _dot(hn, wg_ref[:, c0:c0 + FF_CHUNK])
        u = _dot(hn, wu_ref[:, c0:c0 + FF_CHUNK])
        act = (g * _sigmoid(g) * u).astype(BF16)
        part = _dot(act, wd_ref[c0:c0 + FF_CHUNK, :])
        acc = part if acc is None else acc + part
        yield
    o_ref[...] = h_ref[...] + g2 * acc
    yield


def _ffn_kernel(*refs):
    _interleave(_ffn_steps(*refs))


def _ffn_call(h2d, mods, layer, lw, *, rows_per_mod, mod_base):
    n_tok = h2d.shape[0]
    if mod_base is None:
        tiles_per_mod = rows_per_mod // FFN_ROWS
        mod_map = lambda i: (layer, i // tiles_per_mod, 0, 0)
    else:
        mod_map = lambda i: (layer, mod_base, 0, 0)
    consts = [lw["norm2_g"], lw["w_gate"], lw["w_up"], lw["w_down"]]
    return pl.pallas_call(
        _ffn_kernel,
        grid=(n_tok // FFN_ROWS,),
        in_specs=[pl.BlockSpec((FFN_ROWS, D_MODEL), lambda i: (i, 0)),
                  pl.BlockSpec((None, None, 1, 6 * D_MODEL), mod_map)]
        + [_layer_spec(a, layer) for a in consts],
        out_specs=pl.BlockSpec((FFN_ROWS, D_MODEL), lambda i: (i, 0)),
        out_shape=jax.ShapeDtypeStruct((n_tok, D_MODEL), F32),
        compiler_params=pltpu.CompilerParams(
            dimension_semantics=("arbitrary",),
            vmem_limit_bytes=VMEM_LIMIT_BYTES),
        name="swiglu",
    )(h2d, mods, *consts)


def _rope_partner_index():
    return [d + 8 if (d % 16) < 8 else d - 8 for d in range(QK_ROPE)]


def _prep_weights(p):
    partner = jnp.array(_rope_partner_index(), jnp.int32)
    q_scale = QK_DIM ** -0.5 * math.log2(math.e)
    w_in = p["w_in"][:, :, :C_CKV].astype(BF16)
    w_tail = p["w_in"][:, :, C_CKV:].astype(BF16)
    w_kr = w_tail[:, :, KV_LORA:KV_LORA + QK_ROPE]
    zeros32 = jnp.zeros((DEPTH, D_MODEL, QK_ROPE), BF16)
    w_in_last = jnp.concatenate(
        [w_tail[:, :, :KV_LORA], w_kr[:, :, partner], zeros32, w_kr, zeros32], axis=2)

    wq_h = p["mla_wq_up"].reshape(DEPTH, Q_LORA, MLA_HEADS, QK_DIM)
    wq_pad = jnp.pad(wq_h, ((0, 0), (0, 0), (0, 0), (0, HEAD_PAD - QK_DIM)))
    wq_t = wq_pad.reshape(DEPTH, Q_LORA, MLA_HEADS * HEAD_PAD).transpose(0, 2, 1)
    rope_pad = ((0, 0), (0, 0), (0, 0), (ROPE_LO, HEAD_PAD - ROPE_LO - QK_ROPE))
    wq_partner = jnp.pad(wq_h[:, :, :, QK_NOPE:][:, :, :, partner], rope_pad)
    wq_partner_t = wq_partner.reshape(DEPTH, Q_LORA, MLA_HEADS * HEAD_PAD).transpose(0, 2, 1)

    wkv = p["mla_wkv_up"].reshape(DEPTH, KV_LORA, MLA_HEADS, QK_NOPE + V_DIM)
    wk_pad = jnp.pad(wkv[:, :, :, :QK_NOPE], ((0, 0), (0, 0), (0, 0), (0, HEAD_PAD - QK_NOPE)))
    wk_pad = wk_pad.reshape(DEPTH, KV_LORA, MLA_HEADS * HEAD_PAD)
    wv_ext = jnp.pad(wkv[:, :, :, QK_NOPE:], ((0, 0), (0, 0), (0, 0), (0, V_EXT - V_DIM)))
    wv_t = wv_ext.reshape(DEPTH, KV_LORA, MLA_HEADS * V_EXT).transpose(0, 2, 1)

    def pad_gain(g):
        return jnp.pad(g, ((0, 0), (0, HEAD_PAD - QK_DIM)))[:, None, :]

    def partner_gain(g):
        gp = g[:, QK_NOPE:][:, partner]
        return jnp.pad(gp, ((0, 0), (ROPE_LO, HEAD_PAD - ROPE_LO - QK_ROPE)))[:, None, :]

    def on_sublanes(g):
        return jnp.broadcast_to(g.transpose(0, 2, 1), (DEPTH, HEAD_PAD, ROW_CHUNK))

    qg, kg = p["q_norm_g"], p["k_norm_g"]
    eye = jnp.eye(LRU_BLOCKS, dtype=F32)
    gate_w = jnp.stack([p["lru_wa"][:, 0], p["lru_wi"][:, 0],
                        p["lru_wa"][:, 1], p["lru_wi"][:, 1]], axis=1)
    w_gates = (gate_w[:, :, :, :, None, :] * eye[None, None, :, None, :, None])
    w_gates = w_gates.transpose(0, 2, 3, 1, 4, 5).reshape(DEPTH, LRU_W, 4 * LRU_W)
    b_gates = jnp.stack([p["lru_ba"][:, 0], p["lru_bi"][:, 0],
                         p["lru_ba"][:, 1], p["lru_bi"][:, 1]], axis=1)

    def row(v):
        return v[:, None, :]

    rows = {
        "norm1_g": p["norm1_g"], "conv_b": p["conv_b"], "lru_conv_b": p["lru_conv_b"],
        "b_gates": 0.5 * b_gates.reshape(DEPTH, 4 * LRU_W),
        "mla_qnorm_g": p["mla_qnorm_g"], "mla_kvnorm_g": p["mla_kvnorm_g"],
        "kg": pad_gain(kg)[:, 0, :], "kg_partner": partner_gain(kg)[:, 0, :],
        "gnorm_conv": p["gnorm_conv"], "gnorm_lru": p["gnorm_lru"],
        "gnorm_mla": p["gnorm_mla"],
    }
    for name, (_, width) in VEC_SLOTS.items():
        assert rows[name].shape == (DEPTH, width), name
    vecs = row(jnp.concatenate([rows[name] for name in VEC_SLOTS], axis=1))

    return {
        "vecs": vecs,
        "w_in": w_in,
        "w_in_last": w_in_last,
        "conv_w": p["conv_w"],
        "lru_conv_w": p["lru_conv_w"],
        "w_gates": (0.5 * w_gates).astype(BF16),
        "lru_lambda": p["lru_lambda"],
        "wq_t": wq_t.astype(BF16),
        "wq_partner_t": wq_partner_t.astype(BF16),
        "wk": wk_pad.astype(BF16),
        "wv_t": wv_t.astype(BF16),
        "qg_t": on_sublanes(pad_gain(qg) * q_scale),
        "qg_partner_t": on_sublanes(partner_gain(qg) * q_scale),
        "w_out": p["w_out"].astype(BF16),
        "norm2_g": row(p["norm2_g"]),
        "w_gate": p["w_gate"].astype(BF16),
        "w_up": p["w_up"].astype(BF16),
        "w_down": p["w_down"].astype(BF16),
    }


def _rope_tables(n_tokens):
    n_rows = n_tokens // GRID_W
    row = np.repeat(np.arange(n_rows, dtype=np.float32), GRID_W)
    col = np.tile(np.arange(GRID_W, dtype=np.float32), n_rows)
    n_freq = QK_ROPE // 4
    inv_freq = np.power(np.float32(ROPE_THETA),
                        -np.arange(n_freq, dtype=np.float32) / np.float32(n_freq))
    ang_r = (row[:, None] * inv_freq).astype(np.float32)
    ang_c = (col[:, None] * inv_freq).astype(np.float32)
    cos = np.concatenate([np.cos(ang_r), np.cos(ang_r), np.cos(ang_c), np.cos(ang_c)], 1)
    sin = np.concatenate([-np.sin(ang_r), np.sin(ang_r), -np.sin(ang_c), np.sin(ang_c)], 1)
    ones = np.ones((n_tokens, QK_NOPE), np.float32)
    tail = np.zeros((n_tokens, HEAD_PAD - QK_DIM), np.float32)
    rope_c = np.concatenate([ones, cos, tail], axis=1).astype(np.float32)
    rope_s = np.concatenate([np.zeros((n_tokens, QK_NOPE), np.float32), sin, tail],
                            axis=1).astype(np.float32)
    return rope_c, rope_s


def kernel(x_prompt, x_sample, cache_ckv, cache_krope, state_lru, c, c_ctx, norm1_g, ada_w, ada_b, w_in, conv_w, conv_b, lru_conv_w, lru_conv_b, lru_wa, lru_ba, lru_wi, lru_bi, lru_lambda, mla_qnorm_g, mla_wq_up, mla_kvnorm_g, mla_wkv_up, q_norm_g, k_norm_g, gnorm_conv, gnorm_lru, gnorm_mla, w_out, norm2_g, w_gate, w_up, w_down):
    params = dict(norm1_g=norm1_g, w_in=w_in, conv_w=conv_w, conv_b=conv_b,
                  lru_conv_w=lru_conv_w, lru_conv_b=lru_conv_b, lru_wa=lru_wa,
                  lru_ba=lru_ba, lru_wi=lru_wi, lru_bi=lru_bi, lru_lambda=lru_lambda,
                  mla_qnorm_g=mla_qnorm_g, mla_wq_up=mla_wq_up,
                  mla_kvnorm_g=mla_kvnorm_g, mla_wkv_up=mla_wkv_up, q_norm_g=q_norm_g,
                  k_norm_g=k_norm_g, gnorm_conv=gnorm_conv, gnorm_lru=gnorm_lru,
                  gnorm_mla=gnorm_mla, w_out=w_out, norm2_g=norm2_g, w_gate=w_gate,
                  w_up=w_up, w_down=w_down)
    batch, seq_p, _ = x_prompt.shape
    dec_batch, seq_s, _ = x_sample.shape
    assert dec_batch == 4 and cache_ckv.shape[2] == 256

    cond8 = jnp.concatenate(
        [c, c_ctx[None, :], jnp.zeros((8 - dec_batch - 1, D_MODEL), F32)], axis=0)
    mods = _ada_call(cond8, ada_w, ada_b)

    rope_c, rope_s = _rope_tables(seq_s)
    cache_kr_pad = jnp.pad(
        cache_krope, ((0, 0), (0, 0), (0, 0), (ROPE_LO, HEAD_PAD - ROPE_LO - QK_ROPE)))

    def chunked_t(tab):
        return jnp.asarray(np.ascontiguousarray(
            tab.reshape(seq_s // ROW_CHUNK, ROW_CHUNK, HEAD_PAD).transpose(0, 2, 1)))

    extra = dict(rope_c=jnp.asarray(rope_c), rope_s=jnp.asarray(rope_s),
                 rope_ct=chunked_t(rope_c), rope_st=chunked_t(rope_s), cache_ckv=cache_ckv,
                 cache_kr=cache_kr_pad, state=state_lru)

    xp = x_prompt.reshape(batch * seq_p, D_MODEL)
    xs = x_sample.reshape(dec_batch * seq_s, D_MODEL)
    ckv_l, kr_l, st_l = [], [], []
    lw = _prep_weights(params)
    for l in range(DEPTH):
        (hs,) = _mix_call(xs, mods, l, lw, seq=seq_s, n_seq=1, latent=True, extra=extra)
        hp, ckv, kr, st, xs = _mix_call(xp, mods, l, lw, seq=seq_p, n_seq=CTX_SEQS_PER_STEP,
                                        latent=False, ffn_h2d=hs, ffn_rows_per_mod=seq_s)
        xp = _ffn_call(hp, mods, l, lw, rows_per_mod=None, mod_base=4)
        ckv_l.append(ckv.reshape(batch, seq_p, KV_LORA))
        kr_l.append(kr.reshape(batch, seq_p, QK_ROPE))
        st_l.append(st)
    return (xp.reshape(batch, seq_p, D_MODEL),
            xs.reshape(dec_batch, seq_s, D_MODEL),
            jnp.stack(ckv_l, axis=1),
            jnp.stack(kr_l, axis=1),
            jnp.stack(st_l, axis=1))
```

```python
import functools
import itertools
import math

import jax
import jax.numpy as jnp
import numpy as np
from jax import lax
from jax.experimental import pallas as pl
from jax.experimental.pallas import tpu as pltpu

F32 = jnp.float32
BF16 = jnp.bfloat16

D_MODEL = 1024
DEPTH = 2
GRID_W = 64
CONV_W = 256
LRU_W = 256
LRU_BLOCKS = 4
LRU_BLK = 64
LRU_C = 8.0
MLA_HEADS = 8
QK_NOPE = 64
QK_ROPE = 32
V_DIM = 64
V_EXT = V_DIM + 16
QK_DIM = QK_NOPE + QK_ROPE
Q_LORA = 256
KV_LORA = 128
ROPE_THETA = 10000.0
FF = 2816
EPS = 1e-6

LANES = 128
SUBLANES = 8
HEAD_PAD = LANES
VMEM_LIMIT_BYTES = 60 * 1024 * 1024

C_BG, C_CG, C_H = 0, 256, 512
C_XB, C_YB = 768, 1024
C_Q = 1280
C_CKV = 1536
C_KR = 1664
U_W = 1792
SCAN_A = (C_H, C_CKV)
SCAN_B = (C_Q, C_BG)
ROPE_LO = QK_NOPE

def _slots(widths):
    out, off = {}, 0
    for name, width in widths:
        out[name] = (off, width)
        off += width
    return out


VEC_SLOTS = _slots([
    ("norm1_g", D_MODEL), ("conv_b", CONV_W), ("lru_conv_b", LRU_W), ("b_gates", 4 * LRU_W),
    ("mla_qnorm_g", Q_LORA), ("mla_kvnorm_g", KV_LORA), ("kg", HEAD_PAD),
    ("kg_partner", HEAD_PAD), ("gnorm_conv", CONV_W), ("gnorm_lru", LRU_W),
    ("gnorm_mla", MLA_HEADS * V_DIM)])

ROW_CHUNK = 256
SCAN_CHUNK = 32
Q_CHUNK = 256
FF_CHUNK = 256
PROJ_SLAB = 256
PROJ_CHUNKS = 1
CTX_SEQS_PER_STEP = 2
FUSED_FFN_PERIOD = 4
SEQ_STAGGER = 10
FFN_ROWS = 1024
PAD_ROWS = SUBLANES


def _rms_scale(x, n):
    ms = jnp.sum(x * x, axis=-1, keepdims=True) * (1.0 / n)
    return lax.rsqrt(ms + EPS)


def _sigmoid(x):
    return 0.5 * jnp.tanh(0.5 * x) + 0.5


def _gelu_tanh(x):
    c = math.sqrt(2.0 / math.pi)
    return 0.5 * x * (1.0 + jnp.tanh(c * (x + 0.044715 * (x * x * x))))


def _dot(a, b):
    return jnp.dot(a, b, preferred_element_type=F32)


def _interleave(*gens):
    live = list(gens)
    while live:
        for g in list(live):
            try:
                next(g)
            except StopIteration:
                live.remove(g)


def _interleave_staggered(gens, lag, side=None, side_period=1):
    pending = list(gens)
    live = []
    tick = 0
    while live or pending or side is not None:
        if pending and tick % lag == 0:
            live.append(pending.pop(0))
        if side is not None and (tick % side_period == 0 or not (live or pending)):
            try:
                next(side)
            except StopIteration:
                side = None
        for g in list(live):
            try:
                next(g)
            except StopIteration:
                live.remove(g)
        tick += 1


def _dot_nt(a, b):
    return lax.dot_general(a, b, (((1,), (1,)), ((), ())), preferred_element_type=F32)


ADA_TN = 1536


def _ada_kernel(cond_ref, w_ref, b_ref, o_ref):
    cnd = cond_ref[...]
    s = (cnd * _sigmoid(cnd)).astype(BF16)
    o_ref[:, 0, :] = _dot(s, w_ref[...].astype(BF16)) + b_ref[...]


def _ada_call(cond8, ada_w, ada_b):
    n_out = 6 * D_MODEL
    return pl.pallas_call(
        _ada_kernel,
        grid=(DEPTH, n_out // ADA_TN),
        in_specs=[
            pl.BlockSpec((8, D_MODEL), lambda l, j: (0, 0)),
            pl.BlockSpec((None, D_MODEL, ADA_TN), lambda l, j: (l, 0, j)),
            pl.BlockSpec((None, 1, ADA_TN), lambda l, j: (l, 0, j)),
        ],
        out_specs=pl.BlockSpec((None, 8, 1, ADA_TN), lambda l, j: (l, 0, 0, j)),
        out_shape=jax.ShapeDtypeStruct((DEPTH, 8, 1, n_out), F32),
        compiler_params=pltpu.CompilerParams(
            dimension_semantics=("arbitrary", "arbitrary"),
            vmem_limit_bytes=VMEM_LIMIT_BYTES),
        name="ada_mod",
    )(cond8, ada_w, ada_b.reshape(DEPTH, 1, n_out))


def _mix_kernel(*refs, seq, n_seq, latent, fused_ffn):
    T = seq * n_seq
    n_ctx = 256 if latent else 0
    (x_ref, mod_ref, vec_ref, win_ref, wlast_ref, convw_ref, lcw_ref, wg_ref, lam_ref,
     wqt_ref, wk_ref, wvt_ref, qgt_ref, wout_ref) = refs[:14]
    pos = 14
    if latent:
        (wqpt_ref, qgpt_ref, ropect_ref, ropest_ref, ropec_ref, ropes_ref,
         cckv_ref, ckr_ref, h0_ref) = refs[pos:pos + 9]
        pos += 9

    def vec(name):
        off, width = VEC_SLOTS[name]
        return vec_ref[:, off:off + width]
    if fused_ffn:
        ffn_in_refs = refs[pos:pos + 6]
        pos += 6
    h_out_ref = refs[pos]
    pos += 1
    if not latent:
        ckv_out_ref, kr_out_ref, st_out_ref = refs[pos:pos + 3]
        pos += 3
    if fused_ffn:
        ffn_out_ref = refs[pos]
        pos += 1
    u_scr, qt_scr, k_scr, vt_scr, ymt_scr, ycat_scr = refs[pos:pos + 6]
    if latent:
        st_scr, m_scr = refs[pos + 6:pos + 8]

    n_chunks = T // ROW_CHUNK
    lane = lax.broadcasted_iota(jnp.int32, (1, LANES), 1)
    rope_lanes = (lane >= ROPE_LO) & (lane < ROPE_LO + QK_ROPE)

    sh1 = mod_ref[:, 0:D_MODEL]
    sc1 = mod_ref[:, D_MODEL:2 * D_MODEL]
    g1 = mod_ref[:, 2 * D_MODEL:3 * D_MODEL]
    gain1 = vec("norm1_g") * (1.0 + sc1)

    u_scr[0:PAD_ROWS, :] = jnp.zeros((PAD_ROWS, U_W), F32)
    u_scr[PAD_ROWS + T:2 * PAD_ROWS + T, :] = jnp.zeros((PAD_ROWS, U_W), F32)

    def phase_a(c, n=1):
        r0 = c * ROW_CHUNK
        nrows = n * ROW_CHUNK
        x = x_ref[r0:r0 + nrows, :]
        hn = (x * _rms_scale(x, D_MODEL) * gain1 + sh1).astype(BF16)
        yield
        urows = slice(PAD_ROWS + r0, PAD_ROWS + r0 + nrows)
        for s0 in range(0, C_CKV, PROJ_SLAB):
            u_scr[urows, s0:s0 + PROJ_SLAB] = _dot(hn, win_ref[:, s0:s0 + PROJ_SLAB])
            yield
        u_scr[urows, C_CKV:U_W] = _dot(hn, wlast_ref[...])
        yield
        u_scr[urows, C_CG:C_CG + CONV_W] = (
            u_scr[urows, C_CG:C_CG + CONV_W] * u_scr[urows, C_H:C_H + CONV_W])
        yield

    vrow = lax.broadcasted_iota(jnp.int32, (MLA_HEADS * V_EXT, 1), 0)
    ones_rows = (vrow % V_EXT) >= V_DIM

    def values_t(cb):
        return jnp.where(ones_rows, 1.0, _dot_nt(wvt_ref[...], cb)).astype(BF16)

    def phase_b(c):
        r0 = c * ROW_CHUNK
        rows = slice(PAD_ROWS + r0, PAD_ROWS + r0 + ROW_CHUNK)
        uq = u_scr[rows, C_Q:C_Q + Q_LORA]
        qn = (uq * _rms_scale(uq, Q_LORA) * vec("mla_qnorm_g")).astype(BF16)
        qa_t = _dot_nt(wqt_ref[...], qn)
        if latent:
            qpa_t = _dot_nt(wqpt_ref[...], qn)
            q_tc = ropect_ref[c] * qgt_ref[...]
            q_ts = ropest_ref[c] * qgpt_ref[...]
        yield
        for h in range(MLA_HEADS):
            sl = slice(h * HEAD_PAD, (h + 1) * HEAD_PAD)
            qh = qa_t[sl, :]
            rq = lax.rsqrt(jnp.sum(qh * qh, axis=0, keepdims=True) * (1.0 / QK_DIM) + EPS)
            if latent:
                qh = (qh * q_tc + qpa_t[sl, :] * q_ts) * rq
            else:
                qh = qh * qgt_ref[...] * rq
            qt_scr[h, c] = qh.astype(BF16)
            if h % 2:
                yield
        uc = u_scr[rows, C_CKV:C_CKV + KV_LORA]
        ckv = uc * _rms_scale(uc, KV_LORA) * vec("mla_kvnorm_g")
        krb = u_scr[rows, C_KR:C_KR + LANES]
        kr_rolled = pltpu.roll(krb, 64, 1)
        if not latent:
            ckv_out_ref[r0:r0 + ROW_CHUNK, :] = ckv
            kr_out_ref[r0:r0 + ROW_CHUNK, :] = kr_rolled[:, 0:QK_ROPE]
        krm = jnp.where(rope_lanes, krb, 0.0)
        cb = ckv.astype(BF16)
        ka = _dot(cb, wk_ref[...])
        vt_scr[c] = values_t(cb)
        if latent:
            k_tc = ropec_ref[r0:r0 + ROW_CHUNK, :] * vec("kg")
            k_ts = ropes_ref[r0:r0 + ROW_CHUNK, :] * vec("kg_partner")
        yield
        for h in range(MLA_HEADS):
            sl = slice(h * HEAD_PAD, (h + 1) * HEAD_PAD)
            kpre = ka[:, sl] + krm
            rk = _rms_scale(kpre, QK_DIM)
            if latent:
                kh = (kpre * k_tc + kr_rolled * k_ts) * rk
            else:
                kh = kpre * vec("kg") * rk
            k_scr[h, r0:r0 + ROW_CHUNK, :] = kh.astype(BF16)
            if h % 2:
                yield

    if latent:
        cc = cckv_ref[...].astype(BF16)
        ka = _dot(cc, wk_ref[...])
        vt_scr[n_chunks] = values_t(cc)
        krc = ckr_ref[...]
        for h in range(MLA_HEADS):
            kpre = ka[:, h * HEAD_PAD:(h + 1) * HEAD_PAD] + krc
            kh = kpre * vec("kg") * _rms_scale(kpre, QK_DIM)
            k_scr[h, seq:seq + n_ctx, :] = kh.astype(BF16)

    neg_lam = -lam_ref[...]
    sp = jnp.maximum(neg_lam, 0.0) + jnp.log(1.0 + jnp.exp(-jnp.abs(neg_lam)))
    log2a_coef = (-0.5 * LRU_C * math.log2(math.e)) * sp
    row_rc = lax.broadcasted_iota(jnp.int32, (ROW_CHUNK, 1), 0)

    def phase_c(ci):
        r0 = ci * ROW_CHUNK
        base = PAD_ROWS + r0
        first = (r0 % seq) == 0
        last = ((r0 + ROW_CHUNK) % seq) == 0

        def win(col, shift, width=CONV_W):
            w = u_scr[base + shift:base + shift + ROW_CHUNK, col:col + width]
            if shift < 0 and first:
                w = jnp.where(row_rc < -shift, 0.0, w)
            if shift > 0 and last:
                w = jnp.where(row_rc >= ROW_CHUNK - shift, 0.0, w)
            return w

        z_m, z_0, z_p = win(C_CG, -1), win(C_CG, 0), win(C_CG, 1)
        conv = (z_m * convw_ref[0:1, :] + z_0 * convw_ref[1:2, :]
                + z_p * convw_ref[2:3, :] + vec("conv_b"))
        yc = u_scr[base:base + ROW_CHUNK, C_BG:C_BG + CONV_W] * conv
        ycn = yc * _rms_scale(yc, CONV_W) * vec("gnorm_conv")
        ycat_scr[r0:r0 + ROW_CHUNK, 0:CONV_W] = ycn.astype(BF16)
        yield

        xc = (win(C_XB, -2) * lcw_ref[0:1, :] + win(C_XB, -1) * lcw_ref[1:2, :]
              + win(C_XB, 0) * lcw_ref[2:3, :] + win(C_XB, 1) * lcw_ref[3:4, :]
              + vec("lru_conv_b"))
        half_gates = _dot(xc.astype(BF16), wg_ref[...]) + vec("b_gates")
        half_xc = 0.5 * xc
        yield
        for d in range(2):
            t_r = jnp.tanh(half_gates[:, (2 * d) * LRU_W:(2 * d + 1) * LRU_W])
            t_i = jnp.tanh(half_gates[:, (2 * d + 1) * LRU_W:(2 * d + 2) * LRU_W])
            a = jnp.exp2(log2a_coef[d:d + 1, :] * (t_r + 1.0))
            v = 1.0 - a * a
            mult = v * lax.rsqrt(jnp.maximum(v, 1e-30))
            u_scr[base:base + ROW_CHUNK, SCAN_A[d]:SCAN_A[d] + LRU_W] = a
            u_scr[base:base + ROW_CHUNK, SCAN_B[d]:SCAN_B[d] + LRU_W] = (
                mult * (half_xc * (t_i + 1.0)))
            yield

    n_sc = seq // SCAN_CHUNK
    row_sc = lax.broadcasted_iota(jnp.int32, (SCAN_CHUNK, LRU_W), 0)
    steps = [1 << i for i in range(int(math.log2(SCAN_CHUNK)))]

    def scan_step(rf, rb, cf, cb_):
        rf = pl.multiple_of(rf + PAD_ROWS, SUBLANES)
        rb = pl.multiple_of(rb + PAD_ROWS, SUBLANES)
        a = u_scr[pl.ds(rf, SCAN_CHUNK), SCAN_A[0]:SCAN_A[0] + LRU_W]
        b = u_scr[pl.ds(rf, SCAN_CHUNK), SCAN_B[0]:SCAN_B[0] + LRU_W]
        b = b + jnp.where(row_sc == 0, a * cf, 0.0)
        for d in steps:
            if d % SUBLANES == 0:
                b = jnp.concatenate([b[:d], b[d:] + a[d:] * b[:-d]], axis=0)
                if d != steps[-1]:
                    a = jnp.concatenate([a[:d], a[d:] * a[:-d]], axis=0)
                continue
            b = b + a * jnp.where(row_sc >= d, pltpu.roll(b, d, 0), 0.0)
            if d != steps[-1]:
                a = a * jnp.where(row_sc >= d, pltpu.roll(a, d, 0), 1.0)
        u_scr[pl.ds(rf, SCAN_CHUNK), SCAN_B[0]:SCAN_B[0] + LRU_W] = b
        cf = b[SCAN_CHUNK - 1:SCAN_CHUNK, :]
        a = u_scr[pl.ds(rb, SCAN_CHUNK), SCAN_A[1]:SCAN_A[1] + LRU_W]
        b = u_scr[pl.ds(rb, SCAN_CHUNK), SCAN_B[1]:SCAN_B[1] + LRU_W]
        b = b + jnp.where(row_sc == SCAN_CHUNK - 1, a * cb_, 0.0)
        for d in steps:
            if d % SUBLANES == 0:
                b = jnp.concatenate([b[:-d] + a[:-d] * b[d:], b[-d:]], axis=0)
                if d != steps[-1]:
                    a = jnp.concatenate([a[:-d] * a[d:], a[-d:]], axis=0)
                continue
            keep = row_sc < SCAN_CHUNK - d
            b = b + a * jnp.where(keep, pltpu.roll(b, SCAN_CHUNK - d, 0), 0.0)
            if d != steps[-1]:
                a = a * jnp.where(keep, pltpu.roll(a, SCAN_CHUNK - d, 0), 1.0)
        u_scr[pl.ds(rb, SCAN_CHUNK), SCAN_B[1]:SCAN_B[1] + LRU_W] = b
        cb_ = b[0:1, :]
        return cf, cb_

    cps = seq // ROW_CHUNK

    def softmax_pv(st, vts):
        m = jnp.max(st, axis=0, keepdims=True)
        pt = jnp.exp2(st - m).astype(BF16)
        ot = None
        for i, vt in enumerate(vts):
            part = _dot(vt, pt[i * ROW_CHUNK:(i + 1) * ROW_CHUNK, :])
            ot = part if ot is None else ot + part
        return ot[0:V_DIM, :] * (1.0 / ot[V_DIM:V_DIM + 1, :])

    def f_build(c):
        r0 = c * ROW_CHUNK
        rows = slice(r0, r0 + ROW_CHUNK)
        urows = slice(PAD_ROWS + r0, PAD_ROWS + r0 + ROW_CHUNK)
        hf = u_scr[urows, SCAN_B[0]:SCAN_B[0] + LRU_W]
        hb = u_scr[urows, SCAN_B[1]:SCAN_B[1] + LRU_W]
        yb = u_scr[urows, C_YB:C_YB + LRU_W]
        yl = (hf + hb) * _gelu_tanh(yb)
        yln = yl * _rms_scale(yl, LRU_W) * vec("gnorm_lru")
        ycat_scr[rows, CONV_W:CONV_W + LRU_W] = yln.astype(BF16)
        yield
        ymt = ymt_scr[c]
        ssq = jnp.sum(ymt * ymt, axis=0, keepdims=True)
        ynt = ymt * lax.rsqrt(ssq * (1.0 / (MLA_HEADS * V_DIM)) + EPS)
        ycat_scr[rows, CONV_W + LRU_W:D_MODEL] = (ynt.T * vec("gnorm_mla")).astype(BF16)
        yield

    def f_project(c):
        rows = slice(c * ROW_CHUNK, (c + 1) * ROW_CHUNK)
        for n0 in range(0, D_MODEL, PROJ_SLAB):
            cols = slice(n0, n0 + PROJ_SLAB)
            y = _dot(ycat_scr[rows, :], wout_ref[:, cols])
            h_out_ref[rows, cols] = x_ref[rows, cols] + g1[:, cols] * y
            yield

    if latent:
        groups = [list(range(g0, g0 + PROJ_CHUNKS)) for g0 in range(0, n_chunks, PROJ_CHUNKS)]
        _interleave(phase_a(0, PROJ_CHUNKS))
        for gi, group in enumerate(groups):
            dependent = []
            for c in group:
                dependent.append(phase_b(c))
                if c:
                    dependent.append(phase_c(c - 1))
            ahead = [phase_a(groups[gi + 1][0], PROJ_CHUNKS)] if gi + 1 < len(groups) else []
            _interleave(*ahead, itertools.chain(*dependent))
        _interleave(phase_c(n_chunks - 1))

        key_chunks = list(range(cps)) + [n_chunks]
        n_blk = cps * (MLA_HEADS // 2)
        trips = n_blk // 2
        scans_per_trip = n_sc // trips

        def heads_of(idx):
            qc = idx // (MLA_HEADS // 2)
            j = idx % (MLA_HEADS // 2)
            return qc, [2 * j, 2 * j + 1]

        def scores(idx, slot):
            qc, hs = heads_of(idx)
            for par, h in enumerate(hs):
                qt = qt_scr[h, qc]
                m = None
                for i in range(len(key_chunks)):
                    rows = slice(i * ROW_CHUNK, (i + 1) * ROW_CHUNK)
                    st = _dot(k_scr[h, rows, :], qt)
                    st_scr[slot, par, rows, :] = st
                    mc = jnp.max(st, axis=0, keepdims=True)
                    m = mc if m is None else jnp.maximum(m, mc)
                    yield
                m_scr[slot, par] = m

        def finish(idx, slot):
            qc, hs = heads_of(idx)
            for par, h in enumerate(hs):
                hr = pl.ds(pl.multiple_of(h * V_DIM, V_DIM), V_DIM)
                vr = pl.ds(pl.multiple_of(h * V_EXT, V_EXT - V_DIM), V_EXT)
                m = m_scr[slot, par]
                ot = None
                for i, kc in enumerate(key_chunks):
                    rows = slice(i * ROW_CHUNK, (i + 1) * ROW_CHUNK)
                    pt = jnp.exp2(st_scr[slot, par, rows, :] - m).astype(BF16)
                    part = _dot(vt_scr[kc, vr, :], pt)
                    ot = part if ot is None else ot + part
                    yield
                ymt_scr[qc, hr, :] = ot[0:V_DIM, :] * (1.0 / ot[V_DIM:V_DIM + 1, :])

        _interleave(scores(0, 0))

        def att_body(i, carry):
            b0 = 2 * i
            state = {"carry": carry, "c": i * scans_per_trip}

            def scans(n):
                for _ in range(n):
                    c = state["c"]
                    rf = pl.multiple_of(c * SCAN_CHUNK, SCAN_CHUNK)
                    rb = pl.multiple_of((n_sc - 1 - c) * SCAN_CHUNK, SCAN_CHUNK)
                    state["carry"] = scan_step(rf, rb, *state["carry"])
                    state["c"] = c + 1
                    yield
                    yield
                    yield

            half = scans_per_trip // 2
            _interleave(scores(b0 + 1, 1), finish(b0, 0), scans(half))
            _interleave(scores(jnp.minimum(b0 + 2, n_blk - 1), 0), finish(b0 + 1, 1),
                        scans(scans_per_trip - half))
            return state["carry"]

        lax.fori_loop(0, trips, att_body, (h0_ref[0:1, :], h0_ref[1:2, :]))

        _interleave(f_build(0))
        for c in range(n_chunks):
            ahead = [f_build(c + 1)] if c + 1 < n_chunks else []
            _interleave(f_project(c), *ahead)
    else:
        def phase_d(s):
            carry = (jnp.zeros((1, LRU_W), F32), jnp.zeros((1, LRU_W), F32))
            for c in range(n_sc):
                carry = scan_step(s * seq + c * SCAN_CHUNK,
                                  s * seq + (n_sc - 1 - c) * SCAN_CHUNK, *carry)
                yield
            st_out_ref[s, 0:1, :] = carry[0]
            st_out_ref[s, 1:2, :] = carry[1]

        def phase_e(s):
            blocks = [(qi, h) for qi in range(cps) for h in range(MLA_HEADS)]

            def scores(blk):
                qi, h = blk
                return _dot(k_scr[h, s * seq:(s + 1) * seq, :], qt_scr[h, s * cps + qi])

            st = scores(blocks[0])
            for i, (qi, h) in enumerate(blocks):
                nxt = scores(blocks[i + 1]) if i + 1 < len(blocks) else None
                hr = slice(h * V_DIM, (h + 1) * V_DIM)
                vr = slice(h * V_EXT, (h + 1) * V_EXT)
                vts = [vt_scr[s * cps + j, vr, :] for j in range(cps)]
                ymt_scr[s * cps + qi, hr, :] = softmax_pv(st, vts)
                st = nxt
                yield

        def sequence(s):
            chunks = range(s * cps, (s + 1) * cps)
            for c in chunks:
                yield from phase_a(c)
            for c in chunks:
                yield from phase_b(c)
            for c in chunks:
                yield from phase_c(c)
            yield from phase_d(s)
            yield from phase_e(s)
            for c in chunks:
                yield from f_build(c)
                yield from f_project(c)

        side = _ffn_steps(*ffn_in_refs, ffn_out_ref) if fused_ffn else None
        _interleave_staggered([sequence(s) for s in range(n_seq)], SEQ_STAGGER,
                              side=side, side_period=FUSED_FFN_PERIOD)


def _const_spec(shape):
    nd = len(shape)
    return pl.BlockSpec(shape, lambda i, _n=nd: (0,) * _n,
                        pipeline_mode=pl.Buffered(1))


def _layer_spec(arr, layer):
    nd = arr.ndim - 1
    return pl.BlockSpec((None,) + arr.shape[1:], lambda i, _n=nd: (layer,) + (0,) * _n,
                        pipeline_mode=pl.Buffered(1))


def _mix_call(x2d, mods, layer, lw, *, seq, n_seq, latent, extra=None, ffn_h2d=None,
              ffn_rows_per_mod=None):
    n_tok = x2d.shape[0]
    T = seq * n_seq
    grid = (n_tok // T,)
    fused_ffn = ffn_h2d is not None
    n_keys_buf = T if not latent else seq + 256

    if latent:
        mod_map = lambda i: (layer, i, 0, 0)
    else:
        mod_map = lambda i: (layer, 4, 0, 0)

    consts = [lw["vecs"], lw["w_in"], lw["w_in_last"], lw["conv_w"], lw["lru_conv_w"],
              lw["w_gates"], lw["lru_lambda"], lw["wq_t"], lw["wk"], lw["wv_t"], lw["qg_t"],
              lw["w_out"]]
    args = [x2d, mods] + consts
    in_specs = [pl.BlockSpec((T, D_MODEL), lambda i: (i, 0)),
                pl.BlockSpec((None, None, 1, 6 * D_MODEL), mod_map)]
    in_specs += [_layer_spec(a, layer) for a in consts]
    if latent:
        lconsts = [lw["wq_partner_t"], lw["qg_partner_t"]]
        tables = [extra["rope_ct"], extra["rope_st"], extra["rope_c"], extra["rope_s"]]
        args += lconsts + tables
        in_specs += [_layer_spec(a, layer) for a in lconsts]
        in_specs += [_const_spec(a.shape) for a in tables]
        args += [extra["cache_ckv"], extra["cache_kr"], extra["state"]]
        in_specs += [
            pl.BlockSpec((None, None, 256, KV_LORA), lambda i: (i, layer, 0, 0)),
            pl.BlockSpec((None, None, 256, LANES), lambda i: (i, layer, 0, 0)),
            pl.BlockSpec((None, None, 2, LRU_W), lambda i: (i, layer, 0, 0)),
        ]

    if fused_ffn:
        ffn_rows = ffn_h2d.shape[0] // grid[0]
        tiles_per_mod = ffn_rows_per_mod // ffn_rows
        ffn_consts = [lw["norm2_g"], lw["w_gate"], lw["w_up"], lw["w_down"]]
        args += [ffn_h2d, mods] + ffn_consts
        in_specs += [pl.BlockSpec((ffn_rows, D_MODEL), lambda i: (i, 0)),
                     pl.BlockSpec((None, None, 1, 6 * D_MODEL),
                                  lambda i: (layer, i // tiles_per_mod, 0, 0))]
        in_specs += [_layer_spec(a, layer) for a in ffn_consts]

    out_shape = [jax.ShapeDtypeStruct((n_tok, D_MODEL), F32)]
    out_specs = [pl.BlockSpec((T, D_MODEL), lambda i: (i, 0))]
    if not latent:
        out_shape += [jax.ShapeDtypeStruct((n_tok, KV_LORA), F32),
                      jax.ShapeDtypeStruct((n_tok, QK_ROPE), F32),
                      jax.ShapeDtypeStruct((n_tok // seq, 2, LRU_W), F32)]
        out_specs += [pl.BlockSpec((T, KV_LORA), lambda i: (i, 0)),
                      pl.BlockSpec((T, QK_ROPE), lambda i: (i, 0)),
                      pl.BlockSpec((n_seq, 2, LRU_W), lambda i: (i, 0, 0))]
    if fused_ffn:
        out_shape.append(jax.ShapeDtypeStruct(ffn_h2d.shape, F32))
        out_specs.append(pl.BlockSpec((ffn_rows, D_MODEL), lambda i: (i, 0)))

    scratch = [
        pltpu.VMEM((T + 2 * PAD_ROWS, U_W), F32),
        pltpu.VMEM((MLA_HEADS, T // ROW_CHUNK, HEAD_PAD, ROW_CHUNK), BF16),
        pltpu.VMEM((MLA_HEADS, n_keys_buf, HEAD_PAD), BF16),
        pltpu.VMEM((n_keys_buf // ROW_CHUNK, MLA_HEADS * V_EXT, ROW_CHUNK), BF16),
        pltpu.VMEM((T // ROW_CHUNK, MLA_HEADS * V_DIM, ROW_CHUNK), F32),
        pltpu.VMEM((T, D_MODEL), BF16),
    ]
    if latent:
        scratch.append(pltpu.VMEM((2, 2, n_keys_buf, ROW_CHUNK), F32))
        scratch.append(pltpu.VMEM((2, 2, 1, ROW_CHUNK), F32))
    return pl.pallas_call(
        functools.partial(_mix_kernel, seq=seq, n_seq=n_seq, latent=latent,
                          fused_ffn=fused_ffn),
        grid=grid,
        in_specs=in_specs,
        out_specs=out_specs,
        out_shape=out_shape,
        scratch_shapes=scratch,
        compiler_params=pltpu.CompilerParams(
            dimension_semantics=("arbitrary",),
            vmem_limit_bytes=VMEM_LIMIT_BYTES),
        name="mix_latent" if latent else ("mix_context_swiglu" if fused_ffn else "mix_context"),
    )(*args)


def _ffn_steps(h_ref, mod_ref, n2g_ref, wg_ref, wu_ref, wd_ref, o_ref):
    sh2 = mod_ref[:, 3 * D_MODEL:4 * D_MODEL]
    sc2 = mod_ref[:, 4 * D_MODEL:5 * D_MODEL]
    g2 = mod_ref[:, 5 * D_MODEL:6 * D_MODEL]
    h = h_ref[...]
    hn = (h * _rms_scale(h, D_MODEL) * (n2g_ref[...] * (1.0 + sc2)) + sh2).astype(BF16)
    yield
    acc = None
    for c0 in range(0, FF, FF_CHUNK):
        g = _dot(hn, wg_ref[:, c0:c0 + FF_CHUNK])
        u = _dot(hn, wu_ref[:, c0:c0 + FF_CHUNK])
        act = (g * _sigmoid(g) * u).astype(BF16)
        part = _dot(act, wd_ref[c0:c0 + FF_CHUNK, :])
        acc = part if acc is None else acc + part
        yield
    o_ref[...] = h_ref[...] + g2 * acc
    yield


def _ffn_kernel(*refs):
    _interleave(_ffn_steps(*refs))


def _ffn_call(h2d, mods, layer, lw, *, rows_per_mod, mod_base):
    n_tok = h2d.shape[0]
    if mod_base is None:
        tiles_per_mod = rows_per_mod // FFN_ROWS
        mod_map = lambda i: (layer, i // tiles_per_mod, 0, 0)
    else:
        mod_map = lambda i: (layer, mod_base, 0, 0)
    consts = [lw["norm2_g"], lw["w_gate"], lw["w_up"], lw["w_down"]]
    return pl.pallas_call(
        _ffn_kernel,
        grid=(n_tok // FFN_ROWS,),
        in_specs=[pl.BlockSpec((FFN_ROWS, D_MODEL), lambda i: (i, 0)),
                  pl.BlockSpec((None, None, 1, 6 * D_MODEL), mod_map)]
        + [_layer_spec(a, layer) for a in consts],
        out_specs=pl.BlockSpec((FFN_ROWS, D_MODEL), lambda i: (i, 0)),
        out_shape=jax.ShapeDtypeStruct((n_tok, D_MODEL), F32),
        compiler_params=pltpu.CompilerParams(
            dimension_semantics=("arbitrary",),
            vmem_limit_bytes=VMEM_LIMIT_BYTES),
        name="swiglu",
    )(h2d, mods, *consts)


def _rope_partner_index():
    return [d + 8 if (d % 16) < 8 else d - 8 for d in range(QK_ROPE)]


def _placement_matrices():
    partner = _rope_partner_index()
    kv_w = QK_NOPE + V_DIM
    place_q = np.zeros((MLA_HEADS * HEAD_PAD, MLA_HEADS * QK_DIM), np.float32)
    place_qp = np.zeros_like(place_q)
    place_k = np.zeros((MLA_HEADS * kv_w, MLA_HEADS * HEAD_PAD), np.float32)
    place_v = np.zeros((MLA_HEADS * V_EXT, MLA_HEADS * kv_w), np.float32)
    for h in range(MLA_HEADS):
        for d in range(QK_DIM):
            place_q[h * HEAD_PAD + d, h * QK_DIM + d] = 1.0
        for d in range(QK_ROPE):
            place_qp[h * HEAD_PAD + ROPE_LO + d, h * QK_DIM + QK_NOPE + partner[d]] = 1.0
        for d in range(QK_NOPE):
            place_k[h * kv_w + d, h * HEAD_PAD + d] = 1.0
        for d in range(V_DIM):
            place_v[h * V_EXT + d, h * kv_w + QK_NOPE + d] = 1.0
    return place_q, place_qp, place_k, place_v


def _prep_weights(p):
    partner = jnp.array(_rope_partner_index(), jnp.int32)
    q_scale = QK_DIM ** -0.5 * math.log2(math.e)
    w_in = p["w_in"][:, :, :C_CKV].astype(BF16)
    w_tail = p["w_in"][:, :, C_CKV:].astype(BF16)
    w_kr = w_tail[:, :, KV_LORA:KV_LORA + QK_ROPE]
    zeros32 = jnp.zeros((DEPTH, D_MODEL, QK_ROPE), BF16)
    w_in_last = jnp.concatenate(
        [w_tail[:, :, :KV_LORA], w_kr[:, :, partner], zeros32, w_kr, zeros32], axis=2)

    place_q, place_qp, place_k, place_v = _placement_matrices()
    wq_t = jnp.einsum("rc,lkc->lrk", place_q, p["mla_wq_up"])
    wq_partner_t = jnp.einsum("rc,lkc->lrk", place_qp, p["mla_wq_up"])
    wk_pad = jnp.einsum("lkc,cn->lkn", p["mla_wkv_up"], place_k)
    wv_t = jnp.einsum("rc,lkc->lrk", place_v, p["mla_wkv_up"])

    def pad_gain(g):
        return jnp.pad(g, ((0, 0), (0, HEAD_PAD - QK_DIM)))[:, None, :]

    def partner_gain(g):
        gp = g[:, QK_NOPE:][:, partner]
        return jnp.pad(gp, ((0, 0), (ROPE_LO, HEAD_PAD - ROPE_LO - QK_ROPE)))[:, None, :]

    def on_sublanes(g):
        return jnp.broadcast_to(g.transpose(0, 2, 1), (DEPTH, HEAD_PAD, ROW_CHUNK))

    qg, kg = p["q_norm_g"], p["k_norm_g"]
    eye = jnp.eye(LRU_BLOCKS, dtype=F32)
    gate_w = jnp.stack([p["lru_wa"][:, 0], p["lru_wi"][:, 0],
                        p["lru_wa"][:, 1], p["lru_wi"][:, 1]], axis=1)
    w_gates = (gate_w[:, :, :, :, None, :] * eye[None, None, :, None, :, None])
    w_gates = w_gates.transpose(0, 2, 3, 1, 4, 5).reshape(DEPTH, LRU_W, 4 * LRU_W)
    b_gates = jnp.stack([p["lru_ba"][:, 0], p["lru_bi"][:, 0],
                         p["lru_ba"][:, 1], p["lru_bi"][:, 1]], axis=1)

    def row(v):
        return v[:, None, :]

    rows = {
        "norm1_g": p["norm1_g"], "conv_b": p["conv_b"], "lru_conv_b": p["lru_conv_b"],
        "b_gates": 0.5 * b_gates.reshape(DEPTH, 4 * LRU_W),
        "mla_qnorm_g": p["mla_qnorm_g"], "mla_kvnorm_g": p["mla_kvnorm_g"],
        "kg": pad_gain(kg)[:, 0, :], "kg_partner": partner_gain(kg)[:, 0, :],
        "gnorm_conv": p["gnorm_conv"], "gnorm_lru": p["gnorm_lru"],
        "gnorm_mla": p["gnorm_mla"],
    }
    for name, (_, width) in VEC_SLOTS.items():
        assert rows[name].shape == (DEPTH, width), name
    vecs = row(jnp.concatenate([rows[name] for name in VEC_SLOTS], axis=1))

    return {
        "vecs": vecs,
        "w_in": w_in,
        "w_in_last": w_in_last,
        "conv_w": p["conv_w"],
        "lru_conv_w": p["lru_conv_w"],
        "w_gates": (0.5 * w_gates).astype(BF16),
        "lru_lambda": p["lru_lambda"],
        "wq_t": wq_t.astype(BF16),
        "wq_partner_t": wq_partner_t.astype(BF16),
        "wk": wk_pad.astype(BF16),
        "wv_t": wv_t.astype(BF16),
        "qg_t": on_sublanes(pad_gain(qg) * q_scale),
        "qg_partner_t": on_sublanes(partner_gain(qg) * q_scale),
        "w_out": p["w_out"].astype(BF16),
        "norm2_g": row(p["norm2_g"]),
        "w_gate": p["w_gate"].astype(BF16),
        "w_up": p["w_up"].astype(BF16),
        "w_down": p["w_down"].astype(BF16),
    }


def _rope_tables(n_tokens):
    n_rows = n_tokens // GRID_W
    row = np.repeat(np.arange(n_rows, dtype=np.float32), GRID_W)
    col = np.tile(np.arange(GRID_W, dtype=np.float32), n_rows)
    n_freq = QK_ROPE // 4
    inv_freq = np.power(np.float32(ROPE_THETA),
                        -np.arange(n_freq, dtype=np.float32) / np.float32(n_freq))
    ang_r = (row[:, None] * inv_freq).astype(np.float32)
    ang_c = (col[:, None] * inv_freq).astype(np.float32)
    cos = np.concatenate([np.cos(ang_r), np.cos(ang_r), np.cos(ang_c), np.cos(ang_c)], 1)
    sin = np.concatenate([-np.sin(ang_r), np.sin(ang_r), -np.sin(ang_c), np.sin(ang_c)], 1)
    ones = np.ones((n_tokens, QK_NOPE), np.float32)
    tail = np.zeros((n_tokens, HEAD_PAD - QK_DIM), np.float32)
    rope_c = np.concatenate([ones, cos, tail], axis=1).astype(np.float32)
    rope_s = np.concatenate([np.zeros((n_tokens, QK_NOPE), np.float32), sin, tail],
                            axis=1).astype(np.float32)
    return rope_c, rope_s


def kernel(x_prompt, x_sample, cache_ckv, cache_krope, state_lru, c, c_ctx, norm1_g, ada_w, ada_b, w_in, conv_w, conv_b, lru_conv_w, lru_conv_b, lru_wa, lru_ba, lru_wi, lru_bi, lru_lambda, mla_qnorm_g, mla_wq_up, mla_kvnorm_g, mla_wkv_up, q_norm_g, k_norm_g, gnorm_conv, gnorm_lru, gnorm_mla, w_out, norm2_g, w_gate, w_up, w_down):
    params = dict(norm1_g=norm1_g, w_in=w_in, conv_w=conv_w, conv_b=conv_b,
                  lru_conv_w=lru_conv_w, lru_conv_b=lru_conv_b, lru_wa=lru_wa,
                  lru_ba=lru_ba, lru_wi=lru_wi, lru_bi=lru_bi, lru_lambda=lru_lambda,
                  mla_qnorm_g=mla_qnorm_g, mla_wq_up=mla_wq_up,
                  mla_kvnorm_g=mla_kvnorm_g, mla_wkv_up=mla_wkv_up, q_norm_g=q_norm_g,
                  k_norm_g=k_norm_g, gnorm_conv=gnorm_conv, gnorm_lru=gnorm_lru,
                  gnorm_mla=gnorm_mla, w_out=w_out, norm2_g=norm2_g, w_gate=w_gate,
                  w_up=w_up, w_down=w_down)
    batch, seq_p, _ = x_prompt.shape
    dec_batch, seq_s, _ = x_sample.shape
    assert dec_batch == 4 and cache_ckv.shape[2] == 256

    cond8 = jnp.concatenate(
        [c, c_ctx[None, :], jnp.zeros((8 - dec_batch - 1, D_MODEL), F32)], axis=0)
    mods = _ada_call(cond8, ada_w, ada_b)

    rope_c, rope_s = _rope_tables(seq_s)
    cache_kr_pad = jnp.pad(
        cache_krope, ((0, 0), (0, 0), (0, 0), (ROPE_LO, HEAD_PAD - ROPE_LO - QK_ROPE)))

    def chunked_t(tab):
        return jnp.asarray(np.ascontiguousarray(
            tab.reshape(seq_s // ROW_CHUNK, ROW_CHUNK, HEAD_PAD).transpose(0, 2, 1)))

    extra = dict(rope_c=jnp.asarray(rope_c), rope_s=jnp.asarray(rope_s),
                 rope_ct=chunked_t(rope_c), rope_st=chunked_t(rope_s), cache_ckv=cache_ckv,
                 cache_kr=cache_kr_pad, state=state_lru)

    xp = x_prompt.reshape(batch * seq_p, D_MODEL)
    xs = x_sample.reshape(dec_batch * seq_s, D_MODEL)
    ckv_l, kr_l, st_l = [], [], []
    lw = _prep_weights(params)
    for l in range(DEPTH):
        (hs,) = _mix_call(xs, mods, l, lw, seq=seq_s, n_seq=1, latent=True, extra=extra)
        hp, ckv, kr, st, xs = _mix_call(xp, mods, l, lw, seq=seq_p, n_seq=CTX_SEQS_PER_STEP,
                                        latent=False, ffn_h2d=hs, ffn_rows_per_mod=seq_s)
        xp = _ffn_call(hp, mods, l, lw, rows_per_mod=None, mod_base=4)
        ckv_l.append(ckv.reshape(batch, seq_p, KV_LORA))
        kr_l.append(kr.reshape(batch, seq_p, QK_ROPE))
        st_l.append(st)
    return (xp.reshape(batch, seq_p, D_MODEL),
            xs.reshape(dec_batch, seq_s, D_MODEL),
            jnp.stack(ckv_l, axis=1),
            jnp.stack(kr_l, axis=1),
            jnp.stack(st_l, axis=1))
```

```python
import functools
import itertools
import math

import jax
import jax.numpy as jnp
import numpy as np
from jax import lax
from jax.experimental import pallas as pl
from jax.experimental.pallas import tpu as pltpu

F32 = jnp.float32
BF16 = jnp.bfloat16

D_MODEL = 1024
DEPTH = 2
GRID_W = 64
CONV_W = 256
LRU_W = 256
LRU_BLOCKS = 4
LRU_BLK = 64
LRU_C = 8.0
MLA_HEADS = 8
QK_NOPE = 64
QK_ROPE = 32
V_DIM = 64
V_EXT = V_DIM + 16
QK_DIM = QK_NOPE + QK_ROPE
Q_LORA = 256
KV_LORA = 128
ROPE_THETA = 10000.0
FF = 2816
EPS = 1e-6

LANES = 128
SUBLANES = 8
HEAD_PAD = LANES
VMEM_LIMIT_BYTES = 60 * 1024 * 1024

C_BG, C_CG, C_H = 0, 256, 512
C_XB, C_YB = 768, 1024
C_Q = 1280
C_CKV = 1536
C_KR = 1664
U_W = 1792
SCAN_A = (C_H, C_CKV)
SCAN_B = (C_Q, C_BG)
ROPE_LO = QK_NOPE

def _slots(widths):
    out, off = {}, 0
    for name, width in widths:
        out[name] = (off, width)
        off += width
    return out


VEC_SLOTS = _slots([
    ("norm1_g", D_MODEL), ("conv_b", CONV_W), ("lru_conv_b", LRU_W), ("b_gates", 4 * LRU_W),
    ("mla_qnorm_g", Q_LORA), ("mla_kvnorm_g", KV_LORA), ("kg", HEAD_PAD),
    ("kg_partner", HEAD_PAD), ("gnorm_conv", CONV_W), ("gnorm_lru", LRU_W),
    ("gnorm_mla", MLA_HEADS * V_DIM)])

ROW_CHUNK = 256
SCAN_CHUNK = 32
Q_CHUNK = 256
FF_CHUNK = 256
PROJ_SLAB = 256
PROJ_CHUNKS = 1
CTX_SEQS_PER_STEP = 2
FUSED_FFN_PERIOD = 4
SEQ_STAGGER = 10
FFN_ROWS = 1024
PAD_ROWS = SUBLANES


def _rms_scale(x, n):
    ms = jnp.sum(x * x, axis=-1, keepdims=True) * (1.0 / n)
    return lax.rsqrt(ms + EPS)


def _sigmoid(x):
    return 0.5 * jnp.tanh(0.5 * x) + 0.5


def _gelu_tanh(x):
    c = math.sqrt(2.0 / math.pi)
    return 0.5 * x * (1.0 + jnp.tanh(c * (x + 0.044715 * (x * x * x))))


def _dot(a, b):
    return jnp.dot(a, b, preferred_element_type=F32)


def _interleave(*gens):
    live = list(gens)
    while live:
        for g in list(live):
            try:
                next(g)
            except StopIteration:
                live.remove(g)


def _interleave_staggered(gens, lag, side=None, side_period=1):
    pending = list(gens)
    live = []
    tick = 0
    while live or pending or side is not None:
        if pending and tick % lag == 0:
            live.append(pending.pop(0))
        if side is not None and (tick % side_period == 0 or not (live or pending)):
            try:
                next(side)
            except StopIteration:
                side = None
        for g in list(live):
            try:
                next(g)
            except StopIteration:
                live.remove(g)
        tick += 1


def _dot_nt(a, b):
    return lax.dot_general(a, b, (((1,), (1,)), ((), ())), preferred_element_type=F32)


ADA_TN = 1536


def _ada_kernel(cond_ref, w_ref, b_ref, o_ref):
    cnd = cond_ref[...]
    s = (cnd * _sigmoid(cnd)).astype(BF16)
    o_ref[:, 0, :] = _dot(s, w_ref[...].astype(BF16)) + b_ref[...]


def _ada_call(cond8, ada_w, ada_b):
    n_out = 6 * D_MODEL
    return pl.pallas_call(
        _ada_kernel,
        grid=(DEPTH, n_out // ADA_TN),
        in_specs=[
            pl.BlockSpec((8, D_MODEL), lambda l, j: (0, 0)),
            pl.BlockSpec((None, D_MODEL, ADA_TN), lambda l, j: (l, 0, j)),
            pl.BlockSpec((None, 1, ADA_TN), lambda l, j: (l, 0, j)),
        ],
        out_specs=pl.BlockSpec((None, 8, 1, ADA_TN), lambda l, j: (l, 0, 0, j)),
        out_shape=jax.ShapeDtypeStruct((DEPTH, 8, 1, n_out), F32),
        compiler_params=pltpu.CompilerParams(
            dimension_semantics=("arbitrary", "arbitrary"),
            vmem_limit_bytes=VMEM_LIMIT_BYTES),
        name="ada_mod",
    )(cond8, ada_w, ada_b.reshape(DEPTH, 1, n_out))


def _mix_kernel(*refs, seq, n_seq, latent, fused_ffn, n_aliased):
    T = seq * n_seq
    n_ctx = 256 if latent else 0
    (x_ref, mod_ref, vec_ref, win_ref, wlast_ref, convw_ref, lcw_ref, wg_ref, lam_ref,
     wqt_ref, wk_ref, wvt_ref, qgt_ref, wout_ref) = refs[:14]
    pos = 14
    if latent:
        (wqpt_ref, qgpt_ref, ropect_ref, ropest_ref, ropec_ref, ropes_ref,
         cckv_ref, ckr_ref, h0_ref) = refs[pos:pos + 9]
        pos += 9

    def vec(name):
        off, width = VEC_SLOTS[name]
        return vec_ref[:, off:off + width]
    if fused_ffn:
        ffn_in_refs = refs[pos:pos + 6]
        pos += 6
    pos += n_aliased
    h_out_ref = refs[pos]
    pos += 1
    if not latent:
        ckv_out_ref, kr_out_ref, st_out_ref = refs[pos:pos + 3]
        pos += 3
    if fused_ffn:
        ffn_out_ref = refs[pos]
        pos += 1
    u_scr, qt_scr, k_scr, vt_scr, ymt_scr, ycat_scr = refs[pos:pos + 6]
    if latent:
        st_scr, m_scr = refs[pos + 6:pos + 8]

    n_chunks = T // ROW_CHUNK
    lane = lax.broadcasted_iota(jnp.int32, (1, LANES), 1)
    rope_lanes = (lane >= ROPE_LO) & (lane < ROPE_LO + QK_ROPE)

    sh1 = mod_ref[:, 0:D_MODEL]
    sc1 = mod_ref[:, D_MODEL:2 * D_MODEL]
    g1 = mod_ref[:, 2 * D_MODEL:3 * D_MODEL]
    gain1 = vec("norm1_g") * (1.0 + sc1)

    u_scr[0:PAD_ROWS, :] = jnp.zeros((PAD_ROWS, U_W), F32)
    u_scr[PAD_ROWS + T:2 * PAD_ROWS + T, :] = jnp.zeros((PAD_ROWS, U_W), F32)

    def phase_a(c, n=1):
        r0 = c * ROW_CHUNK
        nrows = n * ROW_CHUNK
        x = x_ref[r0:r0 + nrows, :]
        hn = (x * _rms_scale(x, D_MODEL) * gain1 + sh1).astype(BF16)
        yield
        urows = slice(PAD_ROWS + r0, PAD_ROWS + r0 + nrows)
        for s0 in range(0, C_CKV, PROJ_SLAB):
            u_scr[urows, s0:s0 + PROJ_SLAB] = _dot(hn, win_ref[:, s0:s0 + PROJ_SLAB])
            yield
        u_scr[urows, C_CKV:U_W] = _dot(hn, wlast_ref[...])
        yield
        u_scr[urows, C_CG:C_CG + CONV_W] = (
            u_scr[urows, C_CG:C_CG + CONV_W] * u_scr[urows, C_H:C_H + CONV_W])
        yield

    vrow = lax.broadcasted_iota(jnp.int32, (MLA_HEADS * V_EXT, 1), 0)
    ones_rows = (vrow % V_EXT) >= V_DIM

    def values_t(cb):
        return jnp.where(ones_rows, 1.0, _dot_nt(wvt_ref[...], cb)).astype(BF16)

    def phase_b(c):
        r0 = c * ROW_CHUNK
        rows = slice(PAD_ROWS + r0, PAD_ROWS + r0 + ROW_CHUNK)
        uq = u_scr[rows, C_Q:C_Q + Q_LORA]
        qn = (uq * _rms_scale(uq, Q_LORA) * vec("mla_qnorm_g")).astype(BF16)
        qa_t = _dot_nt(wqt_ref[...], qn)
        if latent:
            qpa_t = _dot_nt(wqpt_ref[...], qn)
            q_tc = ropect_ref[c] * qgt_ref[...]
            q_ts = ropest_ref[c] * qgpt_ref[...]
        yield
        for h in range(MLA_HEADS):
            sl = slice(h * HEAD_PAD, (h + 1) * HEAD_PAD)
            qh = qa_t[sl, :]
            rq = lax.rsqrt(jnp.sum(qh * qh, axis=0, keepdims=True) * (1.0 / QK_DIM) + EPS)
            if latent:
                qh = (qh * q_tc + qpa_t[sl, :] * q_ts) * rq
            else:
                qh = qh * qgt_ref[...] * rq
            qt_scr[h, c] = qh.astype(BF16)
            if h % 2:
                yield
        uc = u_scr[rows, C_CKV:C_CKV + KV_LORA]
        ckv = uc * _rms_scale(uc, KV_LORA) * vec("mla_kvnorm_g")
        krb = u_scr[rows, C_KR:C_KR + LANES]
        kr_rolled = pltpu.roll(krb, 64, 1)
        if not latent:
            sq, rq0 = divmod(r0, seq)
            ckv_out_ref[sq, rq0:rq0 + ROW_CHUNK, :] = ckv
            kr_out_ref[sq, rq0:rq0 + ROW_CHUNK, :] = kr_rolled[:, 0:QK_ROPE]
        krm = jnp.where(rope_lanes, krb, 0.0)
        cb = ckv.astype(BF16)
        ka = _dot(cb, wk_ref[...])
        vt_scr[c] = values_t(cb)
        if latent:
            k_tc = ropec_ref[r0:r0 + ROW_CHUNK, :] * vec("kg")
            k_ts = ropes_ref[r0:r0 + ROW_CHUNK, :] * vec("kg_partner")
        yield
        for h in range(MLA_HEADS):
            sl = slice(h * HEAD_PAD, (h + 1) * HEAD_PAD)
            kpre = ka[:, sl] + krm
            rk = _rms_scale(kpre, QK_DIM)
            if latent:
                kh = (kpre * k_tc + kr_rolled * k_ts) * rk
            else:
                kh = kpre * vec("kg") * rk
            k_scr[h, r0:r0 + ROW_CHUNK, :] = kh.astype(BF16)
            if h % 2:
                yield

    if latent:
        cc = cckv_ref[...].astype(BF16)
        ka = _dot(cc, wk_ref[...])
        vt_scr[n_chunks] = values_t(cc)
        krc = ckr_ref[...]
        for h in range(MLA_HEADS):
            kpre = ka[:, h * HEAD_PAD:(h + 1) * HEAD_PAD] + krc
            kh = kpre * vec("kg") * _rms_scale(kpre, QK_DIM)
            k_scr[h, seq:seq + n_ctx, :] = kh.astype(BF16)

    neg_lam = -lam_ref[...]
    sp = jnp.maximum(neg_lam, 0.0) + jnp.log(1.0 + jnp.exp(-jnp.abs(neg_lam)))
    log2a_coef = (-0.5 * LRU_C * math.log2(math.e)) * sp
    row_rc = lax.broadcasted_iota(jnp.int32, (ROW_CHUNK, 1), 0)

    def phase_c(ci):
        r0 = ci * ROW_CHUNK
        base = PAD_ROWS + r0
        first = (r0 % seq) == 0
        last = ((r0 + ROW_CHUNK) % seq) == 0

        def win(col, shift, width=CONV_W):
            w = u_scr[base + shift:base + shift + ROW_CHUNK, col:col + width]
            if shift < 0 and first:
                w = jnp.where(row_rc < -shift, 0.0, w)
            if shift > 0 and last:
                w = jnp.where(row_rc >= ROW_CHUNK - shift, 0.0, w)
            return w

        z_m, z_0, z_p = win(C_CG, -1), win(C_CG, 0), win(C_CG, 1)
        conv = (z_m * convw_ref[0:1, :] + z_0 * convw_ref[1:2, :]
                + z_p * convw_ref[2:3, :] + vec("conv_b"))
        yc = u_scr[base:base + ROW_CHUNK, C_BG:C_BG + CONV_W] * conv
        ycn = yc * _rms_scale(yc, CONV_W) * vec("gnorm_conv")
        ycat_scr[r0:r0 + ROW_CHUNK, 0:CONV_W] = ycn.astype(BF16)
        yield

        xc = (win(C_XB, -2) * lcw_ref[0:1, :] + win(C_XB, -1) * lcw_ref[1:2, :]
              + win(C_XB, 0) * lcw_ref[2:3, :] + win(C_XB, 1) * lcw_ref[3:4, :]
              + vec("lru_conv_b"))
        half_gates = _dot(xc.astype(BF16), wg_ref[...]) + vec("b_gates")
        half_xc = 0.5 * xc
        yield
        for d in range(2):
            t_r = jnp.tanh(half_gates[:, (2 * d) * LRU_W:(2 * d + 1) * LRU_W])
            t_i = jnp.tanh(half_gates[:, (2 * d + 1) * LRU_W:(2 * d + 2) * LRU_W])
            a = jnp.exp2(log2a_coef[d:d + 1, :] * (t_r + 1.0))
            v = 1.0 - a * a
            mult = v * lax.rsqrt(jnp.maximum(v, 1e-30))
            u_scr[base:base + ROW_CHUNK, SCAN_A[d]:SCAN_A[d] + LRU_W] = a
            u_scr[base:base + ROW_CHUNK, SCAN_B[d]:SCAN_B[d] + LRU_W] = (
                mult * (half_xc * (t_i + 1.0)))
            yield

    n_sc = seq // SCAN_CHUNK
    row_sc = lax.broadcasted_iota(jnp.int32, (SCAN_CHUNK, LRU_W), 0)
    steps = [1 << i for i in range(int(math.log2(SCAN_CHUNK)))]

    def scan_step(rf, rb, cf, cb_):
        rf = pl.multiple_of(rf + PAD_ROWS, SUBLANES)
        rb = pl.multiple_of(rb + PAD_ROWS, SUBLANES)
        a = u_scr[pl.ds(rf, SCAN_CHUNK), SCAN_A[0]:SCAN_A[0] + LRU_W]
        b = u_scr[pl.ds(rf, SCAN_CHUNK), SCAN_B[0]:SCAN_B[0] + LRU_W]
        b = b + jnp.where(row_sc == 0, a * cf, 0.0)
        for d in steps:
            if d % SUBLANES == 0:
                b = jnp.concatenate([b[:d], b[d:] + a[d:] * b[:-d]], axis=0)
                if d != steps[-1]:
                    a = jnp.concatenate([a[:d], a[d:] * a[:-d]], axis=0)
                continue
            b = b + a * jnp.where(row_sc >= d, pltpu.roll(b, d, 0), 0.0)
            if d != steps[-1]:
                a = a * jnp.where(row_sc >= d, pltpu.roll(a, d, 0), 1.0)
        u_scr[pl.ds(rf, SCAN_CHUNK), SCAN_B[0]:SCAN_B[0] + LRU_W] = b
        cf = b[SCAN_CHUNK - 1:SCAN_CHUNK, :]
        a = u_scr[pl.ds(rb, SCAN_CHUNK), SCAN_A[1]:SCAN_A[1] + LRU_W]
        b = u_scr[pl.ds(rb, SCAN_CHUNK), SCAN_B[1]:SCAN_B[1] + LRU_W]
        b = b + jnp.where(row_sc == SCAN_CHUNK - 1, a * cb_, 0.0)
        for d in steps:
            if d % SUBLANES == 0:
                b = jnp.concatenate([b[:-d] + a[:-d] * b[d:], b[-d:]], axis=0)
                if d != steps[-1]:
                    a = jnp.concatenate([a[:-d] * a[d:], a[-d:]], axis=0)
                continue
            keep = row_sc < SCAN_CHUNK - d
            b = b + a * jnp.where(keep, pltpu.roll(b, SCAN_CHUNK - d, 0), 0.0)
            if d != steps[-1]:
                a = a * jnp.where(keep, pltpu.roll(a, SCAN_CHUNK - d, 0), 1.0)
        u_scr[pl.ds(rb, SCAN_CHUNK), SCAN_B[1]:SCAN_B[1] + LRU_W] = b
        cb_ = b[0:1, :]
        return cf, cb_

    cps = seq // ROW_CHUNK

    def softmax_pv(st, vts):
        m = jnp.max(st, axis=0, keepdims=True)
        pt = jnp.exp2(st - m).astype(BF16)
        ot = None
        for i, vt in enumerate(vts):
            part = _dot(vt, pt[i * ROW_CHUNK:(i + 1) * ROW_CHUNK, :])
            ot = part if ot is None else ot + part
        return ot[0:V_DIM, :] * (1.0 / ot[V_DIM:V_DIM + 1, :])

    def f_build(c):
        r0 = c * ROW_CHUNK
        rows = slice(r0, r0 + ROW_CHUNK)
        urows = slice(PAD_ROWS + r0, PAD_ROWS + r0 + ROW_CHUNK)
        hf = u_scr[urows, SCAN_B[0]:SCAN_B[0] + LRU_W]
        hb = u_scr[urows, SCAN_B[1]:SCAN_B[1] + LRU_W]
        yb = u_scr[urows, C_YB:C_YB + LRU_W]
        yl = (hf + hb) * _gelu_tanh(yb)
        yln = yl * _rms_scale(yl, LRU_W) * vec("gnorm_lru")
        ycat_scr[rows, CONV_W:CONV_W + LRU_W] = yln.astype(BF16)
        yield
        ymt = ymt_scr[c]
        ssq = jnp.sum(ymt * ymt, axis=0, keepdims=True)
        ynt = ymt * lax.rsqrt(ssq * (1.0 / (MLA_HEADS * V_DIM)) + EPS)
        ycat_scr[rows, CONV_W + LRU_W:D_MODEL] = (ynt.T * vec("gnorm_mla")).astype(BF16)
        yield

    def f_project(c):
        rows = slice(c * ROW_CHUNK, (c + 1) * ROW_CHUNK)
        for n0 in range(0, D_MODEL, PROJ_SLAB):
            cols = slice(n0, n0 + PROJ_SLAB)
            y = _dot(ycat_scr[rows, :], wout_ref[:, cols])
            h_out_ref[rows, cols] = x_ref[rows, cols] + g1[:, cols] * y
            yield

    if latent:
        groups = [list(range(g0, g0 + PROJ_CHUNKS)) for g0 in range(0, n_chunks, PROJ_CHUNKS)]
        _interleave(phase_a(0, PROJ_CHUNKS))
        for gi, group in enumerate(groups):
            dependent = []
            for c in group:
                dependent.append(phase_b(c))
                if c:
                    dependent.append(phase_c(c - 1))
            ahead = [phase_a(groups[gi + 1][0], PROJ_CHUNKS)] if gi + 1 < len(groups) else []
            _interleave(*ahead, itertools.chain(*dependent))
        _interleave(phase_c(n_chunks - 1))

        key_chunks = list(range(cps)) + [n_chunks]
        n_blk = cps * (MLA_HEADS // 2)
        trips = n_blk // 2
        scans_per_trip = n_sc // trips

        def heads_of(idx):
            qc = idx // (MLA_HEADS // 2)
            j = idx % (MLA_HEADS // 2)
            return qc, [2 * j, 2 * j + 1]

        def scores(idx, slot):
            qc, hs = heads_of(idx)
            for par, h in enumerate(hs):
                qt = qt_scr[h, qc]
                m = None
                for i in range(len(key_chunks)):
                    rows = slice(i * ROW_CHUNK, (i + 1) * ROW_CHUNK)
                    st = _dot(k_scr[h, rows, :], qt)
                    st_scr[slot, par, rows, :] = st
                    mc = jnp.max(st, axis=0, keepdims=True)
                    m = mc if m is None else jnp.maximum(m, mc)
                    yield
                m_scr[slot, par] = m

        def finish(idx, slot):
            qc, hs = heads_of(idx)
            for par, h in enumerate(hs):
                hr = pl.ds(pl.multiple_of(h * V_DIM, V_DIM), V_DIM)
                vr = pl.ds(pl.multiple_of(h * V_EXT, V_EXT - V_DIM), V_EXT)
                m = m_scr[slot, par]
                ot = None
                for i, kc in enumerate(key_chunks):
                    rows = slice(i * ROW_CHUNK, (i + 1) * ROW_CHUNK)
                    pt = jnp.exp2(st_scr[slot, par, rows, :] - m).astype(BF16)
                    part = _dot(vt_scr[kc, vr, :], pt)
                    ot = part if ot is None else ot + part
                    yield
                ymt_scr[qc, hr, :] = ot[0:V_DIM, :] * (1.0 / ot[V_DIM:V_DIM + 1, :])

        _interleave(scores(0, 0))

        def att_body(i, carry):
            b0 = 2 * i
            state = {"carry": carry, "c": i * scans_per_trip}

            def scans(n):
                for _ in range(n):
                    c = state["c"]
                    rf = pl.multiple_of(c * SCAN_CHUNK, SCAN_CHUNK)
                    rb = pl.multiple_of((n_sc - 1 - c) * SCAN_CHUNK, SCAN_CHUNK)
                    state["carry"] = scan_step(rf, rb, *state["carry"])
                    state["c"] = c + 1
                    yield
                    yield
                    yield

            half = scans_per_trip // 2
            _interleave(scores(b0 + 1, 1), finish(b0, 0), scans(half))
            _interleave(scores(jnp.minimum(b0 + 2, n_blk - 1), 0), finish(b0 + 1, 1),
                        scans(scans_per_trip - half))
            return state["carry"]

        lax.fori_loop(0, trips, att_body, (h0_ref[0:1, :], h0_ref[1:2, :]))

        _interleave(f_build(0))
        for c in range(n_chunks):
            ahead = [f_build(c + 1)] if c + 1 < n_chunks else []
            _interleave(f_project(c), *ahead)
    else:
        def phase_d(s):
            carry = (jnp.zeros((1, LRU_W), F32), jnp.zeros((1, LRU_W), F32))
            for c in range(n_sc):
                carry = scan_step(s * seq + c * SCAN_CHUNK,
                                  s * seq + (n_sc - 1 - c) * SCAN_CHUNK, *carry)
                yield
            st_out_ref[s, 0:1, :] = carry[0]
            st_out_ref[s, 1:2, :] = carry[1]

        def phase_e(s):
            blocks = [(qi, h) for qi in range(cps) for h in range(MLA_HEADS)]

            def scores(blk):
                qi, h = blk
                return _dot(k_scr[h, s * seq:(s + 1) * seq, :], qt_scr[h, s * cps + qi])

            st = scores(blocks[0])
            for i, (qi, h) in enumerate(blocks):
                nxt = scores(blocks[i + 1]) if i + 1 < len(blocks) else None
                hr = slice(h * V_DIM, (h + 1) * V_DIM)
                vr = slice(h * V_EXT, (h + 1) * V_EXT)
                vts = [vt_scr[s * cps + j, vr, :] for j in range(cps)]
                ymt_scr[s * cps + qi, hr, :] = softmax_pv(st, vts)
                st = nxt
                yield

        def sequence(s):
            chunks = range(s * cps, (s + 1) * cps)
            for c in chunks:
                yield from phase_a(c)
            for c in chunks:
                yield from phase_b(c)
            for c in chunks:
                yield from phase_c(c)
            yield from phase_d(s)
            yield from phase_e(s)
            for c in chunks:
                yield from f_build(c)
                yield from f_project(c)

        side = _ffn_steps(*ffn_in_refs, ffn_out_ref) if fused_ffn else None
        _interleave_staggered([sequence(s) for s in range(n_seq)], SEQ_STAGGER,
                              side=side, side_period=FUSED_FFN_PERIOD)


def _const_spec(shape):
    nd = len(shape)
    return pl.BlockSpec(shape, lambda i, _n=nd: (0,) * _n,
                        pipeline_mode=pl.Buffered(1))


def _layer_spec(arr, layer):
    nd = arr.ndim - 1
    return pl.BlockSpec((None,) + arr.shape[1:], lambda i, _n=nd: (layer,) + (0,) * _n,
                        pipeline_mode=pl.Buffered(1))


def _mix_call(x2d, mods, layer, lw, *, seq, n_seq, latent, extra=None, ffn_h2d=None,
              ffn_rows_per_mod=None, caches=None):
    n_tok = x2d.shape[0]
    T = seq * n_seq
    grid = (n_tok // T,)
    fused_ffn = ffn_h2d is not None
    n_keys_buf = T if not latent else seq + 256

    if latent:
        mod_map = lambda i: (layer, i, 0, 0)
    else:
        mod_map = lambda i: (layer, 4, 0, 0)

    consts = [lw["vecs"], lw["w_in"], lw["w_in_last"], lw["conv_w"], lw["lru_conv_w"],
              lw["w_gates"], lw["lru_lambda"], lw["wq_t"], lw["wk"], lw["wv_t"], lw["qg_t"],
              lw["w_out"]]
    args = [x2d, mods] + consts
    in_specs = [pl.BlockSpec((T, D_MODEL), lambda i: (i, 0)),
                pl.BlockSpec((None, None, 1, 6 * D_MODEL), mod_map)]
    in_specs += [_layer_spec(a, layer) for a in consts]
    if latent:
        lconsts = [lw["wq_partner_t"], lw["qg_partner_t"]]
        tables = [extra["rope_ct"], extra["rope_st"], extra["rope_c"], extra["rope_s"]]
        args += lconsts + tables
        in_specs += [_layer_spec(a, layer) for a in lconsts]
        in_specs += [_const_spec(a.shape) for a in tables]
        args += [extra["cache_ckv"], extra["cache_kr"], extra["state"]]
        in_specs += [
            pl.BlockSpec((None, None, 256, KV_LORA), lambda i: (i, layer, 0, 0)),
            pl.BlockSpec((None, None, 256, LANES), lambda i: (i, layer, 0, 0)),
            pl.BlockSpec((None, None, 2, LRU_W), lambda i: (i, layer, 0, 0)),
        ]

    if fused_ffn:
        ffn_rows = ffn_h2d.shape[0] // grid[0]
        tiles_per_mod = ffn_rows_per_mod // ffn_rows
        ffn_consts = [lw["norm2_g"], lw["w_gate"], lw["w_up"], lw["w_down"]]
        args += [ffn_h2d, mods] + ffn_consts
        in_specs += [pl.BlockSpec((ffn_rows, D_MODEL), lambda i: (i, 0)),
                     pl.BlockSpec((None, None, 1, 6 * D_MODEL),
                                  lambda i: (layer, i // tiles_per_mod, 0, 0))]
        in_specs += [_layer_spec(a, layer) for a in ffn_consts]

    out_shape = [jax.ShapeDtypeStruct((n_tok, D_MODEL), F32)]
    out_specs = [pl.BlockSpec((T, D_MODEL), lambda i: (i, 0))]
    if not latent:
        n_b = n_tok // seq
        out_shape += [jax.ShapeDtypeStruct((n_b, DEPTH, seq, KV_LORA), F32),
                      jax.ShapeDtypeStruct((n_b, DEPTH, seq, QK_ROPE), F32),
                      jax.ShapeDtypeStruct((n_b, DEPTH, 2, LRU_W), F32)]
        out_specs += [pl.BlockSpec((n_seq, None, seq, KV_LORA), lambda i: (i, layer, 0, 0)),
                      pl.BlockSpec((n_seq, None, seq, QK_ROPE), lambda i: (i, layer, 0, 0)),
                      pl.BlockSpec((n_seq, None, 2, LRU_W), lambda i: (i, layer, 0, 0))]
    aliases = {}
    if caches is not None:
        for k, arr in enumerate(caches):
            aliases[len(args)] = 1 + k
            args.append(arr)
            in_specs.append(pl.BlockSpec(memory_space=pl.ANY))
    if fused_ffn:
        out_shape.append(jax.ShapeDtypeStruct(ffn_h2d.shape, F32))
        out_specs.append(pl.BlockSpec((ffn_rows, D_MODEL), lambda i: (i, 0)))

    scratch = [
        pltpu.VMEM((T + 2 * PAD_ROWS, U_W), F32),
        pltpu.VMEM((MLA_HEADS, T // ROW_CHUNK, HEAD_PAD, ROW_CHUNK), BF16),
        pltpu.VMEM((MLA_HEADS, n_keys_buf, HEAD_PAD), BF16),
        pltpu.VMEM((n_keys_buf // ROW_CHUNK, MLA_HEADS * V_EXT, ROW_CHUNK), BF16),
        pltpu.VMEM((T // ROW_CHUNK, MLA_HEADS * V_DIM, ROW_CHUNK), F32),
        pltpu.VMEM((T, D_MODEL), BF16),
    ]
    if latent:
        scratch.append(pltpu.VMEM((2, 2, n_keys_buf, ROW_CHUNK), F32))
        scratch.append(pltpu.VMEM((2, 2, 1, ROW_CHUNK), F32))
    return pl.pallas_call(
        functools.partial(_mix_kernel, seq=seq, n_seq=n_seq, latent=latent,
                          fused_ffn=fused_ffn, n_aliased=len(aliases)),
        grid=grid,
        in_specs=in_specs,
        out_specs=out_specs,
        out_shape=out_shape,
        scratch_shapes=scratch,
        input_output_aliases=aliases,
        compiler_params=pltpu.CompilerParams(
            dimension_semantics=("arbitrary",),
            vmem_limit_bytes=VMEM_LIMIT_BYTES),
        name="mix_latent" if latent else ("mix_context_swiglu" if fused_ffn else "mix_context"),
    )(*args)


def _ffn_steps(h_ref, mod_ref, n2g_ref, wg_ref, wu_ref, wd_ref, o_ref):
    sh2 = mod_ref[:, 3 * D_MODEL:4 * D_MODEL]
    sc2 = mod_ref[:, 4 * D_MODEL:5 * D_MODEL]
    g2 = mod_ref[:, 5 * D_MODEL:6 * D_MODEL]
    h = h_ref[...]
    hn = (h * _rms_scale(h, D_MODEL) * (n2g_ref[...] * (1.0 + sc2)) + sh2).astype(BF16)
    yield
    acc = None
    for c0 in range(0, FF, FF_CHUNK):
        g = _dot(hn, wg_ref[:, c0:c0 + FF_CHUNK])
        u = _dot(hn, wu_ref[:, c0:c0 + FF_CHUNK])
        act = (g * _sigmoid(g) * u).astype(BF16)
        part = _dot(act, wd_ref[c0:c0 + FF_CHUNK, :])
        acc = part if acc is None else acc + part
        yield
    o_ref[...] = h_ref[...] + g2 * acc
    yield


def _ffn_kernel(*refs):
    _interleave(_ffn_steps(*refs))


def _ffn_call(h2d, mods, layer, lw, *, rows_per_mod, mod_base):
    n_tok = h2d.shape[0]
    if mod_base is None:
        tiles_per_mod = rows_per_mod // FFN_ROWS
        mod_map = lambda i: (layer, i // tiles_per_mod, 0, 0)
    else:
        mod_map = lambda i: (layer, mod_base, 0, 0)
    consts = [lw["norm2_g"], lw["w_gate"], lw["w_up"], lw["w_down"]]
    return pl.pallas_call(
        _ffn_kernel,
        grid=(n_tok // FFN_ROWS,),
        in_specs=[pl.BlockSpec((FFN_ROWS, D_MODEL), lambda i: (i, 0)),
                  pl.BlockSpec((None, None, 1, 6 * D_MODEL), mod_map)]
        + [_layer_spec(a, layer) for a in consts],
        out_specs=pl.BlockSpec((FFN_ROWS, D_MODEL), lambda i: (i, 0)),
        out_shape=jax.ShapeDtypeStruct((n_tok, D_MODEL), F32),
        compiler_params=pltpu.CompilerParams(
            dimension_semantics=("arbitrary",),
            vmem_limit_bytes=VMEM_LIMIT_BYTES),
        name="swiglu",
    )(h2d, mods, *consts)


def _rope_partner_index():
    return [d + 8 if (d % 16) < 8 else d - 8 for d in range(QK_ROPE)]


def _prep_weights(p):
    partner = jnp.array(_rope_partner_index(), jnp.int32)
    q_scale = QK_DIM ** -0.5 * math.log2(math.e)
    w_in = p["w_in"][:, :, :C_CKV].astype(BF16)
    w_tail = p["w_in"][:, :, C_CKV:].astype(BF16)
    w_kr = w_tail[:, :, KV_LORA:KV_LORA + QK_ROPE]
    zeros32 = jnp.zeros((DEPTH, D_MODEL, QK_ROPE), BF16)
    w_in_last = jnp.concatenate(
        [w_tail[:, :, :KV_LORA], w_kr[:, :, partner], zeros32, w_kr, zeros32], axis=2)

    wq_h = p["mla_wq_up"].reshape(DEPTH, Q_LORA, MLA_HEADS, QK_DIM)
    wq_pad = jnp.pad(wq_h, ((0, 0), (0, 0), (0, 0), (0, HEAD_PAD - QK_DIM)))
    wq_t = wq_pad.reshape(DEPTH, Q_LORA, MLA_HEADS * HEAD_PAD).transpose(0, 2, 1)
    rope_pad = ((0, 0), (0, 0), (0, 0), (ROPE_LO, HEAD_PAD - ROPE_LO - QK_ROPE))
    wq_partner = jnp.pad(wq_h[:, :, :, QK_NOPE:][:, :, :, partner], rope_pad)
    wq_partner_t = wq_partner.reshape(DEPTH, Q_LORA, MLA_HEADS * HEAD_PAD).transpose(0, 2, 1)

    wkv = p["mla_wkv_up"].reshape(DEPTH, KV_LORA, MLA_HEADS, QK_NOPE + V_DIM)
    wk_pad = jnp.pad(wkv[:, :, :, :QK_NOPE], ((0, 0), (0, 0), (0, 0), (0, HEAD_PAD - QK_NOPE)))
    wk_pad = wk_pad.reshape(DEPTH, KV_LORA, MLA_HEADS * HEAD_PAD)
    wv_ext = jnp.pad(wkv[:, :, :, QK_NOPE:], ((0, 0), (0, 0), (0, 0), (0, V_EXT - V_DIM)))
    wv_t = wv_ext.reshape(DEPTH, KV_LORA, MLA_HEADS * V_EXT).transpose(0, 2, 1)

    def pad_gain(g):
        return jnp.pad(g, ((0, 0), (0, HEAD_PAD - QK_DIM)))[:, None, :]

    def partner_gain(g):
        gp = g[:, QK_NOPE:][:, partner]
        return jnp.pad(gp, ((0, 0), (ROPE_LO, HEAD_PAD - ROPE_LO - QK_ROPE)))[:, None, :]

    def on_sublanes(g):
        return jnp.broadcast_to(g.transpose(0, 2, 1), (DEPTH, HEAD_PAD, ROW_CHUNK))

    qg, kg = p["q_norm_g"], p["k_norm_g"]
    eye = jnp.eye(LRU_BLOCKS, dtype=F32)
    gate_w = jnp.stack([p["lru_wa"][:, 0], p["lru_wi"][:, 0],
                        p["lru_wa"][:, 1], p["lru_wi"][:, 1]], axis=1)
    w_gates = (gate_w[:, :, :, :, None, :] * eye[None, None, :, None, :, None])
    w_gates = w_gates.transpose(0, 2, 3, 1, 4, 5).reshape(DEPTH, LRU_W, 4 * LRU_W)
    b_gates = jnp.stack([p["lru_ba"][:, 0], p["lru_bi"][:, 0],
                         p["lru_ba"][:, 1], p["lru_bi"][:, 1]], axis=1)

    def row(v):
        return v[:, None, :]

    rows = {
        "norm1_g": p["norm1_g"], "conv_b": p["conv_b"], "lru_conv_b": p["lru_conv_b"],
        "b_gates": 0.5 * b_gates.reshape(DEPTH, 4 * LRU_W),
        "mla_qnorm_g": p["mla_qnorm_g"], "mla_kvnorm_g": p["mla_kvnorm_g"],
        "kg": pad_gain(kg)[:, 0, :], "kg_partner": partner_gain(kg)[:, 0, :],
        "gnorm_conv": p["gnorm_conv"], "gnorm_lru": p["gnorm_lru"],
        "gnorm_mla": p["gnorm_mla"],
    }
    for name, (_, width) in VEC_SLOTS.items():
        assert rows[name].shape == (DEPTH, width), name
    vecs = row(jnp.concatenate([rows[name] for name in VEC_SLOTS], axis=1))

    return {
        "vecs": vecs,
        "w_in": w_in,
        "w_in_last": w_in_last,
        "conv_w": p["conv_w"],
        "lru_conv_w": p["lru_conv_w"],
        "w_gates": (0.5 * w_gates).astype(BF16),
        "lru_lambda": p["lru_lambda"],
        "wq_t": wq_t.astype(BF16),
        "wq_partner_t": wq_partner_t.astype(BF16),
        "wk": wk_pad.astype(BF16),
        "wv_t": wv_t.astype(BF16),
        "qg_t": on_sublanes(pad_gain(qg) * q_scale),
        "qg_partner_t": on_sublanes(partner_gain(qg) * q_scale),
        "w_out": p["w_out"].astype(BF16),
        "norm2_g": row(p["norm2_g"]),
        "w_gate": p["w_gate"].astype(BF16),
        "w_up": p["w_up"].astype(BF16),
        "w_down": p["w_down"].astype(BF16),
    }


def _rope_tables(n_tokens):
    n_rows = n_tokens // GRID_W
    row = np.repeat(np.arange(n_rows, dtype=np.float32), GRID_W)
    col = np.tile(np.arange(GRID_W, dtype=np.float32), n_rows)
    n_freq = QK_ROPE // 4
    inv_freq = np.power(np.float32(ROPE_THETA),
                        -np.arange(n_freq, dtype=np.float32) / np.float32(n_freq))
    ang_r = (row[:, None] * inv_freq).astype(np.float32)
    ang_c = (col[:, None] * inv_freq).astype(np.float32)
    cos = np.concatenate([np.cos(ang_r), np.cos(ang_r), np.cos(ang_c), np.cos(ang_c)], 1)
    sin = np.concatenate([-np.sin(ang_r), np.sin(ang_r), -np.sin(ang_c), np.sin(ang_c)], 1)
    ones = np.ones((n_tokens, QK_NOPE), np.float32)
    tail = np.zeros((n_tokens, HEAD_PAD - QK_DIM), np.float32)
    rope_c = np.concatenate([ones, cos, tail], axis=1).astype(np.float32)
    rope_s = np.concatenate([np.zeros((n_tokens, QK_NOPE), np.float32), sin, tail],
                            axis=1).astype(np.float32)
    return rope_c, rope_s


def kernel(x_prompt, x_sample, cache_ckv, cache_krope, state_lru, c, c_ctx, norm1_g, ada_w, ada_b, w_in, conv_w, conv_b, lru_conv_w, lru_conv_b, lru_wa, lru_ba, lru_wi, lru_bi, lru_lambda, mla_qnorm_g, mla_wq_up, mla_kvnorm_g, mla_wkv_up, q_norm_g, k_norm_g, gnorm_conv, gnorm_lru, gnorm_mla, w_out, norm2_g, w_gate, w_up, w_down):
    params = dict(norm1_g=norm1_g, w_in=w_in, conv_w=conv_w, conv_b=conv_b,
                  lru_conv_w=lru_conv_w, lru_conv_b=lru_conv_b, lru_wa=lru_wa,
                  lru_ba=lru_ba, lru_wi=lru_wi, lru_bi=lru_bi, lru_lambda=lru_lambda,
                  mla_qnorm_g=mla_qnorm_g, mla_wq_up=mla_wq_up,
                  mla_kvnorm_g=mla_kvnorm_g, mla_wkv_up=mla_wkv_up, q_norm_g=q_norm_g,
                  k_norm_g=k_norm_g, gnorm_conv=gnorm_conv, gnorm_lru=gnorm_lru,
                  gnorm_mla=gnorm_mla, w_out=w_out, norm2_g=norm2_g, w_gate=w_gate,
                  w_up=w_up, w_down=w_down)
    batch, seq_p, _ = x_prompt.shape
    dec_batch, seq_s, _ = x_sample.shape
    assert dec_batch == 4 and cache_ckv.shape[2] == 256

    cond8 = jnp.concatenate(
        [c, c_ctx[None, :], jnp.zeros((8 - dec_batch - 1, D_MODEL), F32)], axis=0)
    mods = _ada_call(cond8, ada_w, ada_b)

    rope_c, rope_s = _rope_tables(seq_s)
    cache_kr_pad = jnp.pad(
        cache_krope, ((0, 0), (0, 0), (0, 0), (ROPE_LO, HEAD_PAD - ROPE_LO - QK_ROPE)))

    def chunked_t(tab):
        return jnp.asarray(np.ascontiguousarray(
            tab.reshape(seq_s // ROW_CHUNK, ROW_CHUNK, HEAD_PAD).transpose(0, 2, 1)))

    extra = dict(rope_c=jnp.asarray(rope_c), rope_s=jnp.asarray(rope_s),
                 rope_ct=chunked_t(rope_c), rope_st=chunked_t(rope_s), cache_ckv=cache_ckv,
                 cache_kr=cache_kr_pad, state=state_lru)

    xp = x_prompt.reshape(batch * seq_p, D_MODEL)
    xs = x_sample.reshape(dec_batch * seq_s, D_MODEL)
    caches = None
    lw = _prep_weights(params)
    for l in range(DEPTH):
        (hs,) = _mix_call(xs, mods, l, lw, seq=seq_s, n_seq=1, latent=True, extra=extra)
        hp, *caches, xs = _mix_call(xp, mods, l, lw, seq=seq_p, n_seq=CTX_SEQS_PER_STEP,
                                    latent=False, ffn_h2d=hs, ffn_rows_per_mod=seq_s,
                                    caches=caches)
        xp = _ffn_call(hp, mods, l, lw, rows_per_mod=None, mod_base=4)
    new_cache_ckv, new_cache_krope, new_state_lru = caches
    return (xp.reshape(batch, seq_p, D_MODEL),
            xs.reshape(dec_batch, seq_s, D_MODEL),
            new_cache_ckv, new_cache_krope, new_state_lru)
```

```python
import functools
import itertools
import math

import jax
import jax.numpy as jnp
import numpy as np
from jax import lax
from jax.experimental import pallas as pl
from jax.experimental.pallas import tpu as pltpu

F32 = jnp.float32
BF16 = jnp.bfloat16

D_MODEL = 1024
DEPTH = 2
GRID_W = 64
CONV_W = 256
LRU_W = 256
LRU_BLOCKS = 4
LRU_BLK = 64
LRU_C = 8.0
MLA_HEADS = 8
QK_NOPE = 64
QK_ROPE = 32
V_DIM = 64
V_EXT = V_DIM + 16
QK_DIM = QK_NOPE + QK_ROPE
Q_LORA = 256
KV_LORA = 128
ROPE_THETA = 10000.0
FF = 2816
EPS = 1e-6

LANES = 128
SUBLANES = 8
HEAD_PAD = LANES
VMEM_LIMIT_BYTES = 60 * 1024 * 1024

C_BG, C_CG, C_H = 0, 256, 512
C_XB, C_YB = 768, 1024
C_Q = 1280
C_CKV = 1536
C_KR = 1664
U_W = 1792
SCAN_A = (C_H, C_CKV)
SCAN_B = (C_Q, C_BG)
ROPE_LO = QK_NOPE

def _slots(widths):
    out, off = {}, 0
    for name, width in widths:
        out[name] = (off, width)
        off += width
    return out


VEC_SLOTS = _slots([
    ("norm1_g", D_MODEL), ("conv_b", CONV_W), ("lru_conv_b", LRU_W), ("b_gates", 4 * LRU_W),
    ("mla_qnorm_g", Q_LORA), ("mla_kvnorm_g", KV_LORA), ("kg", HEAD_PAD),
    ("kg_partner", HEAD_PAD), ("gnorm_conv", CONV_W), ("gnorm_lru", LRU_W),
    ("gnorm_mla", MLA_HEADS * V_DIM)])

ROW_CHUNK = 256
SCAN_CHUNK = 32
Q_CHUNK = 256
FF_CHUNK = 256
PROJ_SLAB = 256
PROJ_CHUNKS = 1
CTX_SEQS_PER_STEP = 2
FUSED_FFN_PERIOD = 4
SEQ_STAGGER = 10
FFN_ROWS = 1024
PAD_ROWS = SUBLANES


def _rms_scale(x, n):
    ms = jnp.sum(x * x, axis=-1, keepdims=True) * (1.0 / n)
    return lax.rsqrt(ms + EPS)


def _sigmoid(x):
    return 0.5 * jnp.tanh(0.5 * x) + 0.5


def _gelu_tanh(x):
    c = math.sqrt(2.0 / math.pi)
    return 0.5 * x * (1.0 + jnp.tanh(c * (x + 0.044715 * (x * x * x))))


def _dot(a, b):
    return jnp.dot(a, b, preferred_element_type=F32)


def _interleave(*gens):
    live = list(gens)
    while live:
        for g in list(live):
            try:
                next(g)
            except StopIteration:
                live.remove(g)


def _interleave_staggered(gens, lag, side=None, side_period=1):
    pending = list(gens)
    live = []
    tick = 0
    while live or pending or side is not None:
        if pending and tick % lag == 0:
            live.append(pending.pop(0))
        if side is not None and (tick % side_period == 0 or not (live or pending)):
            try:
                next(side)
            except StopIteration:
                side = None
        for g in list(live):
            try:
                next(g)
            except StopIteration:
                live.remove(g)
        tick += 1


def _dot_nt(a, b):
    return lax.dot_general(a, b, (((1,), (1,)), ((), ())), preferred_element_type=F32)


ADA_TN = 1536


def _ada_kernel(cond_ref, w_ref, b_ref, o_ref):
    cnd = cond_ref[...]
    s = (cnd * _sigmoid(cnd)).astype(BF16)
    o_ref[:, 0, :] = _dot(s, w_ref[...].astype(BF16)) + b_ref[...]


def _ada_call(cond8, ada_w, ada_b):
    n_out = 6 * D_MODEL
    return pl.pallas_call(
        _ada_kernel,
        grid=(DEPTH, n_out // ADA_TN),
        in_specs=[
            pl.BlockSpec((8, D_MODEL), lambda l, j: (0, 0)),
            pl.BlockSpec((None, D_MODEL, ADA_TN), lambda l, j: (l, 0, j)),
            pl.BlockSpec((None, 1, ADA_TN), lambda l, j: (l, 0, j)),
        ],
        out_specs=pl.BlockSpec((None, 8, 1, ADA_TN), lambda l, j: (l, 0, 0, j)),
        out_shape=jax.ShapeDtypeStruct((DEPTH, 8, 1, n_out), F32),
        compiler_params=pltpu.CompilerParams(
            dimension_semantics=("arbitrary", "arbitrary"),
            vmem_limit_bytes=VMEM_LIMIT_BYTES),
        name="ada_mod",
    )(cond8, ada_w, ada_b.reshape(DEPTH, 1, n_out))


def _mix_kernel(*refs, seq, n_seq, latent, fused_ffn, n_aliased):
    T = seq * n_seq
    n_ctx = 256 if latent else 0
    (x_ref, mod_ref, vec_ref, win_ref, wlast_ref, convw_ref, lcw_ref, wg_ref, lam_ref,
     wqt_ref, wk_ref, wvt_ref, qgt_ref, wout_ref) = refs[:14]
    pos = 14
    if latent:
        (wqpt_ref, qgpt_ref, ropect_ref, ropest_ref, ropec_ref, ropes_ref,
         cckv_ref, ckr_ref, h0_ref) = refs[pos:pos + 9]
        pos += 9

    def vec(name):
        off, width = VEC_SLOTS[name]
        return vec_ref[:, off:off + width]
    if fused_ffn:
        ffn_in_refs = refs[pos:pos + 6]
        pos += 6
    pos += n_aliased
    h_out_ref = refs[pos]
    pos += 1
    if not latent:
        ckv_out_ref, kr_out_ref, st_out_ref = refs[pos:pos + 3]
        pos += 3
    if fused_ffn:
        ffn_out_ref = refs[pos]
        pos += 1
    u_scr, qt_scr, k_scr, vt_scr, ymt_scr, ycat_scr = refs[pos:pos + 6]
    if latent:
        st_scr, m_scr = refs[pos + 6:pos + 8]

    n_chunks = T // ROW_CHUNK
    lane = lax.broadcasted_iota(jnp.int32, (1, LANES), 1)
    rope_lanes = (lane >= ROPE_LO) & (lane < ROPE_LO + QK_ROPE)

    sh1 = mod_ref[:, 0:D_MODEL]
    sc1 = mod_ref[:, D_MODEL:2 * D_MODEL]
    g1 = mod_ref[:, 2 * D_MODEL:3 * D_MODEL]
    gain1 = vec("norm1_g") * (1.0 + sc1)

    u_scr[0:PAD_ROWS, :] = jnp.zeros((PAD_ROWS, U_W), F32)
    u_scr[PAD_ROWS + T:2 * PAD_ROWS + T, :] = jnp.zeros((PAD_ROWS, U_W), F32)

    def phase_a(c, n=1):
        r0 = c * ROW_CHUNK
        nrows = n * ROW_CHUNK
        x = x_ref[r0:r0 + nrows, :]
        hn = (x * _rms_scale(x, D_MODEL) * gain1 + sh1).astype(BF16)
        yield
        urows = slice(PAD_ROWS + r0, PAD_ROWS + r0 + nrows)
        for s0 in range(0, C_CKV, PROJ_SLAB):
            u_scr[urows, s0:s0 + PROJ_SLAB] = _dot(hn, win_ref[:, s0:s0 + PROJ_SLAB])
            yield
        u_scr[urows, C_CKV:U_W] = _dot(hn, wlast_ref[...])
        yield
        u_scr[urows, C_CG:C_CG + CONV_W] = (
            u_scr[urows, C_CG:C_CG + CONV_W] * u_scr[urows, C_H:C_H + CONV_W])
        yield

    vrow = lax.broadcasted_iota(jnp.int32, (MLA_HEADS * V_EXT, 1), 0)
    ones_rows = (vrow % V_EXT) >= V_DIM

    def values_t(cb):
        return jnp.where(ones_rows, 1.0, _dot_nt(wvt_ref[...], cb)).astype(BF16)

    def phase_b(c):
        r0 = c * ROW_CHUNK
        rows = slice(PAD_ROWS + r0, PAD_ROWS + r0 + ROW_CHUNK)
        uq = u_scr[rows, C_Q:C_Q + Q_LORA]
        qn = (uq * _rms_scale(uq, Q_LORA) * vec("mla_qnorm_g")).astype(BF16)
        qa_t = _dot_nt(wqt_ref[...], qn)
        if latent:
            qpa_t = _dot_nt(wqpt_ref[...], qn)
            q_tc = ropect_ref[c] * qgt_ref[...]
            q_ts = ropest_ref[c] * qgpt_ref[...]
        yield
        for h in range(MLA_HEADS):
            sl = slice(h * HEAD_PAD, (h + 1) * HEAD_PAD)
            qh = qa_t[sl, :]
            rq = lax.rsqrt(jnp.sum(qh * qh, axis=0, keepdims=True) * (1.0 / QK_DIM) + EPS)
            if latent:
                qh = (qh * q_tc + qpa_t[sl, :] * q_ts) * rq
            else:
                qh = qh * qgt_ref[...] * rq
            qt_scr[h, c] = qh.astype(BF16)
            if h % 2:
                yield
        uc = u_scr[rows, C_CKV:C_CKV + KV_LORA]
        ckv = uc * _rms_scale(uc, KV_LORA) * vec("mla_kvnorm_g")
        krb = u_scr[rows, C_KR:C_KR + LANES]
        kr_rolled = pltpu.roll(krb, 64, 1)
        if not latent:
            sq, rq0 = divmod(r0, seq)
            ckv_out_ref[sq, rq0:rq0 + ROW_CHUNK, :] = ckv
            kr_out_ref[sq, rq0:rq0 + ROW_CHUNK, :] = kr_rolled[:, 0:QK_ROPE]
        krm = jnp.where(rope_lanes, krb, 0.0)
        cb = ckv.astype(BF16)
        ka = _dot(cb, wk_ref[...])
        vt_scr[c] = values_t(cb)
        if latent:
            k_tc = ropec_ref[r0:r0 + ROW_CHUNK, :] * vec("kg")
            k_ts = ropes_ref[r0:r0 + ROW_CHUNK, :] * vec("kg_partner")
        yield
        for h in range(MLA_HEADS):
            sl = slice(h * HEAD_PAD, (h + 1) * HEAD_PAD)
            kpre = ka[:, sl] + krm
            rk = _rms_scale(kpre, QK_DIM)
            if latent:
                kh = (kpre * k_tc + kr_rolled * k_ts) * rk
            else:
                kh = kpre * vec("kg") * rk
            k_scr[h, r0:r0 + ROW_CHUNK, :] = kh.astype(BF16)
            if h % 2:
                yield

    if latent:
        cc = cckv_ref[...].astype(BF16)
        ka = _dot(cc, wk_ref[...])
        vt_scr[n_chunks] = values_t(cc)
        krc = ckr_ref[...]
        for h in range(MLA_HEADS):
            kpre = ka[:, h * HEAD_PAD:(h + 1) * HEAD_PAD] + krc
            kh = kpre * vec("kg") * _rms_scale(kpre, QK_DIM)
            k_scr[h, seq:seq + n_ctx, :] = kh.astype(BF16)

    neg_lam = -lam_ref[...]
    sp = jnp.maximum(neg_lam, 0.0) + jnp.log(1.0 + jnp.exp(-jnp.abs(neg_lam)))
    log2a_coef = (-0.5 * LRU_C * math.log2(math.e)) * sp
    row_rc = lax.broadcasted_iota(jnp.int32, (ROW_CHUNK, 1), 0)

    def phase_c(ci):
        r0 = ci * ROW_CHUNK
        base = PAD_ROWS + r0
        first = (r0 % seq) == 0
        last = ((r0 + ROW_CHUNK) % seq) == 0

        def win(col, shift, width=CONV_W):
            w = u_scr[base + shift:base + shift + ROW_CHUNK, col:col + width]
            if shift < 0 and first:
                w = jnp.where(row_rc < -shift, 0.0, w)
            if shift > 0 and last:
                w = jnp.where(row_rc >= ROW_CHUNK - shift, 0.0, w)
            return w

        z_m, z_0, z_p = win(C_CG, -1), win(C_CG, 0), win(C_CG, 1)
        conv = (z_m * convw_ref[0:1, :] + z_0 * convw_ref[1:2, :]
                + z_p * convw_ref[2:3, :] + vec("conv_b"))
        yc = u_scr[base:base + ROW_CHUNK, C_BG:C_BG + CONV_W] * conv
        ycn = yc * _rms_scale(yc, CONV_W) * vec("gnorm_conv")
        ycat_scr[r0:r0 + ROW_CHUNK, 0:CONV_W] = ycn.astype(BF16)
        yield

        xc = (win(C_XB, -2) * lcw_ref[0:1, :] + win(C_XB, -1) * lcw_ref[1:2, :]
              + win(C_XB, 0) * lcw_ref[2:3, :] + win(C_XB, 1) * lcw_ref[3:4, :]
              + vec("lru_conv_b"))
        half_gates = _dot(xc.astype(BF16), wg_ref[...]) + vec("b_gates")
        half_xc = 0.5 * xc
        yield
        for d in range(2):
            t_r = jnp.tanh(half_gates[:, (2 * d) * LRU_W:(2 * d + 1) * LRU_W])
            t_i = jnp.tanh(half_gates[:, (2 * d + 1) * LRU_W:(2 * d + 2) * LRU_W])
            a = jnp.exp2(log2a_coef[d:d + 1, :] * (t_r + 1.0))
            v = 1.0 - a * a
            mult = v * lax.rsqrt(jnp.maximum(v, 1e-30))
            u_scr[base:base + ROW_CHUNK, SCAN_A[d]:SCAN_A[d] + LRU_W] = a
            u_scr[base:base + ROW_CHUNK, SCAN_B[d]:SCAN_B[d] + LRU_W] = (
                mult * (half_xc * (t_i + 1.0)))
            yield

    n_sc = seq // SCAN_CHUNK
    row_sc = lax.broadcasted_iota(jnp.int32, (SCAN_CHUNK, LRU_W), 0)
    steps = [1 << i for i in range(int(math.log2(SCAN_CHUNK)))]

    def scan_step(rf, rb, cf, cb_):
        rf = pl.multiple_of(rf + PAD_ROWS, SUBLANES)
        rb = pl.multiple_of(rb + PAD_ROWS, SUBLANES)
        a = u_scr[pl.ds(rf, SCAN_CHUNK), SCAN_A[0]:SCAN_A[0] + LRU_W]
        b = u_scr[pl.ds(rf, SCAN_CHUNK), SCAN_B[0]:SCAN_B[0] + LRU_W]
        b = b + jnp.where(row_sc == 0, a * cf, 0.0)
        for d in steps:
            if d % SUBLANES == 0:
                b = jnp.concatenate([b[:d], b[d:] + a[d:] * b[:-d]], axis=0)
                if d != steps[-1]:
                    a = jnp.concatenate([a[:d], a[d:] * a[:-d]], axis=0)
                continue
            b = b + a * jnp.where(row_sc >= d, pltpu.roll(b, d, 0), 0.0)
            if d != steps[-1]:
                a = a * jnp.where(row_sc >= d, pltpu.roll(a, d, 0), 1.0)
        u_scr[pl.ds(rf, SCAN_CHUNK), SCAN_B[0]:SCAN_B[0] + LRU_W] = b
        cf = b[SCAN_CHUNK - 1:SCAN_CHUNK, :]
        a = u_scr[pl.ds(rb, SCAN_CHUNK), SCAN_A[1]:SCAN_A[1] + LRU_W]
        b = u_scr[pl.ds(rb, SCAN_CHUNK), SCAN_B[1]:SCAN_B[1] + LRU_W]
        b = b + jnp.where(row_sc == SCAN_CHUNK - 1, a * cb_, 0.0)
        for d in steps:
            if d % SUBLANES == 0:
                b = jnp.concatenate([b[:-d] + a[:-d] * b[d:], b[-d:]], axis=0)
                if d != steps[-1]:
                    a = jnp.concatenate([a[:-d] * a[d:], a[-d:]], axis=0)
                continue
            keep = row_sc < SCAN_CHUNK - d
            b = b + a * jnp.where(keep, pltpu.roll(b, SCAN_CHUNK - d, 0), 0.0)
            if d != steps[-1]:
                a = a * jnp.where(keep, pltpu.roll(a, SCAN_CHUNK - d, 0), 1.0)
        u_scr[pl.ds(rb, SCAN_CHUNK), SCAN_B[1]:SCAN_B[1] + LRU_W] = b
        cb_ = b[0:1, :]
        return cf, cb_

    cps = seq // ROW_CHUNK

    def softmax_pv(st, vts):
        m = jnp.max(st, axis=0, keepdims=True)
        pt = jnp.exp2(st - m).astype(BF16)
        ot = None
        for i, vt in enumerate(vts):
            part = _dot(vt, pt[i * ROW_CHUNK:(i + 1) * ROW_CHUNK, :])
            ot = part if ot is None else ot + part
        return ot[0:V_DIM, :] * (1.0 / ot[V_DIM:V_DIM + 1, :])

    def f_build(c):
        r0 = c * ROW_CHUNK
        rows = slice(r0, r0 + ROW_CHUNK)
        urows = slice(PAD_ROWS + r0, PAD_ROWS + r0 + ROW_CHUNK)
        hf = u_scr[urows, SCAN_B[0]:SCAN_B[0] + LRU_W]
        hb = u_scr[urows, SCAN_B[1]:SCAN_B[1] + LRU_W]
        yb = u_scr[urows, C_YB:C_YB + LRU_W]
        yl = (hf + hb) * _gelu_tanh(yb)
        yln = yl * _rms_scale(yl, LRU_W) * vec("gnorm_lru")
        ycat_scr[rows, CONV_W:CONV_W + LRU_W] = yln.astype(BF16)
        yield
        ymt = ymt_scr[c]
        ssq = jnp.sum(ymt * ymt, axis=0, keepdims=True)
        ynt = ymt * lax.rsqrt(ssq * (1.0 / (MLA_HEADS * V_DIM)) + EPS)
        ycat_scr[rows, CONV_W + LRU_W:D_MODEL] = (ynt.T * vec("gnorm_mla")).astype(BF16)
        yield

    def f_project(c):
        rows = slice(c * ROW_CHUNK, (c + 1) * ROW_CHUNK)
        for n0 in range(0, D_MODEL, PROJ_SLAB):
            cols = slice(n0, n0 + PROJ_SLAB)
            y = _dot(ycat_scr[rows, :], wout_ref[:, cols])
            h_out_ref[rows, cols] = x_ref[rows, cols] + g1[:, cols] * y
            yield

    if latent:
        groups = [list(range(g0, g0 + PROJ_CHUNKS)) for g0 in range(0, n_chunks, PROJ_CHUNKS)]
        _interleave(phase_a(0, PROJ_CHUNKS))
        for gi, group in enumerate(groups):
            dependent = []
            for c in group:
                dependent.append(phase_b(c))
                if c:
                    dependent.append(phase_c(c - 1))
            ahead = [phase_a(groups[gi + 1][0], PROJ_CHUNKS)] if gi + 1 < len(groups) else []
            _interleave(*ahead, itertools.chain(*dependent))
        _interleave(phase_c(n_chunks - 1))

        key_chunks = list(range(cps)) + [n_chunks]
        n_blk = cps * (MLA_HEADS // 2)
        trips = n_blk // 2
        scans_per_trip = n_sc // trips

        def heads_of(idx):
            qc = idx // (MLA_HEADS // 2)
            j = idx % (MLA_HEADS // 2)
            return qc, [2 * j, 2 * j + 1]

        def scores(idx, slot):
            qc, hs = heads_of(idx)
            for par, h in enumerate(hs):
                qt = qt_scr[h, qc]
                m = None
                for i in range(len(key_chunks)):
                    rows = slice(i * ROW_CHUNK, (i + 1) * ROW_CHUNK)
                    st = _dot(k_scr[h, rows, :], qt)
                    st_scr[slot, par, rows, :] = st
                    mc = jnp.max(st, axis=0, keepdims=True)
                    m = mc if m is None else jnp.maximum(m, mc)
                    yield
                m_scr[slot, par] = m

        def finish(idx, slot):
            qc, hs = heads_of(idx)
            for par, h in enumerate(hs):
                hr = pl.ds(pl.multiple_of(h * V_DIM, V_DIM), V_DIM)
                vr = pl.ds(pl.multiple_of(h * V_EXT, V_EXT - V_DIM), V_EXT)
                m = m_scr[slot, par]
                ot = None
                for i, kc in enumerate(key_chunks):
                    rows = slice(i * ROW_CHUNK, (i + 1) * ROW_CHUNK)
                    pt = jnp.exp2(st_scr[slot, par, rows, :] - m).astype(BF16)
                    part = _dot(vt_scr[kc, vr, :], pt)
                    ot = part if ot is None else ot + part
                    yield
                ymt_scr[qc, hr, :] = ot[0:V_DIM, :] * (1.0 / ot[V_DIM:V_DIM + 1, :])

        _interleave(scores(0, 0))

        def att_body(i, carry):
            b0 = 2 * i
            state = {"carry": carry, "c": i * scans_per_trip}

            def scans(n):
                for _ in range(n):
                    c = state["c"]
                    rf = pl.multiple_of(c * SCAN_CHUNK, SCAN_CHUNK)
                    rb = pl.multiple_of((n_sc - 1 - c) * SCAN_CHUNK, SCAN_CHUNK)
                    state["carry"] = scan_step(rf, rb, *state["carry"])
                    state["c"] = c + 1
                    yield
                    yield
                    yield

            half = scans_per_trip // 2
            _interleave(scores(b0 + 1, 1), finish(b0, 0), scans(half))
            _interleave(scores(jnp.minimum(b0 + 2, n_blk - 1), 0), finish(b0 + 1, 1),
                        scans(scans_per_trip - half))
            return state["carry"]

        lax.fori_loop(0, trips, att_body, (h0_ref[0:1, :], h0_ref[1:2, :]))

        _interleave(f_build(0))
        for c in range(n_chunks):
            ahead = [f_build(c + 1)] if c + 1 < n_chunks else []
            _interleave(f_project(c), *ahead)
    else:
        def phase_d(s):
            carry = (jnp.zeros((1, LRU_W), F32), jnp.zeros((1, LRU_W), F32))
            for c in range(n_sc):
                carry = scan_step(s * seq + c * SCAN_CHUNK,
                                  s * seq + (n_sc - 1 - c) * SCAN_CHUNK, *carry)
                yield
            st_out_ref[s, 0:1, :] = carry[0]
            st_out_ref[s, 1:2, :] = carry[1]

        def phase_e(s):
            blocks = [(qi, h) for qi in range(cps) for h in range(MLA_HEADS)]

            def scores(blk):
                qi, h = blk
                return _dot(k_scr[h, s * seq:(s + 1) * seq, :], qt_scr[h, s * cps + qi])

            st = scores(blocks[0])
            for i, (qi, h) in enumerate(blocks):
                nxt = scores(blocks[i + 1]) if i + 1 < len(blocks) else None
                hr = slice(h * V_DIM, (h + 1) * V_DIM)
                vr = slice(h * V_EXT, (h + 1) * V_EXT)
                vts = [vt_scr[s * cps + j, vr, :] for j in range(cps)]
                ymt_scr[s * cps + qi, hr, :] = softmax_pv(st, vts)
                st = nxt
                yield

        def sequence(s):
            chunks = range(s * cps, (s + 1) * cps)
            for c in chunks:
                yield from phase_a(c)
            for c in chunks:
                yield from phase_b(c)
            for c in chunks:
                yield from phase_c(c)
            yield from phase_d(s)
            yield from phase_e(s)
            for c in chunks:
                yield from f_build(c)
                yield from f_project(c)

        side = _ffn_steps(*ffn_in_refs, ffn_out_ref) if fused_ffn else None
        _interleave_staggered([sequence(s) for s in range(n_seq)], SEQ_STAGGER,
                              side=side, side_period=FUSED_FFN_PERIOD)


def _const_spec(shape):
    nd = len(shape)
    return pl.BlockSpec(shape, lambda i, _n=nd: (0,) * _n,
                        pipeline_mode=pl.Buffered(1))


def _layer_spec(arr, layer):
    nd = arr.ndim - 1
    return pl.BlockSpec((None,) + arr.shape[1:], lambda i, _n=nd: (layer,) + (0,) * _n,
                        pipeline_mode=pl.Buffered(1))


def _mix_call(x2d, mods, layer, lw, *, seq, n_seq, latent, extra=None, ffn_h2d=None,
              ffn_rows_per_mod=None, caches=None):
    n_tok = x2d.shape[0]
    T = seq * n_seq
    grid = (n_tok // T,)
    fused_ffn = ffn_h2d is not None
    n_keys_buf = T if not latent else seq + 256

    if latent:
        mod_map = lambda i: (layer, i, 0, 0)
    else:
        mod_map = lambda i: (layer, 4, 0, 0)

    consts = [lw["vecs"], lw["w_in"], lw["w_in_last"], lw["conv_w"], lw["lru_conv_w"],
              lw["w_gates"], lw["lru_lambda"], lw["wq_t"], lw["wk"], lw["wv_t"], lw["qg_t"],
              lw["w_out"]]
    args = [x2d, mods] + consts
    in_specs = [pl.BlockSpec((T, D_MODEL), lambda i: (i, 0)),
                pl.BlockSpec((None, None, 1, 6 * D_MODEL), mod_map)]
    in_specs += [_layer_spec(a, layer) for a in consts]
    if latent:
        lconsts = [lw["wq_partner_t"], lw["qg_partner_t"]]
        tables = [extra["rope_ct"], extra["rope_st"], extra["rope_c"], extra["rope_s"]]
        args += lconsts + tables
        in_specs += [_layer_spec(a, layer) for a in lconsts]
        in_specs += [_const_spec(a.shape) for a in tables]
        args += [extra["cache_ckv"], extra["cache_kr"], extra["state"]]
        in_specs += [
            pl.BlockSpec((None, None, 256, KV_LORA), lambda i: (i, layer, 0, 0)),
            pl.BlockSpec((None, None, 256, LANES), lambda i: (i, layer, 0, 0)),
            pl.BlockSpec((None, None, 2, LRU_W), lambda i: (i, layer, 0, 0)),
        ]

    if fused_ffn:
        ffn_rows = ffn_h2d.shape[0] // grid[0]
        tiles_per_mod = ffn_rows_per_mod // ffn_rows
        ffn_consts = [lw["norm2_g"], lw["w_gate"], lw["w_up"], lw["w_down"]]
        args += [ffn_h2d, mods] + ffn_consts
        in_specs += [pl.BlockSpec((ffn_rows, D_MODEL), lambda i: (i, 0)),
                     pl.BlockSpec((None, None, 1, 6 * D_MODEL),
                                  lambda i: (layer, i // tiles_per_mod, 0, 0))]
        in_specs += [_layer_spec(a, layer) for a in ffn_consts]

    out_shape = [jax.ShapeDtypeStruct((n_tok, D_MODEL), F32)]
    out_specs = [pl.BlockSpec((T, D_MODEL), lambda i: (i, 0))]
    if not latent:
        n_b = n_tok // seq
        out_shape += [jax.ShapeDtypeStruct((n_b, DEPTH, seq, KV_LORA), F32),
                      jax.ShapeDtypeStruct((n_b, DEPTH, seq, QK_ROPE), F32),
                      jax.ShapeDtypeStruct((n_b, DEPTH, 2, LRU_W), F32)]
        out_specs += [pl.BlockSpec((n_seq, None, seq, KV_LORA), lambda i: (i, layer, 0, 0)),
                      pl.BlockSpec((n_seq, None, seq, QK_ROPE), lambda i: (i, layer, 0, 0)),
                      pl.BlockSpec((n_seq, None, 2, LRU_W), lambda i: (i, layer, 0, 0))]
    aliases = {}
    if caches is not None:
        for k, arr in enumerate(caches):
            aliases[len(args)] = 1 + k
            args.append(arr)
            in_specs.append(pl.BlockSpec(memory_space=pl.ANY))
    if fused_ffn:
        out_shape.append(jax.ShapeDtypeStruct(ffn_h2d.shape, F32))
        out_specs.append(pl.BlockSpec((ffn_rows, D_MODEL), lambda i: (i, 0)))

    scratch = [
        pltpu.VMEM((T + 2 * PAD_ROWS, U_W), F32),
        pltpu.VMEM((MLA_HEADS, T // ROW_CHUNK, HEAD_PAD, ROW_CHUNK), BF16),
        pltpu.VMEM((MLA_HEADS, n_keys_buf, HEAD_PAD), BF16),
        pltpu.VMEM((n_keys_buf // ROW_CHUNK, MLA_HEADS * V_EXT, ROW_CHUNK), BF16),
        pltpu.VMEM((T // ROW_CHUNK, MLA_HEADS * V_DIM, ROW_CHUNK), F32),
        pltpu.VMEM((T, D_MODEL), BF16),
    ]
    if latent:
        scratch.append(pltpu.VMEM((2, 2, n_keys_buf, ROW_CHUNK), F32))
        scratch.append(pltpu.VMEM((2, 2, 1, ROW_CHUNK), F32))
    return pl.pallas_call(
        functools.partial(_mix_kernel, seq=seq, n_seq=n_seq, latent=latent,
                          fused_ffn=fused_ffn, n_aliased=len(aliases)),
        grid=grid,
        in_specs=in_specs,
        out_specs=out_specs,
        out_shape=out_shape,
        scratch_shapes=scratch,
        input_output_aliases=aliases,
        compiler_params=pltpu.CompilerParams(
            dimension_semantics=("arbitrary",),
            vmem_limit_bytes=VMEM_LIMIT_BYTES),
        name="mix_latent" if latent else ("mix_context_swiglu" if fused_ffn else "mix_context"),
    )(*args)


def _ffn_steps(h_ref, mod_ref, n2g_ref, wg_ref, wu_ref, wd_ref, o_ref):
    sh2 = mod_ref[:, 3 * D_MODEL:4 * D_MODEL]
    sc2 = mod_ref[:, 4 * D_MODEL:5 * D_MODEL]
    g2 = mod_ref[:, 5 * D_MODEL:6 * D_MODEL]
    h = h_ref[...]
    hn = (h * _rms_scale(h, D_MODEL) * (n2g_ref[...] * (1.0 + sc2)) + sh2).astype(BF16)
    yield
    acc = None
    for c0 in range(0, FF, FF_CHUNK):
        g = _dot(hn, wg_ref[:, c0:c0 + FF_CHUNK])
        u = _dot(hn, wu_ref[:, c0:c0 + FF_CHUNK])
        act = (g * _sigmoid(g) * u).astype(BF16)
        part = _dot(act, wd_ref[c0:c0 + FF_CHUNK, :])
        acc = part if acc is None else acc + part
        yield
    o_ref[...] = h_ref[...] + g2 * acc
    yield


def _ffn_kernel(*refs):
    _interleave(_ffn_steps(*refs))


def _ffn_call(h2d, mods, layer, lw, *, rows_per_mod, mod_base):
    n_tok = h2d.shape[0]
    if mod_base is None:
        tiles_per_mod = rows_per_mod // FFN_ROWS
        mod_map = lambda i: (layer, i // tiles_per_mod, 0, 0)
    else:
        mod_map = lambda i: (layer, mod_base, 0, 0)
    consts = [lw["norm2_g"], lw["w_gate"], lw["w_up"], lw["w_down"]]
    return pl.pallas_call(
        _ffn_kernel,
        grid=(n_tok // FFN_ROWS,),
        in_specs=[pl.BlockSpec((FFN_ROWS, D_MODEL), lambda i: (i, 0)),
                  pl.BlockSpec((None, None, 1, 6 * D_MODEL), mod_map)]
        + [_layer_spec(a, layer) for a in consts],
        out_specs=pl.BlockSpec((FFN_ROWS, D_MODEL), lambda i: (i, 0)),
        out_shape=jax.ShapeDtypeStruct((n_tok, D_MODEL), F32),
        compiler_params=pltpu.CompilerParams(
            dimension_semantics=("arbitrary",),
            vmem_limit_bytes=VMEM_LIMIT_BYTES),
        name="swiglu",
    )(h2d, mods, *consts)


def _rope_partner_index():
    return [d + 8 if (d % 16) < 8 else d - 8 for d in range(QK_ROPE)]


def _prep_weights(p):
    partner = jnp.array(_rope_partner_index(), jnp.int32)
    q_scale = QK_DIM ** -0.5 * math.log2(math.e)
    w2d = p["w_in"].reshape(DEPTH * D_MODEL, -1)
    w_in = w2d[:, :C_CKV].astype(BF16).reshape(DEPTH, D_MODEL, C_CKV)
    w_tail = w2d[:, C_CKV:].astype(BF16).reshape(DEPTH, D_MODEL, -1)
    w_kr = w_tail[:, :, KV_LORA:KV_LORA + QK_ROPE]
    zeros32 = jnp.zeros((DEPTH, D_MODEL, QK_ROPE), BF16)
    w_in_last = jnp.concatenate(
        [w_tail[:, :, :KV_LORA], w_kr[:, :, partner], zeros32, w_kr, zeros32], axis=2)

    wq_h = p["mla_wq_up"].reshape(DEPTH, Q_LORA, MLA_HEADS, QK_DIM)
    wq_pad = jnp.pad(wq_h, ((0, 0), (0, 0), (0, 0), (0, HEAD_PAD - QK_DIM)))
    wq_t = wq_pad.reshape(DEPTH, Q_LORA, MLA_HEADS * HEAD_PAD).transpose(0, 2, 1)
    rope_pad = ((0, 0), (0, 0), (0, 0), (ROPE_LO, HEAD_PAD - ROPE_LO - QK_ROPE))
    wq_partner = jnp.pad(wq_h[:, :, :, QK_NOPE:][:, :, :, partner], rope_pad)
    wq_partner_t = wq_partner.reshape(DEPTH, Q_LORA, MLA_HEADS * HEAD_PAD).transpose(0, 2, 1)

    wkv = p["mla_wkv_up"].reshape(DEPTH, KV_LORA, MLA_HEADS, QK_NOPE + V_DIM)
    wk_pad = jnp.pad(wkv[:, :, :, :QK_NOPE], ((0, 0), (0, 0), (0, 0), (0, HEAD_PAD - QK_NOPE)))
    wk_pad = wk_pad.reshape(DEPTH, KV_LORA, MLA_HEADS * HEAD_PAD)
    wv_ext = jnp.pad(wkv[:, :, :, QK_NOPE:], ((0, 0), (0, 0), (0, 0), (0, V_EXT - V_DIM)))
    wv_t = wv_ext.reshape(DEPTH, KV_LORA, MLA_HEADS * V_EXT).transpose(0, 2, 1)

    def pad_gain(g):
        return jnp.pad(g, ((0, 0), (0, HEAD_PAD - QK_DIM)))[:, None, :]

    def partner_gain(g):
        gp = g[:, QK_NOPE:][:, partner]
        return jnp.pad(gp, ((0, 0), (ROPE_LO, HEAD_PAD - ROPE_LO - QK_ROPE)))[:, None, :]

    def on_sublanes(g):
        return jnp.broadcast_to(g.transpose(0, 2, 1), (DEPTH, HEAD_PAD, ROW_CHUNK))

    qg, kg = p["q_norm_g"], p["k_norm_g"]
    eye = jnp.eye(LRU_BLOCKS, dtype=F32)
    gate_w = jnp.stack([p["lru_wa"][:, 0], p["lru_wi"][:, 0],
                        p["lru_wa"][:, 1], p["lru_wi"][:, 1]], axis=1)
    w_gates = (gate_w[:, :, :, :, None, :] * eye[None, None, :, None, :, None])
    w_gates = w_gates.transpose(0, 2, 3, 1, 4, 5).reshape(DEPTH, LRU_W, 4 * LRU_W)
    b_gates = jnp.stack([p["lru_ba"][:, 0], p["lru_bi"][:, 0],
                         p["lru_ba"][:, 1], p["lru_bi"][:, 1]], axis=1)

    def row(v):
        return v[:, None, :]

    rows = {
        "norm1_g": p["norm1_g"], "conv_b": p["conv_b"], "lru_conv_b": p["lru_conv_b"],
        "b_gates": 0.5 * b_gates.reshape(DEPTH, 4 * LRU_W),
        "mla_qnorm_g": p["mla_qnorm_g"], "mla_kvnorm_g": p["mla_kvnorm_g"],
        "kg": pad_gain(kg)[:, 0, :], "kg_partner": partner_gain(kg)[:, 0, :],
        "gnorm_conv": p["gnorm_conv"], "gnorm_lru": p["gnorm_lru"],
        "gnorm_mla": p["gnorm_mla"],
    }
    for name, (_, width) in VEC_SLOTS.items():
        assert rows[name].shape == (DEPTH, width), name
    vecs = row(jnp.concatenate([rows[name] for name in VEC_SLOTS], axis=1))

    return {
        "vecs": vecs,
        "w_in": w_in,
        "w_in_last": w_in_last,
        "conv_w": p["conv_w"],
        "lru_conv_w": p["lru_conv_w"],
        "w_gates": (0.5 * w_gates).astype(BF16),
        "lru_lambda": p["lru_lambda"],
        "wq_t": wq_t.astype(BF16),
        "wq_partner_t": wq_partner_t.astype(BF16),
        "wk": wk_pad.astype(BF16),
        "wv_t": wv_t.astype(BF16),
        "qg_t": on_sublanes(pad_gain(qg) * q_scale),
        "qg_partner_t": on_sublanes(partner_gain(qg) * q_scale),
        "w_out": p["w_out"].astype(BF16),
        "norm2_g": row(p["norm2_g"]),
        "w_gate": p["w_gate"].astype(BF16),
        "w_up": p["w_up"].astype(BF16),
        "w_down": p["w_down"].astype(BF16),
    }


def _rope_tables(n_tokens):
    n_rows = n_tokens // GRID_W
    row = np.repeat(np.arange(n_rows, dtype=np.float32), GRID_W)
    col = np.tile(np.arange(GRID_W, dtype=np.float32), n_rows)
    n_freq = QK_ROPE // 4
    inv_freq = np.power(np.float32(ROPE_THETA),
                        -np.arange(n_freq, dtype=np.float32) / np.float32(n_freq))
    ang_r = (row[:, None] * inv_freq).astype(np.float32)
    ang_c = (col[:, None] * inv_freq).astype(np.float32)
    cos = np.concatenate([np.cos(ang_r), np.cos(ang_r), np.cos(ang_c), np.cos(ang_c)], 1)
    sin = np.concatenate([-np.sin(ang_r), np.sin(ang_r), -np.sin(ang_c), np.sin(ang_c)], 1)
    ones = np.ones((n_tokens, QK_NOPE), np.float32)
    tail = np.zeros((n_tokens, HEAD_PAD - QK_DIM), np.float32)
    rope_c = np.concatenate([ones, cos, tail], axis=1).astype(np.float32)
    rope_s = np.concatenate([np.zeros((n_tokens, QK_NOPE), np.float32), sin, tail],
                            axis=1).astype(np.float32)
    return rope_c, rope_s


def kernel(x_prompt, x_sample, cache_ckv, cache_krope, state_lru, c, c_ctx, norm1_g, ada_w, ada_b, w_in, conv_w, conv_b, lru_conv_w, lru_conv_b, lru_wa, lru_ba, lru_wi, lru_bi, lru_lambda, mla_qnorm_g, mla_wq_up, mla_kvnorm_g, mla_wkv_up, q_norm_g, k_norm_g, gnorm_conv, gnorm_lru, gnorm_mla, w_out, norm2_g, w_gate, w_up, w_down):
    params = dict(norm1_g=norm1_g, w_in=w_in, conv_w=conv_w, conv_b=conv_b,
                  lru_conv_w=lru_conv_w, lru_conv_b=lru_conv_b, lru_wa=lru_wa,
                  lru_ba=lru_ba, lru_wi=lru_wi, lru_bi=lru_bi, lru_lambda=lru_lambda,
                  mla_qnorm_g=mla_qnorm_g, mla_wq_up=mla_wq_up,
                  mla_kvnorm_g=mla_kvnorm_g, mla_wkv_up=mla_wkv_up, q_norm_g=q_norm_g,
                  k_norm_g=k_norm_g, gnorm_conv=gnorm_conv, gnorm_lru=gnorm_lru,
                  gnorm_mla=gnorm_mla, w_out=w_out, norm2_g=norm2_g, w_gate=w_gate,
                  w_up=w_up, w_down=w_down)
    batch, seq_p, _ = x_prompt.shape
    dec_batch, seq_s, _ = x_sample.shape
    assert dec_batch == 4 and cache_ckv.shape[2] == 256

    cond8 = jnp.concatenate(
        [c, c_ctx[None, :], jnp.zeros((8 - dec_batch - 1, D_MODEL), F32)], axis=0)
    mods = _ada_call(cond8, ada_w, ada_b)

    rope_c, rope_s = _rope_tables(seq_s)
    cache_kr_pad = jnp.pad(
        cache_krope, ((0, 0), (0, 0), (0, 0), (ROPE_LO, HEAD_PAD - ROPE_LO - QK_ROPE)))

    def chunked_t(tab):
        return jnp.asarray(np.ascontiguousarray(
            tab.reshape(seq_s // ROW_CHUNK, ROW_CHUNK, HEAD_PAD).transpose(0, 2, 1)))

    extra = dict(rope_c=jnp.asarray(rope_c), rope_s=jnp.asarray(rope_s),
                 rope_ct=chunked_t(rope_c), rope_st=chunked_t(rope_s), cache_ckv=cache_ckv,
                 cache_kr=cache_kr_pad, state=state_lru)

    xp = x_prompt.reshape(batch * seq_p, D_MODEL)
    xs = x_sample.reshape(dec_batch * seq_s, D_MODEL)
    caches = None
    lw = _prep_weights(params)
    for l in range(DEPTH):
        (hs,) = _mix_call(xs, mods, l, lw, seq=seq_s, n_seq=1, latent=True, extra=extra)
        hp, *caches, xs = _mix_call(xp, mods, l, lw, seq=seq_p, n_seq=CTX_SEQS_PER_STEP,
                                    latent=False, ffn_h2d=hs, ffn_rows_per_mod=seq_s,
                                    caches=caches)
        xp = _ffn_call(hp, mods, l, lw, rows_per_mod=None, mod_base=4)
    new_cache_ckv, new_cache_krope, new_state_lru = caches
    return (xp.reshape(batch, seq_p, D_MODEL),
            xs.reshape(dec_batch, seq_s, D_MODEL),
            new_cache_ckv, new_cache_krope, new_state_lru)
```

```python
import functools
import itertools
import math

import jax
import jax.numpy as jnp
import numpy as np
from jax import lax
from jax.experimental import pallas as pl
from jax.experimental.pallas import tpu as pltpu

F32 = jnp.float32
BF16 = jnp.bfloat16

D_MODEL = 1024
DEPTH = 2
GRID_W = 64
CONV_W = 256
LRU_W = 256
LRU_BLOCKS = 4
LRU_BLK = 64
LRU_C = 8.0
MLA_HEADS = 8
QK_NOPE = 64
QK_ROPE = 32
V_DIM = 64
V_EXT = V_DIM + 16
QK_DIM = QK_NOPE + QK_ROPE
Q_LORA = 256
KV_LORA = 128
ROPE_THETA = 10000.0
FF = 2816
EPS = 1e-6

LANES = 128
SUBLANES = 8
HEAD_PAD = LANES
VMEM_LIMIT_BYTES = 60 * 1024 * 1024

C_BG, C_CG, C_H = 0, 256, 512
C_XB, C_YB = 768, 1024
C_Q = 1280
C_CKV = 1536
C_KR = 1664
U_W = 1792
SCAN_A = (C_H, C_CKV)
SCAN_B = (C_Q, C_BG)
ROPE_LO = QK_NOPE

def _slots(widths):
    out, off = {}, 0
    for name, width in widths:
        out[name] = (off, width)
        off += width
    return out


VEC_SLOTS = _slots([
    ("norm1_g", D_MODEL), ("conv_b", CONV_W), ("lru_conv_b", LRU_W), ("b_gates", 4 * LRU_W),
    ("mla_qnorm_g", Q_LORA), ("mla_kvnorm_g", KV_LORA), ("kg", HEAD_PAD),
    ("kg_partner", HEAD_PAD), ("gnorm_conv", CONV_W), ("gnorm_lru", LRU_W),
    ("gnorm_mla", MLA_HEADS * V_DIM)])

ROW_CHUNK = 256
SCAN_CHUNK = 32
Q_CHUNK = 256
FF_CHUNK = 256
PROJ_SLAB = 256
PROJ_CHUNKS = 1
CTX_SEQS_PER_STEP = 2
FUSED_FFN_PERIOD = 4
SEQ_STAGGER = 10
FFN_ROWS = 1024
PAD_ROWS = SUBLANES


def _rms_scale(x, n):
    ms = jnp.sum(x * x, axis=-1, keepdims=True) * (1.0 / n)
    return lax.rsqrt(ms + EPS)


def _sigmoid(x):
    return 0.5 * jnp.tanh(0.5 * x) + 0.5


def _gelu_tanh(x):
    c = math.sqrt(2.0 / math.pi)
    return 0.5 * x * (1.0 + jnp.tanh(c * (x + 0.044715 * (x * x * x))))


def _dot(a, b):
    return jnp.dot(a, b, preferred_element_type=F32)


def _interleave(*gens):
    live = list(gens)
    while live:
        for g in list(live):
            try:
                next(g)
            except StopIteration:
                live.remove(g)


def _interleave_staggered(gens, lag, side=None, side_period=1):
    pending = list(gens)
    live = []
    tick = 0
    while live or pending or side is not None:
        if pending and tick % lag == 0:
            live.append(pending.pop(0))
        if side is not None and (tick % side_period == 0 or not (live or pending)):
            try:
                next(side)
            except StopIteration:
                side = None
        for g in list(live):
            try:
                next(g)
            except StopIteration:
                live.remove(g)
        tick += 1


def _dot_nt(a, b):
    return lax.dot_general(a, b, (((1,), (1,)), ((), ())), preferred_element_type=F32)


ADA_TN = 1536


def _ada_kernel(cond_ref, w_ref, b_ref, o_ref):
    cnd = cond_ref[...]
    s = (cnd * _sigmoid(cnd)).astype(BF16)
    o_ref[:, 0, :] = _dot(s, w_ref[...].astype(BF16)) + b_ref[...]


def _ada_call(cond8, ada_w, ada_b):
    n_out = 6 * D_MODEL
    return pl.pallas_call(
        _ada_kernel,
        grid=(DEPTH, n_out // ADA_TN),
        in_specs=[
            pl.BlockSpec((8, D_MODEL), lambda l, j: (0, 0)),
            pl.BlockSpec((None, D_MODEL, ADA_TN), lambda l, j: (l, 0, j)),
            pl.BlockSpec((None, 1, ADA_TN), lambda l, j: (l, 0, j)),
        ],
        out_specs=pl.BlockSpec((None, 8, 1, ADA_TN), lambda l, j: (l, 0, 0, j)),
        out_shape=jax.ShapeDtypeStruct((DEPTH, 8, 1, n_out), F32),
        compiler_params=pltpu.CompilerParams(
            dimension_semantics=("arbitrary", "arbitrary"),
            vmem_limit_bytes=VMEM_LIMIT_BYTES),
        name="ada_mod",
    )(cond8, ada_w, ada_b.reshape(DEPTH, 1, n_out))


def _mix_kernel(*refs, seq, n_seq, latent, fused_ffn, n_aliased):
    T = seq * n_seq
    n_ctx = 256 if latent else 0
    (x_ref, mod_ref, vec_ref, win_ref, wlast_ref, convw_ref, lcw_ref, wg_ref, lam_ref,
     wqt_ref, wk_ref, wvt_ref, qgt_ref, wout_ref) = refs[:14]
    pos = 14
    if latent:
        (wqpt_ref, qgpt_ref, ropect_ref, ropest_ref, ropec_ref, ropes_ref,
         cckv_ref, ckr_ref, h0_ref) = refs[pos:pos + 9]
        pos += 9

    def vec(name):
        off, width = VEC_SLOTS[name]
        return vec_ref[:, off:off + width]
    if fused_ffn:
        ffn_in_refs = refs[pos:pos + 6]
        pos += 6
    pos += n_aliased
    h_out_ref = refs[pos]
    pos += 1
    if not latent:
        ckv_out_ref, kr_out_ref, st_out_ref = refs[pos:pos + 3]
        pos += 3
    if fused_ffn:
        ffn_out_ref = refs[pos]
        pos += 1
    u_scr, qt_scr, k_scr, vt_scr, ymt_scr, ycat_scr = refs[pos:pos + 6]
    if latent:
        st_scr, m_scr = refs[pos + 6:pos + 8]

    n_chunks = T // ROW_CHUNK
    lane = lax.broadcasted_iota(jnp.int32, (1, LANES), 1)
    rope_lanes = (lane >= ROPE_LO) & (lane < ROPE_LO + QK_ROPE)

    sh1 = mod_ref[:, 0:D_MODEL]
    sc1 = mod_ref[:, D_MODEL:2 * D_MODEL]
    g1 = mod_ref[:, 2 * D_MODEL:3 * D_MODEL]
    gain1 = vec("norm1_g") * (1.0 + sc1)

    u_scr[0:PAD_ROWS, :] = jnp.zeros((PAD_ROWS, U_W), F32)
    u_scr[PAD_ROWS + T:2 * PAD_ROWS + T, :] = jnp.zeros((PAD_ROWS, U_W), F32)

    def phase_a(c, n=1):
        r0 = c * ROW_CHUNK
        nrows = n * ROW_CHUNK
        x = x_ref[r0:r0 + nrows, :]
        hn = (x * _rms_scale(x, D_MODEL) * gain1 + sh1).astype(BF16)
        yield
        urows = slice(PAD_ROWS + r0, PAD_ROWS + r0 + nrows)
        for s0 in range(0, C_CKV, PROJ_SLAB):
            u_scr[urows, s0:s0 + PROJ_SLAB] = _dot(hn, win_ref[:, s0:s0 + PROJ_SLAB])
            yield
        u_scr[urows, C_CKV:U_W] = _dot(hn, wlast_ref[...])
        yield
        u_scr[urows, C_CG:C_CG + CONV_W] = (
            u_scr[urows, C_CG:C_CG + CONV_W] * u_scr[urows, C_H:C_H + CONV_W])
        yield

    vrow = lax.broadcasted_iota(jnp.int32, (MLA_HEADS * V_EXT, 1), 0)
    ones_rows = (vrow % V_EXT) >= V_DIM

    def values_t(cb):
        return jnp.where(ones_rows, 1.0, _dot_nt(wvt_ref[...], cb)).astype(BF16)

    def phase_b(c):
        r0 = c * ROW_CHUNK
        rows = slice(PAD_ROWS + r0, PAD_ROWS + r0 + ROW_CHUNK)
        uq = u_scr[rows, C_Q:C_Q + Q_LORA]
        qn = (uq * _rms_scale(uq, Q_LORA) * vec("mla_qnorm_g")).astype(BF16)
        qa_t = _dot_nt(wqt_ref[...], qn)
        if latent:
            qpa_t = _dot_nt(wqpt_ref[...], qn)
            q_tc = ropect_ref[c] * qgt_ref[...]
            q_ts = ropest_ref[c] * qgpt_ref[...]
        yield
        for h in range(MLA_HEADS):
            sl = slice(h * HEAD_PAD, (h + 1) * HEAD_PAD)
            qh = qa_t[sl, :]
            rq = lax.rsqrt(jnp.sum(qh * qh, axis=0, keepdims=True) * (1.0 / QK_DIM) + EPS)
            if latent:
                qh = (qh * q_tc + qpa_t[sl, :] * q_ts) * rq
            else:
                qh = qh * qgt_ref[...] * rq
            qt_scr[h, c] = qh.astype(BF16)
            if h % 2:
                yield
        uc = u_scr[rows, C_CKV:C_CKV + KV_LORA]
        ckv = uc * _rms_scale(uc, KV_LORA) * vec("mla_kvnorm_g")
        krb = u_scr[rows, C_KR:C_KR + LANES]
        kr_rolled = pltpu.roll(krb, 64, 1)
        if not latent:
            sq, rq0 = divmod(r0, seq)
            ckv_out_ref[sq, rq0:rq0 + ROW_CHUNK, :] = ckv
            kr_out_ref[sq, rq0:rq0 + ROW_CHUNK, :] = kr_rolled[:, 0:QK_ROPE]
        krm = jnp.where(rope_lanes, krb, 0.0)
        cb = ckv.astype(BF16)
        ka = _dot(cb, wk_ref[...])
        vt_scr[c] = values_t(cb)
        if latent:
            k_tc = ropec_ref[r0:r0 + ROW_CHUNK, :] * vec("kg")
            k_ts = ropes_ref[r0:r0 + ROW_CHUNK, :] * vec("kg_partner")
        yield
        for h in range(MLA_HEADS):
            sl = slice(h * HEAD_PAD, (h + 1) * HEAD_PAD)
            kpre = ka[:, sl] + krm
            rk = _rms_scale(kpre, QK_DIM)
            if latent:
                kh = (kpre * k_tc + kr_rolled * k_ts) * rk
            else:
                kh = kpre * vec("kg") * rk
            k_scr[h, r0:r0 + ROW_CHUNK, :] = kh.astype(BF16)
            if h % 2:
                yield

    if latent:
        cc = cckv_ref[...].astype(BF16)
        ka = _dot(cc, wk_ref[...])
        vt_scr[n_chunks] = values_t(cc)
        krc = ckr_ref[...]
        for h in range(MLA_HEADS):
            kpre = ka[:, h * HEAD_PAD:(h + 1) * HEAD_PAD] + krc
            kh = kpre * vec("kg") * _rms_scale(kpre, QK_DIM)
            k_scr[h, seq:seq + n_ctx, :] = kh.astype(BF16)

    neg_lam = -lam_ref[...]
    sp = jnp.maximum(neg_lam, 0.0) + jnp.log(1.0 + jnp.exp(-jnp.abs(neg_lam)))
    log2a_coef = (-0.5 * LRU_C * math.log2(math.e)) * sp
    row_rc = lax.broadcasted_iota(jnp.int32, (ROW_CHUNK, 1), 0)

    def phase_c(ci):
        r0 = ci * ROW_CHUNK
        base = PAD_ROWS + r0
        first = (r0 % seq) == 0
        last = ((r0 + ROW_CHUNK) % seq) == 0

        def win(col, shift, width=CONV_W):
            w = u_scr[base + shift:base + shift + ROW_CHUNK, col:col + width]
            if shift < 0 and first:
                w = jnp.where(row_rc < -shift, 0.0, w)
            if shift > 0 and last:
                w = jnp.where(row_rc >= ROW_CHUNK - shift, 0.0, w)
            return w

        z_m, z_0, z_p = win(C_CG, -1), win(C_CG, 0), win(C_CG, 1)
        conv = (z_m * convw_ref[0:1, :] + z_0 * convw_ref[1:2, :]
                + z_p * convw_ref[2:3, :] + vec("conv_b"))
        yc = u_scr[base:base + ROW_CHUNK, C_BG:C_BG + CONV_W] * conv
        ycn = yc * _rms_scale(yc, CONV_W) * vec("gnorm_conv")
        ycat_scr[r0:r0 + ROW_CHUNK, 0:CONV_W] = ycn.astype(BF16)
        yield

        xc = (win(C_XB, -2) * lcw_ref[0:1, :] + win(C_XB, -1) * lcw_ref[1:2, :]
              + win(C_XB, 0) * lcw_ref[2:3, :] + win(C_XB, 1) * lcw_ref[3:4, :]
              + vec("lru_conv_b"))
        half_gates = _dot(xc.astype(BF16), wg_ref[...]) + vec("b_gates")
        half_xc = 0.5 * xc
        yield
        for d in range(2):
            t_r = jnp.tanh(half_gates[:, (2 * d) * LRU_W:(2 * d + 1) * LRU_W])
            t_i = jnp.tanh(half_gates[:, (2 * d + 1) * LRU_W:(2 * d + 2) * LRU_W])
            a = jnp.exp2(log2a_coef[d:d + 1, :] * (t_r + 1.0))
            v = 1.0 - a * a
            mult = v * lax.rsqrt(jnp.maximum(v, 1e-30))
            u_scr[base:base + ROW_CHUNK, SCAN_A[d]:SCAN_A[d] + LRU_W] = a
            u_scr[base:base + ROW_CHUNK, SCAN_B[d]:SCAN_B[d] + LRU_W] = (
                mult * (half_xc * (t_i + 1.0)))
            yield

    n_sc = seq // SCAN_CHUNK
    row_sc = lax.broadcasted_iota(jnp.int32, (SCAN_CHUNK, LRU_W), 0)
    steps = [1 << i for i in range(int(math.log2(SCAN_CHUNK)))]

    def scan_step(rf, rb, cf, cb_):
        rf = pl.multiple_of(rf + PAD_ROWS, SUBLANES)
        rb = pl.multiple_of(rb + PAD_ROWS, SUBLANES)
        a = u_scr[pl.ds(rf, SCAN_CHUNK), SCAN_A[0]:SCAN_A[0] + LRU_W]
        b = u_scr[pl.ds(rf, SCAN_CHUNK), SCAN_B[0]:SCAN_B[0] + LRU_W]
        b = b + jnp.where(row_sc == 0, a * cf, 0.0)
        for d in steps:
            if d % SUBLANES == 0:
                b = jnp.concatenate([b[:d], b[d:] + a[d:] * b[:-d]], axis=0)
                if d != steps[-1]:
                    a = jnp.concatenate([a[:d], a[d:] * a[:-d]], axis=0)
                continue
            b = b + a * jnp.where(row_sc >= d, pltpu.roll(b, d, 0), 0.0)
            if d != steps[-1]:
                a = a * jnp.where(row_sc >= d, pltpu.roll(a, d, 0), 1.0)
        u_scr[pl.ds(rf, SCAN_CHUNK), SCAN_B[0]:SCAN_B[0] + LRU_W] = b
        cf = b[SCAN_CHUNK - 1:SCAN_CHUNK, :]
        a = u_scr[pl.ds(rb, SCAN_CHUNK), SCAN_A[1]:SCAN_A[1] + LRU_W]
        b = u_scr[pl.ds(rb, SCAN_CHUNK), SCAN_B[1]:SCAN_B[1] + LRU_W]
        b = b + jnp.where(row_sc == SCAN_CHUNK - 1, a * cb_, 0.0)
        for d in steps:
            if d % SUBLANES == 0:
                b = jnp.concatenate([b[:-d] + a[:-d] * b[d:], b[-d:]], axis=0)
                if d != steps[-1]:
                    a = jnp.concatenate([a[:-d] * a[d:], a[-d:]], axis=0)
                continue
            keep = row_sc < SCAN_CHUNK - d
            b = b + a * jnp.where(keep, pltpu.roll(b, SCAN_CHUNK - d, 0), 0.0)
            if d != steps[-1]:
                a = a * jnp.where(keep, pltpu.roll(a, SCAN_CHUNK - d, 0), 1.0)
        u_scr[pl.ds(rb, SCAN_CHUNK), SCAN_B[1]:SCAN_B[1] + LRU_W] = b
        cb_ = b[0:1, :]
        return cf, cb_

    cps = seq // ROW_CHUNK

    def softmax_pv(st, vts):
        m = jnp.max(st, axis=0, keepdims=True)
        pt = jnp.exp2(st - m).astype(BF16)
        ot = None
        for i, vt in enumerate(vts):
            part = _dot(vt, pt[i * ROW_CHUNK:(i + 1) * ROW_CHUNK, :])
            ot = part if ot is None else ot + part
        return ot[0:V_DIM, :] * (1.0 / ot[V_DIM:V_DIM + 1, :])

    def f_build(c):
        r0 = c * ROW_CHUNK
        rows = slice(r0, r0 + ROW_CHUNK)
        urows = slice(PAD_ROWS + r0, PAD_ROWS + r0 + ROW_CHUNK)
        hf = u_scr[urows, SCAN_B[0]:SCAN_B[0] + LRU_W]
        hb = u_scr[urows, SCAN_B[1]:SCAN_B[1] + LRU_W]
        yb = u_scr[urows, C_YB:C_YB + LRU_W]
        yl = (hf + hb) * _gelu_tanh(yb)
        yln = yl * _rms_scale(yl, LRU_W) * vec("gnorm_lru")
        ycat_scr[rows, CONV_W:CONV_W + LRU_W] = yln.astype(BF16)
        yield
        ymt = ymt_scr[c]
        ssq = jnp.sum(ymt * ymt, axis=0, keepdims=True)
        ynt = ymt * lax.rsqrt(ssq * (1.0 / (MLA_HEADS * V_DIM)) + EPS)
        ycat_scr[rows, CONV_W + LRU_W:D_MODEL] = (ynt.T * vec("gnorm_mla")).astype(BF16)
        yield

    def f_project(c):
        rows = slice(c * ROW_CHUNK, (c + 1) * ROW_CHUNK)
        for n0 in range(0, D_MODEL, PROJ_SLAB):
            cols = slice(n0, n0 + PROJ_SLAB)
            y = _dot(ycat_scr[rows, :], wout_ref[:, cols])
            h_out_ref[rows, cols] = x_ref[rows, cols] + g1[:, cols] * y
            yield

    if latent:
        groups = [list(range(g0, g0 + PROJ_CHUNKS)) for g0 in range(0, n_chunks, PROJ_CHUNKS)]
        _interleave(phase_a(0, PROJ_CHUNKS))
        for gi, group in enumerate(groups):
            dependent = []
            for c in group:
                dependent.append(phase_b(c))
                if c:
                    dependent.append(phase_c(c - 1))
            ahead = [phase_a(groups[gi + 1][0], PROJ_CHUNKS)] if gi + 1 < len(groups) else []
            _interleave(*ahead, itertools.chain(*dependent))
        _interleave(phase_c(n_chunks - 1))

        key_chunks = list(range(cps)) + [n_chunks]
        n_blk = cps * (MLA_HEADS // 2)
        trips = n_blk // 2
        scans_per_trip = n_sc // trips

        def heads_of(idx):
            qc = idx // (MLA_HEADS // 2)
            j = idx % (MLA_HEADS // 2)
            return qc, [2 * j, 2 * j + 1]

        def scores(idx, slot):
            qc, hs = heads_of(idx)
            for par, h in enumerate(hs):
                qt = qt_scr[h, qc]
                m = None
                for i in range(len(key_chunks)):
                    rows = slice(i * ROW_CHUNK, (i + 1) * ROW_CHUNK)
                    st = _dot(k_scr[h, rows, :], qt)
                    st_scr[slot, par, rows, :] = st
                    mc = jnp.max(st, axis=0, keepdims=True)
                    m = mc if m is None else jnp.maximum(m, mc)
                    yield
                m_scr[slot, par] = m

        def finish(idx, slot):
            qc, hs = heads_of(idx)
            for par, h in enumerate(hs):
                hr = pl.ds(pl.multiple_of(h * V_DIM, V_DIM), V_DIM)
                vr = pl.ds(pl.multiple_of(h * V_EXT, V_EXT - V_DIM), V_EXT)
                m = m_scr[slot, par]
                ot = None
                for i, kc in enumerate(key_chunks):
                    rows = slice(i * ROW_CHUNK, (i + 1) * ROW_CHUNK)
                    pt = jnp.exp2(st_scr[slot, par, rows, :] - m).astype(BF16)
                    part = _dot(vt_scr[kc, vr, :], pt)
                    ot = part if ot is None else ot + part
                    yield
                ymt_scr[qc, hr, :] = ot[0:V_DIM, :] * (1.0 / ot[V_DIM:V_DIM + 1, :])

        _interleave(scores(0, 0))

        def att_body(i, carry):
            b0 = 2 * i
            state = {"carry": carry, "c": i * scans_per_trip}

            def scans(n):
                for _ in range(n):
                    c = state["c"]
                    rf = pl.multiple_of(c * SCAN_CHUNK, SCAN_CHUNK)
                    rb = pl.multiple_of((n_sc - 1 - c) * SCAN_CHUNK, SCAN_CHUNK)
                    state["carry"] = scan_step(rf, rb, *state["carry"])
                    state["c"] = c + 1
                    yield
                    yield
                    yield

            half = scans_per_trip // 2
            _interleave(scores(b0 + 1, 1), finish(b0, 0), scans(half))
            _interleave(scores(jnp.minimum(b0 + 2, n_blk - 1), 0), finish(b0 + 1, 1),
                        scans(scans_per_trip - half))
            return state["carry"]

        lax.fori_loop(0, trips, att_body, (h0_ref[0:1, :], h0_ref[1:2, :]))

        _interleave(f_build(0))
        for c in range(n_chunks):
            ahead = [f_build(c + 1)] if c + 1 < n_chunks else []
            _interleave(f_project(c), *ahead)
    else:
        def phase_d(s):
            carry = (jnp.zeros((1, LRU_W), F32), jnp.zeros((1, LRU_W), F32))
            for c in range(n_sc):
                carry = scan_step(s * seq + c * SCAN_CHUNK,
                                  s * seq + (n_sc - 1 - c) * SCAN_CHUNK, *carry)
                yield
            st_out_ref[s, 0:1, :] = carry[0]
            st_out_ref[s, 1:2, :] = carry[1]

        def phase_e(s):
            blocks = [(qi, h) for qi in range(cps) for h in range(MLA_HEADS)]

            def scores(blk):
                qi, h = blk
                return _dot(k_scr[h, s * seq:(s + 1) * seq, :], qt_scr[h, s * cps + qi])

            st = scores(blocks[0])
            for i, (qi, h) in enumerate(blocks):
                nxt = scores(blocks[i + 1]) if i + 1 < len(blocks) else None
                hr = slice(h * V_DIM, (h + 1) * V_DIM)
                vr = slice(h * V_EXT, (h + 1) * V_EXT)
                vts = [vt_scr[s * cps + j, vr, :] for j in range(cps)]
                ymt_scr[s * cps + qi, hr, :] = softmax_pv(st, vts)
                st = nxt
                yield

        def sequence(s):
            chunks = range(s * cps, (s + 1) * cps)
            for c in chunks:
                yield from phase_a(c)
            for c in chunks:
                yield from phase_b(c)
            for c in chunks:
                yield from phase_c(c)
            yield from phase_d(s)
            yield from phase_e(s)
            for c in chunks:
                yield from f_build(c)
                yield from f_project(c)

        side = _ffn_steps(*ffn_in_refs, ffn_out_ref) if fused_ffn else None
        _interleave_staggered([sequence(s) for s in range(n_seq)], SEQ_STAGGER,
                              side=side, side_period=FUSED_FFN_PERIOD)


def _const_spec(shape):
    nd = len(shape)
    return pl.BlockSpec(shape, lambda i, _n=nd: (0,) * _n,
                        pipeline_mode=pl.Buffered(1))


def _layer_spec(arr, layer):
    nd = arr.ndim - 1
    return pl.BlockSpec((None,) + arr.shape[1:], lambda i, _n=nd: (layer,) + (0,) * _n,
                        pipeline_mode=pl.Buffered(1))


def _mix_call(x2d, mods, layer, lw, *, seq, n_seq, latent, extra=None, ffn_h2d=None,
              ffn_rows_per_mod=None, ffn_w=None, caches=None):
    n_tok = x2d.shape[0]
    T = seq * n_seq
    grid = (n_tok // T,)
    fused_ffn = ffn_h2d is not None
    n_keys_buf = T if not latent else seq + 256

    if latent:
        mod_map = lambda i: (layer, i, 0, 0)
    else:
        mod_map = lambda i: (layer, 4, 0, 0)

    consts = [lw["vecs"], lw["w_in"], lw["w_in_last"], lw["conv_w"], lw["lru_conv_w"],
              lw["w_gates"], lw["lru_lambda"], lw["wq_t"], lw["wk"], lw["wv_t"], lw["qg_t"],
              lw["w_out"]]
    args = [x2d, mods] + consts
    in_specs = [pl.BlockSpec((T, D_MODEL), lambda i: (i, 0)),
                pl.BlockSpec((None, None, 1, 6 * D_MODEL), mod_map)]
    in_specs += [_layer_spec(a, layer) for a in consts]
    if latent:
        lconsts = [lw["wq_partner_t"], lw["qg_partner_t"]]
        tables = [extra["rope_ct"], extra["rope_st"], extra["rope_c"], extra["rope_s"]]
        args += lconsts + tables
        in_specs += [_layer_spec(a, layer) for a in lconsts]
        in_specs += [_const_spec(a.shape) for a in tables]
        args += [extra["cache_ckv"], extra["cache_kr"], extra["state"]]
        in_specs += [
            pl.BlockSpec((None, None, 256, KV_LORA), lambda i: (i, layer, 0, 0)),
            pl.BlockSpec((None, None, 256, LANES), lambda i: (i, layer, 0, 0)),
            pl.BlockSpec((None, None, 2, LRU_W), lambda i: (i, layer, 0, 0)),
        ]

    if fused_ffn:
        ffn_rows = ffn_h2d.shape[0] // grid[0]
        tiles_per_mod = ffn_rows_per_mod // ffn_rows
        args += [ffn_h2d, mods, lw["norm2_g"]] + list(ffn_w)
        in_specs += [pl.BlockSpec((ffn_rows, D_MODEL), lambda i: (i, 0)),
                     pl.BlockSpec((None, None, 1, 6 * D_MODEL),
                                  lambda i: (layer, i // tiles_per_mod, 0, 0))]
        in_specs += [_layer_spec(lw["norm2_g"], layer)] + [_layer_spec(a, 0) for a in ffn_w]

    out_shape = [jax.ShapeDtypeStruct((n_tok, D_MODEL), F32)]
    out_specs = [pl.BlockSpec((T, D_MODEL), lambda i: (i, 0))]
    if not latent:
        n_b = n_tok // seq
        out_shape += [jax.ShapeDtypeStruct((n_b, DEPTH, seq, KV_LORA), F32),
                      jax.ShapeDtypeStruct((n_b, DEPTH, seq, QK_ROPE), F32),
                      jax.ShapeDtypeStruct((n_b, DEPTH, 2, LRU_W), F32)]
        out_specs += [pl.BlockSpec((n_seq, None, seq, KV_LORA), lambda i: (i, layer, 0, 0)),
                      pl.BlockSpec((n_seq, None, seq, QK_ROPE), lambda i: (i, layer, 0, 0)),
                      pl.BlockSpec((n_seq, None, 2, LRU_W), lambda i: (i, layer, 0, 0))]
    aliases = {}
    if caches is not None:
        for k, arr in enumerate(caches):
            aliases[len(args)] = 1 + k
            args.append(arr)
            in_specs.append(pl.BlockSpec(memory_space=pl.ANY))
    if fused_ffn:
        out_shape.append(jax.ShapeDtypeStruct(ffn_h2d.shape, F32))
        out_specs.append(pl.BlockSpec((ffn_rows, D_MODEL), lambda i: (i, 0)))

    scratch = [
        pltpu.VMEM((T + 2 * PAD_ROWS, U_W), F32),
        pltpu.VMEM((MLA_HEADS, T // ROW_CHUNK, HEAD_PAD, ROW_CHUNK), BF16),
        pltpu.VMEM((MLA_HEADS, n_keys_buf, HEAD_PAD), BF16),
        pltpu.VMEM((n_keys_buf // ROW_CHUNK, MLA_HEADS * V_EXT, ROW_CHUNK), BF16),
        pltpu.VMEM((T // ROW_CHUNK, MLA_HEADS * V_DIM, ROW_CHUNK), F32),
        pltpu.VMEM((T, D_MODEL), BF16),
    ]
    if latent:
        scratch.append(pltpu.VMEM((2, 2, n_keys_buf, ROW_CHUNK), F32))
        scratch.append(pltpu.VMEM((2, 2, 1, ROW_CHUNK), F32))
    return pl.pallas_call(
        functools.partial(_mix_kernel, seq=seq, n_seq=n_seq, latent=latent,
                          fused_ffn=fused_ffn, n_aliased=len(aliases)),
        grid=grid,
        in_specs=in_specs,
        out_specs=out_specs,
        out_shape=out_shape,
        scratch_shapes=scratch,
        input_output_aliases=aliases,
        compiler_params=pltpu.CompilerParams(
            dimension_semantics=("arbitrary",),
            vmem_limit_bytes=VMEM_LIMIT_BYTES),
        name="mix_latent" if latent else ("mix_context_swiglu" if fused_ffn else "mix_context"),
    )(*args)


def _ffn_steps(h_ref, mod_ref, n2g_ref, wg_ref, wu_ref, wd_ref, o_ref):
    sh2 = mod_ref[:, 3 * D_MODEL:4 * D_MODEL]
    sc2 = mod_ref[:, 4 * D_MODEL:5 * D_MODEL]
    g2 = mod_ref[:, 5 * D_MODEL:6 * D_MODEL]
    h = h_ref[...]
    hn = (h * _rms_scale(h, D_MODEL) * (n2g_ref[...] * (1.0 + sc2)) + sh2).astype(BF16)
    yield
    acc = None
    for c0 in range(0, FF, FF_CHUNK):
        g = _dot(hn, wg_ref[:, c0:c0 + FF_CHUNK])
        u = _dot(hn, wu_ref[:, c0:c0 + FF_CHUNK])
        act = (g * _sigmoid(g) * u).astype(BF16)
        part = _dot(act, wd_ref[c0:c0 + FF_CHUNK, :])
        acc = part if acc is None else acc + part
        yield
    o_ref[...] = h_ref[...] + g2 * acc
    yield


def _ffn_kernel(*refs, n_cast):
    ffn_in, cast_in = refs[:6], refs[6:6 + n_cast]
    o_ref, cast_out = refs[6 + n_cast], refs[7 + n_cast:]
    for src, dst in zip(cast_in, cast_out):
        dst[...] = src[...].astype(BF16)
    _interleave(_ffn_steps(*ffn_in, o_ref))


def _ffn_call(h2d, mods, layer, lw, ffn_w, *, rows_per_mod, mod_base, cast_next=None):
    n_tok = h2d.shape[0]
    n_steps = n_tok // FFN_ROWS
    if mod_base is None:
        tiles_per_mod = rows_per_mod // FFN_ROWS
        mod_map = lambda i: (layer, i // tiles_per_mod, 0, 0)
    else:
        mod_map = lambda i: (layer, mod_base, 0, 0)
    args = [h2d, mods, lw["norm2_g"]] + list(ffn_w)
    in_specs = [pl.BlockSpec((FFN_ROWS, D_MODEL), lambda i: (i, 0)),
                pl.BlockSpec((None, None, 1, 6 * D_MODEL), mod_map),
                _layer_spec(lw["norm2_g"], layer)]
    in_specs += [_layer_spec(a, 0) for a in ffn_w]
    out_shape = [jax.ShapeDtypeStruct((n_tok, D_MODEL), F32)]
    out_specs = [pl.BlockSpec((FFN_ROWS, D_MODEL), lambda i: (i, 0))]
    for w in cast_next or ():
        rows = w.shape[1] // n_steps
        assert rows * n_steps == w.shape[1] and rows % 16 == 0
        args.append(w)
        in_specs.append(pl.BlockSpec((None, rows, w.shape[2]), lambda i: (layer + 1, i, 0)))
        out_shape.append(jax.ShapeDtypeStruct((1,) + w.shape[1:], BF16))
        out_specs.append(pl.BlockSpec((None, rows, w.shape[2]), lambda i: (0, i, 0)))
    return pl.pallas_call(
        functools.partial(_ffn_kernel, n_cast=len(cast_next or ())),
        grid=(n_steps,),
        in_specs=in_specs,
        out_specs=out_specs,
        out_shape=out_shape,
        compiler_params=pltpu.CompilerParams(
            dimension_semantics=("arbitrary",),
            vmem_limit_bytes=VMEM_LIMIT_BYTES),
        name="swiglu",
    )(*args)


def _rope_partner_index():
    return [d + 8 if (d % 16) < 8 else d - 8 for d in range(QK_ROPE)]


def _prep_weights(p):
    partner = jnp.array(_rope_partner_index(), jnp.int32)
    q_scale = QK_DIM ** -0.5 * math.log2(math.e)
    w_in = p["w_in"][:, :, :C_CKV].astype(BF16)
    w_tail = p["w_in"][:, :, C_CKV:].astype(BF16)
    w_kr = w_tail[:, :, KV_LORA:KV_LORA + QK_ROPE]
    zeros32 = jnp.zeros((DEPTH, D_MODEL, QK_ROPE), BF16)
    w_in_last = jnp.concatenate(
        [w_tail[:, :, :KV_LORA], w_kr[:, :, partner], zeros32, w_kr, zeros32], axis=2)

    wq_h = p["mla_wq_up"].reshape(DEPTH, Q_LORA, MLA_HEADS, QK_DIM)
    wq_pad = jnp.pad(wq_h, ((0, 0), (0, 0), (0, 0), (0, HEAD_PAD - QK_DIM)))
    wq_t = wq_pad.reshape(DEPTH, Q_LORA, MLA_HEADS * HEAD_PAD).transpose(0, 2, 1)
    rope_pad = ((0, 0), (0, 0), (0, 0), (ROPE_LO, HEAD_PAD - ROPE_LO - QK_ROPE))
    wq_partner = jnp.pad(wq_h[:, :, :, QK_NOPE:][:, :, :, partner], rope_pad)
    wq_partner_t = wq_partner.reshape(DEPTH, Q_LORA, MLA_HEADS * HEAD_PAD).transpose(0, 2, 1)

    wkv = p["mla_wkv_up"].reshape(DEPTH, KV_LORA, MLA_HEADS, QK_NOPE + V_DIM)
    wk_pad = jnp.pad(wkv[:, :, :, :QK_NOPE], ((0, 0), (0, 0), (0, 0), (0, HEAD_PAD - QK_NOPE)))
    wk_pad = wk_pad.reshape(DEPTH, KV_LORA, MLA_HEADS * HEAD_PAD)
    wv_ext = jnp.pad(wkv[:, :, :, QK_NOPE:], ((0, 0), (0, 0), (0, 0), (0, V_EXT - V_DIM)))
    wv_t = wv_ext.reshape(DEPTH, KV_LORA, MLA_HEADS * V_EXT).transpose(0, 2, 1)

    def pad_gain(g):
        return jnp.pad(g, ((0, 0), (0, HEAD_PAD - QK_DIM)))[:, None, :]

    def partner_gain(g):
        gp = g[:, QK_NOPE:][:, partner]
        return jnp.pad(gp, ((0, 0), (ROPE_LO, HEAD_PAD - ROPE_LO - QK_ROPE)))[:, None, :]

    def on_sublanes(g):
        return jnp.broadcast_to(g.transpose(0, 2, 1), (DEPTH, HEAD_PAD, ROW_CHUNK))

    qg, kg = p["q_norm_g"], p["k_norm_g"]
    eye = jnp.eye(LRU_BLOCKS, dtype=F32)
    gate_w = jnp.stack([p["lru_wa"][:, 0], p["lru_wi"][:, 0],
                        p["lru_wa"][:, 1], p["lru_wi"][:, 1]], axis=1)
    w_gates = (gate_w[:, :, :, :, None, :] * eye[None, None, :, None, :, None])
    w_gates = w_gates.transpose(0, 2, 3, 1, 4, 5).reshape(DEPTH, LRU_W, 4 * LRU_W)
    b_gates = jnp.stack([p["lru_ba"][:, 0], p["lru_bi"][:, 0],
                         p["lru_ba"][:, 1], p["lru_bi"][:, 1]], axis=1)

    def row(v):
        return v[:, None, :]

    rows = {
        "norm1_g": p["norm1_g"], "conv_b": p["conv_b"], "lru_conv_b": p["lru_conv_b"],
        "b_gates": 0.5 * b_gates.reshape(DEPTH, 4 * LRU_W),
        "mla_qnorm_g": p["mla_qnorm_g"], "mla_kvnorm_g": p["mla_kvnorm_g"],
        "kg": pad_gain(kg)[:, 0, :], "kg_partner": partner_gain(kg)[:, 0, :],
        "gnorm_conv": p["gnorm_conv"], "gnorm_lru": p["gnorm_lru"],
        "gnorm_mla": p["gnorm_mla"],
    }
    for name, (_, width) in VEC_SLOTS.items():
        assert rows[name].shape == (DEPTH, width), name
    vecs = row(jnp.concatenate([rows[name] for name in VEC_SLOTS], axis=1))

    return {
        "vecs": vecs,
        "w_in": w_in,
        "w_in_last": w_in_last,
        "conv_w": p["conv_w"],
        "lru_conv_w": p["lru_conv_w"],
        "w_gates": (0.5 * w_gates).astype(BF16),
        "lru_lambda": p["lru_lambda"],
        "wq_t": wq_t.astype(BF16),
        "wq_partner_t": wq_partner_t.astype(BF16),
        "wk": wk_pad.astype(BF16),
        "wv_t": wv_t.astype(BF16),
        "qg_t": on_sublanes(pad_gain(qg) * q_scale),
        "qg_partner_t": on_sublanes(partner_gain(qg) * q_scale),
        "w_out": p["w_out"].astype(BF16),
        "norm2_g": row(p["norm2_g"]),
        "ffn0": [p[k][0:1].astype(BF16) for k in ("w_gate", "w_up", "w_down")],
    }


def _rope_tables(n_tokens):
    n_rows = n_tokens // GRID_W
    row = np.repeat(np.arange(n_rows, dtype=np.float32), GRID_W)
    col = np.tile(np.arange(GRID_W, dtype=np.float32), n_rows)
    n_freq = QK_ROPE // 4
    inv_freq = np.power(np.float32(ROPE_THETA),
                        -np.arange(n_freq, dtype=np.float32) / np.float32(n_freq))
    ang_r = (row[:, None] * inv_freq).astype(np.float32)
    ang_c = (col[:, None] * inv_freq).astype(np.float32)
    cos = np.concatenate([np.cos(ang_r), np.cos(ang_r), np.cos(ang_c), np.cos(ang_c)], 1)
    sin = np.concatenate([-np.sin(ang_r), np.sin(ang_r), -np.sin(ang_c), np.sin(ang_c)], 1)
    ones = np.ones((n_tokens, QK_NOPE), np.float32)
    tail = np.zeros((n_tokens, HEAD_PAD - QK_DIM), np.float32)
    rope_c = np.concatenate([ones, cos, tail], axis=1).astype(np.float32)
    rope_s = np.concatenate([np.zeros((n_tokens, QK_NOPE), np.float32), sin, tail],
                            axis=1).astype(np.float32)
    return rope_c, rope_s


def kernel(x_prompt, x_sample, cache_ckv, cache_krope, state_lru, c, c_ctx, norm1_g, ada_w, ada_b, w_in, conv_w, conv_b, lru_conv_w, lru_conv_b, lru_wa, lru_ba, lru_wi, lru_bi, lru_lambda, mla_qnorm_g, mla_wq_up, mla_kvnorm_g, mla_wkv_up, q_norm_g, k_norm_g, gnorm_conv, gnorm_lru, gnorm_mla, w_out, norm2_g, w_gate, w_up, w_down):
    params = dict(norm1_g=norm1_g, w_in=w_in, conv_w=conv_w, conv_b=conv_b,
                  lru_conv_w=lru_conv_w, lru_conv_b=lru_conv_b, lru_wa=lru_wa,
                  lru_ba=lru_ba, lru_wi=lru_wi, lru_bi=lru_bi, lru_lambda=lru_lambda,
                  mla_qnorm_g=mla_qnorm_g, mla_wq_up=mla_wq_up,
                  mla_kvnorm_g=mla_kvnorm_g, mla_wkv_up=mla_wkv_up, q_norm_g=q_norm_g,
                  k_norm_g=k_norm_g, gnorm_conv=gnorm_conv, gnorm_lru=gnorm_lru,
                  gnorm_mla=gnorm_mla, w_out=w_out, norm2_g=norm2_g, w_gate=w_gate,
                  w_up=w_up, w_down=w_down)
    batch, seq_p, _ = x_prompt.shape
    dec_batch, seq_s, _ = x_sample.shape
    assert dec_batch == 4 and cache_ckv.shape[2] == 256

    cond8 = jnp.concatenate(
        [c, c_ctx[None, :], jnp.zeros((8 - dec_batch - 1, D_MODEL), F32)], axis=0)
    mods = _ada_call(cond8, ada_w, ada_b)

    rope_c, rope_s = _rope_tables(seq_s)
    cache_kr_pad = jnp.pad(
        cache_krope, ((0, 0), (0, 0), (0, 0), (ROPE_LO, HEAD_PAD - ROPE_LO - QK_ROPE)))

    def chunked_t(tab):
        return jnp.asarray(np.ascontiguousarray(
            tab.reshape(seq_s // ROW_CHUNK, ROW_CHUNK, HEAD_PAD).transpose(0, 2, 1)))

    extra = dict(rope_c=jnp.asarray(rope_c), rope_s=jnp.asarray(rope_s),
                 rope_ct=chunked_t(rope_c), rope_st=chunked_t(rope_s), cache_ckv=cache_ckv,
                 cache_kr=cache_kr_pad, state=state_lru)

    xp = x_prompt.reshape(batch * seq_p, D_MODEL)
    xs = x_sample.reshape(dec_batch * seq_s, D_MODEL)
    caches = None
    lw = _prep_weights(params)
    ffn_w = lw["ffn0"]
    ffn_f32 = (w_gate, w_up, w_down)
    for l in range(DEPTH):
        (hs,) = _mix_call(xs, mods, l, lw, seq=seq_s, n_seq=1, latent=True, extra=extra)
        hp, *caches, xs = _mix_call(xp, mods, l, lw, seq=seq_p, n_seq=CTX_SEQS_PER_STEP,
                                    latent=False, ffn_h2d=hs, ffn_rows_per_mod=seq_s,
                                    ffn_w=ffn_w, caches=caches)
        xp, *ffn_w = _ffn_call(hp, mods, l, lw, ffn_w, rows_per_mod=None, mod_base=4,
                               cast_next=ffn_f32 if l + 1 < DEPTH else None)
    new_cache_ckv, new_cache_krope, new_state_lru = caches
    return (xp.reshape(batch, seq_p, D_MODEL),
            xs.reshape(dec_batch, seq_s, D_MODEL),
            new_cache_ckv, new_cache_krope, new_state_lru)
```

```python
import functools
import itertools
import math

import jax
import jax.numpy as jnp
import numpy as np
from jax import lax
from jax.experimental import pallas as pl
from jax.experimental.pallas import tpu as pltpu

F32 = jnp.float32
BF16 = jnp.bfloat16

D_MODEL = 1024
DEPTH = 2
GRID_W = 64
CONV_W = 256
LRU_W = 256
LRU_BLOCKS = 4
LRU_BLK = 64
LRU_C = 8.0
MLA_HEADS = 8
QK_NOPE = 64
QK_ROPE = 32
V_DIM = 64
V_EXT = V_DIM + 16
QK_DIM = QK_NOPE + QK_ROPE
Q_LORA = 256
KV_LORA = 128
ROPE_THETA = 10000.0
FF = 2816
EPS = 1e-6

LANES = 128
SUBLANES = 8
HEAD_PAD = LANES
VMEM_LIMIT_BYTES = 60 * 1024 * 1024

C_BG, C_CG, C_H = 0, 256, 512
C_XB, C_YB = 768, 1024
C_Q = 1280
C_CKV = 1536
C_KR = 1664
U_W = 1792
SCAN_A = (C_H, C_CKV)
SCAN_B = (C_Q, C_BG)
ROPE_LO = QK_NOPE

def _slots(widths):
    out, off = {}, 0
    for name, width in widths:
        out[name] = (off, width)
        off += width
    return out


VEC_SLOTS = _slots([
    ("norm1_g", D_MODEL), ("conv_b", CONV_W), ("lru_conv_b", LRU_W), ("b_gates", 4 * LRU_W),
    ("mla_qnorm_g", Q_LORA), ("mla_kvnorm_g", KV_LORA), ("kg", HEAD_PAD),
    ("kg_partner", HEAD_PAD), ("gnorm_conv", CONV_W), ("gnorm_lru", LRU_W),
    ("gnorm_mla", MLA_HEADS * V_DIM)])

PAST_LEN = 256
ROW_CHUNK = 256
SCAN_CHUNK = 32
FF_CHUNK = 256
PROJ_SLAB = 256
PROJ_CHUNKS = 1
CTX_SEQS_PER_STEP = 2
FUSED_FFN_PERIOD = 4
SEQ_STAGGER = 10
FFN_ROWS = 1024
PAD_ROWS = SUBLANES


def _rms_scale(x, n):
    ms = jnp.sum(x * x, axis=-1, keepdims=True) * (1.0 / n)
    return lax.rsqrt(ms + EPS)


def _sigmoid(x):
    return 0.5 * jnp.tanh(0.5 * x) + 0.5


def _gelu_tanh(x):
    c = math.sqrt(2.0 / math.pi)
    return 0.5 * x * (1.0 + jnp.tanh(c * (x + 0.044715 * (x * x * x))))


def _dot(a, b):
    return jnp.dot(a, b, preferred_element_type=F32)


def _interleave(*gens):
    live = list(gens)
    while live:
        for g in list(live):
            try:
                next(g)
            except StopIteration:
                live.remove(g)


def _interleave_staggered(gens, lag, side=None, side_period=1):
    pending = list(gens)
    live = []
    tick = 0
    while live or pending or side is not None:
        if pending and tick % lag == 0:
            live.append(pending.pop(0))
        if side is not None and (tick % side_period == 0 or not (live or pending)):
            try:
                next(side)
            except StopIteration:
                side = None
        for g in list(live):
            try:
                next(g)
            except StopIteration:
                live.remove(g)
        tick += 1


def _dot_nt(a, b):
    return lax.dot_general(a, b, (((1,), (1,)), ((), ())), preferred_element_type=F32)


ADA_TN = 1536


def _ada_kernel(cond_ref, w_ref, b_ref, o_ref):
    cnd = cond_ref[...]
    s = (cnd * _sigmoid(cnd)).astype(BF16)
    o_ref[:, 0, :] = _dot(s, w_ref[...].astype(BF16)) + b_ref[...]


def _ada_call(cond8, ada_w, ada_b):
    n_out = 6 * D_MODEL
    return pl.pallas_call(
        _ada_kernel,
        grid=(DEPTH, n_out // ADA_TN),
        in_specs=[
            pl.BlockSpec((8, D_MODEL), lambda l, j: (0, 0)),
            pl.BlockSpec((None, D_MODEL, ADA_TN), lambda l, j: (l, 0, j)),
            pl.BlockSpec((None, 1, ADA_TN), lambda l, j: (l, 0, j)),
        ],
        out_specs=pl.BlockSpec((None, 8, 1, ADA_TN), lambda l, j: (l, 0, 0, j)),
        out_shape=jax.ShapeDtypeStruct((DEPTH, 8, 1, n_out), F32),
        compiler_params=pltpu.CompilerParams(
            dimension_semantics=("arbitrary", "arbitrary"),
            vmem_limit_bytes=VMEM_LIMIT_BYTES),
        name="ada_mod",
    )(cond8, ada_w, ada_b.reshape(DEPTH, 1, n_out))


def _mix_kernel(*refs, seq, n_seq, latent, fused_ffn, n_aliased):
    T = seq * n_seq
    n_ctx = PAST_LEN if latent else 0
    (x_ref, mod_ref, vec_ref, win_ref, wlast_ref, convw_ref, lcw_ref, wg_ref, lam_ref,
     wqt_ref, wk_ref, wvt_ref, qgt_ref, wout_ref) = refs[:14]
    pos = 14
    if latent:
        (wqpt_ref, qgpt_ref, ropect_ref, ropest_ref, ropec_ref, ropes_ref,
         cckv_ref, ckr_ref, h0_ref) = refs[pos:pos + 9]
        pos += 9

    def vec(name):
        off, width = VEC_SLOTS[name]
        return vec_ref[:, off:off + width]
    if fused_ffn:
        ffn_in_refs = refs[pos:pos + 6]
        pos += 6
    pos += n_aliased
    h_out_ref = refs[pos]
    pos += 1
    if not latent:
        ckv_out_ref, kr_out_ref, st_out_ref = refs[pos:pos + 3]
        pos += 3
    if fused_ffn:
        ffn_out_ref = refs[pos]
        pos += 1
    u_scr, qt_scr, k_scr, vt_scr, ymt_scr, ycat_scr = refs[pos:pos + 6]
    if latent:
        st_scr, m_scr = refs[pos + 6:pos + 8]

    n_chunks = T // ROW_CHUNK
    lane = lax.broadcasted_iota(jnp.int32, (1, LANES), 1)
    rope_lanes = (lane >= ROPE_LO) & (lane < ROPE_LO + QK_ROPE)

    sh1 = mod_ref[:, 0:D_MODEL]
    sc1 = mod_ref[:, D_MODEL:2 * D_MODEL]
    g1 = mod_ref[:, 2 * D_MODEL:3 * D_MODEL]
    gain1 = vec("norm1_g") * (1.0 + sc1)

    u_scr[0:PAD_ROWS, :] = jnp.zeros((PAD_ROWS, U_W), F32)
    u_scr[PAD_ROWS + T:2 * PAD_ROWS + T, :] = jnp.zeros((PAD_ROWS, U_W), F32)

    def phase_a(c, n=1):
        r0 = c * ROW_CHUNK
        nrows = n * ROW_CHUNK
        x = x_ref[r0:r0 + nrows, :]
        hn = (x * _rms_scale(x, D_MODEL) * gain1 + sh1).astype(BF16)
        yield
        urows = slice(PAD_ROWS + r0, PAD_ROWS + r0 + nrows)
        for s0 in range(0, C_CKV, PROJ_SLAB):
            u_scr[urows, s0:s0 + PROJ_SLAB] = _dot(hn, win_ref[:, s0:s0 + PROJ_SLAB])
            yield
        u_scr[urows, C_CKV:U_W] = _dot(hn, wlast_ref[...])
        yield
        u_scr[urows, C_CG:C_CG + CONV_W] = (
            u_scr[urows, C_CG:C_CG + CONV_W] * u_scr[urows, C_H:C_H + CONV_W])
        yield

    vrow = lax.broadcasted_iota(jnp.int32, (MLA_HEADS * V_EXT, 1), 0)
    ones_rows = (vrow % V_EXT) >= V_DIM

    def values_t(cb):
        return jnp.where(ones_rows, 1.0, _dot_nt(wvt_ref[...], cb)).astype(BF16)

    def phase_b(c):
        r0 = c * ROW_CHUNK
        rows = slice(PAD_ROWS + r0, PAD_ROWS + r0 + ROW_CHUNK)
        uq = u_scr[rows, C_Q:C_Q + Q_LORA]
        qn = (uq * _rms_scale(uq, Q_LORA) * vec("mla_qnorm_g")).astype(BF16)
        qa_t = _dot_nt(wqt_ref[...], qn)
        if latent:
            qpa_t = _dot_nt(wqpt_ref[...], qn)
            q_tc = ropect_ref[c] * qgt_ref[...]
            q_ts = ropest_ref[c] * qgpt_ref[...]
        yield
        for h in range(MLA_HEADS):
            sl = slice(h * HEAD_PAD, (h + 1) * HEAD_PAD)
            qh = qa_t[sl, :]
            rq = lax.rsqrt(jnp.sum(qh * qh, axis=0, keepdims=True) * (1.0 / QK_DIM) + EPS)
            if latent:
                qh = (qh * q_tc + qpa_t[sl, :] * q_ts) * rq
            else:
                qh = qh * qgt_ref[...] * rq
            qt_scr[h, c] = qh.astype(BF16)
            if h % 2:
                yield
        uc = u_scr[rows, C_CKV:C_CKV + KV_LORA]
        ckv = uc * _rms_scale(uc, KV_LORA) * vec("mla_kvnorm_g")
        krb = u_scr[rows, C_KR:C_KR + LANES]
        kr_rolled = pltpu.roll(krb, 64, 1)
        if not latent:
            sq, rq0 = divmod(r0, seq)
            ckv_out_ref[sq, rq0:rq0 + ROW_CHUNK, :] = ckv
            kr_out_ref[sq, rq0:rq0 + ROW_CHUNK, :] = kr_rolled[:, 0:QK_ROPE]
        krm = jnp.where(rope_lanes, krb, 0.0)
        cb = ckv.astype(BF16)
        ka = _dot(cb, wk_ref[...])
        vt_scr[c] = values_t(cb)
        if latent:
            k_tc = ropec_ref[r0:r0 + ROW_CHUNK, :] * vec("kg")
            k_rot = kr_rolled * (ropes_ref[r0:r0 + ROW_CHUNK, :] * vec("kg_partner"))
        yield
        for h in range(MLA_HEADS):
            sl = slice(h * HEAD_PAD, (h + 1) * HEAD_PAD)
            kpre = ka[:, sl] + krm
            rk = _rms_scale(kpre, QK_DIM)
            if latent:
                kh = (kpre * k_tc + k_rot) * rk
            else:
                kh = kpre * vec("kg") * rk
            k_scr[h, r0:r0 + ROW_CHUNK, :] = kh.astype(BF16)
            if h % 2:
                yield

    if latent:
        cc = cckv_ref[...].astype(BF16)
        ka = _dot(cc, wk_ref[...])
        vt_scr[n_chunks] = values_t(cc)
        krc = ckr_ref[...]
        for h in range(MLA_HEADS):
            kpre = ka[:, h * HEAD_PAD:(h + 1) * HEAD_PAD] + krc
            kh = kpre * vec("kg") * _rms_scale(kpre, QK_DIM)
            k_scr[h, seq:seq + n_ctx, :] = kh.astype(BF16)

    neg_lam = -lam_ref[...]
    sp = jnp.maximum(neg_lam, 0.0) + jnp.log(1.0 + jnp.exp(-jnp.abs(neg_lam)))
    log2a_coef = (-0.5 * LRU_C * math.log2(math.e)) * sp
    row_rc = lax.broadcasted_iota(jnp.int32, (ROW_CHUNK, 1), 0)

    def phase_c(ci):
        r0 = ci * ROW_CHUNK
        base = PAD_ROWS + r0
        first = (r0 % seq) == 0
        last = ((r0 + ROW_CHUNK) % seq) == 0

        def win(col, shift, width=CONV_W):
            w = u_scr[base + shift:base + shift + ROW_CHUNK, col:col + width]
            if shift < 0 and first:
                w = jnp.where(row_rc < -shift, 0.0, w)
            if shift > 0 and last:
                w = jnp.where(row_rc >= ROW_CHUNK - shift, 0.0, w)
            return w

        z_m, z_0, z_p = win(C_CG, -1), win(C_CG, 0), win(C_CG, 1)
        conv = (z_m * convw_ref[0:1, :] + z_0 * convw_ref[1:2, :]
                + z_p * convw_ref[2:3, :] + vec("conv_b"))
        yc = u_scr[base:base + ROW_CHUNK, C_BG:C_BG + CONV_W] * conv
        ycn = yc * _rms_scale(yc, CONV_W) * vec("gnorm_conv")
        ycat_scr[r0:r0 + ROW_CHUNK, 0:CONV_W] = ycn.astype(BF16)
        yield

        xc = (win(C_XB, -2) * lcw_ref[0:1, :] + win(C_XB, -1) * lcw_ref[1:2, :]
              + win(C_XB, 0) * lcw_ref[2:3, :] + win(C_XB, 1) * lcw_ref[3:4, :]
              + vec("lru_conv_b"))
        half_gates = _dot(xc.astype(BF16), wg_ref[...]) + vec("b_gates")
        half_xc = 0.5 * xc
        yield
        for d in range(2):
            t_r = jnp.tanh(half_gates[:, (2 * d) * LRU_W:(2 * d + 1) * LRU_W])
            t_i = jnp.tanh(half_gates[:, (2 * d + 1) * LRU_W:(2 * d + 2) * LRU_W])
            a = jnp.exp2(log2a_coef[d:d + 1, :] * (t_r + 1.0))
            v = 1.0 - a * a
            mult = v * lax.rsqrt(jnp.maximum(v, 1e-30))
            u_scr[base:base + ROW_CHUNK, SCAN_A[d]:SCAN_A[d] + LRU_W] = a
            u_scr[base:base + ROW_CHUNK, SCAN_B[d]:SCAN_B[d] + LRU_W] = (
                mult * (half_xc * (t_i + 1.0)))
            yield

    n_sc = seq // SCAN_CHUNK
    row_sc = lax.broadcasted_iota(jnp.int32, (SCAN_CHUNK, LRU_W), 0)
    steps = [1 << i for i in range(int(math.log2(SCAN_CHUNK)))]

    def scan_step(rf, rb, cf, cb_):
        rf = pl.multiple_of(rf + PAD_ROWS, SUBLANES)
        rb = pl.multiple_of(rb + PAD_ROWS, SUBLANES)
        a = u_scr[pl.ds(rf, SCAN_CHUNK), SCAN_A[0]:SCAN_A[0] + LRU_W]
        b = u_scr[pl.ds(rf, SCAN_CHUNK), SCAN_B[0]:SCAN_B[0] + LRU_W]
        b = b + jnp.where(row_sc == 0, a * cf, 0.0)
        for d in steps:
            if d % SUBLANES == 0:
                b = jnp.concatenate([b[:d], b[d:] + a[d:] * b[:-d]], axis=0)
                if d != steps[-1]:
                    a = jnp.concatenate([a[:d], a[d:] * a[:-d]], axis=0)
                continue
            b = b + a * jnp.where(row_sc >= d, pltpu.roll(b, d, 0), 0.0)
            if d != steps[-1]:
                a = a * jnp.where(row_sc >= d, pltpu.roll(a, d, 0), 1.0)
        u_scr[pl.ds(rf, SCAN_CHUNK), SCAN_B[0]:SCAN_B[0] + LRU_W] = b
        cf = b[SCAN_CHUNK - 1:SCAN_CHUNK, :]
        a = u_scr[pl.ds(rb, SCAN_CHUNK), SCAN_A[1]:SCAN_A[1] + LRU_W]
        b = u_scr[pl.ds(rb, SCAN_CHUNK), SCAN_B[1]:SCAN_B[1] + LRU_W]
        b = b + jnp.where(row_sc == SCAN_CHUNK - 1, a * cb_, 0.0)
        for d in steps:
            if d % SUBLANES == 0:
                b = jnp.concatenate([b[:-d] + a[:-d] * b[d:], b[-d:]], axis=0)
                if d != steps[-1]:
                    a = jnp.concatenate([a[:-d] * a[d:], a[-d:]], axis=0)
                continue
            keep = row_sc < SCAN_CHUNK - d
            b = b + a * jnp.where(keep, pltpu.roll(b, SCAN_CHUNK - d, 0), 0.0)
            if d != steps[-1]:
                a = a * jnp.where(keep, pltpu.roll(a, SCAN_CHUNK - d, 0), 1.0)
        u_scr[pl.ds(rb, SCAN_CHUNK), SCAN_B[1]:SCAN_B[1] + LRU_W] = b
        cb_ = b[0:1, :]
        return cf, cb_

    cps = seq // ROW_CHUNK

    def softmax_pv(st, vts):
        m = jnp.max(st, axis=0, keepdims=True)
        pt = jnp.exp2(st - m).astype(BF16)
        ot = None
        for i, vt in enumerate(vts):
            part = _dot(vt, pt[i * ROW_CHUNK:(i + 1) * ROW_CHUNK, :])
            ot = part if ot is None else ot + part
        return ot[0:V_DIM, :] * (1.0 / ot[V_DIM:V_DIM + 1, :])

    def f_build(c):
        r0 = c * ROW_CHUNK
        rows = slice(r0, r0 + ROW_CHUNK)
        urows = slice(PAD_ROWS + r0, PAD_ROWS + r0 + ROW_CHUNK)
        hf = u_scr[urows, SCAN_B[0]:SCAN_B[0] + LRU_W]
        hb = u_scr[urows, SCAN_B[1]:SCAN_B[1] + LRU_W]
        yb = u_scr[urows, C_YB:C_YB + LRU_W]
        yl = (hf + hb) * _gelu_tanh(yb)
        yln = yl * _rms_scale(yl, LRU_W) * vec("gnorm_lru")
        ycat_scr[rows, CONV_W:CONV_W + LRU_W] = yln.astype(BF16)
        yield
        ymt = ymt_scr[c]
        ssq = jnp.sum(ymt * ymt, axis=0, keepdims=True)
        ynt = ymt * lax.rsqrt(ssq * (1.0 / (MLA_HEADS * V_DIM)) + EPS)
        ycat_scr[rows, CONV_W + LRU_W:D_MODEL] = (ynt.T * vec("gnorm_mla")).astype(BF16)
        yield

    def f_project(c):
        rows = slice(c * ROW_CHUNK, (c + 1) * ROW_CHUNK)
        for n0 in range(0, D_MODEL, PROJ_SLAB):
            cols = slice(n0, n0 + PROJ_SLAB)
            y = _dot(ycat_scr[rows, :], wout_ref[:, cols])
            h_out_ref[rows, cols] = x_ref[rows, cols] + g1[:, cols] * y
            yield

    if latent:
        groups = [list(range(g0, g0 + PROJ_CHUNKS)) for g0 in range(0, n_chunks, PROJ_CHUNKS)]
        _interleave(phase_a(0, PROJ_CHUNKS))
        for gi, group in enumerate(groups):
            dependent = []
            for c in group:
                dependent.append(phase_b(c))
                if c:
                    dependent.append(phase_c(c - 1))
            ahead = [phase_a(groups[gi + 1][0], PROJ_CHUNKS)] if gi + 1 < len(groups) else []
            _interleave(*ahead, itertools.chain(*dependent))
        _interleave(phase_c(n_chunks - 1))

        key_chunks = list(range(cps)) + [n_chunks]
        n_blk = cps * (MLA_HEADS // 2)
        trips = n_blk // 2
        scans_per_trip = n_sc // trips

        def heads_of(idx):
            qc = idx // (MLA_HEADS // 2)
            j = idx % (MLA_HEADS // 2)
            return qc, [2 * j, 2 * j + 1]

        def scores(idx, slot):
            qc, hs = heads_of(idx)
            for par, h in enumerate(hs):
                qt = qt_scr[h, qc]
                m = None
                for i in range(len(key_chunks)):
                    rows = slice(i * ROW_CHUNK, (i + 1) * ROW_CHUNK)
                    st = _dot(k_scr[h, rows, :], qt)
                    st_scr[slot, par, rows, :] = st
                    mc = jnp.max(st, axis=0, keepdims=True)
                    m = mc if m is None else jnp.maximum(m, mc)
                    yield
                m_scr[slot, par] = m

        def finish(idx, slot):
            qc, hs = heads_of(idx)
            for par, h in enumerate(hs):
                hr = pl.ds(pl.multiple_of(h * V_DIM, V_DIM), V_DIM)
                vr = pl.ds(pl.multiple_of(h * V_EXT, V_EXT - V_DIM), V_EXT)
                m = m_scr[slot, par]
                ot = None
                for i, kc in enumerate(key_chunks):
                    rows = slice(i * ROW_CHUNK, (i + 1) * ROW_CHUNK)
                    pt = jnp.exp2(st_scr[slot, par, rows, :] - m).astype(BF16)
                    part = _dot(vt_scr[kc, vr, :], pt)
                    ot = part if ot is None else ot + part
                    yield
                ymt_scr[qc, hr, :] = ot[0:V_DIM, :] * (1.0 / ot[V_DIM:V_DIM + 1, :])

        _interleave(scores(0, 0))

        def att_body(i, carry):
            b0 = 2 * i
            state = {"carry": carry, "c": i * scans_per_trip}

            def scans(n):
                for _ in range(n):
                    c = state["c"]
                    rf = pl.multiple_of(c * SCAN_CHUNK, SCAN_CHUNK)
                    rb = pl.multiple_of((n_sc - 1 - c) * SCAN_CHUNK, SCAN_CHUNK)
                    state["carry"] = scan_step(rf, rb, *state["carry"])
                    state["c"] = c + 1
                    yield
                    yield
                    yield

            half = scans_per_trip // 2
            _interleave(scores(b0 + 1, 1), finish(b0, 0), scans(half))
            _interleave(scores(jnp.minimum(b0 + 2, n_blk - 1), 0), finish(b0 + 1, 1),
                        scans(scans_per_trip - half))
            return state["carry"]

        lax.fori_loop(0, trips, att_body, (h0_ref[0:1, :], h0_ref[1:2, :]))

        _interleave(f_build(0))
        for c in range(n_chunks):
            ahead = [f_build(c + 1)] if c + 1 < n_chunks else []
            _interleave(f_project(c), *ahead)
    else:
        def phase_d(s):
            carry = (jnp.zeros((1, LRU_W), F32), jnp.zeros((1, LRU_W), F32))
            for c in range(n_sc):
                carry = scan_step(s * seq + c * SCAN_CHUNK,
                                  s * seq + (n_sc - 1 - c) * SCAN_CHUNK, *carry)
                yield
            st_out_ref[s, 0:1, :] = carry[0]
            st_out_ref[s, 1:2, :] = carry[1]

        def phase_e(s):
            blocks = [(qi, h) for qi in range(cps) for h in range(MLA_HEADS)]

            def scores(blk):
                qi, h = blk
                return _dot(k_scr[h, s * seq:(s + 1) * seq, :], qt_scr[h, s * cps + qi])

            st = scores(blocks[0])
            for i, (qi, h) in enumerate(blocks):
                nxt = scores(blocks[i + 1]) if i + 1 < len(blocks) else None
                hr = slice(h * V_DIM, (h + 1) * V_DIM)
                vr = slice(h * V_EXT, (h + 1) * V_EXT)
                vts = [vt_scr[s * cps + j, vr, :] for j in range(cps)]
                ymt_scr[s * cps + qi, hr, :] = softmax_pv(st, vts)
                st = nxt
                yield

        def sequence(s):
            chunks = range(s * cps, (s + 1) * cps)
            for c in chunks:
                yield from phase_a(c)
            for c in chunks:
                yield from phase_b(c)
            for c in chunks:
                yield from phase_c(c)
            yield from phase_d(s)
            yield from phase_e(s)
            for c in chunks:
                yield from f_build(c)
                yield from f_project(c)

        side = _ffn_steps(*ffn_in_refs, ffn_out_ref) if fused_ffn else None
        _interleave_staggered([sequence(s) for s in range(n_seq)], SEQ_STAGGER,
                              side=side, side_period=FUSED_FFN_PERIOD)


def _const_spec(shape):
    nd = len(shape)
    return pl.BlockSpec(shape, lambda i, _n=nd: (0,) * _n,
                        pipeline_mode=pl.Buffered(1))


def _layer_spec(arr, layer):
    nd = arr.ndim - 1
    return pl.BlockSpec((None,) + arr.shape[1:], lambda i, _n=nd: (layer,) + (0,) * _n,
                        pipeline_mode=pl.Buffered(1))


def _mix_call(x2d, mods, layer, lw, *, seq, n_seq, latent, extra=None, ffn_h2d=None,
              ffn_rows_per_mod=None, ffn_w=None, caches=None):
    n_tok = x2d.shape[0]
    T = seq * n_seq
    grid = (n_tok // T,)
    fused_ffn = ffn_h2d is not None
    n_keys_buf = T if not latent else seq + PAST_LEN

    if latent:
        mod_map = lambda i: (layer, i, 0, 0)
    else:
        mod_map = lambda i: (layer, 4, 0, 0)

    consts = [lw["vecs"], lw["w_in"], lw["w_in_last"], lw["conv_w"], lw["lru_conv_w"],
              lw["w_gates"], lw["lru_lambda"], lw["wq_t"], lw["wk"], lw["wv_t"], lw["qg_t"],
              lw["w_out"]]
    args = [x2d, mods] + consts
    in_specs = [pl.BlockSpec((T, D_MODEL), lambda i: (i, 0)),
                pl.BlockSpec((None, None, 1, 6 * D_MODEL), mod_map)]
    in_specs += [_layer_spec(a, layer) for a in consts]
    if latent:
        lconsts = [lw["wq_partner_t"], lw["qg_partner_t"]]
        tables = [extra["rope_ct"], extra["rope_st"], extra["rope_c"], extra["rope_s"]]
        args += lconsts + tables
        in_specs += [_layer_spec(a, layer) for a in lconsts]
        in_specs += [_const_spec(a.shape) for a in tables]
        args += [extra["cache_ckv"], extra["cache_kr"], extra["state"]]
        in_specs += [
            pl.BlockSpec((None, None, PAST_LEN, KV_LORA), lambda i: (i, layer, 0, 0)),
            pl.BlockSpec((None, None, PAST_LEN, LANES), lambda i: (i, layer, 0, 0)),
            pl.BlockSpec((None, None, 2, LRU_W), lambda i: (i, layer, 0, 0)),
        ]

    if fused_ffn:
        ffn_rows = ffn_h2d.shape[0] // grid[0]
        tiles_per_mod = ffn_rows_per_mod // ffn_rows
        args += [ffn_h2d, mods, lw["norm2_g"]] + list(ffn_w)
        in_specs += [pl.BlockSpec((ffn_rows, D_MODEL), lambda i: (i, 0)),
                     pl.BlockSpec((None, None, 1, 6 * D_MODEL),
                                  lambda i: (layer, i // tiles_per_mod, 0, 0))]
        in_specs += [_layer_spec(lw["norm2_g"], layer)] + [_layer_spec(a, 0) for a in ffn_w]

    out_shape = [jax.ShapeDtypeStruct((n_tok, D_MODEL), F32)]
    out_specs = [pl.BlockSpec((T, D_MODEL), lambda i: (i, 0))]
    if not latent:
        n_b = n_tok // seq
        out_shape += [jax.ShapeDtypeStruct((n_b, DEPTH, seq, KV_LORA), F32),
                      jax.ShapeDtypeStruct((n_b, DEPTH, seq, QK_ROPE), F32),
                      jax.ShapeDtypeStruct((n_b, DEPTH, 2, LRU_W), F32)]
        out_specs += [pl.BlockSpec((n_seq, None, seq, KV_LORA), lambda i: (i, layer, 0, 0)),
                      pl.BlockSpec((n_seq, None, seq, QK_ROPE), lambda i: (i, layer, 0, 0)),
                      pl.BlockSpec((n_seq, None, 2, LRU_W), lambda i: (i, layer, 0, 0))]
    aliases = {}
    if caches is not None:
        for k, arr in enumerate(caches):
            aliases[len(args)] = 1 + k
            args.append(arr)
            in_specs.append(pl.BlockSpec(memory_space=pl.ANY))
    if fused_ffn:
        out_shape.append(jax.ShapeDtypeStruct(ffn_h2d.shape, F32))
        out_specs.append(pl.BlockSpec((ffn_rows, D_MODEL), lambda i: (i, 0)))

    scratch = [
        pltpu.VMEM((T + 2 * PAD_ROWS, U_W), F32),
        pltpu.VMEM((MLA_HEADS, T // ROW_CHUNK, HEAD_PAD, ROW_CHUNK), BF16),
        pltpu.VMEM((MLA_HEADS, n_keys_buf, HEAD_PAD), BF16),
        pltpu.VMEM((n_keys_buf // ROW_CHUNK, MLA_HEADS * V_EXT, ROW_CHUNK), BF16),
        pltpu.VMEM((T // ROW_CHUNK, MLA_HEADS * V_DIM, ROW_CHUNK), F32),
        pltpu.VMEM((T, D_MODEL), BF16),
    ]
    if latent:
        scratch.append(pltpu.VMEM((2, 2, n_keys_buf, ROW_CHUNK), F32))
        scratch.append(pltpu.VMEM((2, 2, 1, ROW_CHUNK), F32))
    return pl.pallas_call(
        functools.partial(_mix_kernel, seq=seq, n_seq=n_seq, latent=latent,
                          fused_ffn=fused_ffn, n_aliased=len(aliases)),
        grid=grid,
        in_specs=in_specs,
        out_specs=out_specs,
        out_shape=out_shape,
        scratch_shapes=scratch,
        input_output_aliases=aliases,
        compiler_params=pltpu.CompilerParams(
            dimension_semantics=("arbitrary",),
            vmem_limit_bytes=VMEM_LIMIT_BYTES),
        name="mix_latent" if latent else ("mix_context_swiglu" if fused_ffn else "mix_context"),
    )(*args)


def _ffn_steps(h_ref, mod_ref, n2g_ref, wg_ref, wu_ref, wd_ref, o_ref):
    sh2 = mod_ref[:, 3 * D_MODEL:4 * D_MODEL]
    sc2 = mod_ref[:, 4 * D_MODEL:5 * D_MODEL]
    g2 = mod_ref[:, 5 * D_MODEL:6 * D_MODEL]
    h = h_ref[...]
    hn = (h * _rms_scale(h, D_MODEL) * (n2g_ref[...] * (1.0 + sc2)) + sh2).astype(BF16)
    yield
    acc = None
    for c0 in range(0, FF, FF_CHUNK):
        g = _dot(hn, wg_ref[:, c0:c0 + FF_CHUNK])
        u = _dot(hn, wu_ref[:, c0:c0 + FF_CHUNK])
        act = (g * _sigmoid(g) * u).astype(BF16)
        part = _dot(act, wd_ref[c0:c0 + FF_CHUNK, :])
        acc = part if acc is None else acc + part
        yield
    o_ref[...] = h_ref[...] + g2 * acc
    yield


def _ffn_kernel(*refs, n_cast):
    ffn_in, cast_in = refs[:6], refs[6:6 + n_cast]
    o_ref, cast_out = refs[6 + n_cast], refs[7 + n_cast:]
    for src, dst in zip(cast_in, cast_out):
        dst[...] = src[...].astype(BF16)
    _interleave(_ffn_steps(*ffn_in, o_ref))


def _ffn_call(h2d, mods, layer, lw, ffn_w, *, rows_per_mod, mod_base, cast_next=None):
    n_tok = h2d.shape[0]
    n_steps = n_tok // FFN_ROWS
    if mod_base is None:
        tiles_per_mod = rows_per_mod // FFN_ROWS
        mod_map = lambda i: (layer, i // tiles_per_mod, 0, 0)
    else:
        mod_map = lambda i: (layer, mod_base, 0, 0)
    args = [h2d, mods, lw["norm2_g"]] + list(ffn_w)
    in_specs = [pl.BlockSpec((FFN_ROWS, D_MODEL), lambda i: (i, 0)),
                pl.BlockSpec((None, None, 1, 6 * D_MODEL), mod_map),
                _layer_spec(lw["norm2_g"], layer)]
    in_specs += [_layer_spec(a, 0) for a in ffn_w]
    out_shape = [jax.ShapeDtypeStruct((n_tok, D_MODEL), F32)]
    out_specs = [pl.BlockSpec((FFN_ROWS, D_MODEL), lambda i: (i, 0))]
    for w in cast_next or ():
        rows = w.shape[1] // n_steps
        assert rows * n_steps == w.shape[1] and rows % 16 == 0
        args.append(w)
        in_specs.append(pl.BlockSpec((None, rows, w.shape[2]), lambda i: (layer + 1, i, 0)))
        out_shape.append(jax.ShapeDtypeStruct((1,) + w.shape[1:], BF16))
        out_specs.append(pl.BlockSpec((None, rows, w.shape[2]), lambda i: (0, i, 0)))
    return pl.pallas_call(
        functools.partial(_ffn_kernel, n_cast=len(cast_next or ())),
        grid=(n_steps,),
        in_specs=in_specs,
        out_specs=out_specs,
        out_shape=out_shape,
        compiler_params=pltpu.CompilerParams(
            dimension_semantics=("arbitrary",),
            vmem_limit_bytes=VMEM_LIMIT_BYTES),
        name="swiglu",
    )(*args)


def _rope_partner_index():
    return [d + 8 if (d % 16) < 8 else d - 8 for d in range(QK_ROPE)]


def _prep_weights(p):
    partner = jnp.array(_rope_partner_index(), jnp.int32)
    q_scale = QK_DIM ** -0.5 * math.log2(math.e)
    w_in = p["w_in"][:, :, :C_CKV].astype(BF16)
    w_tail = p["w_in"][:, :, C_CKV:].astype(BF16)
    w_kr = w_tail[:, :, KV_LORA:KV_LORA + QK_ROPE]
    zeros32 = jnp.zeros((DEPTH, D_MODEL, QK_ROPE), BF16)
    w_in_last = jnp.concatenate(
        [w_tail[:, :, :KV_LORA], w_kr[:, :, partner], zeros32, w_kr, zeros32], axis=2)

    wq_h = p["mla_wq_up"].reshape(DEPTH, Q_LORA, MLA_HEADS, QK_DIM)
    wq_pad = jnp.pad(wq_h, ((0, 0), (0, 0), (0, 0), (0, HEAD_PAD - QK_DIM)))
    wq_t = wq_pad.reshape(DEPTH, Q_LORA, MLA_HEADS * HEAD_PAD).transpose(0, 2, 1)
    rope_pad = ((0, 0), (0, 0), (0, 0), (ROPE_LO, HEAD_PAD - ROPE_LO - QK_ROPE))
    wq_partner = jnp.pad(wq_h[:, :, :, QK_NOPE:][:, :, :, partner], rope_pad)
    wq_partner_t = wq_partner.reshape(DEPTH, Q_LORA, MLA_HEADS * HEAD_PAD).transpose(0, 2, 1)

    wkv = p["mla_wkv_up"].reshape(DEPTH, KV_LORA, MLA_HEADS, QK_NOPE + V_DIM)
    wk_pad = jnp.pad(wkv[:, :, :, :QK_NOPE], ((0, 0), (0, 0), (0, 0), (0, HEAD_PAD - QK_NOPE)))
    wk_pad = wk_pad.reshape(DEPTH, KV_LORA, MLA_HEADS * HEAD_PAD)
    wv_ext = jnp.pad(wkv[:, :, :, QK_NOPE:], ((0, 0), (0, 0), (0, 0), (0, V_EXT - V_DIM)))
    wv_t = wv_ext.reshape(DEPTH, KV_LORA, MLA_HEADS * V_EXT).transpose(0, 2, 1)

    def pad_gain(g):
        return jnp.pad(g, ((0, 0), (0, HEAD_PAD - QK_DIM)))[:, None, :]

    def partner_gain(g):
        gp = g[:, QK_NOPE:][:, partner]
        return jnp.pad(gp, ((0, 0), (ROPE_LO, HEAD_PAD - ROPE_LO - QK_ROPE)))[:, None, :]

    def on_sublanes(g):
        return jnp.broadcast_to(g.transpose(0, 2, 1), (DEPTH, HEAD_PAD, ROW_CHUNK))

    qg, kg = p["q_norm_g"], p["k_norm_g"]
    eye = jnp.eye(LRU_BLOCKS, dtype=F32)
    gate_w = jnp.stack([p["lru_wa"][:, 0], p["lru_wi"][:, 0],
                        p["lru_wa"][:, 1], p["lru_wi"][:, 1]], axis=1)
    w_gates = (gate_w[:, :, :, :, None, :] * eye[None, None, :, None, :, None])
    w_gates = w_gates.transpose(0, 2, 3, 1, 4, 5).reshape(DEPTH, LRU_W, 4 * LRU_W)
    b_gates = jnp.stack([p["lru_ba"][:, 0], p["lru_bi"][:, 0],
                         p["lru_ba"][:, 1], p["lru_bi"][:, 1]], axis=1)

    def row(v):
        return v[:, None, :]

    rows = {
        "norm1_g": p["norm1_g"], "conv_b": p["conv_b"], "lru_conv_b": p["lru_conv_b"],
        "b_gates": 0.5 * b_gates.reshape(DEPTH, 4 * LRU_W),
        "mla_qnorm_g": p["mla_qnorm_g"], "mla_kvnorm_g": p["mla_kvnorm_g"],
        "kg": pad_gain(kg)[:, 0, :], "kg_partner": partner_gain(kg)[:, 0, :],
        "gnorm_conv": p["gnorm_conv"], "gnorm_lru": p["gnorm_lru"],
        "gnorm_mla": p["gnorm_mla"],
    }
    for name, (_, width) in VEC_SLOTS.items():
        assert rows[name].shape == (DEPTH, width), name
    vecs = row(jnp.concatenate([rows[name] for name in VEC_SLOTS], axis=1))

    return {
        "vecs": vecs,
        "w_in": w_in,
        "w_in_last": w_in_last,
        "conv_w": p["conv_w"],
        "lru_conv_w": p["lru_conv_w"],
        "w_gates": (0.5 * w_gates).astype(BF16),
        "lru_lambda": p["lru_lambda"],
        "wq_t": wq_t.astype(BF16),
        "wq_partner_t": wq_partner_t.astype(BF16),
        "wk": wk_pad.astype(BF16),
        "wv_t": wv_t.astype(BF16),
        "qg_t": on_sublanes(pad_gain(qg) * q_scale),
        "qg_partner_t": on_sublanes(partner_gain(qg) * q_scale),
        "w_out": p["w_out"].astype(BF16),
        "norm2_g": row(p["norm2_g"]),
        "ffn0": [p[k][0:1].astype(BF16) for k in ("w_gate", "w_up", "w_down")],
    }


def _rope_tables(n_tokens):
    n_rows = n_tokens // GRID_W
    row = np.repeat(np.arange(n_rows, dtype=np.float32), GRID_W)
    col = np.tile(np.arange(GRID_W, dtype=np.float32), n_rows)
    n_freq = QK_ROPE // 4
    inv_freq = np.power(np.float32(ROPE_THETA),
                        -np.arange(n_freq, dtype=np.float32) / np.float32(n_freq))
    ang_r = (row[:, None] * inv_freq).astype(np.float32)
    ang_c = (col[:, None] * inv_freq).astype(np.float32)
    cos = np.concatenate([np.cos(ang_r), np.cos(ang_r), np.cos(ang_c), np.cos(ang_c)], 1)
    sin = np.concatenate([-np.sin(ang_r), np.sin(ang_r), -np.sin(ang_c), np.sin(ang_c)], 1)
    ones = np.ones((n_tokens, QK_NOPE), np.float32)
    tail = np.zeros((n_tokens, HEAD_PAD - QK_DIM), np.float32)
    rope_c = np.concatenate([ones, cos, tail], axis=1).astype(np.float32)
    rope_s = np.concatenate([np.zeros((n_tokens, QK_NOPE), np.float32), sin, tail],
                            axis=1).astype(np.float32)
    return rope_c, rope_s


def kernel(x_prompt, x_sample, cache_ckv, cache_krope, state_lru, c, c_ctx, norm1_g, ada_w, ada_b, w_in, conv_w, conv_b, lru_conv_w, lru_conv_b, lru_wa, lru_ba, lru_wi, lru_bi, lru_lambda, mla_qnorm_g, mla_wq_up, mla_kvnorm_g, mla_wkv_up, q_norm_g, k_norm_g, gnorm_conv, gnorm_lru, gnorm_mla, w_out, norm2_g, w_gate, w_up, w_down):
    params = dict(norm1_g=norm1_g, w_in=w_in, conv_w=conv_w, conv_b=conv_b,
                  lru_conv_w=lru_conv_w, lru_conv_b=lru_conv_b, lru_wa=lru_wa,
                  lru_ba=lru_ba, lru_wi=lru_wi, lru_bi=lru_bi, lru_lambda=lru_lambda,
                  mla_qnorm_g=mla_qnorm_g, mla_wq_up=mla_wq_up,
                  mla_kvnorm_g=mla_kvnorm_g, mla_wkv_up=mla_wkv_up, q_norm_g=q_norm_g,
                  k_norm_g=k_norm_g, gnorm_conv=gnorm_conv, gnorm_lru=gnorm_lru,
                  gnorm_mla=gnorm_mla, w_out=w_out, norm2_g=norm2_g, w_gate=w_gate,
                  w_up=w_up, w_down=w_down)
    batch, seq_p, _ = x_prompt.shape
    dec_batch, seq_s, _ = x_sample.shape
    assert dec_batch == 4 and cache_ckv.shape[2] == PAST_LEN == ROW_CHUNK

    cond8 = jnp.concatenate(
        [c, c_ctx[None, :], jnp.zeros((8 - dec_batch - 1, D_MODEL), F32)], axis=0)
    mods = _ada_call(cond8, ada_w, ada_b)

    rope_c, rope_s = _rope_tables(seq_s)
    cache_kr_pad = jnp.pad(
        cache_krope, ((0, 0), (0, 0), (0, 0), (ROPE_LO, HEAD_PAD - ROPE_LO - QK_ROPE)))

    def chunked_t(tab):
        return jnp.asarray(np.ascontiguousarray(
            tab.reshape(seq_s // ROW_CHUNK, ROW_CHUNK, HEAD_PAD).transpose(0, 2, 1)))

    extra = dict(rope_c=jnp.asarray(rope_c), rope_s=jnp.asarray(rope_s),
                 rope_ct=chunked_t(rope_c), rope_st=chunked_t(rope_s), cache_ckv=cache_ckv,
                 cache_kr=cache_kr_pad, state=state_lru)

    xp = x_prompt.reshape(batch * seq_p, D_MODEL)
    xs = x_sample.reshape(dec_batch * seq_s, D_MODEL)
    caches = None
    lw = _prep_weights(params)
    ffn_w = lw["ffn0"]
    ffn_f32 = (w_gate, w_up, w_down)
    for l in range(DEPTH):
        (hs,) = _mix_call(xs, mods, l, lw, seq=seq_s, n_seq=1, latent=True, extra=extra)
        hp, *caches, xs = _mix_call(xp, mods, l, lw, seq=seq_p, n_seq=CTX_SEQS_PER_STEP,
                                    latent=False, ffn_h2d=hs, ffn_rows_per_mod=seq_s,
                                    ffn_w=ffn_w, caches=caches)
        xp, *ffn_w = _ffn_call(hp, mods, l, lw, ffn_w, rows_per_mod=None, mod_base=4,
                               cast_next=ffn_f32 if l + 1 < DEPTH else None)
    new_cache_ckv, new_cache_krope, new_state_lru = caches
    return (xp.reshape(batch, seq_p, D_MODEL),
            xs.reshape(dec_batch, seq_s, D_MODEL),
            new_cache_ckv, new_cache_krope, new_state_lru)
```

```python
import functools
import itertools
import math

import jax
import jax.numpy as jnp
import numpy as np
from jax import lax
from jax.experimental import pallas as pl
from jax.experimental.pallas import tpu as pltpu

F32 = jnp.float32
BF16 = jnp.bfloat16

D_MODEL = 1024
DEPTH = 2
GRID_W = 64
CONV_W = 256
LRU_W = 256
LRU_BLOCKS = 4
LRU_BLK = 64
LRU_C = 8.0
MLA_HEADS = 8
QK_NOPE = 64
QK_ROPE = 32
V_DIM = 64
V_EXT = V_DIM + 16
QK_DIM = QK_NOPE + QK_ROPE
Q_LORA = 256
KV_LORA = 128
ROPE_THETA = 10000.0
FF = 2816
EPS = 1e-6

LANES = 128
SUBLANES = 8
HEAD_PAD = LANES
VMEM_LIMIT_BYTES = 60 * 1024 * 1024

C_BG, C_CG, C_H = 0, 256, 512
C_XB, C_YB = 768, 1024
C_Q = 1280
C_CKV = 1536
C_KR = 1664
U_W = 1792
SCAN_A = (C_H, C_CKV)
SCAN_B = (C_Q, C_BG)
ROPE_LO = QK_NOPE

def _slots(widths):
    out, off = {}, 0
    for name, width in widths:
        out[name] = (off, width)
        off += width
    return out


VEC_SLOTS = _slots([
    ("norm1_g", D_MODEL), ("conv_b", CONV_W), ("lru_conv_b", LRU_W), ("b_gates", 4 * LRU_W),
    ("mla_qnorm_g", Q_LORA), ("mla_kvnorm_g", KV_LORA), ("kg", HEAD_PAD),
    ("kg_partner", HEAD_PAD), ("gnorm_conv", CONV_W), ("gnorm_lru", LRU_W),
    ("gnorm_mla", MLA_HEADS * V_DIM)])

PAST_LEN = 256
ROW_CHUNK = 256
SCAN_CHUNK = 32
FF_CHUNK = 256
PROJ_SLAB = 256
PROJ_CHUNKS = 1
CTX_SEQS_PER_STEP = 2
FUSED_FFN_PERIOD = 4
SEQ_STAGGER = 10
FFN_ROWS = 1024
PAD_ROWS = SUBLANES


def _rms_scale(x, n):
    ms = jnp.sum(x * x, axis=-1, keepdims=True) * (1.0 / n)
    return lax.rsqrt(ms + EPS)


def _sigmoid(x):
    return 0.5 * jnp.tanh(0.5 * x) + 0.5


def _gelu_tanh(x):
    c = math.sqrt(2.0 / math.pi)
    return 0.5 * x * (1.0 + jnp.tanh(c * (x + 0.044715 * (x * x * x))))


def _dot(a, b):
    return jnp.dot(a, b, preferred_element_type=F32)


def _interleave(*gens):
    live = list(gens)
    while live:
        for g in list(live):
            try:
                next(g)
            except StopIteration:
                live.remove(g)


def _interleave_staggered(gens, lag, side=None, side_period=1):
    pending = list(gens)
    live = []
    tick = 0
    while live or pending or side is not None:
        if pending and tick % lag == 0:
            live.append(pending.pop(0))
        if side is not None and (tick % side_period == 0 or not (live or pending)):
            try:
                next(side)
            except StopIteration:
                side = None
        for g in list(live):
            try:
                next(g)
            except StopIteration:
                live.remove(g)
        tick += 1


def _dot_nt(a, b):
    return lax.dot_general(a, b, (((1,), (1,)), ((), ())), preferred_element_type=F32)


ADA_TN = 1536


def _ada_tile(cond_ref, w_ref, b_ref, o_ref):
    cnd = cond_ref[...]
    s = (cnd * _sigmoid(cnd)).astype(BF16)
    o_ref[:, 0, :] = _dot(s, w_ref[...].astype(BF16)) + b_ref[...]


def _ada_kernel(cond_ref, w_ref, b_ref, o_ref):
    _ada_tile(cond_ref, w_ref, b_ref, o_ref)


def _ada_call(cond8, ada_w, ada_b3):
    n_out = 6 * D_MODEL
    return pl.pallas_call(
        _ada_kernel,
        grid=(n_out // ADA_TN,),
        in_specs=[
            pl.BlockSpec((8, D_MODEL), lambda j: (0, 0)),
            pl.BlockSpec((None, D_MODEL, ADA_TN), lambda j: (0, 0, j)),
            pl.BlockSpec((None, 1, ADA_TN), lambda j: (0, 0, j)),
        ],
        out_specs=pl.BlockSpec((None, 8, 1, ADA_TN), lambda j: (0, 0, 0, j)),
        out_shape=jax.ShapeDtypeStruct((1, 8, 1, n_out), F32),
        compiler_params=pltpu.CompilerParams(
            dimension_semantics=("arbitrary",),
            vmem_limit_bytes=VMEM_LIMIT_BYTES),
        name="ada_mod",
    )(cond8, ada_w, ada_b3)


def _mix_kernel(*refs, seq, n_seq, latent, fused_ffn, n_aliased):
    T = seq * n_seq
    n_ctx = PAST_LEN if latent else 0
    (x_ref, mod_ref, vec_ref, win_ref, wlast_ref, convw_ref, lcw_ref, wg_ref, lam_ref,
     wqt_ref, wk_ref, wvt_ref, qgt_ref, wout_ref) = refs[:14]
    pos = 14
    if latent:
        (wqpt_ref, qgpt_ref, ropect_ref, ropest_ref, ropec_ref, ropes_ref,
         cckv_ref, ckr_ref, h0_ref) = refs[pos:pos + 9]
        pos += 9

    def vec(name):
        off, width = VEC_SLOTS[name]
        return vec_ref[:, off:off + width]
    if fused_ffn:
        ffn_in_refs = refs[pos:pos + 6]
        pos += 6
    pos += n_aliased
    h_out_ref = refs[pos]
    pos += 1
    if not latent:
        ckv_out_ref, kr_out_ref, st_out_ref = refs[pos:pos + 3]
        pos += 3
    if fused_ffn:
        ffn_out_ref = refs[pos]
        pos += 1
    u_scr, qt_scr, k_scr, vt_scr, ymt_scr, ycat_scr = refs[pos:pos + 6]
    if latent:
        st_scr, m_scr = refs[pos + 6:pos + 8]

    n_chunks = T // ROW_CHUNK
    lane = lax.broadcasted_iota(jnp.int32, (1, LANES), 1)
    rope_lanes = (lane >= ROPE_LO) & (lane < ROPE_LO + QK_ROPE)

    sh1 = mod_ref[:, 0:D_MODEL]
    sc1 = mod_ref[:, D_MODEL:2 * D_MODEL]
    g1 = mod_ref[:, 2 * D_MODEL:3 * D_MODEL]
    gain1 = vec("norm1_g") * (1.0 + sc1)

    u_scr[0:PAD_ROWS, :] = jnp.zeros((PAD_ROWS, U_W), F32)
    u_scr[PAD_ROWS + T:2 * PAD_ROWS + T, :] = jnp.zeros((PAD_ROWS, U_W), F32)

    def phase_a(c, n=1):
        r0 = c * ROW_CHUNK
        nrows = n * ROW_CHUNK
        x = x_ref[r0:r0 + nrows, :]
        hn = (x * _rms_scale(x, D_MODEL) * gain1 + sh1).astype(BF16)
        yield
        urows = slice(PAD_ROWS + r0, PAD_ROWS + r0 + nrows)
        for s0 in range(0, C_CKV, PROJ_SLAB):
            u_scr[urows, s0:s0 + PROJ_SLAB] = _dot(hn, win_ref[:, s0:s0 + PROJ_SLAB])
            yield
        u_scr[urows, C_CKV:U_W] = _dot(hn, wlast_ref[...])
        yield
        u_scr[urows, C_CG:C_CG + CONV_W] = (
            u_scr[urows, C_CG:C_CG + CONV_W] * u_scr[urows, C_H:C_H + CONV_W])
        yield

    vrow = lax.broadcasted_iota(jnp.int32, (MLA_HEADS * V_EXT, 1), 0)
    ones_rows = (vrow % V_EXT) >= V_DIM

    def values_t(cb):
        return jnp.where(ones_rows, 1.0, _dot_nt(wvt_ref[...], cb)).astype(BF16)

    def phase_b(c):
        r0 = c * ROW_CHUNK
        rows = slice(PAD_ROWS + r0, PAD_ROWS + r0 + ROW_CHUNK)
        uq = u_scr[rows, C_Q:C_Q + Q_LORA]
        qn = (uq * _rms_scale(uq, Q_LORA) * vec("mla_qnorm_g")).astype(BF16)
        qa_t = _dot_nt(wqt_ref[...], qn)
        if latent:
            qpa_t = _dot_nt(wqpt_ref[...], qn)
            q_tc = ropect_ref[c] * qgt_ref[...]
            q_ts = ropest_ref[c] * qgpt_ref[...]
        yield
        for h in range(MLA_HEADS):
            sl = slice(h * HEAD_PAD, (h + 1) * HEAD_PAD)
            qh = qa_t[sl, :]
            rq = lax.rsqrt(jnp.sum(qh * qh, axis=0, keepdims=True) * (1.0 / QK_DIM) + EPS)
            if latent:
                qh = (qh * q_tc + qpa_t[sl, :] * q_ts) * rq
            else:
                qh = qh * qgt_ref[...] * rq
            qt_scr[h, c] = qh.astype(BF16)
            if h % 2:
                yield
        uc = u_scr[rows, C_CKV:C_CKV + KV_LORA]
        ckv = uc * _rms_scale(uc, KV_LORA) * vec("mla_kvnorm_g")
        krb = u_scr[rows, C_KR:C_KR + LANES]
        kr_rolled = pltpu.roll(krb, 64, 1)
        if not latent:
            sq, rq0 = divmod(r0, seq)
            ckv_out_ref[sq, rq0:rq0 + ROW_CHUNK, :] = ckv
            kr_out_ref[sq, rq0:rq0 + ROW_CHUNK, :] = kr_rolled[:, 0:QK_ROPE]
        krm = jnp.where(rope_lanes, krb, 0.0)
        cb = ckv.astype(BF16)
        ka = _dot(cb, wk_ref[...])
        vt_scr[c] = values_t(cb)
        if latent:
            k_tc = ropec_ref[r0:r0 + ROW_CHUNK, :] * vec("kg")
            k_rot = kr_rolled * (ropes_ref[r0:r0 + ROW_CHUNK, :] * vec("kg_partner"))
        yield
        for h in range(MLA_HEADS):
            sl = slice(h * HEAD_PAD, (h + 1) * HEAD_PAD)
            kpre = ka[:, sl] + krm
            rk = _rms_scale(kpre, QK_DIM)
            if latent:
                kh = (kpre * k_tc + k_rot) * rk
            else:
                kh = kpre * vec("kg") * rk
            k_scr[h, r0:r0 + ROW_CHUNK, :] = kh.astype(BF16)
            if h % 2:
                yield

    if latent:
        cc = cckv_ref[...].astype(BF16)
        ka = _dot(cc, wk_ref[...])
        vt_scr[n_chunks] = values_t(cc)
        krc = ckr_ref[...]
        for h in range(MLA_HEADS):
            kpre = ka[:, h * HEAD_PAD:(h + 1) * HEAD_PAD] + krc
            kh = kpre * vec("kg") * _rms_scale(kpre, QK_DIM)
            k_scr[h, seq:seq + n_ctx, :] = kh.astype(BF16)

    neg_lam = -lam_ref[...]
    sp = jnp.maximum(neg_lam, 0.0) + jnp.log(1.0 + jnp.exp(-jnp.abs(neg_lam)))
    log2a_coef = (-0.5 * LRU_C * math.log2(math.e)) * sp
    row_rc = lax.broadcasted_iota(jnp.int32, (ROW_CHUNK, 1), 0)

    def phase_c(ci):
        r0 = ci * ROW_CHUNK
        base = PAD_ROWS + r0
        first = (r0 % seq) == 0
        last = ((r0 + ROW_CHUNK) % seq) == 0

        def win(col, shift, width=CONV_W):
            w = u_scr[base + shift:base + shift + ROW_CHUNK, col:col + width]
            if shift < 0 and first:
                w = jnp.where(row_rc < -shift, 0.0, w)
            if shift > 0 and last:
                w = jnp.where(row_rc >= ROW_CHUNK - shift, 0.0, w)
            return w

        z_m, z_0, z_p = win(C_CG, -1), win(C_CG, 0), win(C_CG, 1)
        conv = (z_m * convw_ref[0:1, :] + z_0 * convw_ref[1:2, :]
                + z_p * convw_ref[2:3, :] + vec("conv_b"))
        yc = u_scr[base:base + ROW_CHUNK, C_BG:C_BG + CONV_W] * conv
        ycn = yc * _rms_scale(yc, CONV_W) * vec("gnorm_conv")
        ycat_scr[r0:r0 + ROW_CHUNK, 0:CONV_W] = ycn.astype(BF16)
        yield

        xc = (win(C_XB, -2) * lcw_ref[0:1, :] + win(C_XB, -1) * lcw_ref[1:2, :]
              + win(C_XB, 0) * lcw_ref[2:3, :] + win(C_XB, 1) * lcw_ref[3:4, :]
              + vec("lru_conv_b"))
        half_gates = _dot(xc.astype(BF16), wg_ref[...]) + vec("b_gates")
        half_xc = 0.5 * xc
        yield
        for d in range(2):
            t_r = jnp.tanh(half_gates[:, (2 * d) * LRU_W:(2 * d + 1) * LRU_W])
            t_i = jnp.tanh(half_gates[:, (2 * d + 1) * LRU_W:(2 * d + 2) * LRU_W])
            a = jnp.exp2(log2a_coef[d:d + 1, :] * (t_r + 1.0))
            v = 1.0 - a * a
            mult = v * lax.rsqrt(jnp.maximum(v, 1e-30))
            u_scr[base:base + ROW_CHUNK, SCAN_A[d]:SCAN_A[d] + LRU_W] = a
            u_scr[base:base + ROW_CHUNK, SCAN_B[d]:SCAN_B[d] + LRU_W] = (
                mult * (half_xc * (t_i + 1.0)))
            yield

    n_sc = seq // SCAN_CHUNK
    row_sc = lax.broadcasted_iota(jnp.int32, (SCAN_CHUNK, LRU_W), 0)
    steps = [1 << i for i in range(int(math.log2(SCAN_CHUNK)))]

    def scan_step(rf, rb, cf, cb_):
        rf = pl.multiple_of(rf + PAD_ROWS, SUBLANES)
        rb = pl.multiple_of(rb + PAD_ROWS, SUBLANES)
        a = u_scr[pl.ds(rf, SCAN_CHUNK), SCAN_A[0]:SCAN_A[0] + LRU_W]
        b = u_scr[pl.ds(rf, SCAN_CHUNK), SCAN_B[0]:SCAN_B[0] + LRU_W]
        b = b + jnp.where(row_sc == 0, a * cf, 0.0)
        for d in steps:
            if d % SUBLANES == 0:
                b = jnp.concatenate([b[:d], b[d:] + a[d:] * b[:-d]], axis=0)
                if d != steps[-1]:
                    a = jnp.concatenate([a[:d], a[d:] * a[:-d]], axis=0)
                continue
            b = b + a * jnp.where(row_sc >= d, pltpu.roll(b, d, 0), 0.0)
            if d != steps[-1]:
                a = a * jnp.where(row_sc >= d, pltpu.roll(a, d, 0), 1.0)
        u_scr[pl.ds(rf, SCAN_CHUNK), SCAN_B[0]:SCAN_B[0] + LRU_W] = b
        cf = b[SCAN_CHUNK - 1:SCAN_CHUNK, :]
        a = u_scr[pl.ds(rb, SCAN_CHUNK), SCAN_A[1]:SCAN_A[1] + LRU_W]
        b = u_scr[pl.ds(rb, SCAN_CHUNK), SCAN_B[1]:SCAN_B[1] + LRU_W]
        b = b + jnp.where(row_sc == SCAN_CHUNK - 1, a * cb_, 0.0)
        for d in steps:
            if d % SUBLANES == 0:
                b = jnp.concatenate([b[:-d] + a[:-d] * b[d:], b[-d:]], axis=0)
                if d != steps[-1]:
                    a = jnp.concatenate([a[:-d] * a[d:], a[-d:]], axis=0)
                continue
            keep = row_sc < SCAN_CHUNK - d
            b = b + a * jnp.where(keep, pltpu.roll(b, SCAN_CHUNK - d, 0), 0.0)
            if d != steps[-1]:
                a = a * jnp.where(keep, pltpu.roll(a, SCAN_CHUNK - d, 0), 1.0)
        u_scr[pl.ds(rb, SCAN_CHUNK), SCAN_B[1]:SCAN_B[1] + LRU_W] = b
        cb_ = b[0:1, :]
        return cf, cb_

    cps = seq // ROW_CHUNK

    def softmax_pv(st, vts):
        m = jnp.max(st, axis=0, keepdims=True)
        pt = jnp.exp2(st - m).astype(BF16)
        ot = None
        for i, vt in enumerate(vts):
            part = _dot(vt, pt[i * ROW_CHUNK:(i + 1) * ROW_CHUNK, :])
            ot = part if ot is None else ot + part
        return ot[0:V_DIM, :] * (1.0 / ot[V_DIM:V_DIM + 1, :])

    def f_build(c):
        r0 = c * ROW_CHUNK
        rows = slice(r0, r0 + ROW_CHUNK)
        urows = slice(PAD_ROWS + r0, PAD_ROWS + r0 + ROW_CHUNK)
        hf = u_scr[urows, SCAN_B[0]:SCAN_B[0] + LRU_W]
        hb = u_scr[urows, SCAN_B[1]:SCAN_B[1] + LRU_W]
        yb = u_scr[urows, C_YB:C_YB + LRU_W]
        yl = (hf + hb) * _gelu_tanh(yb)
        yln = yl * _rms_scale(yl, LRU_W) * vec("gnorm_lru")
        ycat_scr[rows, CONV_W:CONV_W + LRU_W] = yln.astype(BF16)
        yield
        ymt = ymt_scr[c]
        ssq = jnp.sum(ymt * ymt, axis=0, keepdims=True)
        ynt = ymt * lax.rsqrt(ssq * (1.0 / (MLA_HEADS * V_DIM)) + EPS)
        ycat_scr[rows, CONV_W + LRU_W:D_MODEL] = (ynt.T * vec("gnorm_mla")).astype(BF16)
        yield

    def f_project(c):
        rows = slice(c * ROW_CHUNK, (c + 1) * ROW_CHUNK)
        for n0 in range(0, D_MODEL, PROJ_SLAB):
            cols = slice(n0, n0 + PROJ_SLAB)
            y = _dot(ycat_scr[rows, :], wout_ref[:, cols])
            h_out_ref[rows, cols] = x_ref[rows, cols] + g1[:, cols] * y
            yield

    if latent:
        groups = [list(range(g0, g0 + PROJ_CHUNKS)) for g0 in range(0, n_chunks, PROJ_CHUNKS)]
        _interleave(phase_a(0, PROJ_CHUNKS))
        for gi, group in enumerate(groups):
            dependent = []
            for c in group:
                dependent.append(phase_b(c))
                if c:
                    dependent.append(phase_c(c - 1))
            ahead = [phase_a(groups[gi + 1][0], PROJ_CHUNKS)] if gi + 1 < len(groups) else []
            _interleave(*ahead, itertools.chain(*dependent))
        _interleave(phase_c(n_chunks - 1))

        key_chunks = list(range(cps)) + [n_chunks]
        n_blk = cps * (MLA_HEADS // 2)
        trips = n_blk // 2
        scans_per_trip = n_sc // trips

        def heads_of(idx):
            qc = idx // (MLA_HEADS // 2)
            j = idx % (MLA_HEADS // 2)
            return qc, [2 * j, 2 * j + 1]

        def scores(idx, slot):
            qc, hs = heads_of(idx)
            for par, h in enumerate(hs):
                qt = qt_scr[h, qc]
                m = None
                for i in range(len(key_chunks)):
                    rows = slice(i * ROW_CHUNK, (i + 1) * ROW_CHUNK)
                    st = _dot(k_scr[h, rows, :], qt)
                    st_scr[slot, par, rows, :] = st
                    mc = jnp.max(st, axis=0, keepdims=True)
                    m = mc if m is None else jnp.maximum(m, mc)
                    yield
                m_scr[slot, par] = m

        def finish(idx, slot):
            qc, hs = heads_of(idx)
            for par, h in enumerate(hs):
                hr = pl.ds(pl.multiple_of(h * V_DIM, V_DIM), V_DIM)
                vr = pl.ds(pl.multiple_of(h * V_EXT, V_EXT - V_DIM), V_EXT)
                m = m_scr[slot, par]
                ot = None
                for i, kc in enumerate(key_chunks):
                    rows = slice(i * ROW_CHUNK, (i + 1) * ROW_CHUNK)
                    pt = jnp.exp2(st_scr[slot, par, rows, :] - m).astype(BF16)
                    part = _dot(vt_scr[kc, vr, :], pt)
                    ot = part if ot is None else ot + part
                    yield
                ymt_scr[qc, hr, :] = ot[0:V_DIM, :] * (1.0 / ot[V_DIM:V_DIM + 1, :])

        _interleave(scores(0, 0))

        def att_body(i, carry):
            b0 = 2 * i
            state = {"carry": carry, "c": i * scans_per_trip}

            def scans(n):
                for _ in range(n):
                    c = state["c"]
                    rf = pl.multiple_of(c * SCAN_CHUNK, SCAN_CHUNK)
                    rb = pl.multiple_of((n_sc - 1 - c) * SCAN_CHUNK, SCAN_CHUNK)
                    state["carry"] = scan_step(rf, rb, *state["carry"])
                    state["c"] = c + 1
                    yield
                    yield
                    yield

            half = scans_per_trip // 2
            _interleave(scores(b0 + 1, 1), finish(b0, 0), scans(half))
            _interleave(scores(jnp.minimum(b0 + 2, n_blk - 1), 0), finish(b0 + 1, 1),
                        scans(scans_per_trip - half))
            return state["carry"]

        lax.fori_loop(0, trips, att_body, (h0_ref[0:1, :], h0_ref[1:2, :]))

        _interleave(f_build(0))
        for c in range(n_chunks):
            ahead = [f_build(c + 1)] if c + 1 < n_chunks else []
            _interleave(f_project(c), *ahead)
    else:
        def phase_d(s):
            carry = (jnp.zeros((1, LRU_W), F32), jnp.zeros((1, LRU_W), F32))
            for c in range(n_sc):
                carry = scan_step(s * seq + c * SCAN_CHUNK,
                                  s * seq + (n_sc - 1 - c) * SCAN_CHUNK, *carry)
                yield
            st_out_ref[s, 0:1, :] = carry[0]
            st_out_ref[s, 1:2, :] = carry[1]

        def phase_e(s):
            blocks = [(qi, h) for qi in range(cps) for h in range(MLA_HEADS)]

            def scores(blk):
                qi, h = blk
                return _dot(k_scr[h, s * seq:(s + 1) * seq, :], qt_scr[h, s * cps + qi])

            st = scores(blocks[0])
            for i, (qi, h) in enumerate(blocks):
                nxt = scores(blocks[i + 1]) if i + 1 < len(blocks) else None
                hr = slice(h * V_DIM, (h + 1) * V_DIM)
                vr = slice(h * V_EXT, (h + 1) * V_EXT)
                vts = [vt_scr[s * cps + j, vr, :] for j in range(cps)]
                ymt_scr[s * cps + qi, hr, :] = softmax_pv(st, vts)
                st = nxt
                yield

        def sequence(s):
            chunks = range(s * cps, (s + 1) * cps)
            for c in chunks:
                yield from phase_a(c)
            for c in chunks:
                yield from phase_b(c)
            for c in chunks:
                yield from phase_c(c)
            yield from phase_d(s)
            yield from phase_e(s)
            for c in chunks:
                yield from f_build(c)
                yield from f_project(c)

        side = _ffn_steps(*ffn_in_refs, ffn_out_ref) if fused_ffn else None
        _interleave_staggered([sequence(s) for s in range(n_seq)], SEQ_STAGGER,
                              side=side, side_period=FUSED_FFN_PERIOD)


def _const_spec(shape):
    nd = len(shape)
    return pl.BlockSpec(shape, lambda i, _n=nd: (0,) * _n,
                        pipeline_mode=pl.Buffered(1))


def _layer_spec(arr, layer):
    nd = arr.ndim - 1
    return pl.BlockSpec((None,) + arr.shape[1:], lambda i, _n=nd: (layer,) + (0,) * _n,
                        pipeline_mode=pl.Buffered(1))


def _mix_call(x2d, mods, layer, lw, *, seq, n_seq, latent, extra=None, ffn_h2d=None,
              ffn_rows_per_mod=None, ffn_w=None, caches=None):
    n_tok = x2d.shape[0]
    T = seq * n_seq
    grid = (n_tok // T,)
    fused_ffn = ffn_h2d is not None
    n_keys_buf = T if not latent else seq + PAST_LEN

    if latent:
        mod_map = lambda i: (0, i, 0, 0)
    else:
        mod_map = lambda i: (0, 4, 0, 0)

    consts = [lw["vecs"], lw["w_in"], lw["w_in_last"], lw["conv_w"], lw["lru_conv_w"],
              lw["w_gates"], lw["lru_lambda"], lw["wq_t"], lw["wk"], lw["wv_t"], lw["qg_t"],
              lw["w_out"]]
    args = [x2d, mods] + consts
    in_specs = [pl.BlockSpec((T, D_MODEL), lambda i: (i, 0)),
                pl.BlockSpec((None, None, 1, 6 * D_MODEL), mod_map)]
    in_specs += [_layer_spec(a, layer) for a in consts]
    if latent:
        lconsts = [lw["wq_partner_t"], lw["qg_partner_t"]]
        tables = [extra["rope_ct"], extra["rope_st"], extra["rope_c"], extra["rope_s"]]
        args += lconsts + tables
        in_specs += [_layer_spec(a, layer) for a in lconsts]
        in_specs += [_const_spec(a.shape) for a in tables]
        args += [extra["cache_ckv"], extra["cache_kr"], extra["state"]]
        in_specs += [
            pl.BlockSpec((None, None, PAST_LEN, KV_LORA), lambda i: (i, layer, 0, 0)),
            pl.BlockSpec((None, None, PAST_LEN, LANES), lambda i: (i, layer, 0, 0)),
            pl.BlockSpec((None, None, 2, LRU_W), lambda i: (i, layer, 0, 0)),
        ]

    if fused_ffn:
        ffn_rows = ffn_h2d.shape[0] // grid[0]
        tiles_per_mod = ffn_rows_per_mod // ffn_rows
        args += [ffn_h2d, mods, lw["norm2_g"]] + list(ffn_w)
        in_specs += [pl.BlockSpec((ffn_rows, D_MODEL), lambda i: (i, 0)),
                     pl.BlockSpec((None, None, 1, 6 * D_MODEL),
                                  lambda i: (0, i // tiles_per_mod, 0, 0))]
        in_specs += [_layer_spec(lw["norm2_g"], layer)] + [_layer_spec(a, 0) for a in ffn_w]

    out_shape = [jax.ShapeDtypeStruct((n_tok, D_MODEL), F32)]
    out_specs = [pl.BlockSpec((T, D_MODEL), lambda i: (i, 0))]
    if not latent:
        n_b = n_tok // seq
        out_shape += [jax.ShapeDtypeStruct((n_b, DEPTH, seq, KV_LORA), F32),
                      jax.ShapeDtypeStruct((n_b, DEPTH, seq, QK_ROPE), F32),
                      jax.ShapeDtypeStruct((n_b, DEPTH, 2, LRU_W), F32)]
        out_specs += [pl.BlockSpec((n_seq, None, seq, KV_LORA), lambda i: (i, layer, 0, 0)),
                      pl.BlockSpec((n_seq, None, seq, QK_ROPE), lambda i: (i, layer, 0, 0)),
                      pl.BlockSpec((n_seq, None, 2, LRU_W), lambda i: (i, layer, 0, 0))]
    aliases = {}
    if caches is not None:
        for k, arr in enumerate(caches):
            aliases[len(args)] = 1 + k
            args.append(arr)
            in_specs.append(pl.BlockSpec(memory_space=pl.ANY))
    if fused_ffn:
        out_shape.append(jax.ShapeDtypeStruct(ffn_h2d.shape, F32))
        out_specs.append(pl.BlockSpec((ffn_rows, D_MODEL), lambda i: (i, 0)))

    scratch = [
        pltpu.VMEM((T + 2 * PAD_ROWS, U_W), F32),
        pltpu.VMEM((MLA_HEADS, T // ROW_CHUNK, HEAD_PAD, ROW_CHUNK), BF16),
        pltpu.VMEM((MLA_HEADS, n_keys_buf, HEAD_PAD), BF16),
        pltpu.VMEM((n_keys_buf // ROW_CHUNK, MLA_HEADS * V_EXT, ROW_CHUNK), BF16),
        pltpu.VMEM((T // ROW_CHUNK, MLA_HEADS * V_DIM, ROW_CHUNK), F32),
        pltpu.VMEM((T, D_MODEL), BF16),
    ]
    if latent:
        scratch.append(pltpu.VMEM((2, 2, n_keys_buf, ROW_CHUNK), F32))
        scratch.append(pltpu.VMEM((2, 2, 1, ROW_CHUNK), F32))
    return pl.pallas_call(
        functools.partial(_mix_kernel, seq=seq, n_seq=n_seq, latent=latent,
                          fused_ffn=fused_ffn, n_aliased=len(aliases)),
        grid=grid,
        in_specs=in_specs,
        out_specs=out_specs,
        out_shape=out_shape,
        scratch_shapes=scratch,
        input_output_aliases=aliases,
        compiler_params=pltpu.CompilerParams(
            dimension_semantics=("arbitrary",),
            vmem_limit_bytes=VMEM_LIMIT_BYTES),
        name="mix_latent" if latent else ("mix_context_swiglu" if fused_ffn else "mix_context"),
    )(*args)


def _ffn_steps(h_ref, mod_ref, n2g_ref, wg_ref, wu_ref, wd_ref, o_ref):
    sh2 = mod_ref[:, 3 * D_MODEL:4 * D_MODEL]
    sc2 = mod_ref[:, 4 * D_MODEL:5 * D_MODEL]
    g2 = mod_ref[:, 5 * D_MODEL:6 * D_MODEL]
    h = h_ref[...]
    hn = (h * _rms_scale(h, D_MODEL) * (n2g_ref[...] * (1.0 + sc2)) + sh2).astype(BF16)
    yield
    acc = None
    for c0 in range(0, FF, FF_CHUNK):
        g = _dot(hn, wg_ref[:, c0:c0 + FF_CHUNK])
        u = _dot(hn, wu_ref[:, c0:c0 + FF_CHUNK])
        act = (g * _sigmoid(g) * u).astype(BF16)
        part = _dot(act, wd_ref[c0:c0 + FF_CHUNK, :])
        acc = part if acc is None else acc + part
        yield
    o_ref[...] = h_ref[...] + g2 * acc
    yield


def _ffn_kernel(*refs, n_cast, with_ada):
    n_ada = 3 if with_ada else 0
    n_in = 6 + n_cast + n_ada
    ffn_in, cast_in, ada_in = refs[:6], refs[6:6 + n_cast], refs[6 + n_cast:n_in]
    o_ref, cast_out = refs[n_in], refs[n_in + 1:n_in + 1 + n_cast]
    for src, dst in zip(cast_in, cast_out):
        dst[...] = src[...].astype(BF16)
    if with_ada:
        _ada_tile(*ada_in, refs[n_in + 1 + n_cast])
    _interleave(_ffn_steps(*ffn_in, o_ref))


def _ffn_call(h2d, mods, layer, lw, ffn_w, *, rows_per_mod, mod_base, cast_next=None,
              ada_next=None):
    n_tok = h2d.shape[0]
    n_steps = n_tok // FFN_ROWS
    if mod_base is None:
        tiles_per_mod = rows_per_mod // FFN_ROWS
        mod_map = lambda i: (0, i // tiles_per_mod, 0, 0)
    else:
        mod_map = lambda i: (0, mod_base, 0, 0)
    args = [h2d, mods, lw["norm2_g"]] + list(ffn_w)
    in_specs = [pl.BlockSpec((FFN_ROWS, D_MODEL), lambda i: (i, 0)),
                pl.BlockSpec((None, None, 1, 6 * D_MODEL), mod_map),
                _layer_spec(lw["norm2_g"], layer)]
    in_specs += [_layer_spec(a, 0) for a in ffn_w]
    out_shape = [jax.ShapeDtypeStruct((n_tok, D_MODEL), F32)]
    out_specs = [pl.BlockSpec((FFN_ROWS, D_MODEL), lambda i: (i, 0))]
    for w in cast_next or ():
        rows = w.shape[1] // n_steps
        assert rows * n_steps == w.shape[1] and rows % 16 == 0
        args.append(w)
        in_specs.append(pl.BlockSpec((None, rows, w.shape[2]), lambda i: (layer + 1, i, 0)))
        out_shape.append(jax.ShapeDtypeStruct((1,) + w.shape[1:], BF16))
        out_specs.append(pl.BlockSpec((None, rows, w.shape[2]), lambda i: (0, i, 0)))
    if ada_next is not None:
        cond8, ada_w, ada_b3 = ada_next
        n_out = ada_w.shape[2]
        tn = n_out // n_steps
        assert tn * n_steps == n_out and tn % LANES == 0
        args += [cond8, ada_w, ada_b3]
        in_specs += [pl.BlockSpec((8, D_MODEL), lambda i: (0, 0)),
                     pl.BlockSpec((None, D_MODEL, tn), lambda i: (layer + 1, 0, i)),
                     pl.BlockSpec((None, 1, tn), lambda i: (layer + 1, 0, i))]
        out_shape.append(jax.ShapeDtypeStruct((1, 8, 1, n_out), F32))
        out_specs.append(pl.BlockSpec((None, 8, 1, tn), lambda i: (0, 0, 0, i)))
    return pl.pallas_call(
        functools.partial(_ffn_kernel, n_cast=len(cast_next or ()),
                          with_ada=ada_next is not None),
        grid=(n_steps,),
        in_specs=in_specs,
        out_specs=out_specs,
        out_shape=out_shape,
        compiler_params=pltpu.CompilerParams(
            dimension_semantics=("arbitrary",),
            vmem_limit_bytes=VMEM_LIMIT_BYTES),
        name="swiglu",
    )(*args)


def _rope_partner_index():
    return [d + 8 if (d % 16) < 8 else d - 8 for d in range(QK_ROPE)]


def _prep_weights(p):
    partner = jnp.array(_rope_partner_index(), jnp.int32)
    q_scale = QK_DIM ** -0.5 * math.log2(math.e)
    w_in = p["w_in"][:, :, :C_CKV].astype(BF16)
    w_tail = p["w_in"][:, :, C_CKV:].astype(BF16)
    w_kr = w_tail[:, :, KV_LORA:KV_LORA + QK_ROPE]
    zeros32 = jnp.zeros((DEPTH, D_MODEL, QK_ROPE), BF16)
    w_in_last = jnp.concatenate(
        [w_tail[:, :, :KV_LORA], w_kr[:, :, partner], zeros32, w_kr, zeros32], axis=2)

    wq_h = p["mla_wq_up"].reshape(DEPTH, Q_LORA, MLA_HEADS, QK_DIM)
    wq_pad = jnp.pad(wq_h, ((0, 0), (0, 0), (0, 0), (0, HEAD_PAD - QK_DIM)))
    wq_t = wq_pad.reshape(DEPTH, Q_LORA, MLA_HEADS * HEAD_PAD).transpose(0, 2, 1)
    rope_pad = ((0, 0), (0, 0), (0, 0), (ROPE_LO, HEAD_PAD - ROPE_LO - QK_ROPE))
    wq_partner = jnp.pad(wq_h[:, :, :, QK_NOPE:][:, :, :, partner], rope_pad)
    wq_partner_t = wq_partner.reshape(DEPTH, Q_LORA, MLA_HEADS * HEAD_PAD).transpose(0, 2, 1)

    wkv = p["mla_wkv_up"].reshape(DEPTH, KV_LORA, MLA_HEADS, QK_NOPE + V_DIM)
    wk_pad = jnp.pad(wkv[:, :, :, :QK_NOPE], ((0, 0), (0, 0), (0, 0), (0, HEAD_PAD - QK_NOPE)))
    wk_pad = wk_pad.reshape(DEPTH, KV_LORA, MLA_HEADS * HEAD_PAD)
    wv_ext = jnp.pad(wkv[:, :, :, QK_NOPE:], ((0, 0), (0, 0), (0, 0), (0, V_EXT - V_DIM)))
    wv_t = wv_ext.reshape(DEPTH, KV_LORA, MLA_HEADS * V_EXT).transpose(0, 2, 1)

    def pad_gain(g):
        return jnp.pad(g, ((0, 0), (0, HEAD_PAD - QK_DIM)))[:, None, :]

    def partner_gain(g):
        gp = g[:, QK_NOPE:][:, partner]
        return jnp.pad(gp, ((0, 0), (ROPE_LO, HEAD_PAD - ROPE_LO - QK_ROPE)))[:, None, :]

    def on_sublanes(g):
        return jnp.broadcast_to(g.transpose(0, 2, 1), (DEPTH, HEAD_PAD, ROW_CHUNK))

    qg, kg = p["q_norm_g"], p["k_norm_g"]
    eye = jnp.eye(LRU_BLOCKS, dtype=F32)
    gate_w = jnp.stack([p["lru_wa"][:, 0], p["lru_wi"][:, 0],
                        p["lru_wa"][:, 1], p["lru_wi"][:, 1]], axis=1)
    w_gates = (gate_w[:, :, :, :, None, :] * eye[None, None, :, None, :, None])
    w_gates = w_gates.transpose(0, 2, 3, 1, 4, 5).reshape(DEPTH, LRU_W, 4 * LRU_W)
    b_gates = jnp.stack([p["lru_ba"][:, 0], p["lru_bi"][:, 0],
                         p["lru_ba"][:, 1], p["lru_bi"][:, 1]], axis=1)

    def row(v):
        return v[:, None, :]

    rows = {
        "norm1_g": p["norm1_g"], "conv_b": p["conv_b"], "lru_conv_b": p["lru_conv_b"],
        "b_gates": 0.5 * b_gates.reshape(DEPTH, 4 * LRU_W),
        "mla_qnorm_g": p["mla_qnorm_g"], "mla_kvnorm_g": p["mla_kvnorm_g"],
        "kg": pad_gain(kg)[:, 0, :], "kg_partner": partner_gain(kg)[:, 0, :],
        "gnorm_conv": p["gnorm_conv"], "gnorm_lru": p["gnorm_lru"],
        "gnorm_mla": p["gnorm_mla"],
    }
    for name, (_, width) in VEC_SLOTS.items():
        assert rows[name].shape == (DEPTH, width), name
    vecs = row(jnp.concatenate([rows[name] for name in VEC_SLOTS], axis=1))

    return {
        "vecs": vecs,
        "w_in": w_in,
        "w_in_last": w_in_last,
        "conv_w": p["conv_w"],
        "lru_conv_w": p["lru_conv_w"],
        "w_gates": (0.5 * w_gates).astype(BF16),
        "lru_lambda": p["lru_lambda"],
        "wq_t": wq_t.astype(BF16),
        "wq_partner_t": wq_partner_t.astype(BF16),
        "wk": wk_pad.astype(BF16),
        "wv_t": wv_t.astype(BF16),
        "qg_t": on_sublanes(pad_gain(qg) * q_scale),
        "qg_partner_t": on_sublanes(partner_gain(qg) * q_scale),
        "w_out": p["w_out"].astype(BF16),
        "norm2_g": row(p["norm2_g"]),
        "ffn0": [p[k][0:1].astype(BF16) for k in ("w_gate", "w_up", "w_down")],
    }


def _rope_tables(n_tokens):
    n_rows = n_tokens // GRID_W
    row = np.repeat(np.arange(n_rows, dtype=np.float32), GRID_W)
    col = np.tile(np.arange(GRID_W, dtype=np.float32), n_rows)
    n_freq = QK_ROPE // 4
    inv_freq = np.power(np.float32(ROPE_THETA),
                        -np.arange(n_freq, dtype=np.float32) / np.float32(n_freq))
    ang_r = (row[:, None] * inv_freq).astype(np.float32)
    ang_c = (col[:, None] * inv_freq).astype(np.float32)
    cos = np.concatenate([np.cos(ang_r), np.cos(ang_r), np.cos(ang_c), np.cos(ang_c)], 1)
    sin = np.concatenate([-np.sin(ang_r), np.sin(ang_r), -np.sin(ang_c), np.sin(ang_c)], 1)
    ones = np.ones((n_tokens, QK_NOPE), np.float32)
    tail = np.zeros((n_tokens, HEAD_PAD - QK_DIM), np.float32)
    rope_c = np.concatenate([ones, cos, tail], axis=1).astype(np.float32)
    rope_s = np.concatenate([np.zeros((n_tokens, QK_NOPE), np.float32), sin, tail],
                            axis=1).astype(np.float32)
    return rope_c, rope_s


def kernel(x_prompt, x_sample, cache_ckv, cache_krope, state_lru, c, c_ctx, norm1_g, ada_w, ada_b, w_in, conv_w, conv_b, lru_conv_w, lru_conv_b, lru_wa, lru_ba, lru_wi, lru_bi, lru_lambda, mla_qnorm_g, mla_wq_up, mla_kvnorm_g, mla_wkv_up, q_norm_g, k_norm_g, gnorm_conv, gnorm_lru, gnorm_mla, w_out, norm2_g, w_gate, w_up, w_down):
    params = dict(norm1_g=norm1_g, w_in=w_in, conv_w=conv_w, conv_b=conv_b,
                  lru_conv_w=lru_conv_w, lru_conv_b=lru_conv_b, lru_wa=lru_wa,
                  lru_ba=lru_ba, lru_wi=lru_wi, lru_bi=lru_bi, lru_lambda=lru_lambda,
                  mla_qnorm_g=mla_qnorm_g, mla_wq_up=mla_wq_up,
                  mla_kvnorm_g=mla_kvnorm_g, mla_wkv_up=mla_wkv_up, q_norm_g=q_norm_g,
                  k_norm_g=k_norm_g, gnorm_conv=gnorm_conv, gnorm_lru=gnorm_lru,
                  gnorm_mla=gnorm_mla, w_out=w_out, norm2_g=norm2_g, w_gate=w_gate,
                  w_up=w_up, w_down=w_down)
    batch, seq_p, _ = x_prompt.shape
    dec_batch, seq_s, _ = x_sample.shape
    assert dec_batch == 4 and cache_ckv.shape[2] == PAST_LEN == ROW_CHUNK

    cond8 = jnp.concatenate(
        [c, c_ctx[None, :], jnp.zeros((8 - dec_batch - 1, D_MODEL), F32)], axis=0)
    ada_b3 = ada_b.reshape(DEPTH, 1, 6 * D_MODEL)
    mods = _ada_call(cond8, ada_w, ada_b3)

    rope_c, rope_s = _rope_tables(seq_s)
    cache_kr_pad = jnp.pad(
        cache_krope, ((0, 0), (0, 0), (0, 0), (ROPE_LO, HEAD_PAD - ROPE_LO - QK_ROPE)))

    def chunked_t(tab):
        return jnp.asarray(np.ascontiguousarray(
            tab.reshape(seq_s // ROW_CHUNK, ROW_CHUNK, HEAD_PAD).transpose(0, 2, 1)))

    extra = dict(rope_c=jnp.asarray(rope_c), rope_s=jnp.asarray(rope_s),
                 rope_ct=chunked_t(rope_c), rope_st=chunked_t(rope_s), cache_ckv=cache_ckv,
                 cache_kr=cache_kr_pad, state=state_lru)

    xp = x_prompt.reshape(batch * seq_p, D_MODEL)
    xs = x_sample.reshape(dec_batch * seq_s, D_MODEL)
    caches = None
    lw = _prep_weights(params)
    ffn_w = lw["ffn0"]
    ffn_f32 = (w_gate, w_up, w_down)
    for l in range(DEPTH):
        (hs,) = _mix_call(xs, mods, l, lw, seq=seq_s, n_seq=1, latent=True, extra=extra)
        hp, *caches, xs = _mix_call(xp, mods, l, lw, seq=seq_p, n_seq=CTX_SEQS_PER_STEP,
                                    latent=False, ffn_h2d=hs, ffn_rows_per_mod=seq_s,
                                    ffn_w=ffn_w, caches=caches)
        if l + 1 < DEPTH:
            xp, *ffn_w, next_mods = _ffn_call(hp, mods, l, lw, ffn_w, rows_per_mod=None,
                                              mod_base=4, cast_next=ffn_f32,
                                              ada_next=(cond8, ada_w, ada_b3))
        else:
            (xp,) = _ffn_call(hp, mods, l, lw, ffn_w, rows_per_mod=None, mod_base=4)
            next_mods = None
        mods = next_mods
    new_cache_ckv, new_cache_krope, new_state_lru = caches
    return (xp.reshape(batch, seq_p, D_MODEL),
            xs.reshape(dec_batch, seq_s, D_MODEL),
            new_cache_ckv, new_cache_krope, new_state_lru)
```

```python
import functools
import itertools
import math

import jax
import jax.numpy as jnp
import numpy as np
from jax import lax
from jax.experimental import pallas as pl
from jax.experimental.pallas import tpu as pltpu

F32 = jnp.float32
BF16 = jnp.bfloat16

D_MODEL = 1024
DEPTH = 2
GRID_W = 64
CONV_W = 256
LRU_W = 256
LRU_BLOCKS = 4
LRU_BLK = 64
LRU_C = 8.0
MLA_HEADS = 8
QK_NOPE = 64
QK_ROPE = 32
V_DIM = 64
V_EXT = V_DIM + 16
QK_DIM = QK_NOPE + QK_ROPE
Q_LORA = 256
KV_LORA = 128
ROPE_THETA = 10000.0
FF = 2816
EPS = 1e-6

LANES = 128
SUBLANES = 8
HEAD_PAD = LANES
VMEM_LIMIT_BYTES = 60 * 1024 * 1024

C_BG, C_CG, C_H = 0, 256, 512
C_XB, C_YB = 768, 1024
C_Q = 1280
C_CKV = 1536
C_KR = 1664
U_W = 1792
SCAN_A = (C_H, C_CKV)
SCAN_B = (C_Q, C_BG)
ROPE_LO = QK_NOPE

def _slots(widths):
    out, off = {}, 0
    for name, width in widths:
        out[name] = (off, width)
        off += width
    return out


VEC_SLOTS = _slots([
    ("norm1_g", D_MODEL), ("conv_b", CONV_W), ("lru_conv_b", LRU_W), ("b_gates", 4 * LRU_W),
    ("mla_qnorm_g", Q_LORA), ("mla_kvnorm_g", KV_LORA), ("kg", HEAD_PAD),
    ("kg_partner", HEAD_PAD), ("gnorm_conv", CONV_W), ("gnorm_lru", LRU_W),
    ("gnorm_mla", MLA_HEADS * V_DIM)])

PAST_LEN = 256
ROW_CHUNK = 256
SCAN_CHUNK = 32
FF_CHUNK = 256
PROJ_SLAB = 256
PROJ_CHUNKS = 1
CTX_SEQS_PER_STEP = 2
FUSED_FFN_PERIOD = 4
SEQ_STAGGER = 10
FFN_ROWS = 1024
PAD_ROWS = SUBLANES


def _rms_scale(x, n):
    ms = jnp.sum(x * x, axis=-1, keepdims=True) * (1.0 / n)
    return lax.rsqrt(ms + EPS)


def _sigmoid(x):
    return 0.5 * jnp.tanh(0.5 * x) + 0.5


def _gelu_tanh(x):
    c = math.sqrt(2.0 / math.pi)
    return 0.5 * x * (1.0 + jnp.tanh(c * (x + 0.044715 * (x * x * x))))


def _dot(a, b):
    return jnp.dot(a, b, preferred_element_type=F32)


def _interleave(*gens):
    live = list(gens)
    while live:
        for g in list(live):
            try:
                next(g)
            except StopIteration:
                live.remove(g)


def _interleave_staggered(gens, lag, side=None, side_period=1):
    pending = list(gens)
    live = []
    tick = 0
    while live or pending or side is not None:
        if pending and tick % lag == 0:
            live.append(pending.pop(0))
        if side is not None and (tick % side_period == 0 or not (live or pending)):
            try:
                next(side)
            except StopIteration:
                side = None
        for g in list(live):
            try:
                next(g)
            except StopIteration:
                live.remove(g)
        tick += 1


def _dot_nt(a, b):
    return lax.dot_general(a, b, (((1,), (1,)), ((), ())), preferred_element_type=F32)


ADA_TN = 1536


def _ada_tile(cond_ref, w_ref, b_ref, o_ref):
    cnd = cond_ref[...]
    s = (cnd * _sigmoid(cnd)).astype(BF16)
    o_ref[:, 0, :] = _dot(s, w_ref[...].astype(BF16)) + b_ref[...]


def _ada_kernel(cond_ref, w_ref, b_ref, o_ref):
    _ada_tile(cond_ref, w_ref, b_ref, o_ref)


def _ada_call(cond8, ada_w, ada_b3):
    n_out = 6 * D_MODEL
    return pl.pallas_call(
        _ada_kernel,
        grid=(n_out // ADA_TN,),
        in_specs=[
            pl.BlockSpec((8, D_MODEL), lambda j: (0, 0)),
            pl.BlockSpec((None, D_MODEL, ADA_TN), lambda j: (0, 0, j)),
            pl.BlockSpec((None, 1, ADA_TN), lambda j: (0, 0, j)),
        ],
        out_specs=pl.BlockSpec((None, 8, 1, ADA_TN), lambda j: (0, 0, 0, j)),
        out_shape=jax.ShapeDtypeStruct((1, 8, 1, n_out), F32),
        compiler_params=pltpu.CompilerParams(
            dimension_semantics=("arbitrary",),
            vmem_limit_bytes=VMEM_LIMIT_BYTES),
        name="ada_mod",
    )(cond8, ada_w, ada_b3)


def _mix_kernel(*refs, seq, n_seq, latent, fused_ffn, n_aliased):
    T = seq * n_seq
    n_ctx = PAST_LEN if latent else 0
    (x_ref, mod_ref, vec_ref, win_ref, wlast_ref, convw_ref, lcw_ref, wg_ref, lam_ref,
     wqt_ref, wk_ref, wvt_ref, qgt_ref, wout_ref) = refs[:14]
    pos = 14
    if latent:
        (wqpt_ref, qgpt_ref, ropect_ref, ropest_ref, ropec_ref, ropes_ref,
         cckv_ref, ckr_ref, h0_ref) = refs[pos:pos + 9]
        pos += 9

    def vec(name):
        off, width = VEC_SLOTS[name]
        return vec_ref[:, off:off + width]
    if fused_ffn:
        ffn_in_refs = refs[pos:pos + 6]
        pos += 6
    pos += n_aliased
    h_out_ref = refs[pos]
    pos += 1
    if not latent:
        ckv_out_ref, kr_out_ref, st_out_ref = refs[pos:pos + 3]
        pos += 3
    if fused_ffn:
        ffn_out_ref = refs[pos]
        pos += 1
    u_scr, qt_scr, k_scr, vt_scr, ymt_scr, ycat_scr = refs[pos:pos + 6]
    if latent:
        st_scr, m_scr = refs[pos + 6:pos + 8]

    n_chunks = T // ROW_CHUNK
    lane = lax.broadcasted_iota(jnp.int32, (1, LANES), 1)
    rope_lanes = (lane >= ROPE_LO) & (lane < ROPE_LO + QK_ROPE)

    sh1 = mod_ref[:, 0:D_MODEL]
    sc1 = mod_ref[:, D_MODEL:2 * D_MODEL]
    g1 = mod_ref[:, 2 * D_MODEL:3 * D_MODEL]
    gain1 = vec("norm1_g") * (1.0 + sc1)

    u_scr[0:PAD_ROWS, :] = jnp.zeros((PAD_ROWS, U_W), F32)
    u_scr[PAD_ROWS + T:2 * PAD_ROWS + T, :] = jnp.zeros((PAD_ROWS, U_W), F32)

    def phase_a(c, n=1):
        r0 = c * ROW_CHUNK
        nrows = n * ROW_CHUNK
        x = x_ref[r0:r0 + nrows, :]
        hn = (x * _rms_scale(x, D_MODEL) * gain1 + sh1).astype(BF16)
        yield
        urows = slice(PAD_ROWS + r0, PAD_ROWS + r0 + nrows)
        for s0 in range(0, C_CKV, PROJ_SLAB):
            u_scr[urows, s0:s0 + PROJ_SLAB] = _dot(hn, win_ref[:, s0:s0 + PROJ_SLAB])
            yield
        u_scr[urows, C_CKV:U_W] = _dot(hn, wlast_ref[...])
        yield
        u_scr[urows, C_CG:C_CG + CONV_W] = (
            u_scr[urows, C_CG:C_CG + CONV_W] * u_scr[urows, C_H:C_H + CONV_W])
        yield

    vrow = lax.broadcasted_iota(jnp.int32, (MLA_HEADS * V_EXT, 1), 0)
    ones_rows = (vrow % V_EXT) >= V_DIM

    def values_t(cb):
        return jnp.where(ones_rows, 1.0, _dot_nt(wvt_ref[...], cb)).astype(BF16)

    def phase_b(c):
        r0 = c * ROW_CHUNK
        rows = slice(PAD_ROWS + r0, PAD_ROWS + r0 + ROW_CHUNK)
        uq = u_scr[rows, C_Q:C_Q + Q_LORA]
        qn = (uq * _rms_scale(uq, Q_LORA) * vec("mla_qnorm_g")).astype(BF16)
        qa_t = _dot_nt(wqt_ref[...], qn)
        if latent:
            qpa_t = _dot_nt(wqpt_ref[...], qn)
            q_tc = ropect_ref[c] * qgt_ref[...]
            q_ts = ropest_ref[c] * qgpt_ref[...]
        yield
        for h in range(MLA_HEADS):
            sl = slice(h * HEAD_PAD, (h + 1) * HEAD_PAD)
            qh = qa_t[sl, :]
            rq = lax.rsqrt(jnp.sum(qh * qh, axis=0, keepdims=True) * (1.0 / QK_DIM) + EPS)
            if latent:
                qh = (qh * q_tc + qpa_t[sl, :] * q_ts) * rq
            else:
                qh = qh * qgt_ref[...] * rq
            qt_scr[h, c] = qh.astype(BF16)
            if h % 2:
                yield
        uc = u_scr[rows, C_CKV:C_CKV + KV_LORA]
        ckv = uc * _rms_scale(uc, KV_LORA) * vec("mla_kvnorm_g")
        krb = u_scr[rows, C_KR:C_KR + LANES]
        kr_rolled = pltpu.roll(krb, 64, 1)
        if not latent:
            sq, rq0 = divmod(r0, seq)
            ckv_out_ref[sq, rq0:rq0 + ROW_CHUNK, :] = ckv
            kr_out_ref[sq, rq0:rq0 + ROW_CHUNK, :] = kr_rolled[:, 0:QK_ROPE]
        krm = jnp.where(rope_lanes, krb, 0.0)
        cb = ckv.astype(BF16)
        ka = _dot(cb, wk_ref[...])
        vt_scr[c] = values_t(cb)
        if latent:
            k_tc = ropec_ref[r0:r0 + ROW_CHUNK, :] * vec("kg")
            k_rot = kr_rolled * (ropes_ref[r0:r0 + ROW_CHUNK, :] * vec("kg_partner"))
        yield
        for h in range(MLA_HEADS):
            sl = slice(h * HEAD_PAD, (h + 1) * HEAD_PAD)
            kpre = ka[:, sl] + krm
            rk = _rms_scale(kpre, QK_DIM)
            if latent:
                kh = (kpre * k_tc + k_rot) * rk
            else:
                kh = kpre * vec("kg") * rk
            k_scr[h, r0:r0 + ROW_CHUNK, :] = kh.astype(BF16)
            if h % 2:
                yield

    if latent:
        cc = cckv_ref[...].astype(BF16)
        ka = _dot(cc, wk_ref[...])
        vt_scr[n_chunks] = values_t(cc)
        krc = ckr_ref[...]
        for h in range(MLA_HEADS):
            kpre = ka[:, h * HEAD_PAD:(h + 1) * HEAD_PAD] + krc
            kh = kpre * vec("kg") * _rms_scale(kpre, QK_DIM)
            k_scr[h, seq:seq + n_ctx, :] = kh.astype(BF16)

    neg_lam = -lam_ref[...]
    sp = jnp.maximum(neg_lam, 0.0) + jnp.log(1.0 + jnp.exp(-jnp.abs(neg_lam)))
    log2a_coef = (-0.5 * LRU_C * math.log2(math.e)) * sp
    row_rc = lax.broadcasted_iota(jnp.int32, (ROW_CHUNK, 1), 0)

    def phase_c(ci):
        r0 = ci * ROW_CHUNK
        base = PAD_ROWS + r0
        first = (r0 % seq) == 0
        last = ((r0 + ROW_CHUNK) % seq) == 0

        def win(col, shift, width=CONV_W):
            w = u_scr[base + shift:base + shift + ROW_CHUNK, col:col + width]
            if shift < 0 and first:
                w = jnp.where(row_rc < -shift, 0.0, w)
            if shift > 0 and last:
                w = jnp.where(row_rc >= ROW_CHUNK - shift, 0.0, w)
            return w

        z_m, z_0, z_p = win(C_CG, -1), win(C_CG, 0), win(C_CG, 1)
        conv = (z_m * convw_ref[0:1, :] + z_0 * convw_ref[1:2, :]
                + z_p * convw_ref[2:3, :] + vec("conv_b"))
        yc = u_scr[base:base + ROW_CHUNK, C_BG:C_BG + CONV_W] * conv
        ycn = yc * _rms_scale(yc, CONV_W) * vec("gnorm_conv")
        ycat_scr[r0:r0 + ROW_CHUNK, 0:CONV_W] = ycn.astype(BF16)
        yield

        xc = (win(C_XB, -2) * lcw_ref[0:1, :] + win(C_XB, -1) * lcw_ref[1:2, :]
              + win(C_XB, 0) * lcw_ref[2:3, :] + win(C_XB, 1) * lcw_ref[3:4, :]
              + vec("lru_conv_b"))
        half_gates = _dot(xc.astype(BF16), wg_ref[...]) + vec("b_gates")
        half_xc = 0.5 * xc
        yield
        for d in range(2):
            t_r = jnp.tanh(half_gates[:, (2 * d) * LRU_W:(2 * d + 1) * LRU_W])
            t_i = jnp.tanh(half_gates[:, (2 * d + 1) * LRU_W:(2 * d + 2) * LRU_W])
            a = jnp.exp2(log2a_coef[d:d + 1, :] * (t_r + 1.0))
            v = 1.0 - a * a
            mult = v * lax.rsqrt(jnp.maximum(v, 1e-30))
            u_scr[base:base + ROW_CHUNK, SCAN_A[d]:SCAN_A[d] + LRU_W] = a
            u_scr[base:base + ROW_CHUNK, SCAN_B[d]:SCAN_B[d] + LRU_W] = (
                mult * (half_xc * (t_i + 1.0)))
            yield

    n_sc = seq // SCAN_CHUNK
    row_sc = lax.broadcasted_iota(jnp.int32, (SCAN_CHUNK, LRU_W), 0)
    steps = [1 << i for i in range(int(math.log2(SCAN_CHUNK)))]

    def scan_step(rf, rb, cf, cb_):
        rf = pl.multiple_of(rf + PAD_ROWS, SUBLANES)
        rb = pl.multiple_of(rb + PAD_ROWS, SUBLANES)
        a = u_scr[pl.ds(rf, SCAN_CHUNK), SCAN_A[0]:SCAN_A[0] + LRU_W]
        b = u_scr[pl.ds(rf, SCAN_CHUNK), SCAN_B[0]:SCAN_B[0] + LRU_W]
        b = b + jnp.where(row_sc == 0, a * cf, 0.0)
        for d in steps:
            if d % SUBLANES == 0:
                b = jnp.concatenate([b[:d], b[d:] + a[d:] * b[:-d]], axis=0)
                if d != steps[-1]:
                    a = jnp.concatenate([a[:d], a[d:] * a[:-d]], axis=0)
                continue
            b = b + a * jnp.where(row_sc >= d, pltpu.roll(b, d, 0), 0.0)
            if d != steps[-1]:
                a = a * jnp.where(row_sc >= d, pltpu.roll(a, d, 0), 1.0)
        u_scr[pl.ds(rf, SCAN_CHUNK), SCAN_B[0]:SCAN_B[0] + LRU_W] = b
        cf = b[SCAN_CHUNK - 1:SCAN_CHUNK, :]
        a = u_scr[pl.ds(rb, SCAN_CHUNK), SCAN_A[1]:SCAN_A[1] + LRU_W]
        b = u_scr[pl.ds(rb, SCAN_CHUNK), SCAN_B[1]:SCAN_B[1] + LRU_W]
        b = b + jnp.where(row_sc == SCAN_CHUNK - 1, a * cb_, 0.0)
        for d in steps:
            if d % SUBLANES == 0:
                b = jnp.concatenate([b[:-d] + a[:-d] * b[d:], b[-d:]], axis=0)
                if d != steps[-1]:
                    a = jnp.concatenate([a[:-d] * a[d:], a[-d:]], axis=0)
                continue
            keep = row_sc < SCAN_CHUNK - d
            b = b + a * jnp.where(keep, pltpu.roll(b, SCAN_CHUNK - d, 0), 0.0)
            if d != steps[-1]:
                a = a * jnp.where(keep, pltpu.roll(a, SCAN_CHUNK - d, 0), 1.0)
        u_scr[pl.ds(rb, SCAN_CHUNK), SCAN_B[1]:SCAN_B[1] + LRU_W] = b
        cb_ = b[0:1, :]
        return cf, cb_

    cps = seq // ROW_CHUNK

    def softmax_pv(st, vts):
        m = jnp.max(st, axis=0, keepdims=True)
        pt = jnp.exp2(st - m).astype(BF16)
        ot = None
        for i, vt in enumerate(vts):
            part = _dot(vt, pt[i * ROW_CHUNK:(i + 1) * ROW_CHUNK, :])
            ot = part if ot is None else ot + part
        return ot[0:V_DIM, :] * (1.0 / ot[V_DIM:V_DIM + 1, :])

    def f_build(c):
        r0 = c * ROW_CHUNK
        rows = slice(r0, r0 + ROW_CHUNK)
        urows = slice(PAD_ROWS + r0, PAD_ROWS + r0 + ROW_CHUNK)
        hf = u_scr[urows, SCAN_B[0]:SCAN_B[0] + LRU_W]
        hb = u_scr[urows, SCAN_B[1]:SCAN_B[1] + LRU_W]
        yb = u_scr[urows, C_YB:C_YB + LRU_W]
        yl = (hf + hb) * _gelu_tanh(yb)
        yln = yl * _rms_scale(yl, LRU_W) * vec("gnorm_lru")
        ycat_scr[rows, CONV_W:CONV_W + LRU_W] = yln.astype(BF16)
        yield
        ymt = ymt_scr[c]
        ssq = jnp.sum(ymt * ymt, axis=0, keepdims=True)
        ynt = ymt * lax.rsqrt(ssq * (1.0 / (MLA_HEADS * V_DIM)) + EPS)
        ycat_scr[rows, CONV_W + LRU_W:D_MODEL] = (ynt.T * vec("gnorm_mla")).astype(BF16)
        yield

    def f_project(c):
        rows = slice(c * ROW_CHUNK, (c + 1) * ROW_CHUNK)
        for n0 in range(0, D_MODEL, PROJ_SLAB):
            cols = slice(n0, n0 + PROJ_SLAB)
            y = _dot(ycat_scr[rows, :], wout_ref[:, cols])
            h_out_ref[rows, cols] = x_ref[rows, cols] + g1[:, cols] * y
            yield

    if latent:
        groups = [list(range(g0, g0 + PROJ_CHUNKS)) for g0 in range(0, n_chunks, PROJ_CHUNKS)]
        _interleave(phase_a(0, PROJ_CHUNKS))
        for gi, group in enumerate(groups):
            dependent = []
            for c in group:
                dependent.append(phase_b(c))
                if c:
                    dependent.append(phase_c(c - 1))
            ahead = [phase_a(groups[gi + 1][0], PROJ_CHUNKS)] if gi + 1 < len(groups) else []
            _interleave(*ahead, itertools.chain(*dependent))
        _interleave(phase_c(n_chunks - 1))

        key_chunks = list(range(cps)) + [n_chunks]
        n_blk = cps * (MLA_HEADS // 2)
        trips = n_blk // 2
        scans_per_trip = n_sc // trips

        def heads_of(idx):
            qc = idx // (MLA_HEADS // 2)
            j = idx % (MLA_HEADS // 2)
            return qc, [2 * j, 2 * j + 1]

        def scores(idx, slot):
            qc, hs = heads_of(idx)
            for par, h in enumerate(hs):
                qt = qt_scr[h, qc]
                m = None
                for i in range(len(key_chunks)):
                    rows = slice(i * ROW_CHUNK, (i + 1) * ROW_CHUNK)
                    st = _dot(k_scr[h, rows, :], qt)
                    st_scr[slot, par, rows, :] = st
                    mc = jnp.max(st, axis=0, keepdims=True)
                    m = mc if m is None else jnp.maximum(m, mc)
                    yield
                m_scr[slot, par] = m

        def finish(idx, slot):
            qc, hs = heads_of(idx)
            for par, h in enumerate(hs):
                hr = pl.ds(pl.multiple_of(h * V_DIM, V_DIM), V_DIM)
                vr = pl.ds(pl.multiple_of(h * V_EXT, V_EXT - V_DIM), V_EXT)
                m = m_scr[slot, par]
                ot = None
                for i, kc in enumerate(key_chunks):
                    rows = slice(i * ROW_CHUNK, (i + 1) * ROW_CHUNK)
                    pt = jnp.exp2(st_scr[slot, par, rows, :] - m).astype(BF16)
                    part = _dot(vt_scr[kc, vr, :], pt)
                    ot = part if ot is None else ot + part
                    yield
                ymt_scr[qc, hr, :] = ot[0:V_DIM, :] * (1.0 / ot[V_DIM:V_DIM + 1, :])

        _interleave(scores(0, 0))

        def att_body(i, carry):
            b0 = 2 * i
            state = {"carry": carry, "c": i * scans_per_trip}

            def scans(n):
                for _ in range(n):
                    c = state["c"]
                    rf = pl.multiple_of(c * SCAN_CHUNK, SCAN_CHUNK)
                    rb = pl.multiple_of((n_sc - 1 - c) * SCAN_CHUNK, SCAN_CHUNK)
                    state["carry"] = scan_step(rf, rb, *state["carry"])
                    state["c"] = c + 1
                    yield
                    yield
                    yield

            half = scans_per_trip // 2
            _interleave(scores(b0 + 1, 1), finish(b0, 0), scans(half))
            _interleave(scores(jnp.minimum(b0 + 2, n_blk - 1), 0), finish(b0 + 1, 1),
                        scans(scans_per_trip - half))
            return state["carry"]

        lax.fori_loop(0, trips, att_body, (h0_ref[0:1, :], h0_ref[1:2, :]))

        _interleave(f_build(0))
        for c in range(n_chunks):
            ahead = [f_build(c + 1)] if c + 1 < n_chunks else []
            _interleave(f_project(c), *ahead)
    else:
        def phase_d(s):
            carry = (jnp.zeros((1, LRU_W), F32), jnp.zeros((1, LRU_W), F32))
            for c in range(n_sc):
                carry = scan_step(s * seq + c * SCAN_CHUNK,
                                  s * seq + (n_sc - 1 - c) * SCAN_CHUNK, *carry)
                yield
            st_out_ref[s, 0:1, :] = carry[0]
            st_out_ref[s, 1:2, :] = carry[1]

        def phase_e(s):
            blocks = [(qi, h) for qi in range(cps) for h in range(MLA_HEADS)]

            def scores(blk):
                qi, h = blk
                return _dot(k_scr[h, s * seq:(s + 1) * seq, :], qt_scr[h, s * cps + qi])

            st = scores(blocks[0])
            for i, (qi, h) in enumerate(blocks):
                nxt = scores(blocks[i + 1]) if i + 1 < len(blocks) else None
                hr = slice(h * V_DIM, (h + 1) * V_DIM)
                vr = slice(h * V_EXT, (h + 1) * V_EXT)
                vts = [vt_scr[s * cps + j, vr, :] for j in range(cps)]
                ymt_scr[s * cps + qi, hr, :] = softmax_pv(st, vts)
                st = nxt
                yield

        def sequence(s):
            chunks = range(s * cps, (s + 1) * cps)
            for c in chunks:
                yield from phase_a(c)
            for c in chunks:
                yield from phase_b(c)
            for c in chunks:
                yield from phase_c(c)
            yield from phase_d(s)
            yield from phase_e(s)
            for c in chunks:
                yield from f_build(c)
                yield from f_project(c)

        side = _ffn_steps(*ffn_in_refs, ffn_out_ref) if fused_ffn else None
        _interleave_staggered([sequence(s) for s in range(n_seq)], SEQ_STAGGER,
                              side=side, side_period=FUSED_FFN_PERIOD)


def _const_spec(shape):
    nd = len(shape)
    return pl.BlockSpec(shape, lambda i, _n=nd: (0,) * _n,
                        pipeline_mode=pl.Buffered(1))


def _layer_spec(arr, layer):
    nd = arr.ndim - 1
    return pl.BlockSpec((None,) + arr.shape[1:], lambda i, _n=nd: (layer,) + (0,) * _n,
                        pipeline_mode=pl.Buffered(1))


def _mix_call(x2d, mods, layer, lw, *, seq, n_seq, latent, extra=None, ffn_h2d=None,
              ffn_rows_per_mod=None, ffn_w=None, caches=None):
    n_tok = x2d.shape[0]
    T = seq * n_seq
    grid = (n_tok // T,)
    fused_ffn = ffn_h2d is not None
    n_keys_buf = T if not latent else seq + PAST_LEN

    if latent:
        mod_map = lambda i: (0, i, 0, 0)
    else:
        mod_map = lambda i: (0, 4, 0, 0)

    consts = [lw["vecs"], lw["w_in"], lw["w_in_last"], lw["conv_w"], lw["lru_conv_w"],
              lw["w_gates"], lw["lru_lambda"], lw["wq_t"], lw["wk"], lw["wv_t"], lw["qg_t"],
              lw["w_out"]]
    args = [x2d, mods] + consts
    in_specs = [pl.BlockSpec((T, D_MODEL), lambda i: (i, 0)),
                pl.BlockSpec((None, None, 1, 6 * D_MODEL), mod_map)]
    in_specs += [_layer_spec(a, layer) for a in consts]
    if latent:
        lconsts = [lw["wq_partner_t"], lw["qg_partner_t"]]
        tables = [extra["rope_ct"], extra["rope_st"], extra["rope_c"], extra["rope_s"]]
        args += lconsts + tables
        in_specs += [_layer_spec(a, layer) for a in lconsts]
        in_specs += [_const_spec(a.shape) for a in tables]
        args += [extra["cache_ckv"], extra["cache_kr"], extra["state"]]
        in_specs += [
            pl.BlockSpec((None, None, PAST_LEN, KV_LORA), lambda i: (i, layer, 0, 0)),
            pl.BlockSpec((None, None, PAST_LEN, LANES), lambda i: (i, layer, 0, 0)),
            pl.BlockSpec((None, None, 2, LRU_W), lambda i: (i, layer, 0, 0)),
        ]

    if fused_ffn:
        ffn_rows = ffn_h2d.shape[0] // grid[0]
        tiles_per_mod = ffn_rows_per_mod // ffn_rows
        args += [ffn_h2d, mods, lw["norm2_g"]] + list(ffn_w)
        in_specs += [pl.BlockSpec((ffn_rows, D_MODEL), lambda i: (i, 0)),
                     pl.BlockSpec((None, None, 1, 6 * D_MODEL),
                                  lambda i: (0, i // tiles_per_mod, 0, 0))]
        in_specs += [_layer_spec(lw["norm2_g"], layer)] + [_layer_spec(a, 0) for a in ffn_w]

    out_shape = [jax.ShapeDtypeStruct((n_tok, D_MODEL), F32)]
    out_specs = [pl.BlockSpec((T, D_MODEL), lambda i: (i, 0))]
    if not latent:
        n_b = n_tok // seq
        out_shape += [jax.ShapeDtypeStruct((n_b, DEPTH, seq, KV_LORA), F32),
                      jax.ShapeDtypeStruct((n_b, DEPTH, seq, QK_ROPE), F32),
                      jax.ShapeDtypeStruct((n_b, DEPTH, 2, LRU_W), F32)]
        out_specs += [pl.BlockSpec((n_seq, None, seq, KV_LORA), lambda i: (i, layer, 0, 0)),
                      pl.BlockSpec((n_seq, None, seq, QK_ROPE), lambda i: (i, layer, 0, 0)),
                      pl.BlockSpec((n_seq, None, 2, LRU_W), lambda i: (i, layer, 0, 0))]
    aliases = {}
    if caches is not None:
        for k, arr in enumerate(caches):
            aliases[len(args)] = 1 + k
            args.append(arr)
            in_specs.append(pl.BlockSpec(memory_space=pl.ANY))
    if fused_ffn:
        out_shape.append(jax.ShapeDtypeStruct(ffn_h2d.shape, F32))
        out_specs.append(pl.BlockSpec((ffn_rows, D_MODEL), lambda i: (i, 0)))

    scratch = [
        pltpu.VMEM((T + 2 * PAD_ROWS, U_W), F32),
        pltpu.VMEM((MLA_HEADS, T // ROW_CHUNK, HEAD_PAD, ROW_CHUNK), BF16),
        pltpu.VMEM((MLA_HEADS, n_keys_buf, HEAD_PAD), BF16),
        pltpu.VMEM((n_keys_buf // ROW_CHUNK, MLA_HEADS * V_EXT, ROW_CHUNK), BF16),
        pltpu.VMEM((T // ROW_CHUNK, MLA_HEADS * V_DIM, ROW_CHUNK), F32),
        pltpu.VMEM((T, D_MODEL), BF16),
    ]
    if latent:
        scratch.append(pltpu.VMEM((2, 2, n_keys_buf, ROW_CHUNK), F32))
        scratch.append(pltpu.VMEM((2, 2, 1, ROW_CHUNK), F32))
    return pl.pallas_call(
        functools.partial(_mix_kernel, seq=seq, n_seq=n_seq, latent=latent,
                          fused_ffn=fused_ffn, n_aliased=len(aliases)),
        grid=grid,
        in_specs=in_specs,
        out_specs=out_specs,
        out_shape=out_shape,
        scratch_shapes=scratch,
        input_output_aliases=aliases,
        compiler_params=pltpu.CompilerParams(
            dimension_semantics=("arbitrary",),
            vmem_limit_bytes=VMEM_LIMIT_BYTES),
        name="mix_latent" if latent else ("mix_context_swiglu" if fused_ffn else "mix_context"),
    )(*args)


def _ffn_steps(h_ref, mod_ref, n2g_ref, wg_ref, wu_ref, wd_ref, o_ref, per_chunk=None):
    sh2 = mod_ref[:, 3 * D_MODEL:4 * D_MODEL]
    sc2 = mod_ref[:, 4 * D_MODEL:5 * D_MODEL]
    g2 = mod_ref[:, 5 * D_MODEL:6 * D_MODEL]
    h = h_ref[...]
    hn = (h * _rms_scale(h, D_MODEL) * (n2g_ref[...] * (1.0 + sc2)) + sh2).astype(BF16)
    yield
    acc = None
    for c0 in range(0, FF, FF_CHUNK):
        g = _dot(hn, wg_ref[:, c0:c0 + FF_CHUNK])
        u = _dot(hn, wu_ref[:, c0:c0 + FF_CHUNK])
        act = (g * _sigmoid(g) * u).astype(BF16)
        part = _dot(act, wd_ref[c0:c0 + FF_CHUNK, :])
        acc = part if acc is None else acc + part
        if per_chunk is not None:
            per_chunk(c0 // FF_CHUNK)
        yield
    o_ref[...] = h_ref[...] + g2 * acc
    yield


def _ffn_kernel(*refs, n_cast, with_ada):
    n_ada = 3 if with_ada else 0
    n_in = 6 + n_cast + n_ada
    ffn_in, cast_in, ada_in = refs[:6], refs[6:6 + n_cast], refs[6 + n_cast:n_in]
    o_ref, cast_out = refs[n_in], refs[n_in + 1:n_in + 1 + n_cast]
    n_chunks = FF // FF_CHUNK

    def cast_piece(k):
        for src, dst in zip(cast_in, cast_out):
            rows, cols = src.shape
            if cols % n_chunks == 0 and (cols // n_chunks) % LANES == 0:
                sl = (slice(None), slice(k * (cols // n_chunks), (k + 1) * (cols // n_chunks)))
            else:
                assert rows % n_chunks == 0 and (rows // n_chunks) % 16 == 0
                sl = (slice(k * (rows // n_chunks), (k + 1) * (rows // n_chunks)), slice(None))
            dst[sl] = src[sl].astype(BF16)

    if with_ada:
        _ada_tile(*ada_in, refs[n_in + 1 + n_cast])
    _interleave(_ffn_steps(*ffn_in, o_ref, per_chunk=cast_piece if n_cast else None))


def _ffn_call(h2d, mods, layer, lw, ffn_w, *, rows_per_mod, mod_base, cast_next=None,
              ada_next=None):
    n_tok = h2d.shape[0]
    n_steps = n_tok // FFN_ROWS
    if mod_base is None:
        tiles_per_mod = rows_per_mod // FFN_ROWS
        mod_map = lambda i: (0, i // tiles_per_mod, 0, 0)
    else:
        mod_map = lambda i: (0, mod_base, 0, 0)
    args = [h2d, mods, lw["norm2_g"]] + list(ffn_w)
    in_specs = [pl.BlockSpec((FFN_ROWS, D_MODEL), lambda i: (i, 0)),
                pl.BlockSpec((None, None, 1, 6 * D_MODEL), mod_map),
                _layer_spec(lw["norm2_g"], layer)]
    in_specs += [_layer_spec(a, 0) for a in ffn_w]
    out_shape = [jax.ShapeDtypeStruct((n_tok, D_MODEL), F32)]
    out_specs = [pl.BlockSpec((FFN_ROWS, D_MODEL), lambda i: (i, 0))]
    for w in cast_next or ():
        rows = w.shape[1] // n_steps
        assert rows * n_steps == w.shape[1] and rows % 16 == 0
        args.append(w)
        in_specs.append(pl.BlockSpec((None, rows, w.shape[2]), lambda i: (layer + 1, i, 0)))
        out_shape.append(jax.ShapeDtypeStruct((1,) + w.shape[1:], BF16))
        out_specs.append(pl.BlockSpec((None, rows, w.shape[2]), lambda i: (0, i, 0)))
    if ada_next is not None:
        cond8, ada_w, ada_b3 = ada_next
        n_out = ada_w.shape[2]
        tn = n_out // n_steps
        assert tn * n_steps == n_out and tn % LANES == 0
        args += [cond8, ada_w, ada_b3]
        in_specs += [pl.BlockSpec((8, D_MODEL), lambda i: (0, 0)),
                     pl.BlockSpec((None, D_MODEL, tn), lambda i: (layer + 1, 0, i)),
                     pl.BlockSpec((None, 1, tn), lambda i: (layer + 1, 0, i))]
        out_shape.append(jax.ShapeDtypeStruct((1, 8, 1, n_out), F32))
        out_specs.append(pl.BlockSpec((None, 8, 1, tn), lambda i: (0, 0, 0, i)))
    return pl.pallas_call(
        functools.partial(_ffn_kernel, n_cast=len(cast_next or ()),
                          with_ada=ada_next is not None),
        grid=(n_steps,),
        in_specs=in_specs,
        out_specs=out_specs,
        out_shape=out_shape,
        compiler_params=pltpu.CompilerParams(
            dimension_semantics=("arbitrary",),
            vmem_limit_bytes=VMEM_LIMIT_BYTES),
        name="swiglu",
    )(*args)


def _rope_partner_index():
    return [d + 8 if (d % 16) < 8 else d - 8 for d in range(QK_ROPE)]


def _prep_weights(p):
    partner = jnp.array(_rope_partner_index(), jnp.int32)
    q_scale = QK_DIM ** -0.5 * math.log2(math.e)
    w_in = p["w_in"][:, :, :C_CKV].astype(BF16)
    w_tail = p["w_in"][:, :, C_CKV:].astype(BF16)
    w_kr = w_tail[:, :, KV_LORA:KV_LORA + QK_ROPE]
    zeros32 = jnp.zeros((DEPTH, D_MODEL, QK_ROPE), BF16)
    w_in_last = jnp.concatenate(
        [w_tail[:, :, :KV_LORA], w_kr[:, :, partner], zeros32, w_kr, zeros32], axis=2)

    wq_h = p["mla_wq_up"].reshape(DEPTH, Q_LORA, MLA_HEADS, QK_DIM)
    wq_pad = jnp.pad(wq_h, ((0, 0), (0, 0), (0, 0), (0, HEAD_PAD - QK_DIM)))
    wq_t = wq_pad.reshape(DEPTH, Q_LORA, MLA_HEADS * HEAD_PAD).transpose(0, 2, 1)
    rope_pad = ((0, 0), (0, 0), (0, 0), (ROPE_LO, HEAD_PAD - ROPE_LO - QK_ROPE))
    wq_partner = jnp.pad(wq_h[:, :, :, QK_NOPE:][:, :, :, partner], rope_pad)
    wq_partner_t = wq_partner.reshape(DEPTH, Q_LORA, MLA_HEADS * HEAD_PAD).transpose(0, 2, 1)

    wkv = p["mla_wkv_up"].reshape(DEPTH, KV_LORA, MLA_HEADS, QK_NOPE + V_DIM)
    wk_pad = jnp.pad(wkv[:, :, :, :QK_NOPE], ((0, 0), (0, 0), (0, 0), (0, HEAD_PAD - QK_NOPE)))
    wk_pad = wk_pad.reshape(DEPTH, KV_LORA, MLA_HEADS * HEAD_PAD)
    wv_ext = jnp.pad(wkv[:, :, :, QK_NOPE:], ((0, 0), (0, 0), (0, 0), (0, V_EXT - V_DIM)))
    wv_t = wv_ext.reshape(DEPTH, KV_LORA, MLA_HEADS * V_EXT).transpose(0, 2, 1)

    def pad_gain(g):
        return jnp.pad(g, ((0, 0), (0, HEAD_PAD - QK_DIM)))[:, None, :]

    def partner_gain(g):
        gp = g[:, QK_NOPE:][:, partner]
        return jnp.pad(gp, ((0, 0), (ROPE_LO, HEAD_PAD - ROPE_LO - QK_ROPE)))[:, None, :]

    def on_sublanes(g):
        return jnp.broadcast_to(g.transpose(0, 2, 1), (DEPTH, HEAD_PAD, ROW_CHUNK))

    qg, kg = p["q_norm_g"], p["k_norm_g"]
    eye = jnp.eye(LRU_BLOCKS, dtype=F32)
    gate_w = jnp.stack([p["lru_wa"][:, 0], p["lru_wi"][:, 0],
                        p["lru_wa"][:, 1], p["lru_wi"][:, 1]], axis=1)
    w_gates = (gate_w[:, :, :, :, None, :] * eye[None, None, :, None, :, None])
    w_gates = w_gates.transpose(0, 2, 3, 1, 4, 5).reshape(DEPTH, LRU_W, 4 * LRU_W)
    b_gates = jnp.stack([p["lru_ba"][:, 0], p["lru_bi"][:, 0],
                         p["lru_ba"][:, 1], p["lru_bi"][:, 1]], axis=1)

    def row(v):
        return v[:, None, :]

    rows = {
        "norm1_g": p["norm1_g"], "conv_b": p["conv_b"], "lru_conv_b": p["lru_conv_b"],
        "b_gates": 0.5 * b_gates.reshape(DEPTH, 4 * LRU_W),
        "mla_qnorm_g": p["mla_qnorm_g"], "mla_kvnorm_g": p["mla_kvnorm_g"],
        "kg": pad_gain(kg)[:, 0, :], "kg_partner": partner_gain(kg)[:, 0, :],
        "gnorm_conv": p["gnorm_conv"], "gnorm_lru": p["gnorm_lru"],
        "gnorm_mla": p["gnorm_mla"],
    }
    for name, (_, width) in VEC_SLOTS.items():
        assert rows[name].shape == (DEPTH, width), name
    vecs = row(jnp.concatenate([rows[name] for name in VEC_SLOTS], axis=1))

    return {
        "vecs": vecs,
        "w_in": w_in,
        "w_in_last": w_in_last,
        "conv_w": p["conv_w"],
        "lru_conv_w": p["lru_conv_w"],
        "w_gates": (0.5 * w_gates).astype(BF16),
        "lru_lambda": p["lru_lambda"],
        "wq_t": wq_t.astype(BF16),
        "wq_partner_t": wq_partner_t.astype(BF16),
        "wk": wk_pad.astype(BF16),
        "wv_t": wv_t.astype(BF16),
        "qg_t": on_sublanes(pad_gain(qg) * q_scale),
        "qg_partner_t": on_sublanes(partner_gain(qg) * q_scale),
        "w_out": p["w_out"].astype(BF16),
        "norm2_g": row(p["norm2_g"]),
        "ffn0": [p[k][0:1].astype(BF16) for k in ("w_gate", "w_up", "w_down")],
    }


def _rope_tables(n_tokens):
    n_rows = n_tokens // GRID_W
    row = np.repeat(np.arange(n_rows, dtype=np.float32), GRID_W)
    col = np.tile(np.arange(GRID_W, dtype=np.float32), n_rows)
    n_freq = QK_ROPE // 4
    inv_freq = np.power(np.float32(ROPE_THETA),
                        -np.arange(n_freq, dtype=np.float32) / np.float32(n_freq))
    ang_r = (row[:, None] * inv_freq).astype(np.float32)
    ang_c = (col[:, None] * inv_freq).astype(np.float32)
    cos = np.concatenate([np.cos(ang_r), np.cos(ang_r), np.cos(ang_c), np.cos(ang_c)], 1)
    sin = np.concatenate([-np.sin(ang_r), np.sin(ang_r), -np.sin(ang_c), np.sin(ang_c)], 1)
    ones = np.ones((n_tokens, QK_NOPE), np.float32)
    tail = np.zeros((n_tokens, HEAD_PAD - QK_DIM), np.float32)
    rope_c = np.concatenate([ones, cos, tail], axis=1).astype(np.float32)
    rope_s = np.concatenate([np.zeros((n_tokens, QK_NOPE), np.float32), sin, tail],
                            axis=1).astype(np.float32)
    return rope_c, rope_s


def kernel(x_prompt, x_sample, cache_ckv, cache_krope, state_lru, c, c_ctx, norm1_g, ada_w, ada_b, w_in, conv_w, conv_b, lru_conv_w, lru_conv_b, lru_wa, lru_ba, lru_wi, lru_bi, lru_lambda, mla_qnorm_g, mla_wq_up, mla_kvnorm_g, mla_wkv_up, q_norm_g, k_norm_g, gnorm_conv, gnorm_lru, gnorm_mla, w_out, norm2_g, w_gate, w_up, w_down):
    params = dict(norm1_g=norm1_g, w_in=w_in, conv_w=conv_w, conv_b=conv_b,
                  lru_conv_w=lru_conv_w, lru_conv_b=lru_conv_b, lru_wa=lru_wa,
                  lru_ba=lru_ba, lru_wi=lru_wi, lru_bi=lru_bi, lru_lambda=lru_lambda,
                  mla_qnorm_g=mla_qnorm_g, mla_wq_up=mla_wq_up,
                  mla_kvnorm_g=mla_kvnorm_g, mla_wkv_up=mla_wkv_up, q_norm_g=q_norm_g,
                  k_norm_g=k_norm_g, gnorm_conv=gnorm_conv, gnorm_lru=gnorm_lru,
                  gnorm_mla=gnorm_mla, w_out=w_out, norm2_g=norm2_g, w_gate=w_gate,
                  w_up=w_up, w_down=w_down)
    batch, seq_p, _ = x_prompt.shape
    dec_batch, seq_s, _ = x_sample.shape
    assert dec_batch == 4 and cache_ckv.shape[2] == PAST_LEN == ROW_CHUNK

    cond8 = jnp.concatenate(
        [c, c_ctx[None, :], jnp.zeros((8 - dec_batch - 1, D_MODEL), F32)], axis=0)
    ada_b3 = ada_b.reshape(DEPTH, 1, 6 * D_MODEL)
    mods = _ada_call(cond8, ada_w, ada_b3)

    rope_c, rope_s = _rope_tables(seq_s)
    cache_kr_pad = jnp.pad(
        cache_krope, ((0, 0), (0, 0), (0, 0), (ROPE_LO, HEAD_PAD - ROPE_LO - QK_ROPE)))

    def chunked_t(tab):
        return jnp.asarray(np.ascontiguousarray(
            tab.reshape(seq_s // ROW_CHUNK, ROW_CHUNK, HEAD_PAD).transpose(0, 2, 1)))

    extra = dict(rope_c=jnp.asarray(rope_c), rope_s=jnp.asarray(rope_s),
                 rope_ct=chunked_t(rope_c), rope_st=chunked_t(rope_s), cache_ckv=cache_ckv,
                 cache_kr=cache_kr_pad, state=state_lru)

    xp = x_prompt.reshape(batch * seq_p, D_MODEL)
    xs = x_sample.reshape(dec_batch * seq_s, D_MODEL)
    caches = None
    lw = _prep_weights(params)
    ffn_w = lw["ffn0"]
    ffn_f32 = (w_gate, w_up, w_down)
    for l in range(DEPTH):
        (hs,) = _mix_call(xs, mods, l, lw, seq=seq_s, n_seq=1, latent=True, extra=extra)
        hp, *caches, xs = _mix_call(xp, mods, l, lw, seq=seq_p, n_seq=CTX_SEQS_PER_STEP,
                                    latent=False, ffn_h2d=hs, ffn_rows_per_mod=seq_s,
                                    ffn_w=ffn_w, caches=caches)
        if l + 1 < DEPTH:
            xp, *ffn_w, next_mods = _ffn_call(hp, mods, l, lw, ffn_w, rows_per_mod=None,
                                              mod_base=4, cast_next=ffn_f32,
                                              ada_next=(cond8, ada_w, ada_b3))
        else:
            (xp,) = _ffn_call(hp, mods, l, lw, ffn_w, rows_per_mod=None, mod_base=4)
            next_mods = None
        mods = next_mods
    new_cache_ckv, new_cache_krope, new_state_lru = caches
    return (xp.reshape(batch, seq_p, D_MODEL),
            xs.reshape(dec_batch, seq_s, D_MODEL),
            new_cache_ckv, new_cache_krope, new_state_lru)
```

```python
import functools
import itertools
import math

import jax
import jax.numpy as jnp
import numpy as np
from jax import lax
from jax.experimental import pallas as pl
from jax.experimental.pallas import tpu as pltpu

F32 = jnp.float32
BF16 = jnp.bfloat16

D_MODEL = 1024
DEPTH = 2
GRID_W = 64
CONV_W = 256
LRU_W = 256
LRU_BLOCKS = 4
LRU_BLK = 64
LRU_C = 8.0
MLA_HEADS = 8
QK_NOPE = 64
QK_ROPE = 32
V_DIM = 64
V_EXT = V_DIM + 16
QK_DIM = QK_NOPE + QK_ROPE
Q_LORA = 256
KV_LORA = 128
ROPE_THETA = 10000.0
FF = 2816
EPS = 1e-6

LANES = 128
SUBLANES = 8
HEAD_PAD = LANES
VMEM_LIMIT_BYTES = 60 * 1024 * 1024

C_BG, C_CG, C_H = 0, 256, 512
C_XB, C_YB = 768, 1024
C_Q = 1280
C_CKV = 1536
C_KR = 1664
U_W = 1792
SCAN_A = (C_H, C_CKV)
SCAN_B = (C_Q, C_BG)
ROPE_LO = QK_NOPE

def _slots(widths):
    out, off = {}, 0
    for name, width in widths:
        out[name] = (off, width)
        off += width
    return out


VEC_SLOTS = _slots([
    ("norm1_g", D_MODEL), ("conv_b", CONV_W), ("lru_conv_b", LRU_W), ("b_gates", 4 * LRU_W),
    ("mla_qnorm_g", Q_LORA), ("mla_kvnorm_g", KV_LORA), ("kg", HEAD_PAD),
    ("kg_partner", HEAD_PAD), ("gnorm_conv", CONV_W), ("gnorm_lru", LRU_W),
    ("gnorm_mla", MLA_HEADS * V_DIM)])

PAST_LEN = 256
ROW_CHUNK = 256
SCAN_CHUNK = 32
FF_CHUNK = 256
PROJ_SLAB = 256
PROJ_CHUNKS = 1
CTX_SEQS_PER_STEP = 2
FUSED_FFN_PERIOD = 4
SEQ_STAGGER = 10
FFN_ROWS = 1024
PAD_ROWS = SUBLANES


def _rms_scale(x, n):
    ms = jnp.sum(x * x, axis=-1, keepdims=True) * (1.0 / n)
    return lax.rsqrt(ms + EPS)


def _sigmoid(x):
    return 0.5 * jnp.tanh(0.5 * x) + 0.5


def _gelu_tanh(x):
    c = math.sqrt(2.0 / math.pi)
    return 0.5 * x * (1.0 + jnp.tanh(c * (x + 0.044715 * (x * x * x))))


def _dot(a, b):
    return jnp.dot(a, b, preferred_element_type=F32)


def _interleave(*gens):
    live = list(gens)
    while live:
        for g in list(live):
            try:
                next(g)
            except StopIteration:
                live.remove(g)


def _interleave_staggered(gens, lag, side=None, side_period=1):
    pending = list(gens)
    live = []
    tick = 0
    while live or pending or side is not None:
        if pending and tick % lag == 0:
            live.append(pending.pop(0))
        if side is not None and (tick % side_period == 0 or not (live or pending)):
            try:
                next(side)
            except StopIteration:
                side = None
        for g in list(live):
            try:
                next(g)
            except StopIteration:
                live.remove(g)
        tick += 1


def _dot_nt(a, b):
    return lax.dot_general(a, b, (((1,), (1,)), ((), ())), preferred_element_type=F32)


ADA_TN = 1536


def _ada_tile(cond_ref, w_ref, b_ref, o_ref):
    cnd = cond_ref[...]
    s = (cnd * _sigmoid(cnd)).astype(BF16)
    o_ref[:, 0, :] = _dot(s, w_ref[...].astype(BF16)) + b_ref[...]


def _ada_kernel(*refs, n_cast):
    cond_ref, w_ref, b_ref = refs[:3]
    cast_in = refs[3:3 + n_cast]
    o_ref, cast_out = refs[3 + n_cast], refs[4 + n_cast:]
    for src, dst in zip(cast_in, cast_out):
        dst[...] = src[...].astype(BF16)
    _ada_tile(cond_ref, w_ref, b_ref, o_ref)


def _ada_call(cond8, ada_w, ada_b3, cast_first):
    n_out = 6 * D_MODEL
    n_steps = n_out // ADA_TN
    args = [cond8, ada_w, ada_b3]
    in_specs = [
        pl.BlockSpec((8, D_MODEL), lambda j: (0, 0)),
        pl.BlockSpec((None, D_MODEL, ADA_TN), lambda j: (0, 0, j)),
        pl.BlockSpec((None, 1, ADA_TN), lambda j: (0, 0, j)),
    ]
    out_shape = [jax.ShapeDtypeStruct((1, 8, 1, n_out), F32)]
    out_specs = [pl.BlockSpec((None, 8, 1, ADA_TN), lambda j: (0, 0, 0, j))]
    for w in cast_first:
        rows = w.shape[1] // n_steps
        assert rows * n_steps == w.shape[1] and rows % 16 == 0
        args.append(w)
        in_specs.append(pl.BlockSpec((None, rows, w.shape[2]), lambda j: (0, j, 0)))
        out_shape.append(jax.ShapeDtypeStruct((1,) + w.shape[1:], BF16))
        out_specs.append(pl.BlockSpec((None, rows, w.shape[2]), lambda j: (0, j, 0)))
    return pl.pallas_call(
        functools.partial(_ada_kernel, n_cast=len(cast_first)),
        grid=(n_steps,),
        in_specs=in_specs,
        out_specs=out_specs,
        out_shape=out_shape,
        compiler_params=pltpu.CompilerParams(
            dimension_semantics=("arbitrary",),
            vmem_limit_bytes=VMEM_LIMIT_BYTES),
        name="ada_mod",
    )(*args)


def _mix_kernel(*refs, seq, n_seq, latent, fused_ffn, n_aliased):
    T = seq * n_seq
    n_ctx = PAST_LEN if latent else 0
    (x_ref, mod_ref, vec_ref, win_ref, wlast_ref, convw_ref, lcw_ref, wg_ref, lam_ref,
     wqt_ref, wk_ref, wvt_ref, qgt_ref, wout_ref) = refs[:14]
    pos = 14
    if latent:
        (wqpt_ref, qgpt_ref, ropect_ref, ropest_ref, ropec_ref, ropes_ref,
         cckv_ref, ckr_ref, h0_ref) = refs[pos:pos + 9]
        pos += 9

    def vec(name):
        off, width = VEC_SLOTS[name]
        return vec_ref[:, off:off + width]
    if fused_ffn:
        ffn_in_refs = refs[pos:pos + 6]
        pos += 6
    pos += n_aliased
    h_out_ref = refs[pos]
    pos += 1
    if not latent:
        ckv_out_ref, kr_out_ref, st_out_ref = refs[pos:pos + 3]
        pos += 3
    if fused_ffn:
        ffn_out_ref = refs[pos]
        pos += 1
    u_scr, qt_scr, k_scr, vt_scr, ymt_scr, ycat_scr = refs[pos:pos + 6]
    if latent:
        st_scr, m_scr = refs[pos + 6:pos + 8]

    n_chunks = T // ROW_CHUNK
    lane = lax.broadcasted_iota(jnp.int32, (1, LANES), 1)
    rope_lanes = (lane >= ROPE_LO) & (lane < ROPE_LO + QK_ROPE)

    sh1 = mod_ref[:, 0:D_MODEL]
    sc1 = mod_ref[:, D_MODEL:2 * D_MODEL]
    g1 = mod_ref[:, 2 * D_MODEL:3 * D_MODEL]
    gain1 = vec("norm1_g") * (1.0 + sc1)

    u_scr[0:PAD_ROWS, :] = jnp.zeros((PAD_ROWS, U_W), F32)
    u_scr[PAD_ROWS + T:2 * PAD_ROWS + T, :] = jnp.zeros((PAD_ROWS, U_W), F32)

    def phase_a(c, n=1):
        r0 = c * ROW_CHUNK
        nrows = n * ROW_CHUNK
        x = x_ref[r0:r0 + nrows, :]
        hn = (x * _rms_scale(x, D_MODEL) * gain1 + sh1).astype(BF16)
        yield
        urows = slice(PAD_ROWS + r0, PAD_ROWS + r0 + nrows)
        for s0 in range(0, C_CKV, PROJ_SLAB):
            u_scr[urows, s0:s0 + PROJ_SLAB] = _dot(hn, win_ref[:, s0:s0 + PROJ_SLAB])
            yield
        u_scr[urows, C_CKV:U_W] = _dot(hn, wlast_ref[...])
        yield
        u_scr[urows, C_CG:C_CG + CONV_W] = (
            u_scr[urows, C_CG:C_CG + CONV_W] * u_scr[urows, C_H:C_H + CONV_W])
        yield

    vrow = lax.broadcasted_iota(jnp.int32, (MLA_HEADS * V_EXT, 1), 0)
    ones_rows = (vrow % V_EXT) >= V_DIM

    def values_t(cb):
        return jnp.where(ones_rows, 1.0, _dot_nt(wvt_ref[...], cb)).astype(BF16)

    def phase_b(c):
        r0 = c * ROW_CHUNK
        rows = slice(PAD_ROWS + r0, PAD_ROWS + r0 + ROW_CHUNK)
        uq = u_scr[rows, C_Q:C_Q + Q_LORA]
        qn = (uq * _rms_scale(uq, Q_LORA) * vec("mla_qnorm_g")).astype(BF16)
        qa_t = _dot_nt(wqt_ref[...], qn)
        if latent:
            qpa_t = _dot_nt(wqpt_ref[...], qn)
            q_tc = ropect_ref[c] * qgt_ref[...]
            q_ts = ropest_ref[c] * qgpt_ref[...]
        yield
        for h in range(MLA_HEADS):
            sl = slice(h * HEAD_PAD, (h + 1) * HEAD_PAD)
            qh = qa_t[sl, :]
            rq = lax.rsqrt(jnp.sum(qh * qh, axis=0, keepdims=True) * (1.0 / QK_DIM) + EPS)
            if latent:
                qh = (qh * q_tc + qpa_t[sl, :] * q_ts) * rq
            else:
                qh = qh * qgt_ref[...] * rq
            qt_scr[h, c] = qh.astype(BF16)
            if h % 2:
                yield
        uc = u_scr[rows, C_CKV:C_CKV + KV_LORA]
        ckv = uc * _rms_scale(uc, KV_LORA) * vec("mla_kvnorm_g")
        krb = u_scr[rows, C_KR:C_KR + LANES]
        kr_rolled = pltpu.roll(krb, 64, 1)
        if not latent:
            sq, rq0 = divmod(r0, seq)
            ckv_out_ref[sq, rq0:rq0 + ROW_CHUNK, :] = ckv
            kr_out_ref[sq, rq0:rq0 + ROW_CHUNK, :] = kr_rolled[:, 0:QK_ROPE]
        krm = jnp.where(rope_lanes, krb, 0.0)
        cb = ckv.astype(BF16)
        ka = _dot(cb, wk_ref[...])
        vt_scr[c] = values_t(cb)
        if latent:
            k_tc = ropec_ref[r0:r0 + ROW_CHUNK, :] * vec("kg")
            k_rot = kr_rolled * (ropes_ref[r0:r0 + ROW_CHUNK, :] * vec("kg_partner"))
        yield
        for h in range(MLA_HEADS):
            sl = slice(h * HEAD_PAD, (h + 1) * HEAD_PAD)
            kpre = ka[:, sl] + krm
            rk = _rms_scale(kpre, QK_DIM)
            if latent:
                kh = (kpre * k_tc + k_rot) * rk
            else:
                kh = kpre * vec("kg") * rk
            k_scr[h, r0:r0 + ROW_CHUNK, :] = kh.astype(BF16)
            if h % 2:
                yield

    if latent:
        cc = cckv_ref[...].astype(BF16)
        ka = _dot(cc, wk_ref[...])
        vt_scr[n_chunks] = values_t(cc)
        krc = ckr_ref[...]
        for h in range(MLA_HEADS):
            kpre = ka[:, h * HEAD_PAD:(h + 1) * HEAD_PAD] + krc
            kh = kpre * vec("kg") * _rms_scale(kpre, QK_DIM)
            k_scr[h, seq:seq + n_ctx, :] = kh.astype(BF16)

    neg_lam = -lam_ref[...]
    sp = jnp.maximum(neg_lam, 0.0) + jnp.log(1.0 + jnp.exp(-jnp.abs(neg_lam)))
    log2a_coef = (-0.5 * LRU_C * math.log2(math.e)) * sp
    row_rc = lax.broadcasted_iota(jnp.int32, (ROW_CHUNK, 1), 0)

    def phase_c(ci):
        r0 = ci * ROW_CHUNK
        base = PAD_ROWS + r0
        first = (r0 % seq) == 0
        last = ((r0 + ROW_CHUNK) % seq) == 0

        def win(col, shift, width=CONV_W):
            w = u_scr[base + shift:base + shift + ROW_CHUNK, col:col + width]
            if shift < 0 and first:
                w = jnp.where(row_rc < -shift, 0.0, w)
            if shift > 0 and last:
                w = jnp.where(row_rc >= ROW_CHUNK - shift, 0.0, w)
            return w

        z_m, z_0, z_p = win(C_CG, -1), win(C_CG, 0), win(C_CG, 1)
        conv = (z_m * convw_ref[0:1, :] + z_0 * convw_ref[1:2, :]
                + z_p * convw_ref[2:3, :] + vec("conv_b"))
        yc = u_scr[base:base + ROW_CHUNK, C_BG:C_BG + CONV_W] * conv
        ycn = yc * _rms_scale(yc, CONV_W) * vec("gnorm_conv")
        ycat_scr[r0:r0 + ROW_CHUNK, 0:CONV_W] = ycn.astype(BF16)
        yield

        xc = (win(C_XB, -2) * lcw_ref[0:1, :] + win(C_XB, -1) * lcw_ref[1:2, :]
              + win(C_XB, 0) * lcw_ref[2:3, :] + win(C_XB, 1) * lcw_ref[3:4, :]
              + vec("lru_conv_b"))
        half_gates = _dot(xc.astype(BF16), wg_ref[...]) + vec("b_gates")
        half_xc = 0.5 * xc
        yield
        for d in range(2):
            t_r = jnp.tanh(half_gates[:, (2 * d) * LRU_W:(2 * d + 1) * LRU_W])
            t_i = jnp.tanh(half_gates[:, (2 * d + 1) * LRU_W:(2 * d + 2) * LRU_W])
            a = jnp.exp2(log2a_coef[d:d + 1, :] * (t_r + 1.0))
            v = 1.0 - a * a
            mult = v * lax.rsqrt(jnp.maximum(v, 1e-30))
            u_scr[base:base + ROW_CHUNK, SCAN_A[d]:SCAN_A[d] + LRU_W] = a
            u_scr[base:base + ROW_CHUNK, SCAN_B[d]:SCAN_B[d] + LRU_W] = (
                mult * (half_xc * (t_i + 1.0)))
            yield

    n_sc = seq // SCAN_CHUNK
    row_sc = lax.broadcasted_iota(jnp.int32, (SCAN_CHUNK, LRU_W), 0)
    steps = [1 << i for i in range(int(math.log2(SCAN_CHUNK)))]

    def scan_step(rf, rb, cf, cb_):
        rf = pl.multiple_of(rf + PAD_ROWS, SUBLANES)
        rb = pl.multiple_of(rb + PAD_ROWS, SUBLANES)
        a = u_scr[pl.ds(rf, SCAN_CHUNK), SCAN_A[0]:SCAN_A[0] + LRU_W]
        b = u_scr[pl.ds(rf, SCAN_CHUNK), SCAN_B[0]:SCAN_B[0] + LRU_W]
        b = b + jnp.where(row_sc == 0, a * cf, 0.0)
        for d in steps:
            if d % SUBLANES == 0:
                b = jnp.concatenate([b[:d], b[d:] + a[d:] * b[:-d]], axis=0)
                if d != steps[-1]:
                    a = jnp.concatenate([a[:d], a[d:] * a[:-d]], axis=0)
                continue
            b = b + a * jnp.where(row_sc >= d, pltpu.roll(b, d, 0), 0.0)
            if d != steps[-1]:
                a = a * jnp.where(row_sc >= d, pltpu.roll(a, d, 0), 1.0)
        u_scr[pl.ds(rf, SCAN_CHUNK), SCAN_B[0]:SCAN_B[0] + LRU_W] = b
        cf = b[SCAN_CHUNK - 1:SCAN_CHUNK, :]
        a = u_scr[pl.ds(rb, SCAN_CHUNK), SCAN_A[1]:SCAN_A[1] + LRU_W]
        b = u_scr[pl.ds(rb, SCAN_CHUNK), SCAN_B[1]:SCAN_B[1] + LRU_W]
        b = b + jnp.where(row_sc == SCAN_CHUNK - 1, a * cb_, 0.0)
        for d in steps:
            if d % SUBLANES == 0:
                b = jnp.concatenate([b[:-d] + a[:-d] * b[d:], b[-d:]], axis=0)
                if d != steps[-1]:
                    a = jnp.concatenate([a[:-d] * a[d:], a[-d:]], axis=0)
                continue
            keep = row_sc < SCAN_CHUNK - d
            b = b + a * jnp.where(keep, pltpu.roll(b, SCAN_CHUNK - d, 0), 0.0)
            if d != steps[-1]:
                a = a * jnp.where(keep, pltpu.roll(a, SCAN_CHUNK - d, 0), 1.0)
        u_scr[pl.ds(rb, SCAN_CHUNK), SCAN_B[1]:SCAN_B[1] + LRU_W] = b
        cb_ = b[0:1, :]
        return cf, cb_

    cps = seq // ROW_CHUNK

    def softmax_pv(st, vts):
        m = jnp.max(st, axis=0, keepdims=True)
        pt = jnp.exp2(st - m).astype(BF16)
        ot = None
        for i, vt in enumerate(vts):
            part = _dot(vt, pt[i * ROW_CHUNK:(i + 1) * ROW_CHUNK, :])
            ot = part if ot is None else ot + part
        return ot[0:V_DIM, :] * (1.0 / ot[V_DIM:V_DIM + 1, :])

    def f_build(c):
        r0 = c * ROW_CHUNK
        rows = slice(r0, r0 + ROW_CHUNK)
        urows = slice(PAD_ROWS + r0, PAD_ROWS + r0 + ROW_CHUNK)
        hf = u_scr[urows, SCAN_B[0]:SCAN_B[0] + LRU_W]
        hb = u_scr[urows, SCAN_B[1]:SCAN_B[1] + LRU_W]
        yb = u_scr[urows, C_YB:C_YB + LRU_W]
        yl = (hf + hb) * _gelu_tanh(yb)
        yln = yl * _rms_scale(yl, LRU_W) * vec("gnorm_lru")
        ycat_scr[rows, CONV_W:CONV_W + LRU_W] = yln.astype(BF16)
        yield
        ymt = ymt_scr[c]
        ssq = jnp.sum(ymt * ymt, axis=0, keepdims=True)
        ynt = ymt * lax.rsqrt(ssq * (1.0 / (MLA_HEADS * V_DIM)) + EPS)
        ycat_scr[rows, CONV_W + LRU_W:D_MODEL] = (ynt.T * vec("gnorm_mla")).astype(BF16)
        yield

    def f_project(c):
        rows = slice(c * ROW_CHUNK, (c + 1) * ROW_CHUNK)
        for n0 in range(0, D_MODEL, PROJ_SLAB):
            cols = slice(n0, n0 + PROJ_SLAB)
            y = _dot(ycat_scr[rows, :], wout_ref[:, cols])
            h_out_ref[rows, cols] = x_ref[rows, cols] + g1[:, cols] * y
            yield

    if latent:
        groups = [list(range(g0, g0 + PROJ_CHUNKS)) for g0 in range(0, n_chunks, PROJ_CHUNKS)]
        _interleave(phase_a(0, PROJ_CHUNKS))
        for gi, group in enumerate(groups):
            dependent = []
            for c in group:
                dependent.append(phase_b(c))
                if c:
                    dependent.append(phase_c(c - 1))
            ahead = [phase_a(groups[gi + 1][0], PROJ_CHUNKS)] if gi + 1 < len(groups) else []
            _interleave(*ahead, itertools.chain(*dependent))
        _interleave(phase_c(n_chunks - 1))

        key_chunks = list(range(cps)) + [n_chunks]
        n_blk = cps * (MLA_HEADS // 2)
        trips = n_blk // 2
        scans_per_trip = n_sc // trips

        def heads_of(idx):
            qc = idx // (MLA_HEADS // 2)
            j = idx % (MLA_HEADS // 2)
            return qc, [2 * j, 2 * j + 1]

        def scores(idx, slot):
            qc, hs = heads_of(idx)
            for par, h in enumerate(hs):
                qt = qt_scr[h, qc]
                m = None
                for i in range(len(key_chunks)):
                    rows = slice(i * ROW_CHUNK, (i + 1) * ROW_CHUNK)
                    st = _dot(k_scr[h, rows, :], qt)
                    st_scr[slot, par, rows, :] = st
                    mc = jnp.max(st, axis=0, keepdims=True)
                    m = mc if m is None else jnp.maximum(m, mc)
                    yield
                m_scr[slot, par] = m

        def finish(idx, slot):
            qc, hs = heads_of(idx)
            for par, h in enumerate(hs):
                hr = pl.ds(pl.multiple_of(h * V_DIM, V_DIM), V_DIM)
                vr = pl.ds(pl.multiple_of(h * V_EXT, V_EXT - V_DIM), V_EXT)
                m = m_scr[slot, par]
                ot = None
                for i, kc in enumerate(key_chunks):
                    rows = slice(i * ROW_CHUNK, (i + 1) * ROW_CHUNK)
                    pt = jnp.exp2(st_scr[slot, par, rows, :] - m).astype(BF16)
                    part = _dot(vt_scr[kc, vr, :], pt)
                    ot = part if ot is None else ot + part
                    yield
                ymt_scr[qc, hr, :] = ot[0:V_DIM, :] * (1.0 / ot[V_DIM:V_DIM + 1, :])

        _interleave(scores(0, 0))

        def att_body(i, carry):
            b0 = 2 * i
            state = {"carry": carry, "c": i * scans_per_trip}

            def scans(n):
                for _ in range(n):
                    c = state["c"]
                    rf = pl.multiple_of(c * SCAN_CHUNK, SCAN_CHUNK)
                    rb = pl.multiple_of((n_sc - 1 - c) * SCAN_CHUNK, SCAN_CHUNK)
                    state["carry"] = scan_step(rf, rb, *state["carry"])
                    state["c"] = c + 1
                    yield
                    yield
                    yield

            half = scans_per_trip // 2
            _interleave(scores(b0 + 1, 1), finish(b0, 0), scans(half))
            _interleave(scores(jnp.minimum(b0 + 2, n_blk - 1), 0), finish(b0 + 1, 1),
                        scans(scans_per_trip - half))
            return state["carry"]

        lax.fori_loop(0, trips, att_body, (h0_ref[0:1, :], h0_ref[1:2, :]))

        _interleave(f_build(0))
        for c in range(n_chunks):
            ahead = [f_build(c + 1)] if c + 1 < n_chunks else []
            _interleave(f_project(c), *ahead)
    else:
        def phase_d(s):
            carry = (jnp.zeros((1, LRU_W), F32), jnp.zeros((1, LRU_W), F32))
            for c in range(n_sc):
                carry = scan_step(s * seq + c * SCAN_CHUNK,
                                  s * seq + (n_sc - 1 - c) * SCAN_CHUNK, *carry)
                yield
            st_out_ref[s, 0:1, :] = carry[0]
            st_out_ref[s, 1:2, :] = carry[1]

        def phase_e(s):
            blocks = [(qi, h) for qi in range(cps) for h in range(MLA_HEADS)]

            def scores(blk):
                qi, h = blk
                return _dot(k_scr[h, s * seq:(s + 1) * seq, :], qt_scr[h, s * cps + qi])

            st = scores(blocks[0])
            for i, (qi, h) in enumerate(blocks):
                nxt = scores(blocks[i + 1]) if i + 1 < len(blocks) else None
                hr = slice(h * V_DIM, (h + 1) * V_DIM)
                vr = slice(h * V_EXT, (h + 1) * V_EXT)
                vts = [vt_scr[s * cps + j, vr, :] for j in range(cps)]
                ymt_scr[s * cps + qi, hr, :] = softmax_pv(st, vts)
                st = nxt
                yield

        def sequence(s):
            chunks = range(s * cps, (s + 1) * cps)
            for c in chunks:
                yield from phase_a(c)
            for c in chunks:
                yield from phase_b(c)
            for c in chunks:
                yield from phase_c(c)
            yield from phase_d(s)
            yield from phase_e(s)
            for c in chunks:
                yield from f_build(c)
                yield from f_project(c)

        side = _ffn_steps(*ffn_in_refs, ffn_out_ref) if fused_ffn else None
        _interleave_staggered([sequence(s) for s in range(n_seq)], SEQ_STAGGER,
                              side=side, side_period=FUSED_FFN_PERIOD)


def _const_spec(shape):
    nd = len(shape)
    return pl.BlockSpec(shape, lambda i, _n=nd: (0,) * _n,
                        pipeline_mode=pl.Buffered(1))


def _layer_spec(arr, layer):
    nd = arr.ndim - 1
    return pl.BlockSpec((None,) + arr.shape[1:], lambda i, _n=nd: (layer,) + (0,) * _n,
                        pipeline_mode=pl.Buffered(1))


def _mix_call(x2d, mods, layer, lw, *, seq, n_seq, latent, extra=None, ffn_h2d=None,
              ffn_rows_per_mod=None, ffn_w=None, caches=None):
    n_tok = x2d.shape[0]
    T = seq * n_seq
    grid = (n_tok // T,)
    fused_ffn = ffn_h2d is not None
    n_keys_buf = T if not latent else seq + PAST_LEN

    if latent:
        mod_map = lambda i: (0, i, 0, 0)
    else:
        mod_map = lambda i: (0, 4, 0, 0)

    consts = [lw["vecs"], lw["w_in"], lw["w_in_last"], lw["conv_w"], lw["lru_conv_w"],
              lw["w_gates"], lw["lru_lambda"], lw["wq_t"], lw["wk"], lw["wv_t"], lw["qg_t"],
              lw["w_out"]]
    args = [x2d, mods] + consts
    in_specs = [pl.BlockSpec((T, D_MODEL), lambda i: (i, 0)),
                pl.BlockSpec((None, None, 1, 6 * D_MODEL), mod_map)]
    in_specs += [_layer_spec(a, layer) for a in consts]
    if latent:
        lconsts = [lw["wq_partner_t"], lw["qg_partner_t"]]
        tables = [extra["rope_ct"], extra["rope_st"], extra["rope_c"], extra["rope_s"]]
        args += lconsts + tables
        in_specs += [_layer_spec(a, layer) for a in lconsts]
        in_specs += [_const_spec(a.shape) for a in tables]
        args += [extra["cache_ckv"], extra["cache_kr"], extra["state"]]
        in_specs += [
            pl.BlockSpec((None, None, PAST_LEN, KV_LORA), lambda i: (i, layer, 0, 0)),
            pl.BlockSpec((None, None, PAST_LEN, LANES), lambda i: (i, layer, 0, 0)),
            pl.BlockSpec((None, None, 2, LRU_W), lambda i: (i, layer, 0, 0)),
        ]

    if fused_ffn:
        ffn_rows = ffn_h2d.shape[0] // grid[0]
        tiles_per_mod = ffn_rows_per_mod // ffn_rows
        args += [ffn_h2d, mods, lw["norm2_g"]] + list(ffn_w)
        in_specs += [pl.BlockSpec((ffn_rows, D_MODEL), lambda i: (i, 0)),
                     pl.BlockSpec((None, None, 1, 6 * D_MODEL),
                                  lambda i: (0, i // tiles_per_mod, 0, 0))]
        in_specs += [_layer_spec(lw["norm2_g"], layer)] + [_layer_spec(a, 0) for a in ffn_w]

    out_shape = [jax.ShapeDtypeStruct((n_tok, D_MODEL), F32)]
    out_specs = [pl.BlockSpec((T, D_MODEL), lambda i: (i, 0))]
    if not latent:
        n_b = n_tok // seq
        out_shape += [jax.ShapeDtypeStruct((n_b, DEPTH, seq, KV_LORA), F32),
                      jax.ShapeDtypeStruct((n_b, DEPTH, seq, QK_ROPE), F32),
                      jax.ShapeDtypeStruct((n_b, DEPTH, 2, LRU_W), F32)]
        out_specs += [pl.BlockSpec((n_seq, None, seq, KV_LORA), lambda i: (i, layer, 0, 0)),
                      pl.BlockSpec((n_seq, None, seq, QK_ROPE), lambda i: (i, layer, 0, 0)),
                      pl.BlockSpec((n_seq, None, 2, LRU_W), lambda i: (i, layer, 0, 0))]
    aliases = {}
    if caches is not None:
        for k, arr in enumerate(caches):
            aliases[len(args)] = 1 + k
            args.append(arr)
            in_specs.append(pl.BlockSpec(memory_space=pl.ANY))
    if fused_ffn:
        out_shape.append(jax.ShapeDtypeStruct(ffn_h2d.shape, F32))
        out_specs.append(pl.BlockSpec((ffn_rows, D_MODEL), lambda i: (i, 0)))

    scratch = [
        pltpu.VMEM((T + 2 * PAD_ROWS, U_W), F32),
        pltpu.VMEM((MLA_HEADS, T // ROW_CHUNK, HEAD_PAD, ROW_CHUNK), BF16),
        pltpu.VMEM((MLA_HEADS, n_keys_buf, HEAD_PAD), BF16),
        pltpu.VMEM((n_keys_buf // ROW_CHUNK, MLA_HEADS * V_EXT, ROW_CHUNK), BF16),
        pltpu.VMEM((T // ROW_CHUNK, MLA_HEADS * V_DIM, ROW_CHUNK), F32),
        pltpu.VMEM((T, D_MODEL), BF16),
    ]
    if latent:
        scratch.append(pltpu.VMEM((2, 2, n_keys_buf, ROW_CHUNK), F32))
        scratch.append(pltpu.VMEM((2, 2, 1, ROW_CHUNK), F32))
    return pl.pallas_call(
        functools.partial(_mix_kernel, seq=seq, n_seq=n_seq, latent=latent,
                          fused_ffn=fused_ffn, n_aliased=len(aliases)),
        grid=grid,
        in_specs=in_specs,
        out_specs=out_specs,
        out_shape=out_shape,
        scratch_shapes=scratch,
        input_output_aliases=aliases,
        compiler_params=pltpu.CompilerParams(
            dimension_semantics=("arbitrary",),
            vmem_limit_bytes=VMEM_LIMIT_BYTES),
        name="mix_latent" if latent else ("mix_context_swiglu" if fused_ffn else "mix_context"),
    )(*args)


def _ffn_steps(h_ref, mod_ref, n2g_ref, wg_ref, wu_ref, wd_ref, o_ref, per_chunk=None):
    sh2 = mod_ref[:, 3 * D_MODEL:4 * D_MODEL]
    sc2 = mod_ref[:, 4 * D_MODEL:5 * D_MODEL]
    g2 = mod_ref[:, 5 * D_MODEL:6 * D_MODEL]
    h = h_ref[...]
    hn = (h * _rms_scale(h, D_MODEL) * (n2g_ref[...] * (1.0 + sc2)) + sh2).astype(BF16)
    yield
    acc = None
    for c0 in range(0, FF, FF_CHUNK):
        g = _dot(hn, wg_ref[:, c0:c0 + FF_CHUNK])
        u = _dot(hn, wu_ref[:, c0:c0 + FF_CHUNK])
        act = (g * _sigmoid(g) * u).astype(BF16)
        part = _dot(act, wd_ref[c0:c0 + FF_CHUNK, :])
        acc = part if acc is None else acc + part
        if per_chunk is not None:
            per_chunk(c0 // FF_CHUNK)
        yield
    o_ref[...] = h_ref[...] + g2 * acc
    yield


def _ffn_kernel(*refs, n_cast, with_ada):
    n_ada = 3 if with_ada else 0
    n_in = 6 + n_cast + n_ada
    ffn_in, cast_in, ada_in = refs[:6], refs[6:6 + n_cast], refs[6 + n_cast:n_in]
    o_ref, cast_out = refs[n_in], refs[n_in + 1:n_in + 1 + n_cast]
    n_chunks = FF // FF_CHUNK

    def cast_piece(k):
        for src, dst in zip(cast_in, cast_out):
            rows, cols = src.shape
            if cols % n_chunks == 0 and (cols // n_chunks) % LANES == 0:
                sl = (slice(None), slice(k * (cols // n_chunks), (k + 1) * (cols // n_chunks)))
            else:
                assert rows % n_chunks == 0 and (rows // n_chunks) % 16 == 0
                sl = (slice(k * (rows // n_chunks), (k + 1) * (rows // n_chunks)), slice(None))
            dst[sl] = src[sl].astype(BF16)

    if with_ada:
        _ada_tile(*ada_in, refs[n_in + 1 + n_cast])
    _interleave(_ffn_steps(*ffn_in, o_ref, per_chunk=cast_piece if n_cast else None))


def _ffn_call(h2d, mods, layer, lw, ffn_w, *, rows_per_mod, mod_base, cast_next=None,
              ada_next=None):
    n_tok = h2d.shape[0]
    n_steps = n_tok // FFN_ROWS
    if mod_base is None:
        tiles_per_mod = rows_per_mod // FFN_ROWS
        mod_map = lambda i: (0, i // tiles_per_mod, 0, 0)
    else:
        mod_map = lambda i: (0, mod_base, 0, 0)
    args = [h2d, mods, lw["norm2_g"]] + list(ffn_w)
    in_specs = [pl.BlockSpec((FFN_ROWS, D_MODEL), lambda i: (i, 0)),
                pl.BlockSpec((None, None, 1, 6 * D_MODEL), mod_map),
                _layer_spec(lw["norm2_g"], layer)]
    in_specs += [_layer_spec(a, 0) for a in ffn_w]
    out_shape = [jax.ShapeDtypeStruct((n_tok, D_MODEL), F32)]
    out_specs = [pl.BlockSpec((FFN_ROWS, D_MODEL), lambda i: (i, 0))]
    for w in cast_next or ():
        rows = w.shape[1] // n_steps
        assert rows * n_steps == w.shape[1] and rows % 16 == 0
        args.append(w)
        in_specs.append(pl.BlockSpec((None, rows, w.shape[2]), lambda i: (layer + 1, i, 0)))
        out_shape.append(jax.ShapeDtypeStruct((1,) + w.shape[1:], BF16))
        out_specs.append(pl.BlockSpec((None, rows, w.shape[2]), lambda i: (0, i, 0)))
    if ada_next is not None:
        cond8, ada_w, ada_b3 = ada_next
        n_out = ada_w.shape[2]
        tn = n_out // n_steps
        assert tn * n_steps == n_out and tn % LANES == 0
        args += [cond8, ada_w, ada_b3]
        in_specs += [pl.BlockSpec((8, D_MODEL), lambda i: (0, 0)),
                     pl.BlockSpec((None, D_MODEL, tn), lambda i: (layer + 1, 0, i)),
                     pl.BlockSpec((None, 1, tn), lambda i: (layer + 1, 0, i))]
        out_shape.append(jax.ShapeDtypeStruct((1, 8, 1, n_out), F32))
        out_specs.append(pl.BlockSpec((None, 8, 1, tn), lambda i: (0, 0, 0, i)))
    return pl.pallas_call(
        functools.partial(_ffn_kernel, n_cast=len(cast_next or ()),
                          with_ada=ada_next is not None),
        grid=(n_steps,),
        in_specs=in_specs,
        out_specs=out_specs,
        out_shape=out_shape,
        compiler_params=pltpu.CompilerParams(
            dimension_semantics=("arbitrary",),
            vmem_limit_bytes=VMEM_LIMIT_BYTES),
        name="swiglu",
    )(*args)


def _rope_partner_index():
    return [d + 8 if (d % 16) < 8 else d - 8 for d in range(QK_ROPE)]


def _prep_weights(p):
    partner = jnp.array(_rope_partner_index(), jnp.int32)
    q_scale = QK_DIM ** -0.5 * math.log2(math.e)
    w_in = p["w_in"][:, :, :C_CKV].astype(BF16)
    w_tail = p["w_in"][:, :, C_CKV:].astype(BF16)
    w_kr = w_tail[:, :, KV_LORA:KV_LORA + QK_ROPE]
    zeros32 = jnp.zeros((DEPTH, D_MODEL, QK_ROPE), BF16)
    w_in_last = jnp.concatenate(
        [w_tail[:, :, :KV_LORA], w_kr[:, :, partner], zeros32, w_kr, zeros32], axis=2)

    wq_h = p["mla_wq_up"].reshape(DEPTH, Q_LORA, MLA_HEADS, QK_DIM)
    wq_pad = jnp.pad(wq_h, ((0, 0), (0, 0), (0, 0), (0, HEAD_PAD - QK_DIM)))
    wq_t = wq_pad.reshape(DEPTH, Q_LORA, MLA_HEADS * HEAD_PAD).transpose(0, 2, 1)
    rope_pad = ((0, 0), (0, 0), (0, 0), (ROPE_LO, HEAD_PAD - ROPE_LO - QK_ROPE))
    wq_partner = jnp.pad(wq_h[:, :, :, QK_NOPE:][:, :, :, partner], rope_pad)
    wq_partner_t = wq_partner.reshape(DEPTH, Q_LORA, MLA_HEADS * HEAD_PAD).transpose(0, 2, 1)

    wkv = p["mla_wkv_up"].reshape(DEPTH, KV_LORA, MLA_HEADS, QK_NOPE + V_DIM)
    wk_pad = jnp.pad(wkv[:, :, :, :QK_NOPE], ((0, 0), (0, 0), (0, 0), (0, HEAD_PAD - QK_NOPE)))
    wk_pad = wk_pad.reshape(DEPTH, KV_LORA, MLA_HEADS * HEAD_PAD)
    wv_ext = jnp.pad(wkv[:, :, :, QK_NOPE:], ((0, 0), (0, 0), (0, 0), (0, V_EXT - V_DIM)))
    wv_t = wv_ext.reshape(DEPTH, KV_LORA, MLA_HEADS * V_EXT).transpose(0, 2, 1)

    def pad_gain(g):
        return jnp.pad(g, ((0, 0), (0, HEAD_PAD - QK_DIM)))[:, None, :]

    def partner_gain(g):
        gp = g[:, QK_NOPE:][:, partner]
        return jnp.pad(gp, ((0, 0), (ROPE_LO, HEAD_PAD - ROPE_LO - QK_ROPE)))[:, None, :]

    def on_sublanes(g):
        return jnp.broadcast_to(g.transpose(0, 2, 1), (DEPTH, HEAD_PAD, ROW_CHUNK))

    qg, kg = p["q_norm_g"], p["k_norm_g"]
    eye = jnp.eye(LRU_BLOCKS, dtype=F32)
    gate_w = jnp.stack([p["lru_wa"][:, 0], p["lru_wi"][:, 0],
                        p["lru_wa"][:, 1], p["lru_wi"][:, 1]], axis=1)
    w_gates = (gate_w[:, :, :, :, None, :] * eye[None, None, :, None, :, None])
    w_gates = w_gates.transpose(0, 2, 3, 1, 4, 5).reshape(DEPTH, LRU_W, 4 * LRU_W)
    b_gates = jnp.stack([p["lru_ba"][:, 0], p["lru_bi"][:, 0],
                         p["lru_ba"][:, 1], p["lru_bi"][:, 1]], axis=1)

    def row(v):
        return v[:, None, :]

    rows = {
        "norm1_g": p["norm1_g"], "conv_b": p["conv_b"], "lru_conv_b": p["lru_conv_b"],
        "b_gates": 0.5 * b_gates.reshape(DEPTH, 4 * LRU_W),
        "mla_qnorm_g": p["mla_qnorm_g"], "mla_kvnorm_g": p["mla_kvnorm_g"],
        "kg": pad_gain(kg)[:, 0, :], "kg_partner": partner_gain(kg)[:, 0, :],
        "gnorm_conv": p["gnorm_conv"], "gnorm_lru": p["gnorm_lru"],
        "gnorm_mla": p["gnorm_mla"],
    }
    for name, (_, width) in VEC_SLOTS.items():
        assert rows[name].shape == (DEPTH, width), name
    vecs = row(jnp.concatenate([rows[name] for name in VEC_SLOTS], axis=1))

    return {
        "vecs": vecs,
        "w_in": w_in,
        "w_in_last": w_in_last,
        "conv_w": p["conv_w"],
        "lru_conv_w": p["lru_conv_w"],
        "w_gates": (0.5 * w_gates).astype(BF16),
        "lru_lambda": p["lru_lambda"],
        "wq_t": wq_t.astype(BF16),
        "wq_partner_t": wq_partner_t.astype(BF16),
        "wk": wk_pad.astype(BF16),
        "wv_t": wv_t.astype(BF16),
        "qg_t": on_sublanes(pad_gain(qg) * q_scale),
        "qg_partner_t": on_sublanes(partner_gain(qg) * q_scale),
        "w_out": p["w_out"].astype(BF16),
        "norm2_g": row(p["norm2_g"]),
    }


def _rope_tables(n_tokens):
    n_rows = n_tokens // GRID_W
    row = np.repeat(np.arange(n_rows, dtype=np.float32), GRID_W)
    col = np.tile(np.arange(GRID_W, dtype=np.float32), n_rows)
    n_freq = QK_ROPE // 4
    inv_freq = np.power(np.float32(ROPE_THETA),
                        -np.arange(n_freq, dtype=np.float32) / np.float32(n_freq))
    ang_r = (row[:, None] * inv_freq).astype(np.float32)
    ang_c = (col[:, None] * inv_freq).astype(np.float32)
    cos = np.concatenate([np.cos(ang_r), np.cos(ang_r), np.cos(ang_c), np.cos(ang_c)], 1)
    sin = np.concatenate([-np.sin(ang_r), np.sin(ang_r), -np.sin(ang_c), np.sin(ang_c)], 1)
    ones = np.ones((n_tokens, QK_NOPE), np.float32)
    tail = np.zeros((n_tokens, HEAD_PAD - QK_DIM), np.float32)
    rope_c = np.concatenate([ones, cos, tail], axis=1).astype(np.float32)
    rope_s = np.concatenate([np.zeros((n_tokens, QK_NOPE), np.float32), sin, tail],
                            axis=1).astype(np.float32)
    return rope_c, rope_s


def kernel(x_prompt, x_sample, cache_ckv, cache_krope, state_lru, c, c_ctx, norm1_g, ada_w, ada_b, w_in, conv_w, conv_b, lru_conv_w, lru_conv_b, lru_wa, lru_ba, lru_wi, lru_bi, lru_lambda, mla_qnorm_g, mla_wq_up, mla_kvnorm_g, mla_wkv_up, q_norm_g, k_norm_g, gnorm_conv, gnorm_lru, gnorm_mla, w_out, norm2_g, w_gate, w_up, w_down):
    params = dict(norm1_g=norm1_g, w_in=w_in, conv_w=conv_w, conv_b=conv_b,
                  lru_conv_w=lru_conv_w, lru_conv_b=lru_conv_b, lru_wa=lru_wa,
                  lru_ba=lru_ba, lru_wi=lru_wi, lru_bi=lru_bi, lru_lambda=lru_lambda,
                  mla_qnorm_g=mla_qnorm_g, mla_wq_up=mla_wq_up,
                  mla_kvnorm_g=mla_kvnorm_g, mla_wkv_up=mla_wkv_up, q_norm_g=q_norm_g,
                  k_norm_g=k_norm_g, gnorm_conv=gnorm_conv, gnorm_lru=gnorm_lru,
                  gnorm_mla=gnorm_mla, w_out=w_out, norm2_g=norm2_g, w_gate=w_gate,
                  w_up=w_up, w_down=w_down)
    batch, seq_p, _ = x_prompt.shape
    dec_batch, seq_s, _ = x_sample.shape
    assert dec_batch == 4 and cache_ckv.shape[2] == PAST_LEN == ROW_CHUNK

    cond8 = jnp.concatenate(
        [c, c_ctx[None, :], jnp.zeros((8 - dec_batch - 1, D_MODEL), F32)], axis=0)
    ada_b3 = ada_b.reshape(DEPTH, 1, 6 * D_MODEL)
    ffn_f32 = (w_gate, w_up, w_down)
    mods, *ffn_w = _ada_call(cond8, ada_w, ada_b3, ffn_f32)

    rope_c, rope_s = _rope_tables(seq_s)
    cache_kr_pad = jnp.pad(
        cache_krope, ((0, 0), (0, 0), (0, 0), (ROPE_LO, HEAD_PAD - ROPE_LO - QK_ROPE)))

    def chunked_t(tab):
        return jnp.asarray(np.ascontiguousarray(
            tab.reshape(seq_s // ROW_CHUNK, ROW_CHUNK, HEAD_PAD).transpose(0, 2, 1)))

    extra = dict(rope_c=jnp.asarray(rope_c), rope_s=jnp.asarray(rope_s),
                 rope_ct=chunked_t(rope_c), rope_st=chunked_t(rope_s), cache_ckv=cache_ckv,
                 cache_kr=cache_kr_pad, state=state_lru)

    xp = x_prompt.reshape(batch * seq_p, D_MODEL)
    xs = x_sample.reshape(dec_batch * seq_s, D_MODEL)
    caches = None
    lw = _prep_weights(params)
    for l in range(DEPTH):
        (hs,) = _mix_call(xs, mods, l, lw, seq=seq_s, n_seq=1, latent=True, extra=extra)
        hp, *caches, xs = _mix_call(xp, mods, l, lw, seq=seq_p, n_seq=CTX_SEQS_PER_STEP,
                                    latent=False, ffn_h2d=hs, ffn_rows_per_mod=seq_s,
                                    ffn_w=ffn_w, caches=caches)
        if l + 1 < DEPTH:
            xp, *ffn_w, next_mods = _ffn_call(hp, mods, l, lw, ffn_w, rows_per_mod=None,
                                              mod_base=4, cast_next=ffn_f32,
                                              ada_next=(cond8, ada_w, ada_b3))
        else:
            (xp,) = _ffn_call(hp, mods, l, lw, ffn_w, rows_per_mod=None, mod_base=4)
            next_mods = None
        mods = next_mods
    new_cache_ckv, new_cache_krope, new_state_lru = caches
    return (xp.reshape(batch, seq_p, D_MODEL),
            xs.reshape(dec_batch, seq_s, D_MODEL),
            new_cache_ckv, new_cache_krope, new_state_lru)
```

```python
import functools
import itertools
import math

import jax
import jax.numpy as jnp
import numpy as np
from jax import lax
from jax.experimental import pallas as pl
from jax.experimental.pallas import tpu as pltpu

F32 = jnp.float32
BF16 = jnp.bfloat16

D_MODEL = 1024
DEPTH = 2
GRID_W = 64
CONV_W = 256
LRU_W = 256
LRU_BLOCKS = 4
LRU_BLK = 64
LRU_C = 8.0
MLA_HEADS = 8
QK_NOPE = 64
QK_ROPE = 32
V_DIM = 64
V_EXT = V_DIM + 16
QK_DIM = QK_NOPE + QK_ROPE
Q_LORA = 256
KV_LORA = 128
ROPE_THETA = 10000.0
FF = 2816
EPS = 1e-6

LANES = 128
SUBLANES = 8
HEAD_PAD = LANES
VMEM_LIMIT_BYTES = 60 * 1024 * 1024

C_BG, C_CG, C_H = 0, 256, 512
C_XB, C_YB = 768, 1024
C_Q = 1280
C_CKV = 1536
C_KR = 1664
U_W = 1792
SCAN_A = (C_H, C_CKV)
SCAN_B = (C_Q, C_BG)
ROPE_LO = QK_NOPE

def _slots(widths):
    out, off = {}, 0
    for name, width in widths:
        out[name] = (off, width)
        off += width
    return out


VEC_SLOTS = _slots([
    ("norm1_g", D_MODEL), ("conv_b", CONV_W), ("lru_conv_b", LRU_W), ("b_gates", 4 * LRU_W),
    ("mla_qnorm_g", Q_LORA), ("mla_kvnorm_g", KV_LORA), ("kg", HEAD_PAD),
    ("kg_partner", HEAD_PAD), ("gnorm_conv", CONV_W), ("gnorm_lru", LRU_W),
    ("gnorm_mla", MLA_HEADS * V_DIM)])

PAST_LEN = 256
ROW_CHUNK = 256
SCAN_CHUNK = 32
FF_CHUNK = 256
PROJ_SLAB = 256
PROJ_CHUNKS = 1
CTX_SEQS_PER_STEP = 2
FUSED_FFN_PERIOD = 4
SEQ_STAGGER = 10
FFN_ROWS = 1024
PAD_ROWS = SUBLANES


def _rms_scale(x, n):
    ms = jnp.sum(x * x, axis=-1, keepdims=True) * (1.0 / n)
    return lax.rsqrt(ms + EPS)


def _sigmoid(x):
    return 0.5 * jnp.tanh(0.5 * x) + 0.5


def _gelu_tanh(x):
    c = math.sqrt(2.0 / math.pi)
    return 0.5 * x * (1.0 + jnp.tanh(c * (x + 0.044715 * (x * x * x))))


def _dot(a, b):
    return jnp.dot(a, b, preferred_element_type=F32)


def _interleave(*gens):
    live = list(gens)
    while live:
        for g in list(live):
            try:
                next(g)
            except StopIteration:
                live.remove(g)


def _interleave_staggered(gens, lag, side=None, side_period=1):
    pending = list(gens)
    live = []
    tick = 0
    while live or pending or side is not None:
        if pending and tick % lag == 0:
            live.append(pending.pop(0))
        if side is not None and (tick % side_period == 0 or not (live or pending)):
            try:
                next(side)
            except StopIteration:
                side = None
        for g in list(live):
            try:
                next(g)
            except StopIteration:
                live.remove(g)
        tick += 1


def _dot_nt(a, b):
    return lax.dot_general(a, b, (((1,), (1,)), ((), ())), preferred_element_type=F32)


ADA_TN = 1536


def _ada_tile(cond_ref, w_ref, b_ref, o_ref):
    cnd = cond_ref[...]
    s = (cnd * _sigmoid(cnd)).astype(BF16)
    o_ref[:, 0, :] = _dot(s, w_ref[...].astype(BF16)) + b_ref[...]


def _ada_kernel(cond_ref, w_ref, b_ref, o_ref):
    _ada_tile(cond_ref, w_ref, b_ref, o_ref)


def _ada_call(cond8, ada_w, ada_b3):
    n_out = 6 * D_MODEL
    return pl.pallas_call(
        _ada_kernel,
        grid=(n_out // ADA_TN,),
        in_specs=[
            pl.BlockSpec((8, D_MODEL), lambda j: (0, 0)),
            pl.BlockSpec((None, D_MODEL, ADA_TN), lambda j: (0, 0, j)),
            pl.BlockSpec((None, 1, ADA_TN), lambda j: (0, 0, j)),
        ],
        out_specs=pl.BlockSpec((None, 8, 1, ADA_TN), lambda j: (0, 0, 0, j)),
        out_shape=jax.ShapeDtypeStruct((1, 8, 1, n_out), F32),
        compiler_params=pltpu.CompilerParams(
            dimension_semantics=("arbitrary",),
            vmem_limit_bytes=VMEM_LIMIT_BYTES),
        name="ada_mod",
    )(cond8, ada_w, ada_b3)


def _mix_kernel(*refs, seq, n_seq, latent, fused_ffn, n_aliased):
    T = seq * n_seq
    n_ctx = PAST_LEN if latent else 0
    (x_ref, mod_ref, vec_ref, win_ref, wlast_ref, convw_ref, lcw_ref, wg_ref, lam_ref,
     wqt_ref, wk_ref, wvt_ref, qgt_ref, wout_ref) = refs[:14]
    pos = 14
    if latent:
        (wqpt_ref, qgpt_ref, ropect_ref, ropest_ref, ropec_ref, ropes_ref,
         cckv_ref, ckr_ref, h0_ref) = refs[pos:pos + 9]
        pos += 9

    def vec(name):
        off, width = VEC_SLOTS[name]
        return vec_ref[:, off:off + width]
    if fused_ffn:
        ffn_in_refs = refs[pos:pos + 6]
        pos += 6
    pos += n_aliased
    h_out_ref = refs[pos]
    pos += 1
    if not latent:
        ckv_out_ref, kr_out_ref, st_out_ref = refs[pos:pos + 3]
        pos += 3
    if fused_ffn:
        ffn_out_ref = refs[pos]
        pos += 1
    u_scr, qt_scr, k_scr, vt_scr, ymt_scr, ycat_scr = refs[pos:pos + 6]
    if latent:
        st_scr, m_scr = refs[pos + 6:pos + 8]

    n_chunks = T // ROW_CHUNK
    lane = lax.broadcasted_iota(jnp.int32, (1, LANES), 1)
    rope_lanes = (lane >= ROPE_LO) & (lane < ROPE_LO + QK_ROPE)

    sh1 = mod_ref[:, 0:D_MODEL]
    sc1 = mod_ref[:, D_MODEL:2 * D_MODEL]
    g1 = mod_ref[:, 2 * D_MODEL:3 * D_MODEL]
    gain1 = vec("norm1_g") * (1.0 + sc1)

    u_scr[0:PAD_ROWS, :] = jnp.zeros((PAD_ROWS, U_W), F32)
    u_scr[PAD_ROWS + T:2 * PAD_ROWS + T, :] = jnp.zeros((PAD_ROWS, U_W), F32)

    def phase_a(c, n=1):
        r0 = c * ROW_CHUNK
        nrows = n * ROW_CHUNK
        x = x_ref[r0:r0 + nrows, :]
        hn = (x * _rms_scale(x, D_MODEL) * gain1 + sh1).astype(BF16)
        yield
        urows = slice(PAD_ROWS + r0, PAD_ROWS + r0 + nrows)
        for s0 in range(0, C_CKV, PROJ_SLAB):
            u_scr[urows, s0:s0 + PROJ_SLAB] = _dot(hn, win_ref[:, s0:s0 + PROJ_SLAB])
            yield
        u_scr[urows, C_CKV:U_W] = _dot(hn, wlast_ref[...])
        yield
        u_scr[urows, C_CG:C_CG + CONV_W] = (
            u_scr[urows, C_CG:C_CG + CONV_W] * u_scr[urows, C_H:C_H + CONV_W])
        yield

    vrow = lax.broadcasted_iota(jnp.int32, (MLA_HEADS * V_EXT, 1), 0)
    ones_rows = (vrow % V_EXT) >= V_DIM

    def values_t(cb):
        return jnp.where(ones_rows, 1.0, _dot_nt(wvt_ref[...], cb)).astype(BF16)

    def phase_b(c):
        r0 = c * ROW_CHUNK
        rows = slice(PAD_ROWS + r0, PAD_ROWS + r0 + ROW_CHUNK)
        uq = u_scr[rows, C_Q:C_Q + Q_LORA]
        qn = (uq * _rms_scale(uq, Q_LORA) * vec("mla_qnorm_g")).astype(BF16)
        qa_t = _dot_nt(wqt_ref[...], qn)
        if latent:
            qpa_t = _dot_nt(wqpt_ref[...], qn)
            q_tc = ropect_ref[c] * qgt_ref[...]
            q_ts = ropest_ref[c] * qgpt_ref[...]
        yield
        for h in range(MLA_HEADS):
            sl = slice(h * HEAD_PAD, (h + 1) * HEAD_PAD)
            qh = qa_t[sl, :]
            rq = lax.rsqrt(jnp.sum(qh * qh, axis=0, keepdims=True) * (1.0 / QK_DIM) + EPS)
            if latent:
                qh = (qh * q_tc + qpa_t[sl, :] * q_ts) * rq
            else:
                qh = qh * qgt_ref[...] * rq
            qt_scr[h, c] = qh.astype(BF16)
            if h % 2:
                yield
        uc = u_scr[rows, C_CKV:C_CKV + KV_LORA]
        ckv = uc * _rms_scale(uc, KV_LORA) * vec("mla_kvnorm_g")
        krb = u_scr[rows, C_KR:C_KR + LANES]
        kr_rolled = pltpu.roll(krb, 64, 1)
        if not latent:
            sq, rq0 = divmod(r0, seq)
            ckv_out_ref[sq, rq0:rq0 + ROW_CHUNK, :] = ckv
            kr_out_ref[sq, rq0:rq0 + ROW_CHUNK, :] = kr_rolled[:, 0:QK_ROPE]
        krm = jnp.where(rope_lanes, krb, 0.0)
        cb = ckv.astype(BF16)
        ka = _dot(cb, wk_ref[...])
        vt_scr[c] = values_t(cb)
        if latent:
            k_tc = ropec_ref[r0:r0 + ROW_CHUNK, :] * vec("kg")
            k_rot = kr_rolled * (ropes_ref[r0:r0 + ROW_CHUNK, :] * vec("kg_partner"))
        yield
        for h in range(MLA_HEADS):
            sl = slice(h * HEAD_PAD, (h + 1) * HEAD_PAD)
            kpre = ka[:, sl] + krm
            rk = _rms_scale(kpre, QK_DIM)
            if latent:
                kh = (kpre * k_tc + k_rot) * rk
            else:
                kh = kpre * vec("kg") * rk
            k_scr[h, r0:r0 + ROW_CHUNK, :] = kh.astype(BF16)
            if h % 2:
                yield

    if latent:
        cc = cckv_ref[...].astype(BF16)
        ka = _dot(cc, wk_ref[...])
        vt_scr[n_chunks] = values_t(cc)
        krc = ckr_ref[...]
        for h in range(MLA_HEADS):
            kpre = ka[:, h * HEAD_PAD:(h + 1) * HEAD_PAD] + krc
            kh = kpre * vec("kg") * _rms_scale(kpre, QK_DIM)
            k_scr[h, seq:seq + n_ctx, :] = kh.astype(BF16)

    neg_lam = -lam_ref[...]
    sp = jnp.maximum(neg_lam, 0.0) + jnp.log(1.0 + jnp.exp(-jnp.abs(neg_lam)))
    log2a_coef = (-0.5 * LRU_C * math.log2(math.e)) * sp
    row_rc = lax.broadcasted_iota(jnp.int32, (ROW_CHUNK, 1), 0)

    def phase_c(ci):
        r0 = ci * ROW_CHUNK
        base = PAD_ROWS + r0
        first = (r0 % seq) == 0
        last = ((r0 + ROW_CHUNK) % seq) == 0

        def win(col, shift, width=CONV_W):
            w = u_scr[base + shift:base + shift + ROW_CHUNK, col:col + width]
            if shift < 0 and first:
                w = jnp.where(row_rc < -shift, 0.0, w)
            if shift > 0 and last:
                w = jnp.where(row_rc >= ROW_CHUNK - shift, 0.0, w)
            return w

        z_m, z_0, z_p = win(C_CG, -1), win(C_CG, 0), win(C_CG, 1)
        conv = (z_m * convw_ref[0:1, :] + z_0 * convw_ref[1:2, :]
                + z_p * convw_ref[2:3, :] + vec("conv_b"))
        yc = u_scr[base:base + ROW_CHUNK, C_BG:C_BG + CONV_W] * conv
        ycn = yc * _rms_scale(yc, CONV_W) * vec("gnorm_conv")
        ycat_scr[r0:r0 + ROW_CHUNK, 0:CONV_W] = ycn.astype(BF16)
        yield

        xc = (win(C_XB, -2) * lcw_ref[0:1, :] + win(C_XB, -1) * lcw_ref[1:2, :]
              + win(C_XB, 0) * lcw_ref[2:3, :] + win(C_XB, 1) * lcw_ref[3:4, :]
              + vec("lru_conv_b"))
        half_gates = _dot(xc.astype(BF16), wg_ref[...]) + vec("b_gates")
        half_xc = 0.5 * xc
        yield
        for d in range(2):
            t_r = jnp.tanh(half_gates[:, (2 * d) * LRU_W:(2 * d + 1) * LRU_W])
            t_i = jnp.tanh(half_gates[:, (2 * d + 1) * LRU_W:(2 * d + 2) * LRU_W])
            a = jnp.exp2(log2a_coef[d:d + 1, :] * (t_r + 1.0))
            v = 1.0 - a * a
            mult = v * lax.rsqrt(jnp.maximum(v, 1e-30))
            u_scr[base:base + ROW_CHUNK, SCAN_A[d]:SCAN_A[d] + LRU_W] = a
            u_scr[base:base + ROW_CHUNK, SCAN_B[d]:SCAN_B[d] + LRU_W] = (
                mult * (half_xc * (t_i + 1.0)))
            yield

    n_sc = seq // SCAN_CHUNK
    row_sc = lax.broadcasted_iota(jnp.int32, (SCAN_CHUNK, LRU_W), 0)
    steps = [1 << i for i in range(int(math.log2(SCAN_CHUNK)))]

    def scan_step(rf, rb, cf, cb_):
        rf = pl.multiple_of(rf + PAD_ROWS, SUBLANES)
        rb = pl.multiple_of(rb + PAD_ROWS, SUBLANES)
        a = u_scr[pl.ds(rf, SCAN_CHUNK), SCAN_A[0]:SCAN_A[0] + LRU_W]
        b = u_scr[pl.ds(rf, SCAN_CHUNK), SCAN_B[0]:SCAN_B[0] + LRU_W]
        b = b + jnp.where(row_sc == 0, a * cf, 0.0)
        for d in steps:
            if d % SUBLANES == 0:
                b = jnp.concatenate([b[:d], b[d:] + a[d:] * b[:-d]], axis=0)
                if d != steps[-1]:
                    a = jnp.concatenate([a[:d], a[d:] * a[:-d]], axis=0)
                continue
            b = b + a * jnp.where(row_sc >= d, pltpu.roll(b, d, 0), 0.0)
            if d != steps[-1]:
                a = a * jnp.where(row_sc >= d, pltpu.roll(a, d, 0), 1.0)
        u_scr[pl.ds(rf, SCAN_CHUNK), SCAN_B[0]:SCAN_B[0] + LRU_W] = b
        cf = b[SCAN_CHUNK - 1:SCAN_CHUNK, :]
        a = u_scr[pl.ds(rb, SCAN_CHUNK), SCAN_A[1]:SCAN_A[1] + LRU_W]
        b = u_scr[pl.ds(rb, SCAN_CHUNK), SCAN_B[1]:SCAN_B[1] + LRU_W]
        b = b + jnp.where(row_sc == SCAN_CHUNK - 1, a * cb_, 0.0)
        for d in steps:
            if d % SUBLANES == 0:
                b = jnp.concatenate([b[:-d] + a[:-d] * b[d:], b[-d:]], axis=0)
                if d != steps[-1]:
                    a = jnp.concatenate([a[:-d] * a[d:], a[-d:]], axis=0)
                continue
            keep = row_sc < SCAN_CHUNK - d
            b = b + a * jnp.where(keep, pltpu.roll(b, SCAN_CHUNK - d, 0), 0.0)
            if d != steps[-1]:
                a = a * jnp.where(keep, pltpu.roll(a, SCAN_CHUNK - d, 0), 1.0)
        u_scr[pl.ds(rb, SCAN_CHUNK), SCAN_B[1]:SCAN_B[1] + LRU_W] = b
        cb_ = b[0:1, :]
        return cf, cb_

    cps = seq // ROW_CHUNK

    def softmax_pv(st, vts):
        m = jnp.max(st, axis=0, keepdims=True)
        pt = jnp.exp2(st - m).astype(BF16)
        ot = None
        for i, vt in enumerate(vts):
            part = _dot(vt, pt[i * ROW_CHUNK:(i + 1) * ROW_CHUNK, :])
            ot = part if ot is None else ot + part
        return ot[0:V_DIM, :] * (1.0 / ot[V_DIM:V_DIM + 1, :])

    def f_build(c):
        r0 = c * ROW_CHUNK
        rows = slice(r0, r0 + ROW_CHUNK)
        urows = slice(PAD_ROWS + r0, PAD_ROWS + r0 + ROW_CHUNK)
        hf = u_scr[urows, SCAN_B[0]:SCAN_B[0] + LRU_W]
        hb = u_scr[urows, SCAN_B[1]:SCAN_B[1] + LRU_W]
        yb = u_scr[urows, C_YB:C_YB + LRU_W]
        yl = (hf + hb) * _gelu_tanh(yb)
        yln = yl * _rms_scale(yl, LRU_W) * vec("gnorm_lru")
        ycat_scr[rows, CONV_W:CONV_W + LRU_W] = yln.astype(BF16)
        yield
        ymt = ymt_scr[c]
        ssq = jnp.sum(ymt * ymt, axis=0, keepdims=True)
        ynt = ymt * lax.rsqrt(ssq * (1.0 / (MLA_HEADS * V_DIM)) + EPS)
        ycat_scr[rows, CONV_W + LRU_W:D_MODEL] = (ynt.T * vec("gnorm_mla")).astype(BF16)
        yield

    def f_project(c):
        rows = slice(c * ROW_CHUNK, (c + 1) * ROW_CHUNK)
        for n0 in range(0, D_MODEL, PROJ_SLAB):
            cols = slice(n0, n0 + PROJ_SLAB)
            y = _dot(ycat_scr[rows, :], wout_ref[:, cols])
            h_out_ref[rows, cols] = x_ref[rows, cols] + g1[:, cols] * y
            yield

    if latent:
        groups = [list(range(g0, g0 + PROJ_CHUNKS)) for g0 in range(0, n_chunks, PROJ_CHUNKS)]
        _interleave(phase_a(0, PROJ_CHUNKS))
        for gi, group in enumerate(groups):
            dependent = []
            for c in group:
                dependent.append(phase_b(c))
                if c:
                    dependent.append(phase_c(c - 1))
            ahead = [phase_a(groups[gi + 1][0], PROJ_CHUNKS)] if gi + 1 < len(groups) else []
            _interleave(*ahead, itertools.chain(*dependent))
        _interleave(phase_c(n_chunks - 1))

        key_chunks = list(range(cps)) + [n_chunks]
        n_blk = cps * (MLA_HEADS // 2)
        trips = n_blk // 2
        scans_per_trip = n_sc // trips

        def heads_of(idx):
            qc = idx // (MLA_HEADS // 2)
            j = idx % (MLA_HEADS // 2)
            return qc, [2 * j, 2 * j + 1]

        def scores(idx, slot):
            qc, hs = heads_of(idx)
            for par, h in enumerate(hs):
                qt = qt_scr[h, qc]
                m = None
                for i in range(len(key_chunks)):
                    rows = slice(i * ROW_CHUNK, (i + 1) * ROW_CHUNK)
                    st = _dot(k_scr[h, rows, :], qt)
                    st_scr[slot, par, rows, :] = st
                    mc = jnp.max(st, axis=0, keepdims=True)
                    m = mc if m is None else jnp.maximum(m, mc)
                    yield
                m_scr[slot, par] = m

        def finish(idx, slot):
            qc, hs = heads_of(idx)
            for par, h in enumerate(hs):
                hr = pl.ds(pl.multiple_of(h * V_DIM, V_DIM), V_DIM)
                vr = pl.ds(pl.multiple_of(h * V_EXT, V_EXT - V_DIM), V_EXT)
                m = m_scr[slot, par]
                ot = None
                for i, kc in enumerate(key_chunks):
                    rows = slice(i * ROW_CHUNK, (i + 1) * ROW_CHUNK)
                    pt = jnp.exp2(st_scr[slot, par, rows, :] - m).astype(BF16)
                    part = _dot(vt_scr[kc, vr, :], pt)
                    ot = part if ot is None else ot + part
                    yield
                ymt_scr[qc, hr, :] = ot[0:V_DIM, :] * (1.0 / ot[V_DIM:V_DIM + 1, :])

        _interleave(scores(0, 0))

        def att_body(i, carry):
            b0 = 2 * i
            state = {"carry": carry, "c": i * scans_per_trip}

            def scans(n):
                for _ in range(n):
                    c = state["c"]
                    rf = pl.multiple_of(c * SCAN_CHUNK, SCAN_CHUNK)
                    rb = pl.multiple_of((n_sc - 1 - c) * SCAN_CHUNK, SCAN_CHUNK)
                    state["carry"] = scan_step(rf, rb, *state["carry"])
                    state["c"] = c + 1
                    yield
                    yield
                    yield

            half = scans_per_trip // 2
            _interleave(scores(b0 + 1, 1), finish(b0, 0), scans(half))
            _interleave(scores(jnp.minimum(b0 + 2, n_blk - 1), 0), finish(b0 + 1, 1),
                        scans(scans_per_trip - half))
            return state["carry"]

        lax.fori_loop(0, trips, att_body, (h0_ref[0:1, :], h0_ref[1:2, :]))

        _interleave(f_build(0))
        for c in range(n_chunks):
            ahead = [f_build(c + 1)] if c + 1 < n_chunks else []
            _interleave(f_project(c), *ahead)
    else:
        def phase_d(s):
            carry = (jnp.zeros((1, LRU_W), F32), jnp.zeros((1, LRU_W), F32))
            for c in range(n_sc):
                carry = scan_step(s * seq + c * SCAN_CHUNK,
                                  s * seq + (n_sc - 1 - c) * SCAN_CHUNK, *carry)
                yield
            st_out_ref[s, 0:1, :] = carry[0]
            st_out_ref[s, 1:2, :] = carry[1]

        def phase_e(s):
            blocks = [(qi, h) for qi in range(cps) for h in range(MLA_HEADS)]

            def scores(blk):
                qi, h = blk
                return _dot(k_scr[h, s * seq:(s + 1) * seq, :], qt_scr[h, s * cps + qi])

            st = scores(blocks[0])
            for i, (qi, h) in enumerate(blocks):
                nxt = scores(blocks[i + 1]) if i + 1 < len(blocks) else None
                hr = slice(h * V_DIM, (h + 1) * V_DIM)
                vr = slice(h * V_EXT, (h + 1) * V_EXT)
                vts = [vt_scr[s * cps + j, vr, :] for j in range(cps)]
                ymt_scr[s * cps + qi, hr, :] = softmax_pv(st, vts)
                st = nxt
                yield

        def sequence(s):
            chunks = range(s * cps, (s + 1) * cps)
            for c in chunks:
                yield from phase_a(c)
            for c in chunks:
                yield from phase_b(c)
            for c in chunks:
                yield from phase_c(c)
            yield from phase_d(s)
            yield from phase_e(s)
            for c in chunks:
                yield from f_build(c)
                yield from f_project(c)

        side = _ffn_steps(*ffn_in_refs, ffn_out_ref) if fused_ffn else None
        _interleave_staggered([sequence(s) for s in range(n_seq)], SEQ_STAGGER,
                              side=side, side_period=FUSED_FFN_PERIOD)


def _const_spec(shape):
    nd = len(shape)
    return pl.BlockSpec(shape, lambda i, _n=nd: (0,) * _n,
                        pipeline_mode=pl.Buffered(1))


def _layer_spec(arr, layer):
    nd = arr.ndim - 1
    return pl.BlockSpec((None,) + arr.shape[1:], lambda i, _n=nd: (layer,) + (0,) * _n,
                        pipeline_mode=pl.Buffered(1))


def _mix_call(x2d, mods, layer, lw, *, seq, n_seq, latent, extra=None, ffn_h2d=None,
              ffn_rows_per_mod=None, ffn_w=None, caches=None):
    n_tok = x2d.shape[0]
    T = seq * n_seq
    grid = (n_tok // T,)
    fused_ffn = ffn_h2d is not None
    n_keys_buf = T if not latent else seq + PAST_LEN

    if latent:
        mod_map = lambda i: (0, i, 0, 0)
    else:
        mod_map = lambda i: (0, 4, 0, 0)

    consts = [lw["vecs"], lw["w_in"], lw["w_in_last"], lw["conv_w"], lw["lru_conv_w"],
              lw["w_gates"], lw["lru_lambda"], lw["wq_t"], lw["wk"], lw["wv_t"], lw["qg_t"],
              lw["w_out"]]
    args = [x2d, mods] + consts
    in_specs = [pl.BlockSpec((T, D_MODEL), lambda i: (i, 0)),
                pl.BlockSpec((None, None, 1, 6 * D_MODEL), mod_map)]
    in_specs += [_layer_spec(a, layer) for a in consts]
    if latent:
        lconsts = [lw["wq_partner_t"], lw["qg_partner_t"]]
        tables = [extra["rope_ct"], extra["rope_st"], extra["rope_c"], extra["rope_s"]]
        args += lconsts + tables
        in_specs += [_layer_spec(a, layer) for a in lconsts]
        in_specs += [_const_spec(a.shape) for a in tables]
        args += [extra["cache_ckv"], extra["cache_kr"], extra["state"]]
        in_specs += [
            pl.BlockSpec((None, None, PAST_LEN, KV_LORA), lambda i: (i, layer, 0, 0)),
            pl.BlockSpec((None, None, PAST_LEN, LANES), lambda i: (i, layer, 0, 0)),
            pl.BlockSpec((None, None, 2, LRU_W), lambda i: (i, layer, 0, 0)),
        ]

    if fused_ffn:
        ffn_rows = ffn_h2d.shape[0] // grid[0]
        tiles_per_mod = ffn_rows_per_mod // ffn_rows
        args += [ffn_h2d, mods, lw["norm2_g"]] + list(ffn_w)
        in_specs += [pl.BlockSpec((ffn_rows, D_MODEL), lambda i: (i, 0)),
                     pl.BlockSpec((None, None, 1, 6 * D_MODEL),
                                  lambda i: (0, i // tiles_per_mod, 0, 0))]
        in_specs += [_layer_spec(lw["norm2_g"], layer)] + [_layer_spec(a, 0) for a in ffn_w]

    out_shape = [jax.ShapeDtypeStruct((n_tok, D_MODEL), F32)]
    out_specs = [pl.BlockSpec((T, D_MODEL), lambda i: (i, 0))]
    if not latent:
        n_b = n_tok // seq
        out_shape += [jax.ShapeDtypeStruct((n_b, DEPTH, seq, KV_LORA), F32),
                      jax.ShapeDtypeStruct((n_b, DEPTH, seq, QK_ROPE), F32),
                      jax.ShapeDtypeStruct((n_b, DEPTH, 2, LRU_W), F32)]
        out_specs += [pl.BlockSpec((n_seq, None, seq, KV_LORA), lambda i: (i, layer, 0, 0)),
                      pl.BlockSpec((n_seq, None, seq, QK_ROPE), lambda i: (i, layer, 0, 0)),
                      pl.BlockSpec((n_seq, None, 2, LRU_W), lambda i: (i, layer, 0, 0))]
    aliases = {}
    if caches is not None:
        for k, arr in enumerate(caches):
            aliases[len(args)] = 1 + k
            args.append(arr)
            in_specs.append(pl.BlockSpec(memory_space=pl.ANY))
    if fused_ffn:
        out_shape.append(jax.ShapeDtypeStruct(ffn_h2d.shape, F32))
        out_specs.append(pl.BlockSpec((ffn_rows, D_MODEL), lambda i: (i, 0)))

    scratch = [
        pltpu.VMEM((T + 2 * PAD_ROWS, U_W), F32),
        pltpu.VMEM((MLA_HEADS, T // ROW_CHUNK, HEAD_PAD, ROW_CHUNK), BF16),
        pltpu.VMEM((MLA_HEADS, n_keys_buf, HEAD_PAD), BF16),
        pltpu.VMEM((n_keys_buf // ROW_CHUNK, MLA_HEADS * V_EXT, ROW_CHUNK), BF16),
        pltpu.VMEM((T // ROW_CHUNK, MLA_HEADS * V_DIM, ROW_CHUNK), F32),
        pltpu.VMEM((T, D_MODEL), BF16),
    ]
    if latent:
        scratch.append(pltpu.VMEM((2, 2, n_keys_buf, ROW_CHUNK), F32))
        scratch.append(pltpu.VMEM((2, 2, 1, ROW_CHUNK), F32))
    return pl.pallas_call(
        functools.partial(_mix_kernel, seq=seq, n_seq=n_seq, latent=latent,
                          fused_ffn=fused_ffn, n_aliased=len(aliases)),
        grid=grid,
        in_specs=in_specs,
        out_specs=out_specs,
        out_shape=out_shape,
        scratch_shapes=scratch,
        input_output_aliases=aliases,
        compiler_params=pltpu.CompilerParams(
            dimension_semantics=("arbitrary",),
            vmem_limit_bytes=VMEM_LIMIT_BYTES),
        name="mix_latent" if latent else ("mix_context_swiglu" if fused_ffn else "mix_context"),
    )(*args)


def _ffn_steps(h_ref, mod_ref, n2g_ref, wg_ref, wu_ref, wd_ref, o_ref, per_chunk=None):
    sh2 = mod_ref[:, 3 * D_MODEL:4 * D_MODEL]
    sc2 = mod_ref[:, 4 * D_MODEL:5 * D_MODEL]
    g2 = mod_ref[:, 5 * D_MODEL:6 * D_MODEL]
    h = h_ref[...]
    hn = (h * _rms_scale(h, D_MODEL) * (n2g_ref[...] * (1.0 + sc2)) + sh2).astype(BF16)
    yield
    acc = None
    for c0 in range(0, FF, FF_CHUNK):
        g = _dot(hn, wg_ref[:, c0:c0 + FF_CHUNK])
        u = _dot(hn, wu_ref[:, c0:c0 + FF_CHUNK])
        act = (g * _sigmoid(g) * u).astype(BF16)
        part = _dot(act, wd_ref[c0:c0 + FF_CHUNK, :])
        acc = part if acc is None else acc + part
        if per_chunk is not None:
            per_chunk(c0 // FF_CHUNK)
        yield
    o_ref[...] = h_ref[...] + g2 * acc
    yield


def _ffn_kernel(*refs, n_cast, with_ada):
    n_ada = 3 if with_ada else 0
    n_in = 6 + n_cast + n_ada
    ffn_in, cast_in, ada_in = refs[:6], refs[6:6 + n_cast], refs[6 + n_cast:n_in]
    o_ref, cast_out = refs[n_in], refs[n_in + 1:n_in + 1 + n_cast]
    n_chunks = FF // FF_CHUNK

    def cast_piece(k):
        for src, dst in zip(cast_in, cast_out):
            rows, cols = src.shape
            if cols % n_chunks == 0 and (cols // n_chunks) % LANES == 0:
                sl = (slice(None), slice(k * (cols // n_chunks), (k + 1) * (cols // n_chunks)))
            else:
                assert rows % n_chunks == 0 and (rows // n_chunks) % 16 == 0
                sl = (slice(k * (rows // n_chunks), (k + 1) * (rows // n_chunks)), slice(None))
            dst[sl] = src[sl].astype(BF16)

    if with_ada:
        _ada_tile(*ada_in, refs[n_in + 1 + n_cast])
    _interleave(_ffn_steps(*ffn_in, o_ref, per_chunk=cast_piece if n_cast else None))


def _ffn_call(h2d, mods, layer, lw, ffn_w, *, rows_per_mod, mod_base, cast_next=None,
              ada_next=None):
    n_tok = h2d.shape[0]
    n_steps = n_tok // FFN_ROWS
    if mod_base is None:
        tiles_per_mod = rows_per_mod // FFN_ROWS
        mod_map = lambda i: (0, i // tiles_per_mod, 0, 0)
    else:
        mod_map = lambda i: (0, mod_base, 0, 0)
    args = [h2d, mods, lw["norm2_g"]] + list(ffn_w)
    in_specs = [pl.BlockSpec((FFN_ROWS, D_MODEL), lambda i: (i, 0)),
                pl.BlockSpec((None, None, 1, 6 * D_MODEL), mod_map),
                _layer_spec(lw["norm2_g"], layer)]
    in_specs += [_layer_spec(a, 0) for a in ffn_w]
    out_shape = [jax.ShapeDtypeStruct((n_tok, D_MODEL), F32)]
    out_specs = [pl.BlockSpec((FFN_ROWS, D_MODEL), lambda i: (i, 0))]
    for w in cast_next or ():
        rows = w.shape[1] // n_steps
        assert rows * n_steps == w.shape[1] and rows % 16 == 0
        args.append(w)
        in_specs.append(pl.BlockSpec((None, rows, w.shape[2]), lambda i: (layer + 1, i, 0)))
        out_shape.append(jax.ShapeDtypeStruct((1,) + w.shape[1:], BF16))
        out_specs.append(pl.BlockSpec((None, rows, w.shape[2]), lambda i: (0, i, 0)))
    if ada_next is not None:
        cond8, ada_w, ada_b3 = ada_next
        n_out = ada_w.shape[2]
        tn = n_out // n_steps
        assert tn * n_steps == n_out and tn % LANES == 0
        args += [cond8, ada_w, ada_b3]
        in_specs += [pl.BlockSpec((8, D_MODEL), lambda i: (0, 0)),
                     pl.BlockSpec((None, D_MODEL, tn), lambda i: (layer + 1, 0, i)),
                     pl.BlockSpec((None, 1, tn), lambda i: (layer + 1, 0, i))]
        out_shape.append(jax.ShapeDtypeStruct((1, 8, 1, n_out), F32))
        out_specs.append(pl.BlockSpec((None, 8, 1, tn), lambda i: (0, 0, 0, i)))
    return pl.pallas_call(
        functools.partial(_ffn_kernel, n_cast=len(cast_next or ()),
                          with_ada=ada_next is not None),
        grid=(n_steps,),
        in_specs=in_specs,
        out_specs=out_specs,
        out_shape=out_shape,
        compiler_params=pltpu.CompilerParams(
            dimension_semantics=("arbitrary",),
            vmem_limit_bytes=VMEM_LIMIT_BYTES),
        name="swiglu",
    )(*args)


def _rope_partner_index():
    return [d + 8 if (d % 16) < 8 else d - 8 for d in range(QK_ROPE)]


def _prep_weights(p):
    partner = jnp.array(_rope_partner_index(), jnp.int32)
    q_scale = QK_DIM ** -0.5 * math.log2(math.e)
    w_in = p["w_in"].astype(BF16)
    w_tail = w_in[:, :, C_CKV:]
    w_kr = w_tail[:, :, KV_LORA:KV_LORA + QK_ROPE]
    zeros32 = jnp.zeros((DEPTH, D_MODEL, QK_ROPE), BF16)
    w_in_last = jnp.concatenate(
        [w_tail[:, :, :KV_LORA], w_kr[:, :, partner], zeros32, w_kr, zeros32], axis=2)

    wq_h = p["mla_wq_up"].reshape(DEPTH, Q_LORA, MLA_HEADS, QK_DIM)
    wq_pad = jnp.pad(wq_h, ((0, 0), (0, 0), (0, 0), (0, HEAD_PAD - QK_DIM)))
    wq_t = wq_pad.reshape(DEPTH, Q_LORA, MLA_HEADS * HEAD_PAD).transpose(0, 2, 1)
    rope_pad = ((0, 0), (0, 0), (0, 0), (ROPE_LO, HEAD_PAD - ROPE_LO - QK_ROPE))
    wq_partner = jnp.pad(wq_h[:, :, :, QK_NOPE:][:, :, :, partner], rope_pad)
    wq_partner_t = wq_partner.reshape(DEPTH, Q_LORA, MLA_HEADS * HEAD_PAD).transpose(0, 2, 1)

    wkv = p["mla_wkv_up"].reshape(DEPTH, KV_LORA, MLA_HEADS, QK_NOPE + V_DIM)
    wk_pad = jnp.pad(wkv[:, :, :, :QK_NOPE], ((0, 0), (0, 0), (0, 0), (0, HEAD_PAD - QK_NOPE)))
    wk_pad = wk_pad.reshape(DEPTH, KV_LORA, MLA_HEADS * HEAD_PAD)
    wv_ext = jnp.pad(wkv[:, :, :, QK_NOPE:], ((0, 0), (0, 0), (0, 0), (0, V_EXT - V_DIM)))
    wv_t = wv_ext.reshape(DEPTH, KV_LORA, MLA_HEADS * V_EXT).transpose(0, 2, 1)

    def pad_gain(g):
        return jnp.pad(g, ((0, 0), (0, HEAD_PAD - QK_DIM)))[:, None, :]

    def partner_gain(g):
        gp = g[:, QK_NOPE:][:, partner]
        return jnp.pad(gp, ((0, 0), (ROPE_LO, HEAD_PAD - ROPE_LO - QK_ROPE)))[:, None, :]

    def on_sublanes(g):
        return jnp.broadcast_to(g.transpose(0, 2, 1), (DEPTH, HEAD_PAD, ROW_CHUNK))

    qg, kg = p["q_norm_g"], p["k_norm_g"]
    eye = jnp.eye(LRU_BLOCKS, dtype=F32)
    gate_w = jnp.stack([p["lru_wa"][:, 0], p["lru_wi"][:, 0],
                        p["lru_wa"][:, 1], p["lru_wi"][:, 1]], axis=1)
    w_gates = (gate_w[:, :, :, :, None, :] * eye[None, None, :, None, :, None])
    w_gates = w_gates.transpose(0, 2, 3, 1, 4, 5).reshape(DEPTH, LRU_W, 4 * LRU_W)
    b_gates = jnp.stack([p["lru_ba"][:, 0], p["lru_bi"][:, 0],
                         p["lru_ba"][:, 1], p["lru_bi"][:, 1]], axis=1)

    def row(v):
        return v[:, None, :]

    rows = {
        "norm1_g": p["norm1_g"], "conv_b": p["conv_b"], "lru_conv_b": p["lru_conv_b"],
        "b_gates": 0.5 * b_gates.reshape(DEPTH, 4 * LRU_W),
        "mla_qnorm_g": p["mla_qnorm_g"], "mla_kvnorm_g": p["mla_kvnorm_g"],
        "kg": pad_gain(kg)[:, 0, :], "kg_partner": partner_gain(kg)[:, 0, :],
        "gnorm_conv": p["gnorm_conv"], "gnorm_lru": p["gnorm_lru"],
        "gnorm_mla": p["gnorm_mla"],
    }
    for name, (_, width) in VEC_SLOTS.items():
        assert rows[name].shape == (DEPTH, width), name
    vecs = row(jnp.concatenate([rows[name] for name in VEC_SLOTS], axis=1))

    return {
        "vecs": vecs,
        "w_in": w_in,
        "w_in_last": w_in_last,
        "conv_w": p["conv_w"],
        "lru_conv_w": p["lru_conv_w"],
        "w_gates": (0.5 * w_gates).astype(BF16),
        "lru_lambda": p["lru_lambda"],
        "wq_t": wq_t.astype(BF16),
        "wq_partner_t": wq_partner_t.astype(BF16),
        "wk": wk_pad.astype(BF16),
        "wv_t": wv_t.astype(BF16),
        "qg_t": on_sublanes(pad_gain(qg) * q_scale),
        "qg_partner_t": on_sublanes(partner_gain(qg) * q_scale),
        "w_out": p["w_out"].astype(BF16),
        "norm2_g": row(p["norm2_g"]),
        "ffn0": [p[k][0:1].astype(BF16) for k in ("w_gate", "w_up", "w_down")],
    }


def _rope_tables(n_tokens):
    n_rows = n_tokens // GRID_W
    row = np.repeat(np.arange(n_rows, dtype=np.float32), GRID_W)
    col = np.tile(np.arange(GRID_W, dtype=np.float32), n_rows)
    n_freq = QK_ROPE // 4
    inv_freq = np.power(np.float32(ROPE_THETA),
                        -np.arange(n_freq, dtype=np.float32) / np.float32(n_freq))
    ang_r = (row[:, None] * inv_freq).astype(np.float32)
    ang_c = (col[:, None] * inv_freq).astype(np.float32)
    cos = np.concatenate([np.cos(ang_r), np.cos(ang_r), np.cos(ang_c), np.cos(ang_c)], 1)
    sin = np.concatenate([-np.sin(ang_r), np.sin(ang_r), -np.sin(ang_c), np.sin(ang_c)], 1)
    ones = np.ones((n_tokens, QK_NOPE), np.float32)
    tail = np.zeros((n_tokens, HEAD_PAD - QK_DIM), np.float32)
    rope_c = np.concatenate([ones, cos, tail], axis=1).astype(np.float32)
    rope_s = np.concatenate([np.zeros((n_tokens, QK_NOPE), np.float32), sin, tail],
                            axis=1).astype(np.float32)
    return rope_c, rope_s


def kernel(x_prompt, x_sample, cache_ckv, cache_krope, state_lru, c, c_ctx, norm1_g, ada_w, ada_b, w_in, conv_w, conv_b, lru_conv_w, lru_conv_b, lru_wa, lru_ba, lru_wi, lru_bi, lru_lambda, mla_qnorm_g, mla_wq_up, mla_kvnorm_g, mla_wkv_up, q_norm_g, k_norm_g, gnorm_conv, gnorm_lru, gnorm_mla, w_out, norm2_g, w_gate, w_up, w_down):
    params = dict(norm1_g=norm1_g, w_in=w_in, conv_w=conv_w, conv_b=conv_b,
                  lru_conv_w=lru_conv_w, lru_conv_b=lru_conv_b, lru_wa=lru_wa,
                  lru_ba=lru_ba, lru_wi=lru_wi, lru_bi=lru_bi, lru_lambda=lru_lambda,
                  mla_qnorm_g=mla_qnorm_g, mla_wq_up=mla_wq_up,
                  mla_kvnorm_g=mla_kvnorm_g, mla_wkv_up=mla_wkv_up, q_norm_g=q_norm_g,
                  k_norm_g=k_norm_g, gnorm_conv=gnorm_conv, gnorm_lru=gnorm_lru,
                  gnorm_mla=gnorm_mla, w_out=w_out, norm2_g=norm2_g, w_gate=w_gate,
                  w_up=w_up, w_down=w_down)
    batch, seq_p, _ = x_prompt.shape
    dec_batch, seq_s, _ = x_sample.shape
    assert dec_batch == 4 and cache_ckv.shape[2] == PAST_LEN == ROW_CHUNK

    cond8 = jnp.concatenate(
        [c, c_ctx[None, :], jnp.zeros((8 - dec_batch - 1, D_MODEL), F32)], axis=0)
    ada_b3 = ada_b.reshape(DEPTH, 1, 6 * D_MODEL)
    mods = _ada_call(cond8, ada_w, ada_b3)

    rope_c, rope_s = _rope_tables(seq_s)
    cache_kr_pad = jnp.pad(
        cache_krope, ((0, 0), (0, 0), (0, 0), (ROPE_LO, HEAD_PAD - ROPE_LO - QK_ROPE)))

    def chunked_t(tab):
        return jnp.asarray(np.ascontiguousarray(
            tab.reshape(seq_s // ROW_CHUNK, ROW_CHUNK, HEAD_PAD).transpose(0, 2, 1)))

    extra = dict(rope_c=jnp.asarray(rope_c), rope_s=jnp.asarray(rope_s),
                 rope_ct=chunked_t(rope_c), rope_st=chunked_t(rope_s), cache_ckv=cache_ckv,
                 cache_kr=cache_kr_pad, state=state_lru)

    xp = x_prompt.reshape(batch * seq_p, D_MODEL)
    xs = x_sample.reshape(dec_batch * seq_s, D_MODEL)
    caches = None
    lw = _prep_weights(params)
    ffn_w = lw["ffn0"]
    ffn_f32 = (w_gate, w_up, w_down)
    for l in range(DEPTH):
        (hs,) = _mix_call(xs, mods, l, lw, seq=seq_s, n_seq=1, latent=True, extra=extra)
        hp, *caches, xs = _mix_call(xp, mods, l, lw, seq=seq_p, n_seq=CTX_SEQS_PER_STEP,
                                    latent=False, ffn_h2d=hs, ffn_rows_per_mod=seq_s,
                                    ffn_w=ffn_w, caches=caches)
        if l + 1 < DEPTH:
            xp, *ffn_w, next_mods = _ffn_call(hp, mods, l, lw, ffn_w, rows_per_mod=None,
                                              mod_base=4, cast_next=ffn_f32,
                                              ada_next=(cond8, ada_w, ada_b3))
        else:
            (xp,) = _ffn_call(hp, mods, l, lw, ffn_w, rows_per_mod=None, mod_base=4)
            next_mods = None
        mods = next_mods
    new_cache_ckv, new_cache_krope, new_state_lru = caches
    return (xp.reshape(batch, seq_p, D_MODEL),
            xs.reshape(dec_batch, seq_s, D_MODEL),
            new_cache_ckv, new_cache_krope, new_state_lru)
```
